```python
import math
import jax, jax.numpy as jnp
from jax import lax
import numpy as np

D_MODEL = 1024
BATCH = 8
SEQ = 2048
DEPTH = 2
DEC_BATCH = 128
DEC_SEQ = 4
PAST_LEN = 16384
PAGE_SIZE = 128

N_META = 16
MIX_WIDTH = D_MODEL // 2
GLA_HEADS = 4
GLA_DV = MIX_WIDTH // GLA_HEADS
GLA_DK = GLA_DV // 2
GLA_GATE_RANK = 16
GLA_GATE_NORM = 16.0
GLA_CHUNK = 16
SSM_HEAD_DIM = 64
SSM_HEADS = MIX_WIDTH // SSM_HEAD_DIM
SSM_GROUPS = 2
SSM_STATE = 64
SSM_CONV = 4
SSM_CONV_DIM = MIX_WIDTH + 2 * SSM_GROUPS * SSM_STATE
SSM_CHUNK = 128
RET_HEADS = 4
RET_DV = MIX_WIDTH // RET_HEADS
RET_DK = RET_DV // 2
RET_CHUNK = 128
ROPE_BASE = 10000.0
D_FF = 4 * D_MODEL
ALPHA = (2 * DEPTH) ** 0.25
BETA = (8 * DEPTH) ** -0.25
N_BRANCH = 3
SPLIT_SIZES = (GLA_HEADS * GLA_DK, GLA_HEADS * GLA_DK, MIX_WIDTH, MIX_WIDTH, GLA_GATE_RANK,
               MIX_WIDTH, SSM_CONV_DIM, SSM_HEADS,
               RET_HEADS * RET_DK, RET_HEADS * RET_DK, MIX_WIDTH, MIX_WIDTH,
               N_BRANCH * D_MODEL)
D_IN_PROJ = sum(SPLIT_SIZES)

kernel_name = 'hybrid_gla_ssd_retention_decoder_step'


def layer_norm(x, w, b, eps=1e-5):
    xf = x.astype(jnp.float32)
    mu = jnp.mean(xf, axis=-1, keepdims=True)
    var = jnp.mean(jnp.square(xf - mu), axis=-1, keepdims=True)
    return ((xf - mu) * lax.rsqrt(var + eps) * w + b).astype(x.dtype)


def rms_norm(x, w=None, eps=1e-6):
    xf = x.astype(jnp.float32)
    y = xf * lax.rsqrt(jnp.mean(xf * xf, axis=-1, keepdims=True) + eps)
    if w is not None:
        y = y * w
    return y.astype(x.dtype)


def rotary(x, pos):
    half = x.shape[-1] // 2
    inv_freq = ROPE_BASE ** (-jnp.arange(half, dtype=jnp.float32) / half)
    ang = pos.astype(jnp.float32)[:, None] * inv_freq[None, :]
    cos = jnp.cos(ang)[None, :, None, :]
    sin = jnp.sin(ang)[None, :, None, :]
    xf = x.astype(jnp.float32)
    x1, x2 = xf[..., :half], xf[..., half:]
    return jnp.concatenate([x1 * cos - x2 * sin, x1 * sin + x2 * cos], axis=-1).astype(x.dtype)


def gated_linear_recurrence(q, k, v, g, s0, chunk):
    f32 = jnp.float32
    b, t, h, _ = q.shape
    dv = v.shape[-1]
    n = t // chunk
    scalar_decay = g.shape[-1] == 1
    mask = jnp.tril(jnp.ones((chunk, chunk), dtype=bool))[None, :, :, None, None]

    def blocks(a):
        return a.astype(f32).reshape(b, n, chunk, h, a.shape[-1]).swapaxes(0, 1)

    def step(s, inp):
        qc, kc, vc, gc = inp
        G = jnp.cumsum(gc, axis=1)
        G_end = G[:, -1]
        diff = G[:, :, None] - G[:, None, :]
        decay = jnp.exp(jnp.where(mask, diff, -jnp.inf))
        if scalar_decay:
            att = jnp.einsum('bthk,bshk->btsh', qc, kc) * decay[..., 0]
        else:
            att = jnp.einsum('bthk,bshk,btshk->btsh', qc, kc, decay)
        o = (jnp.einsum('btsh,bshv->bthv', att, vc)
             + jnp.einsum('bthk,bhkv->bthv', qc * jnp.exp(G), s))
        s_new = (jnp.exp(G_end)[..., None] * s
                 + jnp.einsum('bshk,bshv->bhkv', kc * jnp.exp(G_end[:, None] - G), vc))
        return s_new, o

    s_fin, o = lax.scan(step, s0.astype(f32), (blocks(q), blocks(k), blocks(v), blocks(g)))
    o = o.swapaxes(0, 1).reshape(b, t, h, dv)
    return o.astype(v.dtype), s_fin.astype(s0.dtype)


def run_recurrence(q, k, v, g, s0, chunk, prompt):
    if not prompt:
        return gated_linear_recurrence(q, k, v, g, s0, q.shape[1])
    o_meta, s = gated_linear_recurrence(q[:, :N_META], k[:, :N_META], v[:, :N_META],
                                        g[:, :N_META], s0, N_META)
    o_body, s = gated_linear_recurrence(q[:, N_META:], k[:, N_META:], v[:, N_META:],
                                        g[:, N_META:], s, chunk)
    return jnp.concatenate([o_meta, o_body], axis=1), s


def token_mixers(u, pos, state, wl, prompt):
    b, t, _ = u.shape
    f32 = jnp.float32
    dtype = u.dtype
    proj = jnp.einsum('btd,de->bte', u, wl['w_in'])
    offs = np.cumsum(SPLIT_SIZES)[:-1].tolist()
    (gla_q, gla_k, gla_v, gla_r, gla_a, ssm_z, ssm_xbc, ssm_dt,
     ret_q, ret_k, ret_v, ret_g, gate_logits) = jnp.split(proj, offs, axis=-1)
    if prompt:
        s_gla = jnp.zeros((b, GLA_HEADS, GLA_DK, GLA_DV), dtype)
        s_ssm = jnp.zeros((b, SSM_HEADS, SSM_STATE, SSM_HEAD_DIM), dtype)
        conv_buf = jnp.zeros((b, SSM_CONV - 1, SSM_CONV_DIM), dtype)
        s_ret = jnp.zeros((b, RET_HEADS, RET_DK, RET_DV), dtype)
    else:
        s_gla, s_ssm, conv_buf, s_ret = state

    q = gla_q.reshape(b, t, GLA_HEADS, GLA_DK) * (GLA_DK ** -0.5)
    k = gla_k.reshape(b, t, GLA_HEADS, GLA_DK)
    v = gla_v.reshape(b, t, GLA_HEADS, GLA_DV)
    a = jnp.einsum('btr,re->bte', gla_a, wl['w_gla_a2']) + wl['b_gla_a']
    g = (jax.nn.log_sigmoid(a.astype(f32)) / GLA_GATE_NORM).reshape(b, t, GLA_HEADS, GLA_DK)
    o, s_gla_new = run_recurrence(q, k, v, g, s_gla, GLA_CHUNK, prompt)
    y_gla = jax.nn.silu(gla_r) * rms_norm(o, wl['w_gla_norm']).reshape(b, t, MIX_WIDTH)

    xbc_in = jnp.concatenate([conv_buf.astype(dtype), ssm_xbc], axis=1)
    conv_new = xbc_in[:, -(SSM_CONV - 1):]
    cw = wl['conv_w']
    xbc = sum(cw[i] * xbc_in[:, i:i + t] for i in range(SSM_CONV)) + wl['conv_b']
    xbc = jax.nn.silu(xbc)
    xs, bm, cm = jnp.split(xbc, [MIX_WIDTH, MIX_WIDTH + SSM_GROUPS * SSM_STATE], axis=-1)
    dt = jax.nn.softplus(ssm_dt.astype(f32) + wl['dt_bias'])
    a_neg = -jnp.exp(wl['a_log'].astype(f32))
    xh = xs.reshape(b, t, SSM_HEADS, SSM_HEAD_DIM)
    rep = SSM_HEADS // SSM_GROUPS
    bh = jnp.repeat(bm.reshape(b, t, SSM_GROUPS, SSM_STATE), rep, axis=2)
    ch = jnp.repeat(cm.reshape(b, t, SSM_GROUPS, SSM_STATE), rep, axis=2)
    v_ssm = (xh.astype(f32) * dt[..., None]).astype(dtype)
    o, s_ssm_new = run_recurrence(ch, bh, v_ssm, (dt * a_neg)[..., None], s_ssm, SSM_CHUNK, prompt)
    y = o + wl['d_skip'][:, None] * xh
    y = (y.reshape(b, t, MIX_WIDTH) * jax.nn.silu(ssm_z)).reshape(b, t, SSM_GROUPS, MIX_WIDTH // SSM_GROUPS)
    y_ssm = rms_norm(y).reshape(b, t, MIX_WIDTH) * wl['w_ssm_norm']

    q = rotary(ret_q.reshape(b, t, RET_HEADS, RET_DK), pos)
    k = rotary(ret_k.reshape(b, t, RET_HEADS, RET_DK), pos) * (RET_DK ** -0.5)
    v = ret_v.reshape(b, t, RET_HEADS, RET_DV)
    log_gamma = jnp.log1p(-jnp.exp2(-5.0 - jnp.arange(RET_HEADS, dtype=f32)))
    g = jnp.broadcast_to(log_gamma[:, None], (b, t, RET_HEADS, 1))
    o, s_ret_new = run_recurrence(q, k, v, g, s_ret, RET_CHUNK, prompt)
    y_ret = jax.nn.silu(ret_g) * rms_norm(o).reshape(b, t, MIX_WIDTH)

    gates = jax.nn.sigmoid(gate_logits.reshape(b, t, N_BRANCH, D_MODEL))
    merged = (gates[..., 0, :] * jnp.einsum('btm,md->btd', y_gla, wl['w_gla_out'])
              + gates[..., 1, :] * jnp.einsum('btm,md->btd', y_ssm, wl['w_ssm_out'])
              + gates[..., 2, :] * jnp.einsum('btm,md->btd', y_ret, wl['w_ret_out']))
    out = jnp.einsum('btd,de->bte', merged, wl['w_o'])
    return out, (s_gla_new, s_ssm_new, conv_new, s_ret_new)


def run_trunk(x, pos, states, ln_in_w, ln_in_b, layer_w, prompt):
    x = layer_norm(x, ln_in_w, ln_in_b)
    collected = ([], [], [], [])
    for l in range(DEPTH):
        wl = {name: arr[l] for name, arr in layer_w.items()}
        st = None if prompt else tuple(s[l] for s in states)
        mix, new_st = token_mixers(x, pos, st, wl, prompt)
        h = layer_norm(ALPHA * x + mix, wl['ln1_w'], wl['ln1_b'])
        hid = jnp.square(jax.nn.relu(jnp.einsum('btd,df->btf', h, wl['w_ff1']) + wl['b_ff1']))
        ff = jnp.einsum('btf,fd->btd', hid, wl['w_ff2']) + wl['b_ff2']
        x = layer_norm(ALPHA * h + ff, wl['ln2_w'], wl['ln2_b'])
        for store, s in zip(collected, new_st):
            store.append(s)
    return x, tuple(jnp.stack(store) for store in collected)


def setup_inputs(seed: int = 0) -> dict:
    key = jax.random.key(seed)
    ks = jax.random.split(key, 32)
    nrm = jax.random.normal
    f32 = jnp.float32
    dt0 = jnp.exp(jax.random.uniform(ks[15], (DEPTH, SSM_HEADS), minval=math.log(1e-3), maxval=math.log(1e-1)))
    return {
        'x_prompt': nrm(ks[0], (BATCH, SEQ, D_MODEL), f32),
        'x_sample': nrm(ks[1], (DEC_BATCH, DEC_SEQ, D_MODEL), f32),
        'state_gla': nrm(ks[2], (DEPTH, DEC_BATCH, GLA_HEADS, GLA_DK, GLA_DV), f32),
        'state_ssm': 0.1 * nrm(ks[3], (DEPTH, DEC_BATCH, SSM_HEADS, SSM_STATE, SSM_HEAD_DIM), f32),
        'state_conv': nrm(ks[4], (DEPTH, DEC_BATCH, SSM_CONV - 1, SSM_CONV_DIM), f32),
        'state_ret': nrm(ks[5], (DEPTH, DEC_BATCH, RET_HEADS, RET_DK, RET_DV), f32),
        'meta_tokens': nrm(ks[6], (N_META, D_MODEL), f32),
        'ln_in_w': 1.0 + 0.02 * nrm(ks[7], (D_MODEL,), f32),
        'ln_in_b': 0.02 * nrm(ks[8], (D_MODEL,), f32),
        'w_in': nrm(ks[9], (DEPTH, D_MODEL, D_IN_PROJ), f32) * D_MODEL ** -0.5,
        'w_gla_a2': nrm(ks[10], (DEPTH, GLA_GATE_RANK, GLA_HEADS * GLA_DK), f32) * GLA_GATE_RANK ** -0.5,
        'b_gla_a': 0.02 * nrm(ks[11], (DEPTH, GLA_HEADS * GLA_DK), f32),
        'w_gla_norm': 1.0 + 0.02 * nrm(ks[12], (DEPTH, GLA_DV), f32),
        'conv_w': nrm(ks[13], (DEPTH, SSM_CONV, SSM_CONV_DIM), f32) * SSM_CONV ** -0.5,
        'conv_b': 0.02 * nrm(ks[14], (DEPTH, SSM_CONV_DIM), f32),
        'dt_bias': dt0 + jnp.log(-jnp.expm1(-dt0)),
        'a_log': jnp.log(jax.random.uniform(ks[16], (DEPTH, SSM_HEADS), minval=1.0, maxval=16.0)),
        'd_skip': 1.0 + 0.02 * nrm(ks[17], (DEPTH, SSM_HEADS), f32),
        'w_ssm_norm': 1.0 + 0.02 * nrm(ks[18], (DEPTH, MIX_WIDTH), f32),
        'w_gla_out': nrm(ks[19], (DEPTH, MIX_WIDTH, D_MODEL), f32) * (BETA * MIX_WIDTH ** -0.5),
        'w_ssm_out': nrm(ks[20], (DEPTH, MIX_WIDTH, D_MODEL), f32) * (BETA * MIX_WIDTH ** -0.5),
        'w_ret_out': nrm(ks[21], (DEPTH, MIX_WIDTH, D_MODEL), f32) * (BETA * MIX_WIDTH ** -0.5),
        'w_o': nrm(ks[22], (DEPTH, D_MODEL, D_MODEL), f32) * (BETA * D_MODEL ** -0.5),
        'ln1_w': 1.0 + 0.02 * nrm(ks[23], (DEPTH, D_MODEL), f32),
        'ln1_b': 0.02 * nrm(ks[24], (DEPTH, D_MODEL), f32),
        'w_ff1': nrm(ks[25], (DEPTH, D_MODEL, D_FF), f32) * (BETA * D_MODEL ** -0.5),
        'b_ff1': 0.02 * nrm(ks[26], (DEPTH, D_FF), f32),
        'w_ff2': nrm(ks[27], (DEPTH, D_FF, D_MODEL), f32) * (BETA * D_FF ** -0.5),
        'b_ff2': 0.02 * nrm(ks[28], (DEPTH, D_MODEL), f32),
        'ln2_w': 1.0 + 0.02 * nrm(ks[29], (DEPTH, D_MODEL), f32),
        'ln2_b': 0.02 * nrm(ks[30], (DEPTH, D_MODEL), f32),
    }


def reference(x_prompt, x_sample, state_gla, state_ssm, state_conv, state_ret, meta_tokens,
              ln_in_w, ln_in_b, w_in, w_gla_a2, b_gla_a, w_gla_norm, conv_w, conv_b, dt_bias,
              a_log, d_skip, w_ssm_norm, w_gla_out, w_ssm_out, w_ret_out, w_o, ln1_w, ln1_b,
              w_ff1, b_ff1, w_ff2, b_ff2, ln2_w, ln2_b):
    layer_w = {
        'w_in': w_in, 'w_gla_a2': w_gla_a2, 'b_gla_a': b_gla_a, 'w_gla_norm': w_gla_norm,
        'conv_w': conv_w, 'conv_b': conv_b, 'dt_bias': dt_bias, 'a_log': a_log, 'd_skip': d_skip,
        'w_ssm_norm': w_ssm_norm, 'w_gla_out': w_gla_out, 'w_ssm_out': w_ssm_out,
        'w_ret_out': w_ret_out, 'w_o': w_o, 'ln1_w': ln1_w, 'ln1_b': ln1_b,
        'w_ff1': w_ff1, 'b_ff1': b_ff1, 'w_ff2': w_ff2, 'b_ff2': b_ff2,
        'ln2_w': ln2_w, 'ln2_b': ln2_b,
    }
    bp = x_prompt.shape[0]
    meta = jnp.broadcast_to(meta_tokens[None].astype(x_prompt.dtype), (bp, N_META, D_MODEL))
    xp = jnp.concatenate([meta, x_prompt], axis=1)
    pos_p = jnp.arange(xp.shape[1], dtype=jnp.int32)
    yp, (state_gla_prompt, state_ssm_prompt, state_conv_prompt, state_ret_prompt) = run_trunk(
        xp, pos_p, None, ln_in_w, ln_in_b, layer_w, True)
    y_prompt = yp[:, N_META:]
    pos_s = PAST_LEN + jnp.arange(x_sample.shape[1], dtype=jnp.int32)
    y_sample, (state_gla_sample, state_ssm_sample, state_conv_sample, state_ret_sample) = run_trunk(
        x_sample, pos_s, (state_gla, state_ssm, state_conv, state_ret), ln_in_w, ln_in_b, layer_w, False)
    return (y_prompt, y_sample, state_gla_prompt, state_gla_sample, state_ssm_prompt, state_ssm_sample,
            state_conv_prompt, state_conv_sample, state_ret_prompt, state_ret_sample)
```

```python
import functools

import jax
import jax.numpy as jnp
from jax import lax
from jax.experimental import pallas as pl
from jax.experimental.pallas import tpu as pltpu

F32 = jnp.float32
BF16 = jnp.bfloat16

D_MODEL = 1024
DEPTH = 2
N_META = 16
MIX = 512
GLA_H, GLA_DK, GLA_DV, GLA_RANK = 4, 64, 128, 16
GLA_GATE_NORM = 16.0
SSM_H, SSM_P, SSM_N, SSM_G, SSM_CONV = 8, 64, 64, 2, 4
SSM_CONV_DIM = MIX + 2 * SSM_G * SSM_N
RET_H, RET_DK, RET_DV = 4, 64, 128
ROPE_BASE = 10000.0
D_FF = 4 * D_MODEL
ALPHA = (2 * DEPTH) ** 0.25
PAST_LEN = 16384
SPLIT_SIZES = (256, 256, 512, 512, 16, 512, 768, 8, 256, 256, 512, 512, 3072)

LANES = 128
SUBLANES = 8
VMEM_LIMIT = 56 * 1024 * 1024

COL = dict(gates=0, gla_v=3072, gla_r=3584, ssm_z=4096, ret_v=4608, ret_g=5120,
           gla_q=5632, gla_k=5888, ssm_xbc=6144, ret_q=6912, ret_k=7168, gla_a=7424, ssm_dt=7552)
N_PROJ = 7680
SAMPLE_ROWS = 8
BLK = 16


def _dot(a, b):
    return jnp.dot(a.astype(BF16), b.astype(BF16), preferred_element_type=F32)


def _dot_nt(a, b):
    return lax.dot_general(a.astype(BF16), b.astype(BF16), (((1,), (1,)), ((), ())),
                           preferred_element_type=F32)


def _dot_tn(a, b):
    return lax.dot_general(a.astype(BF16), b.astype(BF16), (((0,), (0,)), ((), ())),
                           preferred_element_type=F32)


def _dot_f32(a, b, dims=(((1,), (0,)), ((), ()))):
    return lax.dot_general(a, b, dims, precision=lax.Precision.HIGHEST,
                           preferred_element_type=F32)


def _layer_norm(x, w, b):
    mu = jnp.mean(x, axis=-1, keepdims=True)
    xc = x - mu
    var = jnp.mean(xc * xc, axis=-1, keepdims=True)
    return xc * lax.rsqrt(var + 1e-5) * w + b


def _rms(x):
    return x * lax.rsqrt(jnp.mean(x * x, axis=-1, keepdims=True) + 1e-6)


def _silu(x):
    return x * jax.nn.sigmoid(x)


def _tril(n):
    r = lax.broadcasted_iota(jnp.int32, (n, n), 0)
    c = lax.broadcasted_iota(jnp.int32, (n, n), 1)
    return r >= c


def _inproj_kernel(x_ref, lnw_ref, lnb_ref, w_ref, *refs, apply_ln):
    if apply_ln:
        proj_ref, xn_ref, xb_scr = refs
    else:
        proj_ref, xb_scr = refs

    @pl.when(pl.program_id(1) == 0)
    def _():
        x = x_ref[...]
        if apply_ln:
            x = _layer_norm(x, lnw_ref[...], lnb_ref[...])
            xn_ref[...] = x
        xb_scr[...] = x.astype(BF16)

    proj_ref[...] = jnp.dot(xb_scr[...], w_ref[...], preferred_element_type=F32)


def _in_proj(x, lnw, lnb, w, *, apply_ln, tm, tn, name):
    m = x.shape[0]
    grid = (m // tm, N_PROJ // tn)
    out_shape = [jax.ShapeDtypeStruct((m, N_PROJ), F32)]
    out_specs = [pl.BlockSpec((tm, tn), lambda i, j: (i, j))]
    if apply_ln:
        out_shape.append(jax.ShapeDtypeStruct((m, D_MODEL), F32))
        out_specs.append(pl.BlockSpec((tm, D_MODEL), lambda i, j: (i, 0)))
    res = pl.pallas_call(
        functools.partial(_inproj_kernel, apply_ln=apply_ln),
        grid=grid,
        in_specs=[pl.BlockSpec((tm, D_MODEL), lambda i, j: (i, 0)),
                  pl.BlockSpec((1, D_MODEL), lambda i, j: (0, 0)),
                  pl.BlockSpec((1, D_MODEL), lambda i, j: (0, 0)),
                  pl.BlockSpec((D_MODEL, tn), lambda i, j: (0, j))],
        out_specs=out_specs,
        out_shape=out_shape,
        scratch_shapes=[pltpu.VMEM((tm, D_MODEL), BF16)],
        compiler_params=pltpu.CompilerParams(
            dimension_semantics=("parallel", "arbitrary"), vmem_limit_bytes=VMEM_LIMIT),
        name=name,
    )(x, lnw, lnb, w)
    return (res[0], res[1]) if apply_ln else (res[0], x)


def _gla_kernel(q_ref, k_ref, v_ref, r_ref, a_ref, w2_ref, ba_ref, wn_ref, e_ref, s0_ref,
                y_ref, so_ref, s_scr, *, lc, lb, tv):
    c = pl.program_id(1)

    @pl.when(c == 0)
    def _():
        s_scr[...] = s0_ref[0]

    tril = _tril(lb).astype(F32)
    ones = jnp.ones((lb, LANES), F32)
    t_iota = lax.broadcasted_iota(jnp.int32, (lb, 1), 0)
    valid = (t_iota < tv) if tv < lb else None

    def block(i, carry):
        sl = pl.ds(pl.multiple_of(i * lb, lb), lb)
        q = q_ref[sl, :] * (GLA_DK ** -0.5)
        k = k_ref[sl, :]
        v = v_ref[sl, :]
        a = _dot_f32(a_ref[sl, :], w2_ref[...]) + ba_ref[...]
        g = jax.nn.log_sigmoid(a) * (1.0 / GLA_GATE_NORM)
        if valid is not None:
            g = jnp.where(valid, g, 0.0)
            k = jnp.where(valid, k, 0.0)
        gs = _dot_f32(tril, g)
        gend = gs[lb - 1:lb, :]
        qd = q * jnp.exp(gs)
        kd = k * jnp.exp(gend - gs)
        ds = jnp.exp(_dot_f32(g, ones, (((0,), (0,)), ((), ()))))

        pieces = []
        for s in range(lb):
            dd = jnp.minimum(gs - gs[s:s + 1, :], 0.0)
            w = q * k[s:s + 1, :] * jnp.exp(dd)
            pieces.append(jnp.where(t_iota >= s, w, 0.0))
        scores = jnp.dot(jnp.concatenate(pieces, axis=0).astype(BF16), e_ref[...],
                         preferred_element_type=F32)
        o = jnp.zeros((lb, MIX), F32)
        for s in range(lb):
            o = o + scores[s * lb:(s + 1) * lb, :] * v[s:s + 1, :]

        outs = []
        for h in range(GLA_H):
            ks = slice(h * GLA_DK, (h + 1) * GLA_DK)
            vs = slice(h * GLA_DV, (h + 1) * GLA_DV)
            st = s_scr[h]
            oh = o[:, vs] + _dot(qd[:, ks], st)
            s_scr[h] = ds[ks, :] * st + _dot_tn(kd[:, ks], v[:, vs])
            outs.append(_rms(oh) * wn_ref[...])
        y_ref[sl, :] = _silu(r_ref[sl, :]) * jnp.concatenate(outs, axis=1)
        return carry

    lax.fori_loop(0, lc // lb, block, 0)

    @pl.when(c == pl.num_programs(1) - 1)
    def _():
        so_ref[0] = s_scr[...]


def _gla(proj, row0, nseq, nchunk, lc, lb, tv, w2p, ba, wn, emat, s0, per_seq_state, name):
    rb = row0 // lc

    def col(seg, w):
        cb = COL[seg] // w
        return pl.BlockSpec((lc, w), lambda b, c: (rb + b * nchunk + c, cb))

    const2 = lambda b, c: (0, 0)
    st_map = (lambda b, c: (b, 0, 0, 0)) if per_seq_state else (lambda b, c: (0, 0, 0, 0))
    y, so = pl.pallas_call(
        functools.partial(_gla_kernel, lc=lc, lb=lb, tv=tv),
        grid=(nseq, nchunk),
        in_specs=[col("gla_q", 256), col("gla_k", 256), col("gla_v", 512), col("gla_r", 512),
                  col("gla_a", 128),
                  pl.BlockSpec((LANES, 256), const2), pl.BlockSpec((1, 256), const2),
                  pl.BlockSpec((1, GLA_DV), const2), pl.BlockSpec((256, MIX), const2),
                  pl.BlockSpec((1, GLA_H, GLA_DK, GLA_DV), st_map)],
        out_specs=[pl.BlockSpec((lc, MIX), lambda b, c: (b * nchunk + c, 0)),
                   pl.BlockSpec((1, GLA_H, GLA_DK, GLA_DV), lambda b, c: (b, 0, 0, 0))],
        out_shape=[jax.ShapeDtypeStruct((nseq * nchunk * lc, MIX), F32),
                   jax.ShapeDtypeStruct((nseq, GLA_H, GLA_DK, GLA_DV), F32)],
        scratch_shapes=[pltpu.VMEM((GLA_H, GLA_DK, GLA_DV), F32)],
        compiler_params=pltpu.CompilerParams(
            dimension_semantics=("parallel", "arbitrary"), vmem_limit_bytes=VMEM_LIMIT),
        name=name,
    )(proj, proj, proj, proj, proj, w2p, ba, wn, emat, s0)
    return y, so


def _ssd_kernel(z_ref, x_ref, dt_ref, cw_ref, cb_ref, dtb_ref, alog_ref, dsk_ref, wn_ref,
                c0_ref, s0_ref, y_ref, so_ref, co_ref, s_scr, ext_scr, *, lc, tv):
    c = pl.program_id(1)

    @pl.when(c == 0)
    def _():
        s_scr[...] = s0_ref[0]
        ext_scr[0:SUBLANES, :] = c0_ref[0]

    ext_scr[SUBLANES:SUBLANES + lc, :] = x_ref[...]
    conv = cb_ref[...]
    for i in range(SSM_CONV):
        conv = conv + cw_ref[i:i + 1, :] * ext_scr[pl.ds(SUBLANES - (SSM_CONV - 1) + i, lc), :]
    act = _silu(conv)
    xs = act[:, :MIX]

    t_iota = lax.broadcasted_iota(jnp.int32, (lc, 1), 0)
    dt = jax.nn.softplus(dt_ref[...] + dtb_ref[...])
    gdt = dt * (-jnp.exp(alog_ref[...]))
    if tv < lc:
        gdt = jnp.where(t_iota < tv, gdt, 0.0)
    tril = _tril(lc)
    gcum = _dot_f32(tril.astype(F32), gdt)
    gend = gcum[lc - 1:lc, :]
    lane = lax.broadcasted_iota(jnp.int32, (lc, LANES), 1)

    cbs = []
    for gi in range(SSM_G):
        bm = act[:, MIX + gi * SSM_N:MIX + (gi + 1) * SSM_N]
        cm = act[:, MIX + SSM_G * SSM_N + gi * SSM_N:MIX + SSM_G * SSM_N + (gi + 1) * SSM_N]
        cbs.append((bm, cm, _dot_nt(cm, bm)))

    outs = []
    for h in range(SSM_H):
        bm, cm, cb = cbs[h // (SSM_H // SSM_G)]
        gc = gcum[:, h:h + 1]
        grow = _dot_f32(jnp.where(lane == h, 1.0, 0.0), gcum, (((1,), (1,)), ((), ())))
        dec = jnp.where(tril, jnp.exp(jnp.minimum(gc - grow, 0.0)), 0.0)
        xh = xs[:, h * SSM_P:(h + 1) * SSM_P]
        vh = xh * dt[:, h:h + 1]
        if tv < lc:
            vh = jnp.where(t_iota < tv, vh, 0.0)
        st = s_scr[h]
        oh = _dot(cb * dec, vh) + _dot(cm * jnp.exp(gc), st)
        ge = gend[:, h:h + 1]
        s_scr[h] = jnp.exp(ge) * st + _dot_tn(bm * jnp.exp(ge - gc), vh)
        outs.append(oh + dsk_ref[:, h * SSM_P:(h + 1) * SSM_P] * xh)
    y = jnp.concatenate(outs, axis=1) * _silu(z_ref[...])
    half = MIX // SSM_G
    y = jnp.concatenate([_rms(y[:, gi * half:(gi + 1) * half]) for gi in range(SSM_G)], axis=1)
    y_ref[...] = y * wn_ref[...]

    if lc >= SUBLANES:
        tail = ext_scr[lc:lc + SUBLANES, :]
        ext_scr[0:SUBLANES, :] = tail

    @pl.when(c == pl.num_programs(1) - 1)
    def _():
        so_ref[0] = s_scr[...]
        if lc == tv:
            co_ref[0] = ext_scr[SUBLANES - (SSM_CONV - 1):SUBLANES, :]
        else:
            co_ref[0] = ext_scr[SUBLANES - (SSM_CONV - 1) + tv:SUBLANES + tv, :]


def _ssd(proj, row0, nseq, nchunk, lc, tv, cw, cb, dtb, alog, dsk, wn, c0, s0, per_seq_state, name):
    rb = row0 // lc

    def col(seg, w):
        cbi = COL[seg] // w
        return pl.BlockSpec((lc, w), lambda b, c: (rb + b * nchunk + c, cbi))

    const2 = lambda b, c: (0, 0)
    st_map = (lambda b, c: (b, 0, 0, 0)) if per_seq_state else (lambda b, c: (0, 0, 0, 0))
    c_map = (lambda b, c: (b, 0, 0)) if per_seq_state else (lambda b, c: (0, 0, 0))
    y, so, co = pl.pallas_call(
        functools.partial(_ssd_kernel, lc=lc, tv=tv),
        grid=(nseq, nchunk),
        in_specs=[col("ssm_z", 512), col("ssm_xbc", 768), col("ssm_dt", 128),
                  pl.BlockSpec((SSM_CONV, SSM_CONV_DIM), const2),
                  pl.BlockSpec((1, SSM_CONV_DIM), const2),
                  pl.BlockSpec((1, LANES), const2), pl.BlockSpec((1, LANES), const2),
                  pl.BlockSpec((1, MIX), const2), pl.BlockSpec((1, MIX), const2),
                  pl.BlockSpec((1, SUBLANES, SSM_CONV_DIM), c_map),
                  pl.BlockSpec((1, SSM_H, SSM_N, SSM_P), st_map)],
        out_specs=[pl.BlockSpec((lc, MIX), lambda b, c: (b * nchunk + c, 0)),
                   pl.BlockSpec((1, SSM_H, SSM_N, SSM_P), lambda b, c: (b, 0, 0, 0)),
                   pl.BlockSpec((1, SSM_CONV - 1, SSM_CONV_DIM), lambda b, c: (b, 0, 0))],
        out_shape=[jax.ShapeDtypeStruct((nseq * nchunk * lc, MIX), F32),
                   jax.ShapeDtypeStruct((nseq, SSM_H, SSM_N, SSM_P), F32),
                   jax.ShapeDtypeStruct((nseq, SSM_CONV - 1, SSM_CONV_DIM), F32)],
        scratch_shapes=[pltpu.VMEM((SSM_H, SSM_N, SSM_P), F32),
                        pltpu.VMEM((SUBLANES + lc, SSM_CONV_DIM), F32)],
        compiler_params=pltpu.CompilerParams(
            dimension_semantics=("parallel", "arbitrary"), vmem_limit_bytes=VMEM_LIMIT),
        name=name,
    )(proj, proj, proj, cw, cb, dtb, alog, dsk, wn, c0, s0)
    return y, so, co


def _ret_kernel(q_ref, k_ref, v_ref, g_ref, cos_ref, sin_ref, lg_ref, s0_ref,
                y_ref, so_ref, s_scr, *, lc, tv):
    c = pl.program_id(1)

    @pl.when(c == 0)
    def _():
        s_scr[...] = s0_ref[0]

    width = RET_H * RET_DK
    lane = lax.broadcasted_iota(jnp.int32, (lc, width), 1)
    first_half = (lane % RET_DK) < (RET_DK // 2)
    cos = cos_ref[...]
    sin = jnp.where(first_half, -sin_ref[...], sin_ref[...])

    def rope(x):
        partner = jnp.where(first_half, pltpu.roll(x, width - RET_DK // 2, axis=1),
                            pltpu.roll(x, RET_DK // 2, axis=1))
        return x * cos + partner * sin

    q = rope(q_ref[...])
    k = rope(k_ref[...]) * (RET_DK ** -0.5)
    v = v_ref[...]
    t_col = lax.broadcasted_iota(jnp.int32, (lc, 1), 0)
    s_row = lax.broadcasted_iota(jnp.int32, (1, lc), 1)
    if tv < lc:
        k = jnp.where(t_col < tv, k, 0.0)
    n_col = jnp.minimum(t_col + 1, tv).astype(F32)
    n_row = jnp.minimum(s_row + 1, tv).astype(F32)
    tril = _tril(lc)

    outs = []
    for h in range(RET_H):
        lg = lg_ref[:, h:h + 1]
        gc = n_col * lg
        dec = jnp.where(tril, jnp.exp(jnp.minimum(gc - n_row * lg, 0.0)), 0.0)
        ks = slice(h * RET_DK, (h + 1) * RET_DK)
        vs = slice(h * RET_DV, (h + 1) * RET_DV)
        st = s_scr[h]
        oh = _dot(_dot_nt(q[:, ks], k[:, ks]) * dec, v[:, vs]) + _dot(q[:, ks] * jnp.exp(gc), st)
        ge = float(tv) * lg
        s_scr[h] = jnp.exp(ge) * st + _dot_tn(k[:, ks] * jnp.exp(ge - gc), v[:, vs])
        outs.append(_rms(oh))
    y_ref[...] = _silu(g_ref[...]) * jnp.concatenate(outs, axis=1)

    @pl.when(c == pl.num_programs(1) - 1)
    def _():
        so_ref[0] = s_scr[...]


def _ret(proj, row0, nseq, nchunk, lc, tv, cos, sin, lg, s0, per_seq_state, name):
    rb = row0 // lc

    def col(seg, w):
        cbi = COL[seg] // w
        return pl.BlockSpec((lc, w), lambda b, c: (rb + b * nchunk + c, cbi))

    st_map = (lambda b, c: (b, 0, 0, 0)) if per_seq_state else (lambda b, c: (0, 0, 0, 0))
    y, so = pl.pallas_call(
        functools.partial(_ret_kernel, lc=lc, tv=tv),
        grid=(nseq, nchunk),
        in_specs=[col("ret_q", 256), col("ret_k", 256), col("ret_v", 512), col("ret_g", 512),
                  pl.BlockSpec((lc, RET_H * RET_DK), lambda b, c: (c, 0)),
                  pl.BlockSpec((lc, RET_H * RET_DK), lambda b, c: (c, 0)),
                  pl.BlockSpec((1, LANES), lambda b, c: (0, 0)),
                  pl.BlockSpec((1, RET_H, RET_DK, RET_DV), st_map)],
        out_specs=[pl.BlockSpec((lc, MIX), lambda b, c: (b * nchunk + c, 0)),
                   pl.BlockSpec((1, RET_H, RET_DK, RET_DV), lambda b, c: (b, 0, 0, 0))],
        out_shape=[jax.ShapeDtypeStruct((nseq * nchunk * lc, MIX), F32),
                   jax.ShapeDtypeStruct((nseq, RET_H, RET_DK, RET_DV), F32)],
        scratch_shapes=[pltpu.VMEM((RET_H, RET_DK, RET_DV), F32)],
        compiler_params=pltpu.CompilerParams(
            dimension_semantics=("parallel", "arbitrary"), vmem_limit_bytes=VMEM_LIMIT),
        name=name,
    )(proj, proj, proj, proj, cos, sin, lg, s0)
    return y, so


def _dense2_kernel(x_ref, gate_ref, yg_ref, ys_ref, yr_ref, wg_ref, ws_ref, wr_ref, wo_ref,
                   l1w_ref, l1b_ref, w1_ref, b1_ref, w2_ref, b2_ref, l2w_ref, l2b_ref, o_ref,
                   *, ff_chunk):
    merged = (jax.nn.sigmoid(gate_ref[:, 0:D_MODEL]) * _dot(yg_ref[...], wg_ref[...])
              + jax.nn.sigmoid(gate_ref[:, D_MODEL:2 * D_MODEL]) * _dot(ys_ref[...], ws_ref[...])
              + jax.nn.sigmoid(gate_ref[:, 2 * D_MODEL:3 * D_MODEL]) * _dot(yr_ref[...], wr_ref[...]))
    mix = _dot(merged, wo_ref[...])
    h = _layer_norm(ALPHA * x_ref[...] + mix, l1w_ref[...], l1b_ref[...])
    hb = h.astype(BF16)
    ff = jnp.zeros_like(h) + b2_ref[...]
    for c0 in range(0, D_FF, ff_chunk):
        hid = jnp.dot(hb, w1_ref[:, c0:c0 + ff_chunk], preferred_element_type=F32)
        hid = jnp.square(jnp.maximum(hid + b1_ref[:, c0:c0 + ff_chunk], 0.0))
        ff = ff + _dot(hid, w2_ref[c0:c0 + ff_chunk, :])
    o_ref[...] = _layer_norm(ALPHA * h + ff, l2w_ref[...], l2b_ref[...])


def _dense2(x, proj, yg, ys, yr, wl, *, tm, name):
    m = x.shape[0]
    row = lambda w: pl.BlockSpec((tm, w), lambda i: (i, 0))
    const = lambda r, w: pl.BlockSpec((r, w), lambda i: (0, 0), pipeline_mode=pl.Buffered(1))
    return pl.pallas_call(
        functools.partial(_dense2_kernel, ff_chunk=1024),
        grid=(m // tm,),
        in_specs=[row(D_MODEL), row(3 * D_MODEL), row(MIX), row(MIX), row(MIX),
                  const(MIX, D_MODEL), const(MIX, D_MODEL), const(MIX, D_MODEL),
                  const(D_MODEL, D_MODEL), const(1, D_MODEL), const(1, D_MODEL),
                  const(D_MODEL, D_FF), const(1, D_FF), const(D_FF, D_MODEL), const(1, D_MODEL),
                  const(1, D_MODEL), const(1, D_MODEL)],
        out_specs=row(D_MODEL),
        out_shape=jax.ShapeDtypeStruct((m, D_MODEL), F32),
        compiler_params=pltpu.CompilerParams(
            dimension_semantics=("parallel",), vmem_limit_bytes=VMEM_LIMIT),
        name=name,
    )(x, proj, yg, ys, yr, wl["w_gla_out"], wl["w_ssm_out"], wl["w_ret_out"], wl["w_o"],
      wl["ln1_w"], wl["ln1_b"], wl["w_ff1"], wl["b_ff1"], wl["w_ff2"], wl["b_ff2"],
      wl["ln2_w"], wl["ln2_b"])


def _pick_tile(n, pref):
    t = min(n, pref)
    while n % t or t % SUBLANES:
        t -= 1
    return t


def _rearrange_w_in(w):
    offs = [0]
    for s in SPLIT_SIZES:
        offs.append(offs[-1] + s)
    names = ("gla_q", "gla_k", "gla_v", "gla_r", "gla_a", "ssm_z", "ssm_xbc", "ssm_dt",
             "ret_q", "ret_k", "ret_v", "ret_g", "gates")
    seg = {n: w[:, offs[i]:offs[i + 1]] for i, n in enumerate(names)}
    pad = lambda a: jnp.pad(a, ((0, 0), (0, LANES - a.shape[1])))
    order = sorted(COL, key=COL.get)
    parts = [pad(seg[n]) if n in ("gla_a", "ssm_dt") else seg[n] for n in order]
    return jnp.concatenate(parts, axis=1).astype(BF16)


def _rope_tables(pos):
    half = RET_DK // 2
    inv_freq = ROPE_BASE ** (-jnp.arange(half, dtype=F32) / half)
    ang = pos.astype(F32)[:, None] * inv_freq[None, :]
    cos = jnp.tile(jnp.cos(ang), (1, 2 * RET_H))
    sin = jnp.tile(jnp.sin(ang), (1, 2 * RET_H))
    return cos, sin


def kernel(x_prompt, x_sample, state_gla, state_ssm, state_conv, state_ret, meta_tokens,
           ln_in_w, ln_in_b, w_in, w_gla_a2, b_gla_a, w_gla_norm, conv_w, conv_b, dt_bias,
           a_log, d_skip, w_ssm_norm, w_gla_out, w_ssm_out, w_ret_out, w_o, ln1_w, ln1_b,
           w_ff1, b_ff1, w_ff2, b_ff2, ln2_w, ln2_b):
    bp, tp, d = x_prompt.shape
    bs, ts, _ = x_sample.shape
    assert d == D_MODEL and tp % LANES == 0 and ts <= SAMPLE_ROWS
    lc_body = LANES
    nchunk = tp // lc_body

    x_body = x_prompt.reshape(bp * tp, d)
    n_sample = bs * SAMPLE_ROWS
    meta_row0 = n_sample
    n_small = n_sample + LANES
    x_small = jnp.concatenate(
        [jnp.pad(x_sample, ((0, 0), (0, SAMPLE_ROWS - ts), (0, 0))).reshape(n_sample, d),
         meta_tokens.astype(F32), jnp.zeros((LANES - N_META, d), F32)], axis=0)

    cos_b, sin_b = _rope_tables(N_META + jnp.arange(tp, dtype=jnp.int32))
    cos_m, sin_m = _rope_tables(jnp.arange(N_META, dtype=jnp.int32))
    cos_s, sin_s = _rope_tables(PAST_LEN + jnp.arange(SAMPLE_ROWS, dtype=jnp.int32))
    lg_ret = jnp.pad(jnp.log1p(-jnp.exp2(-5.0 - jnp.arange(RET_H, dtype=F32))),
                     (0, LANES - RET_H)).reshape(1, LANES)
    ki = lax.broadcasted_iota(jnp.int32, (GLA_H * GLA_DK, MIX), 0) // GLA_DK
    vi = lax.broadcasted_iota(jnp.int32, (GLA_H * GLA_DK, MIX), 1) // GLA_DV
    emat = (ki == vi).astype(BF16)

    zero_gla = jnp.zeros((1, GLA_H, GLA_DK, GLA_DV), F32)
    zero_ssm = jnp.zeros((1, SSM_H, SSM_N, SSM_P), F32)
    zero_conv = jnp.zeros((1, SUBLANES, SSM_CONV_DIM), F32)
    zero_ret = jnp.zeros((1, RET_H, RET_DK, RET_DV), F32)
    pad_conv = lambda cst: jnp.pad(cst, ((0, 0), (SUBLANES - (SSM_CONV - 1), 0), (0, 0)))
    row2 = lambda a: a.reshape(1, -1)
    lane_pad = lambda a: jnp.pad(a, (0, LANES - a.shape[0])).reshape(1, LANES)

    tm_in = _pick_tile(bp * tp, 1024)
    tm_d2 = _pick_tile(bp * tp, 256)
    tm_d2s = _pick_tile(n_small, 384)
    tn = 1536

    outs = {k: [] for k in ("gla_p", "gla_s", "ssm_p", "ssm_s", "conv_p", "conv_s", "ret_p", "ret_s")}
    xb, xs = x_body, x_small
    for l in range(DEPTH):
        w_in_l = _rearrange_w_in(w_in[l])
        wl = dict(w_gla_out=w_gla_out[l].astype(BF16), w_ssm_out=w_ssm_out[l].astype(BF16),
                  w_ret_out=w_ret_out[l].astype(BF16), w_o=w_o[l].astype(BF16),
                  ln1_w=row2(ln1_w[l]), ln1_b=row2(ln1_b[l]),
                  w_ff1=w_ff1[l].astype(BF16), b_ff1=row2(b_ff1[l]),
                  w_ff2=w_ff2[l].astype(BF16), b_ff2=row2(b_ff2[l]),
                  ln2_w=row2(ln2_w[l]), ln2_b=row2(ln2_b[l]))
        w2p = jnp.pad(w_gla_a2[l], ((0, LANES - GLA_RANK), (0, 0)))
        gla_w = (w2p, row2(b_gla_a[l]), row2(w_gla_norm[l]), emat)
        ssd_w = (conv_w[l], row2(conv_b[l]), lane_pad(dt_bias[l]), lane_pad(a_log[l]),
                 row2(jnp.repeat(d_skip[l], SSM_P)), row2(w_ssm_norm[l]))
        ln_w, ln_b = row2(ln_in_w), row2(ln_in_b)

        proj_s, xs = _in_proj(xs, ln_w, ln_b, w_in_l, apply_ln=(l == 0), tm=n_small, tn=tn,
                              name=f"inproj_small_{l}")
        yg_m, sg_m = _gla(proj_s, meta_row0, 1, 1, N_META, BLK, BLK, *gla_w, zero_gla, False,
                          f"gla_meta_{l}")
        ys_m, ss_m, cv_m = _ssd(proj_s, meta_row0, 1, 1, N_META, N_META, *ssd_w, zero_conv,
                                zero_ssm, False, f"ssd_meta_{l}")
        yr_m, sr_m = _ret(proj_s, meta_row0, 1, 1, N_META, N_META, cos_m, sin_m, lg_ret,
                          zero_ret, False, f"ret_meta_{l}")
        yg_s, sg_s = _gla(proj_s, 0, bs, 1, SAMPLE_ROWS, SAMPLE_ROWS, ts, *gla_w, state_gla[l],
                          True, f"gla_sample_{l}")
        ys_s, ss_s, cv_s = _ssd(proj_s, 0, bs, 1, SAMPLE_ROWS, ts, *ssd_w, pad_conv(state_conv[l]),
                                state_ssm[l], True, f"ssd_sample_{l}")
        yr_s, sr_s = _ret(proj_s, 0, bs, 1, SAMPLE_ROWS, ts, cos_s, sin_s, lg_ret, state_ret[l],
                          True, f"ret_sample_{l}")
        zpad = jnp.zeros((LANES - N_META, MIX), F32)
        yg = jnp.concatenate([yg_s, yg_m, zpad], axis=0)
        ys = jnp.concatenate([ys_s, ys_m, zpad], axis=0)
        yr = jnp.concatenate([yr_s, yr_m, zpad], axis=0)
        xs = _dense2(xs, proj_s, yg, ys, yr, wl, tm=tm_d2s, name=f"dense2_small_{l}")

        proj_b, xb = _in_proj(xb, ln_w, ln_b, w_in_l, apply_ln=(l == 0), tm=tm_in, tn=tn,
                              name=f"inproj_body_{l}")
        yg_b, sg_b = _gla(proj_b, 0, bp, nchunk, lc_body, BLK, BLK, *gla_w, sg_m, False,
                          f"gla_body_{l}")
        ys_b, ss_b, cv_b = _ssd(proj_b, 0, bp, nchunk, lc_body, lc_body, *ssd_w, pad_conv(cv_m),
                                ss_m, False, f"ssd_body_{l}")
        yr_b, sr_b = _ret(proj_b, 0, bp, nchunk, lc_body, lc_body, cos_b, sin_b, lg_ret, sr_m,
                          False, f"ret_body_{l}")
        xb = _dense2(xb, proj_b, yg_b, ys_b, yr_b, wl, tm=tm_d2, name=f"dense2_body_{l}")

        for key, val in (("gla_p", sg_b), ("gla_s", sg_s), ("ssm_p", ss_b), ("ssm_s", ss_s),
                         ("conv_p", cv_b), ("conv_s", cv_s), ("ret_p", sr_b), ("ret_s", sr_s)):
            outs[key].append(val)

    y_prompt = xb.reshape(bp, tp, d)
    y_sample = xs[:n_sample].reshape(bs, SAMPLE_ROWS, d)[:, :ts]
    st = {k: jnp.stack(v) for k, v in outs.items()}
    return (y_prompt, y_sample, st["gla_p"], st["gla_s"], st["ssm_p"], st["ssm_s"],
            st["conv_p"], st["conv_s"], st["ret_p"], st["ret_s"])
```

```python
import functools

import jax
import jax.numpy as jnp
from jax import lax
from jax.experimental import pallas as pl
from jax.experimental.pallas import tpu as pltpu

F32 = jnp.float32
BF16 = jnp.bfloat16

D_MODEL = 1024
DEPTH = 2
N_META = 16
MIX = 512
GLA_H, GLA_DK, GLA_DV, GLA_RANK = 4, 64, 128, 16
GLA_GATE_NORM = 16.0
SSM_H, SSM_P, SSM_N, SSM_G, SSM_CONV = 8, 64, 64, 2, 4
SSM_CONV_DIM = MIX + 2 * SSM_G * SSM_N
SSM_PAIRS = SSM_H // 2
RET_H, RET_DK, RET_DV = 4, 64, 128
ROPE_BASE = 10000.0
D_FF = 4 * D_MODEL
ALPHA = (2 * DEPTH) ** 0.25
PAST_LEN = 16384
SPLIT_SIZES = (256, 256, 512, 512, 16, 512, 768, 8, 256, 256, 512, 512, 3072)

LANES = 128
SUBLANES = 8
VMEM_LIMIT = 56 * 1024 * 1024

COL = dict(gates=0, gla_v=3072, gla_r=3584, ssm_z=4096, ret_v=4608, ret_g=5120,
           gla_q=5632, gla_k=5888, ssm_xbc=6144, ret_q=6912, ret_k=7168, gla_a=7424, ssm_dt=7552)
N_PROJ = 7680
SAMPLE_ROWS = SUBLANES
GLA_BLK = 16
CHUNK = 128

NN = (((1,), (0,)), ((), ()))
NT = (((1,), (1,)), ((), ()))
TN = (((0,), (0,)), ((), ()))


def _dot(a, b, dims=NN):
    return lax.dot_general(a.astype(BF16), b.astype(BF16), dims, preferred_element_type=F32)


def _dot_f32(a, b, dims=NN):
    return lax.dot_general(a, b, dims, precision=lax.Precision.HIGHEST,
                           preferred_element_type=F32)


def _dot_sel(sel, x, dims=NN, sel_first=True):
    hi = x.astype(BF16)
    r1 = x - hi.astype(F32)
    mid = r1.astype(BF16)
    lo = (r1 - mid.astype(F32)).astype(BF16)
    sb = sel.astype(BF16)
    out = None
    for part in (hi, mid, lo):
        ops = (sb, part) if sel_first else (part, sb)
        term = lax.dot_general(*ops, dims, preferred_element_type=F32)
        out = term if out is None else out + term
    return out


def _layer_norm(x, w, b):
    mu = jnp.mean(x, axis=-1, keepdims=True)
    xc = x - mu
    var = jnp.mean(xc * xc, axis=-1, keepdims=True)
    return xc * lax.rsqrt(var + 1e-5) * w + b


def _rms(x):
    return x * lax.rsqrt(jnp.mean(x * x, axis=-1, keepdims=True) + 1e-6)


def _silu(x):
    return x * jax.nn.sigmoid(x)


def _block_masks(rows, lb):
    r = lax.broadcasted_iota(jnp.int32, (rows, rows), 0)
    c = lax.broadcasted_iota(jnp.int32, (rows, rows), 1)
    same = (r // lb) == (c // lb)
    return same, same & (r >= c)


def _row_in_block(rows, lb):
    return lax.broadcasted_iota(jnp.int32, (rows, 1), 0) % lb


def _inproj_kernel(x_ref, lnw_ref, lnb_ref, w_ref, *refs, apply_ln):
    if apply_ln:
        proj_ref, xn_ref, xb_scr = refs
    else:
        proj_ref, xb_scr = refs

    @pl.when(pl.program_id(1) == 0)
    def _():
        x = x_ref[...]
        if apply_ln:
            x = _layer_norm(x, lnw_ref[...], lnb_ref[...])
            xn_ref[...] = x
        xb_scr[...] = x.astype(BF16)

    proj_ref[...] = jnp.dot(xb_scr[...], w_ref[...], preferred_element_type=F32)


def _in_proj(x, lnw, lnb, w, *, apply_ln, tm, tn, name):
    m = x.shape[0]
    grid = (m // tm, N_PROJ // tn)
    out_shape = [jax.ShapeDtypeStruct((m, N_PROJ), F32)]
    out_specs = [pl.BlockSpec((tm, tn), lambda i, j: (i, j))]
    if apply_ln:
        out_shape.append(jax.ShapeDtypeStruct((m, D_MODEL), F32))
        out_specs.append(pl.BlockSpec((tm, D_MODEL), lambda i, j: (i, 0)))
    res = pl.pallas_call(
        functools.partial(_inproj_kernel, apply_ln=apply_ln),
        grid=grid,
        in_specs=[pl.BlockSpec((tm, D_MODEL), lambda i, j: (i, 0)),
                  pl.BlockSpec((1, D_MODEL), lambda i, j: (0, 0)),
                  pl.BlockSpec((1, D_MODEL), lambda i, j: (0, 0)),
                  pl.BlockSpec((D_MODEL, tn), lambda i, j: (0, j))],
        out_specs=out_specs,
        out_shape=out_shape,
        scratch_shapes=[pltpu.VMEM((tm, D_MODEL), BF16)],
        compiler_params=pltpu.CompilerParams(
            dimension_semantics=("parallel", "arbitrary"), vmem_limit_bytes=VMEM_LIMIT),
        name=name,
    )(x, lnw, lnb, w)
    return (res[0], res[1]) if apply_ln else (res[0], x)


def _rec_call(kernel_fn, proj, segs, consts, states, out_widths, *, row0, nseq, nchunk, rows,
              chained, per_seq_state, scratch, name, tables=()):
    rb = row0 // rows
    nb = 1 if chained else rows // SAMPLE_ROWS
    grid = (nseq, nchunk) if chained else (nseq // nb, 1)

    def row_idx(b, c):
        return b * nchunk + c if chained else b

    def col(seg, w):
        cbi = COL[seg] // w
        return pl.BlockSpec((rows, w), lambda b, c: (rb + row_idx(b, c), cbi))

    def whole(a):
        return pl.BlockSpec(a.shape, lambda b, c: (0,) * a.ndim)

    def state_spec(a, as_input):
        blk = (nb,) + a.shape[1:]
        zeros = (0,) * (a.ndim - 1)
        if as_input and not per_seq_state:
            return pl.BlockSpec(blk, lambda b, c: (0,) + zeros)
        return pl.BlockSpec(blk, lambda b, c: (b,) + zeros)

    in_specs = ([col(s, w) for s, w in segs] + [whole(a) for a in consts]
                + [pl.BlockSpec((rows, t.shape[1]), lambda b, c: (c, 0)) for t in tables]
                + [state_spec(a, True) for a in states])
    n_rows = nseq * nchunk * rows if chained else nseq * SAMPLE_ROWS
    out_specs = ([pl.BlockSpec((rows, w), lambda b, c: (row_idx(b, c), 0)) for w in out_widths]
                 + [state_spec(a, False) for a in states])
    out_shape = ([jax.ShapeDtypeStruct((n_rows, w), F32) for w in out_widths]
                 + [jax.ShapeDtypeStruct((nseq,) + a.shape[1:], F32) for a in states])
    return pl.pallas_call(
        kernel_fn, grid=grid, in_specs=in_specs, out_specs=out_specs, out_shape=out_shape,
        scratch_shapes=scratch,
        compiler_params=pltpu.CompilerParams(
            dimension_semantics=("parallel", "arbitrary"), vmem_limit_bytes=VMEM_LIMIT),
        name=name,
    )(*([proj] * len(segs)), *consts, *tables, *states)


def _gla_kernel(q_ref, k_ref, v_ref, r_ref, a_ref, w2_ref, ba_ref, wn_ref, e_ref, s0_ref,
                y_ref, so_ref, *scr, rows, lb, tv, chained):
    nblk = rows // lb
    if chained:
        s_scr, = scr

        @pl.when(pl.program_id(1) == 0)
        def _():
            s_scr[...] = s0_ref[0]

    q = q_ref[...] * (GLA_DK ** -0.5)
    k = k_ref[...]
    v = v_ref[...]
    a = _dot_f32(a_ref[...], w2_ref[...]) + ba_ref[...]
    g = jax.nn.log_sigmoid(a) * (1.0 / GLA_GATE_NORM)
    if tv < lb:
        valid = _row_in_block(rows, lb) >= lb - tv
        g = jnp.where(valid, g, 0.0)
        k = jnp.where(valid, k, 0.0)
    same, causal = _block_masks(rows, lb)
    gs = _dot_sel(causal, g)
    gtot = _dot_sel(same, g)
    qd = q * jnp.exp(gs)
    kd = k * jnp.exp(gtot - gs)
    sel = (lax.broadcasted_iota(jnp.int32, (rows, nblk * LANES), 0) // lb
           == lax.broadcasted_iota(jnp.int32, (rows, nblk * LANES), 1) // LANES)
    ds = jnp.exp(_dot_sel(sel, g, TN, sel_first=False))

    width = GLA_H * GLA_DK
    q3 = q.reshape(nblk, lb, width)
    k3 = k.reshape(nblk, lb, width)
    g3 = gs.reshape(nblk, lb, width)
    v3 = v.reshape(nblk, lb, MIX)
    t3 = lax.broadcasted_iota(jnp.int32, (1, lb, 1), 1)
    sources = range(lb - tv, lb)
    pieces = []
    for s in sources:
        dd = jnp.minimum(g3 - g3[:, s:s + 1, :], 0.0)
        w = q3 * k3[:, s:s + 1, :] * jnp.exp(dd)
        pieces.append(jnp.where(t3 >= s, w, 0.0).reshape(rows, width))
    scores = jnp.dot(jnp.concatenate(pieces, axis=0).astype(BF16), e_ref[...],
                     preferred_element_type=F32)
    o3 = jnp.zeros((nblk, lb, MIX), F32)
    for i, s in enumerate(sources):
        o3 = o3 + scores[i * rows:(i + 1) * rows, :].reshape(nblk, lb, MIX) * v3[:, s:s + 1, :]
    o = o3.reshape(rows, MIX)

    st = [s_scr[h] for h in range(GLA_H)] if chained else None
    o_rows = []
    for b in range(nblk):
        rs = slice(b * lb, (b + 1) * lb)
        o_heads = []
        for h in range(GLA_H):
            ks = slice(h * GLA_DK, (h + 1) * GLA_DK)
            vs = slice(h * GLA_DV, (h + 1) * GLA_DV)
            cur = st[h] if chained else s0_ref[b, h]
            o_heads.append(_dot(qd[rs, ks], cur))
            new = ds[ks, b * LANES:(b + 1) * LANES] * cur + _dot(kd[rs, ks], v[rs, vs], TN)
            if chained:
                st[h] = new
            else:
                so_ref[b, h] = new
        o_rows.append(jnp.concatenate(o_heads, axis=1))
    o = o + jnp.concatenate(o_rows, axis=0)
    y = jnp.concatenate(
        [_rms(o[:, h * GLA_DV:(h + 1) * GLA_DV]) * wn_ref[...] for h in range(GLA_H)], axis=1)
    y_ref[...] = _silu(r_ref[...]) * y

    if chained:
        for h in range(GLA_H):
            s_scr[h] = st[h]

        @pl.when(pl.program_id(1) == pl.num_programs(1) - 1)
        def _():
            so_ref[0] = s_scr[...]


def _gla(proj, weights, s0, *, row0, nseq, nchunk, rows, lb, tv, chained, per_seq_state, name):
    kern = functools.partial(_gla_kernel, rows=rows, lb=lb, tv=tv, chained=chained)
    scratch = [pltpu.VMEM((GLA_H, GLA_DK, GLA_DV), F32)] if chained else []
    return _rec_call(kern, proj,
                     [("gla_q", 256), ("gla_k", 256), ("gla_v", 512), ("gla_r", 512), ("gla_a", 128)],
                     weights, [s0], [MIX], row0=row0, nseq=nseq, nchunk=nchunk, rows=rows,
                     chained=chained, per_seq_state=per_seq_state, scratch=scratch, name=name)


def _ssd_kernel(z_ref, x_ref, dt_ref, cw_ref, cb_ref, dtb_ref, alog_ref, dsk_ref, wn_ref,
                c0_ref, s0_ref, y_ref, co_ref, so_ref, *scr, rows, lb, tv, chained):
    nblk = rows // lb
    t_in = _row_in_block(rows, lb)
    valid = (t_in >= lb - tv) if tv < lb else None
    if chained:
        s_scr, ext_scr = scr

        @pl.when(pl.program_id(1) == 0)
        def _():
            s_scr[...] = s0_ref[0]
            ext_scr[0:SUBLANES, :] = c0_ref[0]

        xin = x_ref[...]
    else:
        ext_scr, = scr
        ext_scr[0:SUBLANES, :] = jnp.zeros((SUBLANES, SSM_CONV_DIM), F32)
        xin = jnp.where(valid, x_ref[...], c0_ref[...].reshape(rows, SSM_CONV_DIM))

    ext_scr[SUBLANES:SUBLANES + rows, :] = xin
    conv = cb_ref[...]
    for i in range(SSM_CONV):
        conv = conv + cw_ref[i:i + 1, :] * ext_scr[pl.ds(SUBLANES - (SSM_CONV - 1) + i, rows), :]
    act = _silu(conv)
    if chained:
        co_ref[0] = ext_scr[rows:rows + SUBLANES, :]
        ext_scr[0:SUBLANES, :] = ext_scr[rows:rows + SUBLANES, :]
    else:
        co_ref[...] = xin.reshape(nblk, lb, SSM_CONV_DIM)

    dt = jax.nn.softplus(dt_ref[...] + dtb_ref[...])
    gdt = dt * (-jnp.exp(alog_ref[...]))
    if valid is not None:
        gdt = jnp.where(valid, gdt, 0.0)
    same, causal = _block_masks(rows, lb)
    gcum = _dot_sel(causal, gdt)
    if rows < LANES:
        gsq = jnp.concatenate([gcum, jnp.zeros((LANES - rows, LANES), F32)], axis=0)
        gcum_t = gsq.T[:, :rows]
    else:
        gcum_t = gcum.T
    gtot = gcum[rows - 1:rows, :] if nblk == 1 else _dot_sel(same, gdt)

    lane = lax.broadcasted_iota(jnp.int32, (rows, LANES), 1)
    low = lane < SSM_P
    bcol = act[:, MIX:MIX + LANES]
    ccol = act[:, MIX + LANES:MIX + 2 * LANES]
    bswap = pltpu.roll(bcol, SSM_N, axis=1)
    cswap = pltpu.roll(ccol, SSM_N, axis=1)
    b2 = (jnp.where(low, bcol, bswap), jnp.where(low, bswap, bcol))
    c2 = (jnp.where(low, ccol, cswap), jnp.where(low, cswap, ccol))
    cb = (_dot(jnp.where(low, ccol, 0.0), bcol, NT), _dot(jnp.where(low, 0.0, ccol), bcol, NT))

    def pair_lanes(x, p):
        return jnp.where(low[:x.shape[0]], x[:, 2 * p:2 * p + 1], x[:, 2 * p + 1:2 * p + 2])

    def block_diag(scat):
        low64 = lax.broadcasted_iota(jnp.int32, (SSM_N, LANES), 1) < SSM_P
        return jnp.concatenate([jnp.where(low64, scat, 0.0), jnp.where(low64, 0.0, scat)], axis=0)

    def diag_blocks(u):
        low64 = lax.broadcasted_iota(jnp.int32, (SSM_N, LANES), 1) < SSM_P
        return jnp.where(low64, u[:SSM_N, :], u[SSM_N:, :])

    y_pairs = []
    for p in range(SSM_PAIRS):
        gi = p // (SSM_PAIRS // SSM_G)
        decs = []
        for h in (2 * p, 2 * p + 1):
            diff = jnp.minimum(gcum[:, h:h + 1] - gcum_t[h:h + 1, :], 0.0)
            decs.append(cb[gi] * jnp.where(causal, jnp.exp(diff), 0.0))
        xp = act[:, p * LANES:(p + 1) * LANES]
        vp = xp * pair_lanes(dt, p)
        if valid is not None:
            vp = jnp.where(valid, vp, 0.0)
        vbd = jnp.concatenate([jnp.where(low, vp, 0.0), jnp.where(low, 0.0, vp)], axis=0)
        o = _dot(jnp.concatenate(decs, axis=1), vbd)
        g2 = pair_lanes(gcum, p)
        ge2 = pair_lanes(gtot, p)
        cin = c2[gi] * jnp.exp(g2)
        bout = b2[gi] * jnp.exp(ge2 - g2)
        if chained:
            cur = s_scr[p]
            o = o + _dot(cin, block_diag(cur))
            s_scr[p] = jnp.exp(ge2) * cur + diag_blocks(_dot(bout, vp, TN))
        else:
            o_rows = []
            for b in range(nblk):
                rs = slice(b * lb, (b + 1) * lb)
                cur = s0_ref[b, p]
                o_rows.append(_dot(cin[rs], block_diag(cur)))
                so_ref[b, p] = (jnp.exp(ge2[b * lb:b * lb + 1, :]) * cur
                                + diag_blocks(_dot(bout[rs], vp[rs], TN)))
            o = o + jnp.concatenate(o_rows, axis=0)
        y_pairs.append(o + dsk_ref[:, p * LANES:(p + 1) * LANES] * xp)
    y = jnp.concatenate(y_pairs, axis=1) * _silu(z_ref[...])
    half = MIX // SSM_G
    y = jnp.concatenate([_rms(y[:, gi * half:(gi + 1) * half]) for gi in range(SSM_G)], axis=1)
    y_ref[...] = y * wn_ref[...]

    if chained:
        @pl.when(pl.program_id(1) == pl.num_programs(1) - 1)
        def _():
            so_ref[0] = s_scr[...]


def _ssd(proj, weights, c0, s0, *, row0, nseq, nchunk, rows, lb, tv, chained, per_seq_state, name):
    kern = functools.partial(_ssd_kernel, rows=rows, lb=lb, tv=tv, chained=chained)
    scratch = [pltpu.VMEM((SUBLANES + rows, SSM_CONV_DIM), F32)]
    if chained:
        scratch = [pltpu.VMEM((SSM_PAIRS, SSM_N, LANES), F32)] + scratch
    return _rec_call(kern, proj, [("ssm_z", 512), ("ssm_xbc", 768), ("ssm_dt", 128)],
                     weights, [c0, s0], [MIX], row0=row0, nseq=nseq, nchunk=nchunk, rows=rows,
                     chained=chained, per_seq_state=per_seq_state, scratch=scratch, name=name)


def _ret_kernel(q_ref, k_ref, v_ref, g_ref, lg_ref, cos_ref, sin_ref, s0_ref,
                y_ref, so_ref, *scr, rows, lb, tv, chained):
    nblk = rows // lb
    if chained:
        s_scr, = scr

        @pl.when(pl.program_id(1) == 0)
        def _():
            s_scr[...] = s0_ref[0]

    width = RET_H * RET_DK
    lane = lax.broadcasted_iota(jnp.int32, (rows, width), 1)
    first_half = (lane % RET_DK) < (RET_DK // 2)
    cos = cos_ref[...]
    sin = jnp.where(first_half, -sin_ref[...], sin_ref[...])

    def rope(x):
        partner = jnp.where(first_half, pltpu.roll(x, width - RET_DK // 2, axis=1),
                            pltpu.roll(x, RET_DK // 2, axis=1))
        return x * cos + partner * sin

    q = rope(q_ref[...])
    k = rope(k_ref[...]) * (RET_DK ** -0.5)
    v = v_ref[...]
    t_col = _row_in_block(rows, lb)
    s_row = lax.broadcasted_iota(jnp.int32, (1, rows), 1) % lb
    if tv < lb:
        k = jnp.where(t_col >= lb - tv, k, 0.0)
    n_col = jnp.maximum(t_col - (lb - tv) + 1, 0).astype(F32)
    n_row = jnp.maximum(s_row - (lb - tv) + 1, 0).astype(F32)
    _, causal = _block_masks(rows, lb)

    outs = []
    for h in range(RET_H):
        lg = lg_ref[:, h:h + 1]
        gc = n_col * lg
        dec = jnp.where(causal, jnp.exp(jnp.minimum(gc - n_row * lg, 0.0)), 0.0)
        ks = slice(h * RET_DK, (h + 1) * RET_DK)
        vs = slice(h * RET_DV, (h + 1) * RET_DV)
        oh = _dot(_dot(q[:, ks], k[:, ks], NT) * dec, v[:, vs])
        ge = float(tv) * lg
        qin = q[:, ks] * jnp.exp(gc)
        kout = k[:, ks] * jnp.exp(ge - gc)
        if chained:
            cur = s_scr[h]
            oh = oh + _dot(qin, cur)
            s_scr[h] = jnp.exp(ge) * cur + _dot(kout, v[:, vs], TN)
        else:
            o_rows = []
            for b in range(nblk):
                rs = slice(b * lb, (b + 1) * lb)
                cur = s0_ref[b, h]
                o_rows.append(_dot(qin[rs], cur))
                so_ref[b, h] = jnp.exp(ge) * cur + _dot(kout[rs], v[rs, vs], TN)
            oh = oh + jnp.concatenate(o_rows, axis=0)
        outs.append(_rms(oh))
    y_ref[...] = _silu(g_ref[...]) * jnp.concatenate(outs, axis=1)

    if chained:
        @pl.when(pl.program_id(1) == pl.num_programs(1) - 1)
        def _():
            so_ref[0] = s_scr[...]


def _ret(proj, lg, cos, sin, s0, *, row0, nseq, nchunk, rows, lb, tv, chained, per_seq_state, name):
    kern = functools.partial(_ret_kernel, rows=rows, lb=lb, tv=tv, chained=chained)
    scratch = [pltpu.VMEM((RET_H, RET_DK, RET_DV), F32)] if chained else []
    return _rec_call(kern, proj, [("ret_q", 256), ("ret_k", 256), ("ret_v", 512), ("ret_g", 512)],
                     [lg], [s0], [MIX], row0=row0, nseq=nseq, nchunk=nchunk, rows=rows,
                     chained=chained, per_seq_state=per_seq_state, scratch=scratch, name=name,
                     tables=(cos, sin))


def _dense2_kernel(x_ref, gate_ref, yg_ref, ys_ref, yr_ref, wg_ref, ws_ref, wr_ref, wo_ref,
                   l1w_ref, l1b_ref, w1_ref, b1_ref, w2_ref, b2_ref, l2w_ref, l2b_ref, o_ref,
                   *, ff_chunk):
    merged = (jax.nn.sigmoid(gate_ref[:, 0:D_MODEL]) * _dot(yg_ref[...], wg_ref[...])
              + jax.nn.sigmoid(gate_ref[:, D_MODEL:2 * D_MODEL]) * _dot(ys_ref[...], ws_ref[...])
              + jax.nn.sigmoid(gate_ref[:, 2 * D_MODEL:3 * D_MODEL]) * _dot(yr_ref[...], wr_ref[...]))
    mix = _dot(merged, wo_ref[...])
    h = _layer_norm(ALPHA * x_ref[...] + mix, l1w_ref[...], l1b_ref[...])
    hb = h.astype(BF16)
    ff = jnp.zeros_like(h) + b2_ref[...]
    for c0 in range(0, D_FF, ff_chunk):
        hid = jnp.dot(hb, w1_ref[:, c0:c0 + ff_chunk], preferred_element_type=F32)
        hid = jnp.square(jnp.maximum(hid + b1_ref[:, c0:c0 + ff_chunk], 0.0))
        ff = ff + _dot(hid, w2_ref[c0:c0 + ff_chunk, :])
    o_ref[...] = _layer_norm(ALPHA * h + ff, l2w_ref[...], l2b_ref[...])


def _dense2(x, proj, yg, ys, yr, wl, *, tm, name):
    m = x.shape[0]
    row = lambda w: pl.BlockSpec((tm, w), lambda i: (i, 0))
    const = lambda r, w: pl.BlockSpec((r, w), lambda i: (0, 0), pipeline_mode=pl.Buffered(1))
    return pl.pallas_call(
        functools.partial(_dense2_kernel, ff_chunk=1024),
        grid=(m // tm,),
        in_specs=[row(D_MODEL), row(3 * D_MODEL), row(MIX), row(MIX), row(MIX),
                  const(MIX, D_MODEL), const(MIX, D_MODEL), const(MIX, D_MODEL),
                  const(D_MODEL, D_MODEL), const(1, D_MODEL), const(1, D_MODEL),
                  const(D_MODEL, D_FF), const(1, D_FF), const(D_FF, D_MODEL), const(1, D_MODEL),
                  const(1, D_MODEL), const(1, D_MODEL)],
        out_specs=row(D_MODEL),
        out_shape=jax.ShapeDtypeStruct((m, D_MODEL), F32),
        compiler_params=pltpu.CompilerParams(
            dimension_semantics=("parallel",), vmem_limit_bytes=VMEM_LIMIT),
        name=name,
    )(x, proj, yg, ys, yr, wl["w_gla_out"], wl["w_ssm_out"], wl["w_ret_out"], wl["w_o"],
      wl["ln1_w"], wl["ln1_b"], wl["w_ff1"], wl["b_ff1"], wl["w_ff2"], wl["b_ff2"],
      wl["ln2_w"], wl["ln2_b"])


def _pick_tile(n, pref):
    t = min(n, pref)
    while n % t or t % SUBLANES:
        t -= 1
    return t


def _rearrange_w_in(w):
    offs = [0]
    for s in SPLIT_SIZES:
        offs.append(offs[-1] + s)
    names = ("gla_q", "gla_k", "gla_v", "gla_r", "gla_a", "ssm_z", "ssm_xbc", "ssm_dt",
             "ret_q", "ret_k", "ret_v", "ret_g", "gates")
    seg = {n: w[:, offs[i]:offs[i + 1]] for i, n in enumerate(names)}
    pad = lambda a: jnp.pad(a, ((0, 0), (0, LANES - a.shape[1])))
    order = sorted(COL, key=COL.get)
    parts = [pad(seg[n]) if n in ("gla_a", "ssm_dt") else seg[n] for n in order]
    return jnp.concatenate(parts, axis=1).astype(BF16)


def _rope_tables(pos):
    half = RET_DK // 2
    inv_freq = ROPE_BASE ** (-jnp.arange(half, dtype=F32) / half)
    ang = pos.astype(F32)[:, None] * inv_freq[None, :]
    cos = jnp.tile(jnp.cos(ang), (1, 2 * RET_H))
    sin = jnp.tile(jnp.sin(ang), (1, 2 * RET_H))
    return cos, sin


def _ssm_pairs(s):
    b = s.shape[0]
    return (s.reshape(b, SSM_PAIRS, 2, SSM_N, SSM_P).transpose(0, 1, 3, 2, 4)
            .reshape(b, SSM_PAIRS, SSM_N, 2 * SSM_P))


def _ssm_unpairs(s):
    b = s.shape[0]
    return (s.reshape(b, SSM_PAIRS, SSM_N, 2, SSM_P).transpose(0, 1, 3, 2, 4)
            .reshape(b, SSM_H, SSM_N, SSM_P))


def kernel(x_prompt, x_sample, state_gla, state_ssm, state_conv, state_ret, meta_tokens,
           ln_in_w, ln_in_b, w_in, w_gla_a2, b_gla_a, w_gla_norm, conv_w, conv_b, dt_bias,
           a_log, d_skip, w_ssm_norm, w_gla_out, w_ssm_out, w_ret_out, w_o, ln1_w, ln1_b,
           w_ff1, b_ff1, w_ff2, b_ff2, ln2_w, ln2_b):
    bp, tp, d = x_prompt.shape
    bs, ts, _ = x_sample.shape
    assert d == D_MODEL and tp % CHUNK == 0
    assert SSM_CONV - 1 <= SAMPLE_ROWS - ts
    nchunk = tp // CHUNK
    pad_rows = SAMPLE_ROWS - ts

    x_body = x_prompt.reshape(bp * tp, d)
    n_sample = bs * SAMPLE_ROWS
    rows_s = min(CHUNK, n_sample)
    assert n_sample % rows_s == 0 and n_sample % N_META == 0
    n_small = n_sample + CHUNK
    x_small = jnp.concatenate(
        [jnp.pad(x_sample, ((0, 0), (pad_rows, 0), (0, 0))).reshape(n_sample, d),
         meta_tokens.astype(F32), jnp.zeros((CHUNK - N_META, d), F32)], axis=0)

    cos_b, sin_b = _rope_tables(N_META + jnp.arange(tp, dtype=jnp.int32))
    cos_m, sin_m = _rope_tables(jnp.arange(N_META, dtype=jnp.int32))
    pos_tile = PAST_LEN - pad_rows + jnp.arange(SAMPLE_ROWS, dtype=jnp.int32)
    cos_s, sin_s = _rope_tables(jnp.tile(pos_tile, rows_s // SAMPLE_ROWS))
    lg_ret = jnp.pad(jnp.log1p(-jnp.exp2(-5.0 - jnp.arange(RET_H, dtype=F32))),
                     (0, LANES - RET_H)).reshape(1, LANES)
    ki = lax.broadcasted_iota(jnp.int32, (GLA_H * GLA_DK, MIX), 0) // GLA_DK
    vi = lax.broadcasted_iota(jnp.int32, (GLA_H * GLA_DK, MIX), 1) // GLA_DV
    emat = (ki == vi).astype(BF16)

    zero_gla = jnp.zeros((1, GLA_H, GLA_DK, GLA_DV), F32)
    zero_ssm = jnp.zeros((1, SSM_PAIRS, SSM_N, LANES), F32)
    zero_conv = jnp.zeros((1, SUBLANES, SSM_CONV_DIM), F32)
    zero_ret = jnp.zeros((1, RET_H, RET_DK, RET_DV), F32)
    row2 = lambda a: a.reshape(1, -1)
    lane_pad = lambda a: jnp.pad(a, (0, LANES - a.shape[0])).reshape(1, LANES)
    tail3 = lambda c: c[:, SUBLANES - (SSM_CONV - 1):, :]

    tm_in = _pick_tile(bp * tp, 1024)
    tm_d2 = _pick_tile(bp * tp, 256)
    tm_d2s = _pick_tile(n_small, 384)
    tn = 1536
    meta = dict(row0=n_sample, nseq=1, nchunk=1, rows=N_META, tv=N_META, chained=True,
                per_seq_state=False)
    body = dict(row0=0, nseq=bp, nchunk=nchunk, rows=CHUNK, chained=True, per_seq_state=False)
    samp = dict(row0=0, nseq=bs, nchunk=1, rows=rows_s, lb=SAMPLE_ROWS, tv=ts, chained=False,
                per_seq_state=True)

    outs = {k: [] for k in ("gla_p", "gla_s", "ssm_p", "ssm_s", "conv_p", "conv_s", "ret_p", "ret_s")}
    xb, xs = x_body, x_small
    for l in range(DEPTH):
        w_in_l = _rearrange_w_in(w_in[l])
        wl = dict(w_gla_out=w_gla_out[l].astype(BF16), w_ssm_out=w_ssm_out[l].astype(BF16),
                  w_ret_out=w_ret_out[l].astype(BF16), w_o=w_o[l].astype(BF16),
                  ln1_w=row2(ln1_w[l]), ln1_b=row2(ln1_b[l]),
                  w_ff1=w_ff1[l].astype(BF16), b_ff1=row2(b_ff1[l]),
                  w_ff2=w_ff2[l].astype(BF16), b_ff2=row2(b_ff2[l]),
                  ln2_w=row2(ln2_w[l]), ln2_b=row2(ln2_b[l]))
        w2p = jnp.pad(w_gla_a2[l], ((0, LANES - GLA_RANK), (0, 0)))
        gla_w = [w2p, row2(b_gla_a[l]), row2(w_gla_norm[l]), emat]
        ssd_w = [conv_w[l], row2(conv_b[l]), lane_pad(dt_bias[l]), lane_pad(a_log[l]),
                 row2(jnp.repeat(d_skip[l], SSM_P)), row2(w_ssm_norm[l])]
        ln_w, ln_b = row2(ln_in_w), row2(ln_in_b)

        proj_s, xs = _in_proj(xs, ln_w, ln_b, w_in_l, apply_ln=(l == 0), tm=n_small, tn=tn,
                              name=f"inproj_small_{l}")
        yg_m, sg_m = _gla(proj_s, gla_w, zero_gla, lb=GLA_BLK, name=f"gla_meta_{l}", **meta)
        ys_m, cv_m, ss_m = _ssd(proj_s, ssd_w, zero_conv, zero_ssm, lb=N_META,
                                name=f"ssd_meta_{l}", **meta)
        yr_m, sr_m = _ret(proj_s, lg_ret, cos_m, sin_m, zero_ret, lb=N_META,
                          name=f"ret_meta_{l}", **meta)
        conv_in = jnp.pad(state_conv[l], ((0, 0), (pad_rows - (SSM_CONV - 1), ts), (0, 0)))
        yg_s, sg_s = _gla(proj_s, gla_w, state_gla[l], name=f"gla_sample_{l}", **samp)
        ys_s, cv_s, ss_s = _ssd(proj_s, ssd_w, conv_in, _ssm_pairs(state_ssm[l]),
                                name=f"ssd_sample_{l}", **samp)
        yr_s, sr_s = _ret(proj_s, lg_ret, cos_s, sin_s, state_ret[l], name=f"ret_sample_{l}", **samp)
        zpad = jnp.zeros((CHUNK - N_META, MIX), F32)
        yg = jnp.concatenate([yg_s, yg_m, zpad], axis=0)
        ys = jnp.concatenate([ys_s, ys_m, zpad], axis=0)
        yr = jnp.concatenate([yr_s, yr_m, zpad], axis=0)
        xs = _dense2(xs, proj_s, yg, ys, yr, wl, tm=tm_d2s, name=f"dense2_small_{l}")

        proj_b, xb = _in_proj(xb, ln_w, ln_b, w_in_l, apply_ln=(l == 0), tm=tm_in, tn=tn,
                              name=f"inproj_body_{l}")
        yg_b, sg_b = _gla(proj_b, gla_w, sg_m, lb=GLA_BLK, tv=GLA_BLK, name=f"gla_body_{l}", **body)
        ys_b, cv_b, ss_b = _ssd(proj_b, ssd_w, cv_m, ss_m, lb=CHUNK, tv=CHUNK,
                                name=f"ssd_body_{l}", **body)
        yr_b, sr_b = _ret(proj_b, lg_ret, cos_b, sin_b, sr_m, lb=CHUNK, tv=CHUNK,
                          name=f"ret_body_{l}", **body)
        xb = _dense2(xb, proj_b, yg_b, ys_b, yr_b, wl, tm=tm_d2, name=f"dense2_body_{l}")

        for key, val in (("gla_p", sg_b), ("gla_s", sg_s),
                         ("ssm_p", _ssm_unpairs(ss_b)), ("ssm_s", _ssm_unpairs(ss_s)),
                         ("conv_p", tail3(cv_b)), ("conv_s", tail3(cv_s)),
                         ("ret_p", sr_b), ("ret_s", sr_s)):
            outs[key].append(val)

    y_prompt = xb.reshape(bp, tp, d)
    y_sample = xs[:n_sample].reshape(bs, SAMPLE_ROWS, d)[:, pad_rows:]
    st = {k: jnp.stack(v) for k, v in outs.items()}
    return (y_prompt, y_sample, st["gla_p"], st["gla_s"], st["ssm_p"], st["ssm_s"],
            st["conv_p"], st["conv_s"], st["ret_p"], st["ret_s"])
```

```python
import functools

import jax
import jax.numpy as jnp
from jax import lax
from jax.experimental import pallas as pl
from jax.experimental.pallas import tpu as pltpu

F32 = jnp.float32
BF16 = jnp.bfloat16

D_MODEL = 1024
DEPTH = 2
N_META = 16
MIX = 512
GLA_H, GLA_DK, GLA_DV, GLA_RANK = 4, 64, 128, 16
GLA_GATE_NORM = 16.0
SSM_H, SSM_P, SSM_N, SSM_G, SSM_CONV = 8, 64, 64, 2, 4
SSM_CONV_DIM = MIX + 2 * SSM_G * SSM_N
SSM_PAIRS = SSM_H // 2
RET_H, RET_DK, RET_DV = 4, 64, 128
ROPE_BASE = 10000.0
D_FF = 4 * D_MODEL
ALPHA = (2 * DEPTH) ** 0.25
PAST_LEN = 16384
SPLIT_SIZES = (256, 256, 512, 512, 16, 512, 768, 8, 256, 256, 512, 512, 3072)

LANES = 128
SUBLANES = 8
VMEM_LIMIT = 56 * 1024 * 1024

COL = dict(gates=0, gla_v=3072, gla_r=3584, ssm_z=4096, ret_v=4608, ret_g=5120,
           gla_q=5632, gla_k=5888, ssm_xbc=6144, ret_q=6912, ret_k=7168, gla_a=7424, ssm_dt=7552)
N_PROJ = 7680
SAMPLE_ROWS = SUBLANES
GLA_BLK = 16
CHUNK = 128

NN = (((1,), (0,)), ((), ()))
NT = (((1,), (1,)), ((), ()))
TN = (((0,), (0,)), ((), ()))


def _dot(a, b, dims=NN):
    return lax.dot_general(a.astype(BF16), b.astype(BF16), dims, preferred_element_type=F32)


def _dot_f32(a, b, dims=NN):
    return lax.dot_general(a, b, dims, precision=lax.Precision.HIGHEST,
                           preferred_element_type=F32)


def _dot_sel(sel, x, dims=NN, sel_first=True):
    hi = x.astype(BF16)
    r1 = x - hi.astype(F32)
    mid = r1.astype(BF16)
    lo = (r1 - mid.astype(F32)).astype(BF16)
    sb = sel.astype(BF16)
    out = None
    for part in (hi, mid, lo):
        ops = (sb, part) if sel_first else (part, sb)
        term = lax.dot_general(*ops, dims, preferred_element_type=F32)
        out = term if out is None else out + term
    return out


def _layer_norm(x, w, b):
    mu = jnp.mean(x, axis=-1, keepdims=True)
    xc = x - mu
    var = jnp.mean(xc * xc, axis=-1, keepdims=True)
    return xc * lax.rsqrt(var + 1e-5) * w + b


def _rms(x):
    return x * lax.rsqrt(jnp.mean(x * x, axis=-1, keepdims=True) + 1e-6)


def _silu(x):
    return x * jax.nn.sigmoid(x)


def _block_masks(rows, lb):
    r = lax.broadcasted_iota(jnp.int32, (rows, rows), 0)
    c = lax.broadcasted_iota(jnp.int32, (rows, rows), 1)
    same = (r // lb) == (c // lb)
    return same, same & (r >= c)


def _row_in_block(rows, lb):
    return lax.broadcasted_iota(jnp.int32, (rows, 1), 0) % lb


def _inproj_kernel(x_ref, lnw_ref, lnb_ref, w_ref, *refs, apply_ln):
    if apply_ln:
        proj_ref, xn_ref, xb_scr = refs
    else:
        proj_ref, xb_scr = refs

    @pl.when(pl.program_id(1) == 0)
    def _():
        x = x_ref[...]
        if apply_ln:
            x = _layer_norm(x, lnw_ref[...], lnb_ref[...])
            xn_ref[...] = x
        xb_scr[...] = x.astype(BF16)

    proj_ref[...] = jnp.dot(xb_scr[...], w_ref[...], preferred_element_type=F32).astype(BF16)


def _in_proj(x, lnw, lnb, w, *, apply_ln, tm, tn, name):
    m = x.shape[0]
    grid = (m // tm, N_PROJ // tn)
    out_shape = [jax.ShapeDtypeStruct((m, N_PROJ), BF16)]
    out_specs = [pl.BlockSpec((tm, tn), lambda i, j: (i, j))]
    if apply_ln:
        out_shape.append(jax.ShapeDtypeStruct((m, D_MODEL), F32))
        out_specs.append(pl.BlockSpec((tm, D_MODEL), lambda i, j: (i, 0)))
    res = pl.pallas_call(
        functools.partial(_inproj_kernel, apply_ln=apply_ln),
        grid=grid,
        in_specs=[pl.BlockSpec((tm, D_MODEL), lambda i, j: (i, 0)),
                  pl.BlockSpec((1, D_MODEL), lambda i, j: (0, 0)),
                  pl.BlockSpec((1, D_MODEL), lambda i, j: (0, 0)),
                  pl.BlockSpec((D_MODEL, tn), lambda i, j: (0, j))],
        out_specs=out_specs,
        out_shape=out_shape,
        scratch_shapes=[pltpu.VMEM((tm, D_MODEL), BF16)],
        compiler_params=pltpu.CompilerParams(
            dimension_semantics=("parallel", "arbitrary"), vmem_limit_bytes=VMEM_LIMIT),
        name=name,
    )(x, lnw, lnb, w)
    return (res[0], res[1]) if apply_ln else (res[0], x)


def _rec_call(kernel_fn, proj, segs, consts, states, out_widths, *, row0, nseq, nchunk, rows,
              chained, per_seq_state, scratch, name, tables=()):
    rb = row0 // rows
    nb = 1 if chained else rows // SAMPLE_ROWS
    grid = (nseq, nchunk) if chained else (nseq // nb, 1)

    def row_idx(b, c):
        return b * nchunk + c if chained else b

    def col(seg, w):
        cbi = COL[seg] // w
        return pl.BlockSpec((rows, w), lambda b, c: (rb + row_idx(b, c), cbi))

    def whole(a):
        return pl.BlockSpec(a.shape, lambda b, c: (0,) * a.ndim)

    def state_in_spec(a, layer):
        zeros = (0,) * (a.ndim - 2)
        if per_seq_state:
            return pl.BlockSpec((None, nb) + a.shape[2:], lambda b, c: (layer, b) + zeros)
        return pl.BlockSpec((None, nb) + a.shape[2:], lambda b, c: (layer, 0) + zeros)

    def state_out_spec(a):
        zeros = (0,) * (a.ndim - 2)
        return pl.BlockSpec((nb,) + a.shape[2:], lambda b, c: (b,) + zeros)

    in_specs = ([col(s, w) for s, w in segs] + [whole(a) for a in consts]
                + [pl.BlockSpec((rows, t.shape[1]), lambda b, c: (c, 0)) for t in tables]
                + [state_in_spec(a, layer) for a, layer in states])
    n_rows = nseq * nchunk * rows if chained else nseq * SAMPLE_ROWS
    out_specs = ([pl.BlockSpec((rows, w), lambda b, c: (row_idx(b, c), 0)) for w in out_widths]
                 + [state_out_spec(a) for a, _ in states])
    out_shape = ([jax.ShapeDtypeStruct((n_rows, w), BF16) for w in out_widths]
                 + [jax.ShapeDtypeStruct((nseq,) + a.shape[2:], F32) for a, _ in states])
    return pl.pallas_call(
        kernel_fn, grid=grid, in_specs=in_specs, out_specs=out_specs, out_shape=out_shape,
        scratch_shapes=scratch,
        compiler_params=pltpu.CompilerParams(
            dimension_semantics=("parallel", "arbitrary"), vmem_limit_bytes=VMEM_LIMIT),
        name=name,
    )(*([proj] * len(segs)), *consts, *tables, *[a for a, _ in states])


def _gla_kernel(q_ref, k_ref, v_ref, r_ref, a_ref, w2_ref, ba_ref, wn_ref, e_ref, s0_ref,
                y_ref, so_ref, *scr, rows, lb, tv, chained):
    nblk = rows // lb
    if chained:
        s_scr, = scr

        @pl.when(pl.program_id(1) == 0)
        def _():
            s_scr[...] = s0_ref[0]

    q = q_ref[...].astype(F32) * (GLA_DK ** -0.5)
    k = k_ref[...].astype(F32)
    v = v_ref[...].astype(F32)
    a = _dot_f32(a_ref[...].astype(F32), w2_ref[...]) + ba_ref[...]
    g = jax.nn.log_sigmoid(a) * (1.0 / GLA_GATE_NORM)
    if tv < lb:
        valid = _row_in_block(rows, lb) >= lb - tv
        g = jnp.where(valid, g, 0.0)
        k = jnp.where(valid, k, 0.0)
    same, causal = _block_masks(rows, lb)
    gs = _dot_sel(causal, g)
    gtot = _dot_sel(same, g)
    qd = q * jnp.exp(gs)
    kd = k * jnp.exp(gtot - gs)
    sel = (lax.broadcasted_iota(jnp.int32, (rows, nblk * LANES), 0) // lb
           == lax.broadcasted_iota(jnp.int32, (rows, nblk * LANES), 1) // LANES)
    ds = jnp.exp(_dot_sel(sel, g, TN, sel_first=False))

    width = GLA_H * GLA_DK
    q3 = q.reshape(nblk, lb, width)
    k3 = k.reshape(nblk, lb, width)
    g3 = gs.reshape(nblk, lb, width)
    v3 = v.reshape(nblk, lb, MIX)
    t3 = lax.broadcasted_iota(jnp.int32, (1, lb, 1), 1)
    sources = range(lb - tv, lb)
    pieces = []
    for s in sources:
        dd = jnp.minimum(g3 - g3[:, s:s + 1, :], 0.0)
        w = q3 * k3[:, s:s + 1, :] * jnp.exp(dd)
        pieces.append(jnp.where(t3 >= s, w, 0.0).reshape(rows, width))
    scores = jnp.dot(jnp.concatenate(pieces, axis=0).astype(BF16), e_ref[...],
                     preferred_element_type=F32)
    o3 = jnp.zeros((nblk, lb, MIX), F32)
    for i, s in enumerate(sources):
        o3 = o3 + scores[i * rows:(i + 1) * rows, :].reshape(nblk, lb, MIX) * v3[:, s:s + 1, :]
    o = o3.reshape(rows, MIX)

    st = [s_scr[h] for h in range(GLA_H)] if chained else None
    o_rows = []
    for b in range(nblk):
        rs = slice(b * lb, (b + 1) * lb)
        o_heads = []
        for h in range(GLA_H):
            ks = slice(h * GLA_DK, (h + 1) * GLA_DK)
            vs = slice(h * GLA_DV, (h + 1) * GLA_DV)
            cur = st[h] if chained else s0_ref[b, h]
            o_heads.append(_dot(qd[rs, ks], cur))
            new = ds[ks, b * LANES:(b + 1) * LANES] * cur + _dot(kd[rs, ks], v[rs, vs], TN)
            if chained:
                st[h] = new
            else:
                so_ref[b, h] = new
        o_rows.append(jnp.concatenate(o_heads, axis=1))
    o = o + jnp.concatenate(o_rows, axis=0)
    y = jnp.concatenate(
        [_rms(o[:, h * GLA_DV:(h + 1) * GLA_DV]) * wn_ref[...] for h in range(GLA_H)], axis=1)
    y_ref[...] = (_silu(r_ref[...].astype(F32)) * y).astype(BF16)

    if chained:
        for h in range(GLA_H):
            s_scr[h] = st[h]

        @pl.when(pl.program_id(1) == pl.num_programs(1) - 1)
        def _():
            so_ref[0] = s_scr[...]


def _gla(proj, weights, s0, *, row0, nseq, nchunk, rows, lb, tv, chained, per_seq_state, name):
    kern = functools.partial(_gla_kernel, rows=rows, lb=lb, tv=tv, chained=chained)
    scratch = [pltpu.VMEM((GLA_H, GLA_DK, GLA_DV), F32)] if chained else []
    return _rec_call(kern, proj,
                     [("gla_q", 256), ("gla_k", 256), ("gla_v", 512), ("gla_r", 512), ("gla_a", 128)],
                     weights, [s0], [MIX], row0=row0, nseq=nseq, nchunk=nchunk, rows=rows,
                     chained=chained, per_seq_state=per_seq_state, scratch=scratch, name=name)


def _ssd_kernel(z_ref, x_ref, dt_ref, cw_ref, cb_ref, dtb_ref, alog_ref, dsk_ref, wn_ref,
                c0_ref, s0_ref, y_ref, co_ref, so_ref, *scr, rows, lb, tv, chained):
    nblk = rows // lb
    t_in = _row_in_block(rows, lb)
    valid = (t_in >= lb - tv) if tv < lb else None
    if chained:
        s_scr, ext_scr = scr

        @pl.when(pl.program_id(1) == 0)
        def _():
            s_scr[...] = s0_ref[0]
            ext_scr[0:SUBLANES, :] = c0_ref[0]

        xin = x_ref[...].astype(F32)
    else:
        ext_scr, = scr
        ext_scr[0:SUBLANES, :] = jnp.zeros((SUBLANES, SSM_CONV_DIM), F32)
        xin = jnp.where(valid, x_ref[...].astype(F32), c0_ref[...].reshape(rows, SSM_CONV_DIM))

    ext_scr[SUBLANES:SUBLANES + rows, :] = xin
    conv = cb_ref[...]
    for i in range(SSM_CONV):
        conv = conv + cw_ref[i:i + 1, :] * ext_scr[pl.ds(SUBLANES - (SSM_CONV - 1) + i, rows), :]
    act = _silu(conv)
    if chained:
        co_ref[0] = ext_scr[rows:rows + SUBLANES, :]
        ext_scr[0:SUBLANES, :] = ext_scr[rows:rows + SUBLANES, :]
    else:
        co_ref[...] = xin.reshape(nblk, lb, SSM_CONV_DIM)

    dt = jax.nn.softplus(dt_ref[...].astype(F32) + dtb_ref[...])
    gdt = dt * (-jnp.exp(alog_ref[...]))
    if valid is not None:
        gdt = jnp.where(valid, gdt, 0.0)
    same, causal = _block_masks(rows, lb)
    gcum = _dot_sel(causal, gdt)
    if rows < LANES:
        gsq = jnp.concatenate([gcum, jnp.zeros((LANES - rows, LANES), F32)], axis=0)
        gcum_t = gsq.T[:, :rows]
    else:
        gcum_t = gcum.T
    gtot = gcum[rows - 1:rows, :] if nblk == 1 else _dot_sel(same, gdt)

    lane = lax.broadcasted_iota(jnp.int32, (rows, LANES), 1)
    low = lane < SSM_P
    bcol = act[:, MIX:MIX + LANES]
    ccol = act[:, MIX + LANES:MIX + 2 * LANES]
    bswap = pltpu.roll(bcol, SSM_N, axis=1)
    cswap = pltpu.roll(ccol, SSM_N, axis=1)
    b2 = (jnp.where(low, bcol, bswap), jnp.where(low, bswap, bcol))
    c2 = (jnp.where(low, ccol, cswap), jnp.where(low, cswap, ccol))
    cb = (_dot(jnp.where(low, ccol, 0.0), bcol, NT), _dot(jnp.where(low, 0.0, ccol), bcol, NT))

    def pair_lanes(x, p):
        return jnp.where(low[:x.shape[0]], x[:, 2 * p:2 * p + 1], x[:, 2 * p + 1:2 * p + 2])

    def block_diag(scat):
        low64 = lax.broadcasted_iota(jnp.int32, (SSM_N, LANES), 1) < SSM_P
        return jnp.concatenate([jnp.where(low64, scat, 0.0), jnp.where(low64, 0.0, scat)], axis=0)

    def diag_blocks(u):
        low64 = lax.broadcasted_iota(jnp.int32, (SSM_N, LANES), 1) < SSM_P
        return jnp.where(low64, u[:SSM_N, :], u[SSM_N:, :])

    y_pairs = []
    for p in range(SSM_PAIRS):
        gi = p // (SSM_PAIRS // SSM_G)
        decs = []
        for h in (2 * p, 2 * p + 1):
            diff = jnp.minimum(gcum[:, h:h + 1] - gcum_t[h:h + 1, :], 0.0)
            decs.append(cb[gi] * jnp.where(causal, jnp.exp(diff), 0.0))
        xp = act[:, p * LANES:(p + 1) * LANES]
        vp = xp * pair_lanes(dt, p)
        if valid is not None:
            vp = jnp.where(valid, vp, 0.0)
        vbd = jnp.concatenate([jnp.where(low, vp, 0.0), jnp.where(low, 0.0, vp)], axis=0)
        o = _dot(jnp.concatenate(decs, axis=1), vbd)
        g2 = pair_lanes(gcum, p)
        ge2 = pair_lanes(gtot, p)
        cin = c2[gi] * jnp.exp(g2)
        bout = b2[gi] * jnp.exp(ge2 - g2)
        if chained:
            cur = s_scr[p]
            o = o + _dot(cin, block_diag(cur))
            s_scr[p] = jnp.exp(ge2) * cur + diag_blocks(_dot(bout, vp, TN))
        else:
            o_rows = []
            for b in range(nblk):
                rs = slice(b * lb, (b + 1) * lb)
                cur = s0_ref[b, p]
                o_rows.append(_dot(cin[rs], block_diag(cur)))
                so_ref[b, p] = (jnp.exp(ge2[b * lb:b * lb + 1, :]) * cur
                                + diag_blocks(_dot(bout[rs], vp[rs], TN)))
            o = o + jnp.concatenate(o_rows, axis=0)
        y_pairs.append(o + dsk_ref[:, p * LANES:(p + 1) * LANES] * xp)
    y = jnp.concatenate(y_pairs, axis=1) * _silu(z_ref[...].astype(F32))
    half = MIX // SSM_G
    y = jnp.concatenate([_rms(y[:, gi * half:(gi + 1) * half]) for gi in range(SSM_G)], axis=1)
    y_ref[...] = (y * wn_ref[...]).astype(BF16)

    if chained:
        @pl.when(pl.program_id(1) == pl.num_programs(1) - 1)
        def _():
            so_ref[0] = s_scr[...]


def _ssd(proj, weights, c0, s0, *, row0, nseq, nchunk, rows, lb, tv, chained, per_seq_state, name):
    kern = functools.partial(_ssd_kernel, rows=rows, lb=lb, tv=tv, chained=chained)
    scratch = [pltpu.VMEM((SUBLANES + rows, SSM_CONV_DIM), F32)]
    if chained:
        scratch = [pltpu.VMEM((SSM_PAIRS, SSM_N, LANES), F32)] + scratch
    return _rec_call(kern, proj, [("ssm_z", 512), ("ssm_xbc", 768), ("ssm_dt", 128)],
                     weights, [c0, s0], [MIX], row0=row0, nseq=nseq, nchunk=nchunk, rows=rows,
                     chained=chained, per_seq_state=per_seq_state, scratch=scratch, name=name)


def _ret_kernel(q_ref, k_ref, v_ref, g_ref, lg_ref, cos_ref, sin_ref, s0_ref,
                y_ref, so_ref, *scr, rows, lb, tv, chained):
    nblk = rows // lb
    t_col = _row_in_block(rows, lb)
    s_row = lax.broadcasted_iota(jnp.int32, (1, rows), 1) % lb
    n_col = jnp.maximum(t_col - (lb - tv) + 1, 0).astype(F32)
    n_row = jnp.maximum(s_row - (lb - tv) + 1, 0).astype(F32)

    def decay_matrix(h):
        _, causal = _block_masks(rows, lb)
        diff = (n_col - n_row) * lg_ref[:, h:h + 1]
        return jnp.where(causal, jnp.exp(jnp.minimum(diff, 0.0)), 0.0)

    if chained:
        s_scr, dec_scr = scr

        @pl.when(pl.program_id(1) == 0)
        def _():
            s_scr[...] = s0_ref[0]
            for h in range(RET_H):
                dec_scr[h] = decay_matrix(h)

    width = RET_H * RET_DK
    lane = lax.broadcasted_iota(jnp.int32, (rows, width), 1)
    first_half = (lane % RET_DK) < (RET_DK // 2)
    cos = cos_ref[...]
    sin = jnp.where(first_half, -sin_ref[...], sin_ref[...])

    def rope(x):
        partner = jnp.where(first_half, pltpu.roll(x, width - RET_DK // 2, axis=1),
                            pltpu.roll(x, RET_DK // 2, axis=1))
        return x * cos + partner * sin

    q = rope(q_ref[...].astype(F32))
    k = rope(k_ref[...].astype(F32)) * (RET_DK ** -0.5)
    v = v_ref[...]
    if tv < lb:
        k = jnp.where(t_col >= lb - tv, k, 0.0)

    outs = []
    for h in range(RET_H):
        lg = lg_ref[:, h:h + 1]
        gc = n_col * lg
        ks = slice(h * RET_DK, (h + 1) * RET_DK)
        vs = slice(h * RET_DV, (h + 1) * RET_DV)
        att = _dot(q[:, ks], k[:, ks], NT) * (dec_scr[h] if chained else decay_matrix(h))
        ge = float(tv) * lg
        qin = q[:, ks] * jnp.exp(gc)
        kout = k[:, ks] * jnp.exp(ge - gc)
        if chained:
            cur = s_scr[h]
            if rows % LANES == 0:
                oh = _dot(jnp.concatenate([att.astype(BF16), qin.astype(BF16)], axis=1),
                          jnp.concatenate([v[:, vs], cur.astype(BF16)], axis=0))
            else:
                oh = _dot(att, v[:, vs]) + _dot(qin, cur)
            s_scr[h] = jnp.exp(ge) * cur + _dot(kout, v[:, vs], TN)
        else:
            oh = _dot(att, v[:, vs])
            o_rows = []
            for b in range(nblk):
                rs = slice(b * lb, (b + 1) * lb)
                cur = s0_ref[b, h]
                o_rows.append(_dot(qin[rs], cur))
                so_ref[b, h] = jnp.exp(ge) * cur + _dot(kout[rs], v[rs, vs], TN)
            oh = oh + jnp.concatenate(o_rows, axis=0)
        outs.append(_rms(oh))
    y_ref[...] = (_silu(g_ref[...].astype(F32)) * jnp.concatenate(outs, axis=1)).astype(BF16)

    if chained:
        @pl.when(pl.program_id(1) == pl.num_programs(1) - 1)
        def _():
            so_ref[0] = s_scr[...]


def _ret(proj, lg, cos, sin, s0, *, row0, nseq, nchunk, rows, lb, tv, chained, per_seq_state, name):
    kern = functools.partial(_ret_kernel, rows=rows, lb=lb, tv=tv, chained=chained)
    scratch = ([pltpu.VMEM((RET_H, RET_DK, RET_DV), F32), pltpu.VMEM((RET_H, rows, rows), F32)]
               if chained else [])
    return _rec_call(kern, proj, [("ret_q", 256), ("ret_k", 256), ("ret_v", 512), ("ret_g", 512)],
                     [lg], [s0], [MIX], row0=row0, nseq=nseq, nchunk=nchunk, rows=rows,
                     chained=chained, per_seq_state=per_seq_state, scratch=scratch, name=name,
                     tables=(cos, sin))


def _dense2_kernel(x_ref, gate_ref, yg_ref, ys_ref, yr_ref, wg_ref, ws_ref, wr_ref, wo_ref,
                   l1w_ref, l1b_ref, w1_ref, b1_ref, w2_ref, b2_ref, l2w_ref, l2b_ref, o_ref,
                   *, ff_chunk):
    gate = lambda i: jax.nn.sigmoid(gate_ref[:, i * D_MODEL:(i + 1) * D_MODEL].astype(F32))
    merged = (gate(0) * _dot(yg_ref[...], wg_ref[...]) + gate(1) * _dot(ys_ref[...], ws_ref[...])
              + gate(2) * _dot(yr_ref[...], wr_ref[...]))
    mix = _dot(merged, wo_ref[...])
    h = _layer_norm(ALPHA * x_ref[...] + mix, l1w_ref[...], l1b_ref[...])
    hb = h.astype(BF16)
    ff = jnp.zeros_like(h) + b2_ref[...]
    for c0 in range(0, D_FF, ff_chunk):
        hid = jnp.dot(hb, w1_ref[:, c0:c0 + ff_chunk], preferred_element_type=F32)
        hid = jnp.square(jnp.maximum(hid + b1_ref[:, c0:c0 + ff_chunk], 0.0))
        ff = ff + _dot(hid, w2_ref[c0:c0 + ff_chunk, :])
    o_ref[...] = _layer_norm(ALPHA * h + ff, l2w_ref[...], l2b_ref[...])


def _dense2(x, proj, yg, ys, yr, wl, *, tm, name):
    m = x.shape[0]
    row = lambda w: pl.BlockSpec((tm, w), lambda i: (i, 0))
    const = lambda r, w: pl.BlockSpec((r, w), lambda i: (0, 0), pipeline_mode=pl.Buffered(1))
    return pl.pallas_call(
        functools.partial(_dense2_kernel, ff_chunk=1024),
        grid=(m // tm,),
        in_specs=[row(D_MODEL), row(3 * D_MODEL), row(MIX), row(MIX), row(MIX),
                  const(MIX, D_MODEL), const(MIX, D_MODEL), const(MIX, D_MODEL),
                  const(D_MODEL, D_MODEL), const(1, D_MODEL), const(1, D_MODEL),
                  const(D_MODEL, D_FF), const(1, D_FF), const(D_FF, D_MODEL), const(1, D_MODEL),
                  const(1, D_MODEL), const(1, D_MODEL)],
        out_specs=row(D_MODEL),
        out_shape=jax.ShapeDtypeStruct((m, D_MODEL), F32),
        compiler_params=pltpu.CompilerParams(
            dimension_semantics=("parallel",), vmem_limit_bytes=VMEM_LIMIT),
        name=name,
    )(x, proj, yg, ys, yr, wl["w_gla_out"], wl["w_ssm_out"], wl["w_ret_out"], wl["w_o"],
      wl["ln1_w"], wl["ln1_b"], wl["w_ff1"], wl["b_ff1"], wl["w_ff2"], wl["b_ff2"],
      wl["ln2_w"], wl["ln2_b"])


def _pick_tile(n, pref):
    t = min(n, pref)
    while n % t or t % SUBLANES:
        t -= 1
    return t


def _rearrange_w_in(w):
    offs = [0]
    for s in SPLIT_SIZES:
        offs.append(offs[-1] + s)
    names = ("gla_q", "gla_k", "gla_v", "gla_r", "gla_a", "ssm_z", "ssm_xbc", "ssm_dt",
             "ret_q", "ret_k", "ret_v", "ret_g", "gates")
    seg = {n: w[..., offs[i]:offs[i + 1]].astype(BF16) for i, n in enumerate(names)}
    pad = lambda a: jnp.pad(a, ((0, 0),) * (a.ndim - 1) + ((0, LANES - a.shape[-1]),))
    order = sorted(COL, key=COL.get)
    parts = [pad(seg[n]) if n in ("gla_a", "ssm_dt") else seg[n] for n in order]
    return jnp.concatenate(parts, axis=-1)


def _rope_tables(pos):
    half = RET_DK // 2
    inv_freq = ROPE_BASE ** (-jnp.arange(half, dtype=F32) / half)
    ang = pos.astype(F32)[:, None] * inv_freq[None, :]
    cos = jnp.tile(jnp.cos(ang), (1, 2 * RET_H))
    sin = jnp.tile(jnp.sin(ang), (1, 2 * RET_H))
    return cos, sin


def _ssm_pairs(s):
    lead = s.shape[:-3]
    n = len(lead)
    return (s.reshape(lead + (SSM_PAIRS, 2, SSM_N, SSM_P))
            .transpose(tuple(range(n)) + (n, n + 2, n + 1, n + 3))
            .reshape(lead + (SSM_PAIRS, SSM_N, 2 * SSM_P)))


def _ssm_unpairs(s):
    b = s.shape[0]
    return (s.reshape(b, SSM_PAIRS, SSM_N, 2, SSM_P).transpose(0, 1, 3, 2, 4)
            .reshape(b, SSM_H, SSM_N, SSM_P))


def kernel(x_prompt, x_sample, state_gla, state_ssm, state_conv, state_ret, meta_tokens,
           ln_in_w, ln_in_b, w_in, w_gla_a2, b_gla_a, w_gla_norm, conv_w, conv_b, dt_bias,
           a_log, d_skip, w_ssm_norm, w_gla_out, w_ssm_out, w_ret_out, w_o, ln1_w, ln1_b,
           w_ff1, b_ff1, w_ff2, b_ff2, ln2_w, ln2_b):
    bp, tp, d = x_prompt.shape
    bs, ts, _ = x_sample.shape
    assert d == D_MODEL and tp % CHUNK == 0
    assert SSM_CONV - 1 <= SAMPLE_ROWS - ts
    nchunk = tp // CHUNK
    pad_rows = SAMPLE_ROWS - ts

    x_body = x_prompt.reshape(bp * tp, d)
    n_sample = bs * SAMPLE_ROWS
    rows_s = min(CHUNK, n_sample)
    assert n_sample % rows_s == 0 and n_sample % N_META == 0
    n_small = n_sample + CHUNK
    x_small = jnp.concatenate(
        [jnp.pad(x_sample, ((0, 0), (pad_rows, 0), (0, 0))).reshape(n_sample, d),
         meta_tokens.astype(F32), jnp.zeros((CHUNK - N_META, d), F32)], axis=0)

    cos_b, sin_b = _rope_tables(N_META + jnp.arange(tp, dtype=jnp.int32))
    cos_m, sin_m = _rope_tables(jnp.arange(N_META, dtype=jnp.int32))
    pos_tile = PAST_LEN - pad_rows + jnp.arange(SAMPLE_ROWS, dtype=jnp.int32)
    cos_s, sin_s = _rope_tables(jnp.tile(pos_tile, rows_s // SAMPLE_ROWS))
    lg_ret = jnp.pad(jnp.log1p(-jnp.exp2(-5.0 - jnp.arange(RET_H, dtype=F32))),
                     (0, LANES - RET_H)).reshape(1, LANES)
    ki = lax.broadcasted_iota(jnp.int32, (GLA_H * GLA_DK, MIX), 0) // GLA_DK
    vi = lax.broadcasted_iota(jnp.int32, (GLA_H * GLA_DK, MIX), 1) // GLA_DV
    emat = (ki == vi).astype(BF16)

    zero_gla = (jnp.zeros((1, 1, GLA_H, GLA_DK, GLA_DV), F32), 0)
    zero_ssm = (jnp.zeros((1, 1, SSM_PAIRS, SSM_N, LANES), F32), 0)
    zero_conv = (jnp.zeros((1, 1, SUBLANES, SSM_CONV_DIM), F32), 0)
    zero_ret = (jnp.zeros((1, 1, RET_H, RET_DK, RET_DV), F32), 0)
    from_meta = lambda a: (a[None], 0)
    ssm_pairs_in = _ssm_pairs(state_ssm)
    conv_in = jnp.pad(state_conv, ((0, 0), (0, 0), (pad_rows - (SSM_CONV - 1), ts), (0, 0)))
    row2 = lambda a: a.reshape(1, -1)
    lane_pad = lambda a: jnp.pad(a, (0, LANES - a.shape[0])).reshape(1, LANES)
    tail3 = lambda c: c[:, SUBLANES - (SSM_CONV - 1):, :]

    tm_in = _pick_tile(bp * tp, 2048)
    tm_d2 = _pick_tile(bp * tp, 512)
    tm_d2s = _pick_tile(n_small, 384)
    tn = 1536
    meta = dict(row0=n_sample, nseq=1, nchunk=1, rows=N_META, tv=N_META, chained=True,
                per_seq_state=False)
    body = dict(row0=0, nseq=bp, nchunk=nchunk, rows=CHUNK, chained=True, per_seq_state=False)
    samp = dict(row0=0, nseq=bs, nchunk=1, rows=rows_s, lb=SAMPLE_ROWS, tv=ts, chained=False,
                per_seq_state=True)

    outs = {k: [] for k in ("gla_p", "gla_s", "ssm_p", "ssm_s", "conv_p", "conv_s", "ret_p", "ret_s")}
    xb, xs = x_body, x_small
    w_in_all = _rearrange_w_in(w_in)
    for l in range(DEPTH):
        w_in_l = w_in_all[l]
        wl = dict(w_gla_out=w_gla_out[l].astype(BF16), w_ssm_out=w_ssm_out[l].astype(BF16),
                  w_ret_out=w_ret_out[l].astype(BF16), w_o=w_o[l].astype(BF16),
                  ln1_w=row2(ln1_w[l]), ln1_b=row2(ln1_b[l]),
                  w_ff1=w_ff1[l].astype(BF16), b_ff1=row2(b_ff1[l]),
                  w_ff2=w_ff2[l].astype(BF16), b_ff2=row2(b_ff2[l]),
                  ln2_w=row2(ln2_w[l]), ln2_b=row2(ln2_b[l]))
        w2p = jnp.pad(w_gla_a2[l], ((0, LANES - GLA_RANK), (0, 0)))
        gla_w = [w2p, row2(b_gla_a[l]), row2(w_gla_norm[l]), emat]
        ssd_w = [conv_w[l], row2(conv_b[l]), lane_pad(dt_bias[l]), lane_pad(a_log[l]),
                 row2(jnp.repeat(d_skip[l], SSM_P)), row2(w_ssm_norm[l])]
        ln_w, ln_b = row2(ln_in_w), row2(ln_in_b)

        proj_s, xs = _in_proj(xs, ln_w, ln_b, w_in_l, apply_ln=(l == 0), tm=n_small, tn=tn,
                              name=f"inproj_small_{l}")
        yg_m, sg_m = _gla(proj_s, gla_w, zero_gla, lb=GLA_BLK, name=f"gla_meta_{l}", **meta)
        ys_m, cv_m, ss_m = _ssd(proj_s, ssd_w, zero_conv, zero_ssm, lb=N_META,
                                name=f"ssd_meta_{l}", **meta)
        yr_m, sr_m = _ret(proj_s, lg_ret, cos_m, sin_m, zero_ret, lb=N_META,
                          name=f"ret_meta_{l}", **meta)
        yg_s, sg_s = _gla(proj_s, gla_w, (state_gla, l), name=f"gla_sample_{l}", **samp)
        ys_s, cv_s, ss_s = _ssd(proj_s, ssd_w, (conv_in, l), (ssm_pairs_in, l),
                                name=f"ssd_sample_{l}", **samp)
        yr_s, sr_s = _ret(proj_s, lg_ret, cos_s, sin_s, (state_ret, l), name=f"ret_sample_{l}", **samp)
        zpad = jnp.zeros((CHUNK - N_META, MIX), BF16)
        yg = jnp.concatenate([yg_s, yg_m, zpad], axis=0)
        ys = jnp.concatenate([ys_s, ys_m, zpad], axis=0)
        yr = jnp.concatenate([yr_s, yr_m, zpad], axis=0)
        xs = _dense2(xs, proj_s, yg, ys, yr, wl, tm=tm_d2s, name=f"dense2_small_{l}")

        proj_b, xb = _in_proj(xb, ln_w, ln_b, w_in_l, apply_ln=(l == 0),
                              tm=(tm_in // 2 if l == 0 else tm_in), tn=tn,
                              name=f"inproj_body_{l}")
        yg_b, sg_b = _gla(proj_b, gla_w, from_meta(sg_m), lb=GLA_BLK, tv=GLA_BLK,
                          name=f"gla_body_{l}", **body)
        ys_b, cv_b, ss_b = _ssd(proj_b, ssd_w, from_meta(cv_m), from_meta(ss_m), lb=CHUNK, tv=CHUNK,
                                name=f"ssd_body_{l}", **body)
        yr_b, sr_b = _ret(proj_b, lg_ret, cos_b, sin_b, from_meta(sr_m), lb=CHUNK, tv=CHUNK,
                          name=f"ret_body_{l}", **body)
        xb = _dense2(xb, proj_b, yg_b, ys_b, yr_b, wl, tm=tm_d2, name=f"dense2_body_{l}")

        for key, val in (("gla_p", sg_b), ("gla_s", sg_s),
                         ("ssm_p", _ssm_unpairs(ss_b)), ("ssm_s", _ssm_unpairs(ss_s)),
                         ("conv_p", tail3(cv_b)), ("conv_s", tail3(cv_s)),
                         ("ret_p", sr_b), ("ret_s", sr_s)):
            outs[key].append(val)

    y_prompt = xb.reshape(bp, tp, d)
    y_sample = xs[:n_sample].reshape(bs, SAMPLE_ROWS, d)[:, pad_rows:]
    st = {k: jnp.stack(v) for k, v in outs.items()}
    return (y_prompt, y_sample, st["gla_p"], st["gla_s"], st["ssm_p"], st["ssm_s"],
            st["conv_p"], st["conv_s"], st["ret_p"], st["ret_s"])
```

```python
import functools

import jax
import jax.numpy as jnp
from jax import lax
from jax.experimental import pallas as pl
from jax.experimental.pallas import tpu as pltpu

F32 = jnp.float32
BF16 = jnp.bfloat16

D_MODEL = 1024
DEPTH = 2
N_META = 16
MIX = 512
GLA_H, GLA_DK, GLA_DV, GLA_RANK = 4, 64, 128, 16
GLA_GATE_NORM = 16.0
SSM_H, SSM_P, SSM_N, SSM_G, SSM_CONV = 8, 64, 64, 2, 4
SSM_CONV_DIM = MIX + 2 * SSM_G * SSM_N
SSM_PAIRS = SSM_H // 2
RET_H, RET_DK, RET_DV = 4, 64, 128
ROPE_BASE = 10000.0
D_FF = 4 * D_MODEL
ALPHA = (2 * DEPTH) ** 0.25
PAST_LEN = 16384
SPLIT_SIZES = (256, 256, 512, 512, 16, 512, 768, 8, 256, 256, 512, 512, 3072)

LANES = 128
SUBLANES = 8
VMEM_LIMIT = 56 * 1024 * 1024

COL = dict(gates=0, gla_v=3072, gla_r=3584, ssm_z=4096, ret_v=4608, ret_g=5120,
           gla_q=5632, gla_k=5888, ssm_xbc=6144, ret_q=6912, ret_k=7168, gla_a=7424, ssm_dt=7552)
N_PROJ = 7680
SAMPLE_ROWS = SUBLANES
GLA_BASE = SUBLANES
CHUNK = 128

NN = (((1,), (0,)), ((), ()))
NT = (((1,), (1,)), ((), ()))
TN = (((0,), (0,)), ((), ()))


def _dot(a, b, dims=NN):
    return lax.dot_general(a.astype(BF16), b.astype(BF16), dims, preferred_element_type=F32)


def _dot_f32(a, b, dims=NN):
    return lax.dot_general(a, b, dims, precision=lax.Precision.HIGHEST,
                           preferred_element_type=F32)


def _dot_sel(sel, x, dims=NN, sel_first=True):
    hi = x.astype(BF16)
    r1 = x - hi.astype(F32)
    mid = r1.astype(BF16)
    lo = (r1 - mid.astype(F32)).astype(BF16)
    sb = sel.astype(BF16)
    out = None
    for part in (hi, mid, lo):
        ops = (sb, part) if sel_first else (part, sb)
        term = lax.dot_general(*ops, dims, preferred_element_type=F32)
        out = term if out is None else out + term
    return out


def _layer_norm(x, w, b):
    mu = jnp.mean(x, axis=-1, keepdims=True)
    xc = x - mu
    var = jnp.mean(xc * xc, axis=-1, keepdims=True)
    return xc * lax.rsqrt(var + 1e-5) * w + b


def _rms(x):
    return x * lax.rsqrt(jnp.mean(x * x, axis=-1, keepdims=True) + 1e-6)


def _silu(x):
    return x * jax.nn.sigmoid(x)


def _iotas(rows):
    return (lax.broadcasted_iota(jnp.int32, (rows, rows), 0),
            lax.broadcasted_iota(jnp.int32, (rows, rows), 1))


def _row_in_block(rows, lb):
    return lax.broadcasted_iota(jnp.int32, (rows, 1), 0) % lb


def _inproj_kernel(x_ref, lnw_ref, lnb_ref, w_ref, *refs, apply_ln):
    if apply_ln:
        proj_ref, xn_ref, xb_scr = refs
    else:
        proj_ref, xb_scr = refs

    @pl.when(pl.program_id(1) == 0)
    def _():
        x = x_ref[...]
        if apply_ln:
            x = _layer_norm(x, lnw_ref[...], lnb_ref[...])
            xn_ref[...] = x
        xb_scr[...] = x.astype(BF16)

    proj_ref[...] = jnp.dot(xb_scr[...], w_ref[...], preferred_element_type=F32).astype(BF16)


def _in_proj(x, lnw, lnb, w, layer, *, apply_ln, tm, tn, name):
    m = x.shape[0]
    grid = (m // tm, N_PROJ // tn)
    out_shape = [jax.ShapeDtypeStruct((m, N_PROJ), BF16)]
    out_specs = [pl.BlockSpec((tm, tn), lambda i, j: (i, j))]
    if apply_ln:
        out_shape.append(jax.ShapeDtypeStruct((m, D_MODEL), F32))
        out_specs.append(pl.BlockSpec((tm, D_MODEL), lambda i, j: (i, 0)))
    res = pl.pallas_call(
        functools.partial(_inproj_kernel, apply_ln=apply_ln),
        grid=grid,
        in_specs=[pl.BlockSpec((tm, D_MODEL), lambda i, j: (i, 0)),
                  pl.BlockSpec((1, D_MODEL), lambda i, j: (0, 0)),
                  pl.BlockSpec((1, D_MODEL), lambda i, j: (0, 0)),
                  pl.BlockSpec((None, D_MODEL, tn), lambda i, j: (layer, 0, j))],
        out_specs=out_specs,
        out_shape=out_shape,
        scratch_shapes=[pltpu.VMEM((tm, D_MODEL), BF16)],
        compiler_params=pltpu.CompilerParams(
            dimension_semantics=("parallel", "arbitrary"), vmem_limit_bytes=VMEM_LIMIT),
        name=name,
    )(x, lnw, lnb, w)
    return (res[0], res[1]) if apply_ln else (res[0], x)


def _rec_call(kernel_fn, proj, segs, consts, states, out_widths, *, layer, row0, nseq, nchunk,
              rows, chained, per_seq_state, scratch, name, tables=(), stacked=None, nsq=1):
    rb = row0 // rows
    nb = nsq if chained else rows // SAMPLE_ROWS
    grid = (nseq // nb, nchunk) if chained else (nseq // nb, 1)
    assert nseq % nb == 0 and (nsq == 1 or (chained and row0 == 0))
    out_mode = "plain" if stacked is None else ("first" if layer == 0 else "later")
    aliased = list(stacked) if out_mode == "later" else []

    def row_idx(b, c):
        return b * nchunk + c if chained else b

    def col(seg, w):
        cbi = COL[seg] // w
        if nsq > 1:
            return pl.BlockSpec((nsq, rows, w), lambda b, c: (b, c, cbi))
        return pl.BlockSpec((rows, w), lambda b, c: (rb + row_idx(b, c), cbi))

    def row_out(w):
        if nsq > 1:
            return pl.BlockSpec((nsq, rows, w), lambda b, c: (b, c, 0))
        return pl.BlockSpec((rows, w), lambda b, c: (row_idx(b, c), 0))

    def const_spec(a, lyr):
        zeros = (0,) * (a.ndim - 1)
        return pl.BlockSpec((None,) + a.shape[1:], lambda b, c: (lyr,) + zeros)

    def state_in_spec(a, lyr):
        zeros = (0,) * (a.ndim - 2)
        if per_seq_state:
            return pl.BlockSpec((None, nb) + a.shape[2:], lambda b, c: (lyr, b) + zeros)
        return pl.BlockSpec((None, 1) + a.shape[2:], lambda b, c: (lyr, 0) + zeros)

    def state_out_spec(a):
        zeros = (0,) * (a.ndim - 2)
        if out_mode == "plain":
            return pl.BlockSpec((nb,) + a.shape[2:], lambda b, c: (b,) + zeros)
        if out_mode == "first":
            return pl.BlockSpec((DEPTH, nb) + a.shape[2:], lambda b, c: (0, b) + zeros)
        return pl.BlockSpec((None, nb) + a.shape[2:], lambda b, c: (layer, b) + zeros)

    def state_out_shape(a):
        lead = (nseq,) if out_mode == "plain" else (DEPTH, nseq)
        return jax.ShapeDtypeStruct(lead + a.shape[2:], F32)

    consts = [c if isinstance(c, tuple) else (c, layer) for c in consts]
    in_specs = ([col(s, w) for s, w in segs] + [const_spec(a, lyr) for a, lyr in consts]
                + [pl.BlockSpec((rows, t.shape[1]), lambda b, c: (c, 0)) for t in tables]
                + [state_in_spec(a, lyr) for a, lyr in states]
                + [pl.BlockSpec(memory_space=pl.ANY) for _ in aliased])
    n_in = len(in_specs) - len(aliased)
    n_rows = nseq * nchunk * rows if chained else nseq * SAMPLE_ROWS
    out_specs = [row_out(w) for w in out_widths] + [state_out_spec(a) for a, _ in states]
    row_shape = (lambda w: (nseq, nchunk * rows, w)) if nsq > 1 else (lambda w: (n_rows, w))
    out_shape = ([jax.ShapeDtypeStruct(row_shape(w), BF16) for w in out_widths]
                 + [state_out_shape(a) for a, _ in states])
    if nsq > 1:
        proj = proj.reshape(nseq, nchunk * rows, proj.shape[-1])
    res = pl.pallas_call(
        functools.partial(kernel_fn, n_alias=len(aliased), out_mode=out_mode, nsq=nsq),
        grid=grid, in_specs=in_specs, out_specs=out_specs, out_shape=out_shape,
        scratch_shapes=scratch,
        input_output_aliases={n_in + i: len(out_widths) + i for i in range(len(aliased))},
        compiler_params=pltpu.CompilerParams(
            dimension_semantics=("parallel", "arbitrary"), vmem_limit_bytes=VMEM_LIMIT),
        name=name,
    )(*([proj] * len(segs)), *[a for a, _ in consts], *tables, *[a for a, _ in states], *aliased)
    if nsq > 1:
        res = ([r.reshape(n_rows, r.shape[-1]) for r in res[:len(out_widths)]]
               + list(res[len(out_widths):]))
    return res


def _state_slot(ref, out_mode):
    if out_mode != "first":
        return ref
    ref[1:] = jnp.zeros((DEPTH - 1,) + ref.shape[1:], F32)
    return ref.at[0]


def _interleave(stages):
    stages = list(stages)
    while stages:
        for gen in list(stages):
            try:
                next(gen)
            except StopIteration:
                stages.remove(gen)


def _run_chunks(chunk, row_refs, y_ref, s0_ref, so_ref, scr, chained, nsq, out_mode):
    if not chained:
        _interleave([chunk(*row_refs, y_ref, None)])
        return
    s_all, = scr

    @pl.when(pl.program_id(1) == 0)
    def _():
        for j in range(nsq):
            s_all[j] = s0_ref[0]

    view = lambda r, j: r.at[j] if nsq > 1 else r
    _interleave([chunk(*[view(r, j) for r in row_refs], view(y_ref, j), s_all.at[j])
                 for j in range(nsq)])

    @pl.when(pl.program_id(1) == pl.num_programs(1) - 1)
    def _():
        dst = _state_slot(so_ref, out_mode)
        for j in range(nsq):
            dst[j] = s_all[j]


def _gla_kernel(*refs, rows, lb, tv, chained, nsq, n_alias, out_mode):
    row_refs = refs[:5]
    w2_ref, ba_ref, wn_ref, eb_ref, sh_ref, s0_ref = refs[5:11]
    y_out, so_ref = refs[11 + n_alias:13 + n_alias]
    scr = refs[13 + n_alias:]
    nblk = rows // lb
    width = GLA_H * GLA_DK

    def chunk(q_ref, k_ref, v_ref, r_ref, a_ref, y_ref, s_scr):
        q = q_ref[...].astype(F32) * (GLA_DK ** -0.5)
        k = k_ref[...].astype(F32)
        vb = v_ref[...]
        a = _dot(a_ref[...], w2_ref[...]) + ba_ref[...]
        g = jax.nn.log_sigmoid(a) * (1.0 / GLA_GATE_NORM)
        t_in = _row_in_block(rows, lb)
        if tv < lb:
            valid = t_in >= lb - tv
            g = jnp.where(valid, g, 0.0)
            k = jnp.where(valid, k, 0.0)
        yield
        r_i, c_i = _iotas(rows)
        same = (r_i // lb) == (c_i // lb)
        sizes = []
        while 2 * GLA_BASE * 2 ** len(sizes) <= lb:
            sizes.append(2 * GLA_BASE * 2 ** len(sizes))
        sums = [same & (r_i >= c_i)] + ([same] if nblk > 1 else [])
        sums += [same & (c_i <= (r_i // sz) * sz + sz // 2 - 1) for sz in sizes]
        gsums = _dot_sel(jnp.concatenate(sums, axis=0), g)
        gcum = gsums[:rows]
        gtot = gcum[rows - 1:rows, :] if nblk == 1 else gsums[rows:2 * rows]
        g_mids = [gsums[(len(sums) - len(sizes) + i) * rows:(len(sums) - len(sizes) + i + 1) * rows]
                  for i in range(len(sizes))]
        qd = q * jnp.exp(gcum)
        kd = k * jnp.exp(gtot - gcum)
        sel = (lax.broadcasted_iota(jnp.int32, (rows, nblk * LANES), 0) // lb
               == lax.broadcasted_iota(jnp.int32, (rows, nblk * LANES), 1) // LANES)
        ds = jnp.exp(_dot_sel(sel, g, TN, sel_first=False))
        yield

        nbase = rows // GLA_BASE
        q3 = q.reshape(nbase, GLA_BASE, width)
        k3 = k.reshape(nbase, GLA_BASE, width)
        g3 = gcum.reshape(nbase, GLA_BASE, width)
        t3 = lax.broadcasted_iota(jnp.int32, (1, GLA_BASE, 1), 1)
        pieces = []
        for s in range(max(0, GLA_BASE - tv), GLA_BASE):
            dd = jnp.minimum(g3 - g3[:, s:s + 1, :], 0.0)
            w = q3 * k3[:, s:s + 1, :] * jnp.exp(dd)
            pieces.append(jnp.where(t3 >= s, w, 0.0).reshape(rows, width).astype(BF16))
            yield
        compact = jnp.dot(jnp.concatenate(pieces, axis=1), eb_ref[...], preferred_element_type=F32)
        spread = jnp.dot(compact.astype(BF16), sh_ref[...], preferred_element_type=F32)
        base_mask = (r_i // GLA_BASE) == (c_i // GLA_BASE)
        att = [jnp.where(base_mask, spread[:, h * LANES:h * LANES + rows], 0.0) for h in range(GLA_H)]
        yield

        for size, g_mid in zip(sizes, g_mids):
            second = (t_in % size) >= size // 2
            ql = jnp.where(second, q * jnp.exp(jnp.minimum(gcum - g_mid, 0.0)), 0.0)
            kl = jnp.where(second, 0.0, k * jnp.exp(jnp.minimum(g_mid - gcum, 0.0)))
            group = (r_i // size) == (c_i // size)
            for h in range(GLA_H):
                ks = slice(h * GLA_DK, (h + 1) * GLA_DK)
                att[h] = att[h] + jnp.where(group, _dot(ql[:, ks], kl[:, ks], NT), 0.0)
            yield

        so = _state_slot(so_ref, out_mode) if not chained else None
        o_heads = []
        for h in range(GLA_H):
            ks = slice(h * GLA_DK, (h + 1) * GLA_DK)
            vs = slice(h * GLA_DV, (h + 1) * GLA_DV)
            if chained:
                cur = s_scr[h]
                if rows % LANES == 0:
                    oh = _dot(jnp.concatenate([att[h].astype(BF16), qd[:, ks].astype(BF16)], axis=1),
                              jnp.concatenate([vb[:, vs], cur.astype(BF16)], axis=0))
                else:
                    oh = _dot(att[h], vb[:, vs]) + _dot(qd[:, ks], cur)
                s_scr[h] = ds[ks, :] * cur + _dot(kd[:, ks], vb[:, vs], TN)
            else:
                o_rows = []
                for b in range(nblk):
                    rs = slice(b * lb, (b + 1) * lb)
                    cur = s0_ref[b, h]
                    o_rows.append(_dot(qd[rs, ks], cur))
                    so[b, h] = ds[ks, b * LANES:(b + 1) * LANES] * cur + _dot(kd[rs, ks], vb[rs, vs], TN)
                oh = _dot(att[h], vb[:, vs]) + jnp.concatenate(o_rows, axis=0)
            o_heads.append(_rms(oh) * wn_ref[...])
            yield
        y_ref[...] = (_silu(r_ref[...].astype(F32)) * jnp.concatenate(o_heads, axis=1)).astype(BF16)

    _run_chunks(chunk, row_refs, y_out, s0_ref, so_ref, scr, chained, nsq, out_mode)


def _gla_consts(tv):
    sources = jnp.arange(max(0, GLA_BASE - tv), GLA_BASE, dtype=jnp.int32)
    row = jnp.arange(sources.shape[0] * GLA_H * GLA_DK, dtype=jnp.int32)
    target = sources[row // (GLA_H * GLA_DK)] * GLA_H + (row % (GLA_H * GLA_DK)) // GLA_DK
    eb = (target[:, None] == jnp.arange(LANES, dtype=jnp.int32)[None, :]).astype(BF16)
    r = jnp.arange(LANES, dtype=jnp.int32)[:, None]
    c = jnp.arange(GLA_H * LANES, dtype=jnp.int32)[None, :]
    sh = ((r < GLA_BASE * GLA_H) & (r % GLA_H == c // LANES)
          & (r // GLA_H == (c % LANES) % GLA_BASE)).astype(BF16)
    return (eb[None], 0), (sh[None], 0)


def _gla(proj, weights, s0, *, lb, tv, rows, chained, name, nsq=1, **kw):
    kern = functools.partial(_gla_kernel, rows=rows, lb=lb, tv=tv, chained=chained)
    scratch = [pltpu.VMEM((nsq, GLA_H, GLA_DK, GLA_DV), F32)] if chained else []
    return _rec_call(kern, proj,
                     [("gla_q", 256), ("gla_k", 256), ("gla_v", 512), ("gla_r", 512), ("gla_a", 128)],
                     list(weights) + list(_gla_consts(tv)), [s0], [MIX], rows=rows,
                     chained=chained, scratch=scratch, name=name, nsq=nsq, **kw)


def _ssd_kernel(*refs, rows, lb, tv, chained, nsq, n_alias, out_mode):
    row_refs = refs[:3]
    cw_ref, cb_ref, dtb_ref, alog_ref, dsk_ref, wn_ref, c0_ref, s0_ref = refs[3:11]
    y_out, co_ref, so_ref = refs[11 + n_alias:14 + n_alias]
    scr = refs[14 + n_alias:]
    nblk = rows // lb
    t_in = _row_in_block(rows, lb)
    valid = (t_in >= lb - tv) if tv < lb else None
    low64 = lax.broadcasted_iota(jnp.int32, (SSM_N, LANES), 1) < SSM_P

    def chunk(z_ref, x_ref, dt_ref, y_ref, s_scr, ext_scr, j):
        if chained:
            xin = x_ref[...].astype(F32)
        else:
            ext_scr[0:SUBLANES, :] = jnp.zeros((SUBLANES, SSM_CONV_DIM), F32)
            xin = jnp.where(valid, x_ref[...].astype(F32), c0_ref[...].reshape(rows, SSM_CONV_DIM))

        ext_scr[SUBLANES:SUBLANES + rows, :] = xin
        conv = cb_ref[...]
        for i in range(SSM_CONV):
            conv = conv + cw_ref[i:i + 1, :] * ext_scr[pl.ds(SUBLANES - (SSM_CONV - 1) + i, rows), :]
        act = _silu(conv)
        co = _state_slot(co_ref, out_mode)
        if chained:
            co[j] = ext_scr[rows:rows + SUBLANES, :]
            ext_scr[0:SUBLANES, :] = ext_scr[rows:rows + SUBLANES, :]
        else:
            co[...] = xin.reshape(nblk, lb, SSM_CONV_DIM)
        yield

        dt = jax.nn.softplus(dt_ref[...].astype(F32) + dtb_ref[...])
        gdt = dt * (-jnp.exp(alog_ref[...]))
        if valid is not None:
            gdt = jnp.where(valid, gdt, 0.0)
        r_i, c_i = _iotas(rows)
        same = (r_i // lb) == (c_i // lb)
        causal = same & (r_i >= c_i)
        gcum = _dot_sel(causal, gdt)
        if rows < LANES:
            gsq = jnp.concatenate([gcum, jnp.zeros((LANES - rows, LANES), F32)], axis=0)
            gcum_t = gsq.T[:, :rows]
        else:
            gcum_t = gcum.T
        gtot = gcum[rows - 1:rows, :] if nblk == 1 else _dot_sel(same, gdt)

        lane = lax.broadcasted_iota(jnp.int32, (rows, LANES), 1)
        low = lane < SSM_P
        bcol = act[:, MIX:MIX + LANES]
        ccol = act[:, MIX + LANES:MIX + 2 * LANES]
        bswap = pltpu.roll(bcol, SSM_N, axis=1)
        cswap = pltpu.roll(ccol, SSM_N, axis=1)
        b2 = (jnp.where(low, bcol, bswap), jnp.where(low, bswap, bcol))
        c2 = (jnp.where(low, ccol, cswap), jnp.where(low, cswap, ccol))
        cb = (_dot(jnp.where(low, ccol, 0.0), bcol, NT), _dot(jnp.where(low, 0.0, ccol), bcol, NT))

        def pair_lanes(x, p):
            return jnp.where(low[:x.shape[0]], x[:, 2 * p:2 * p + 1], x[:, 2 * p + 1:2 * p + 2])

        yield
        so = _state_slot(so_ref, out_mode) if not chained else None
        y_pairs = []
        for p in range(SSM_PAIRS):
            gi = p // (SSM_PAIRS // SSM_G)
            decs = []
            for h in (2 * p, 2 * p + 1):
                diff = jnp.minimum(gcum[:, h:h + 1] - gcum_t[h:h + 1, :], 0.0)
                decs.append(cb[gi] * jnp.where(causal, jnp.exp(diff), 0.0))
            xp = act[:, p * LANES:(p + 1) * LANES]
            vp = xp * pair_lanes(dt, p)
            if valid is not None:
                vp = jnp.where(valid, vp, 0.0)
            vbd = jnp.concatenate([jnp.where(low, vp, 0.0), jnp.where(low, 0.0, vp)], axis=0)
            g2 = pair_lanes(gcum, p)
            ge2 = pair_lanes(gtot, p)
            cin = c2[gi] * jnp.exp(g2)
            bout = b2[gi] * jnp.exp(ge2 - g2)
            if chained:
                cur = s_scr[p]
                bd = jnp.concatenate([jnp.where(low64, cur, 0.0), jnp.where(low64, 0.0, cur)], axis=0)
                if rows % LANES == 0:
                    o = _dot(jnp.concatenate(decs + [cin], axis=1).astype(BF16),
                             jnp.concatenate([vbd.astype(BF16), bd.astype(BF16)], axis=0))
                else:
                    o = _dot(jnp.concatenate(decs, axis=1), vbd) + _dot(cin, bd)
                u = _dot(bout, vp, TN)
                s_scr[p] = jnp.exp(ge2) * cur + jnp.where(low64, u[:SSM_N, :], u[SSM_N:, :])
            else:
                o = _dot(jnp.concatenate(decs, axis=1), vbd)
                halves = []
                for odd, (ci, bo, vv) in enumerate(
                        ((cin, bout, vp),
                         tuple(pltpu.roll(t, SSM_P, axis=1) for t in (cin, bout, vp)))):
                    h = 2 * p + odd
                    o_rows = []
                    for b in range(nblk):
                        rs = slice(b * lb, (b + 1) * lb)
                        cur = s0_ref[b, h]
                        o_rows.append(_dot(ci[rs, :SSM_N], cur))
                        so[b, h] = (jnp.exp(gtot[b * lb:b * lb + 1, h:h + 1]) * cur
                                    + _dot(bo[rs, :SSM_N], vv[rs, :SSM_P], TN))
                    halves.append(jnp.concatenate(o_rows, axis=0))
                o = o + jnp.concatenate(halves, axis=1)
            y_pairs.append(o + dsk_ref[:, p * LANES:(p + 1) * LANES] * xp)
            yield
        y = jnp.concatenate(y_pairs, axis=1) * _silu(z_ref[...].astype(F32))
        half = MIX // SSM_G
        y = jnp.concatenate([_rms(y[:, gi * half:(gi + 1) * half]) for gi in range(SSM_G)], axis=1)
        y_ref[...] = (y * wn_ref[...]).astype(BF16)

    if not chained:
        _interleave([chunk(*row_refs, y_out, None, scr[0], 0)])
        return
    s_all, ext_all = scr

    @pl.when(pl.program_id(1) == 0)
    def _():
        for j in range(nsq):
            for p in range(SSM_PAIRS):
                s_all[j, p] = jnp.concatenate([s0_ref[0, 2 * p], s0_ref[0, 2 * p + 1]], axis=1)
            ext_all[j, 0:SUBLANES, :] = c0_ref[0]

    view = lambda r, j: r.at[j] if nsq > 1 else r
    _interleave([chunk(*[view(r, j) for r in row_refs], view(y_out, j), s_all.at[j],
                       ext_all.at[j], j) for j in range(nsq)])

    @pl.when(pl.program_id(1) == pl.num_programs(1) - 1)
    def _():
        dst = _state_slot(so_ref, out_mode)
        for j in range(nsq):
            for p in range(SSM_PAIRS):
                dst[j, 2 * p] = s_all[j, p][:, :SSM_P]
                dst[j, 2 * p + 1] = s_all[j, p][:, SSM_P:]


def _ssd(proj, weights, c0, s0, *, lb, tv, rows, chained, name, nsq=1, **kw):
    kern = functools.partial(_ssd_kernel, rows=rows, lb=lb, tv=tv, chained=chained)
    if chained:
        scratch = [pltpu.VMEM((nsq, SSM_PAIRS, SSM_N, LANES), F32),
                   pltpu.VMEM((nsq, SUBLANES + rows, SSM_CONV_DIM), F32)]
    else:
        scratch = [pltpu.VMEM((SUBLANES + rows, SSM_CONV_DIM), F32)]
    return _rec_call(kern, proj, [("ssm_z", 512), ("ssm_xbc", 768), ("ssm_dt", 128)],
                     weights, [c0, s0], [MIX], rows=rows, chained=chained, scratch=scratch,
                     name=name, nsq=nsq, **kw)


def _ret_kernel(*refs, rows, lb, tv, chained, nsq, n_alias, out_mode):
    row_refs = refs[:4]
    lg_ref, cos_ref, sin_ref, s0_ref = refs[4:8]
    y_out, so_ref = refs[8 + n_alias:10 + n_alias]
    scr = refs[10 + n_alias:]
    nblk = rows // lb
    t_col = _row_in_block(rows, lb)
    s_row = lax.broadcasted_iota(jnp.int32, (1, rows), 1) % lb
    n_col = jnp.maximum(t_col - (lb - tv) + 1, 0).astype(F32)
    n_row = jnp.maximum(s_row - (lb - tv) + 1, 0).astype(F32)

    def decay_matrix(h):
        r_i, c_i = _iotas(rows)
        causal = ((r_i // lb) == (c_i // lb)) & (r_i >= c_i)
        diff = (n_col - n_row) * lg_ref[:, h:h + 1]
        return jnp.where(causal, jnp.exp(jnp.minimum(diff, 0.0)), 0.0)

    if chained:
        dec_scr = scr[1]

        @pl.when(pl.program_id(1) == 0)
        def _():
            for h in range(RET_H):
                dec_scr[h] = decay_matrix(h)

    width = RET_H * RET_DK
    lane = lax.broadcasted_iota(jnp.int32, (rows, width), 1)
    first_half = (lane % RET_DK) < (RET_DK // 2)
    cos = cos_ref[...]
    sin = jnp.where(first_half, -sin_ref[...], sin_ref[...])

    def rope(x):
        partner = jnp.where(first_half, pltpu.roll(x, width - RET_DK // 2, axis=1),
                            pltpu.roll(x, RET_DK // 2, axis=1))
        return x * cos + partner * sin

    def chunk(q_ref, k_ref, v_ref, g_ref, y_ref, s_scr):
        q = rope(q_ref[...].astype(F32))
        k = rope(k_ref[...].astype(F32)) * (RET_DK ** -0.5)
        v = v_ref[...]
        if tv < lb:
            k = jnp.where(t_col >= lb - tv, k, 0.0)
        yield

        so = _state_slot(so_ref, out_mode) if not chained else None
        outs = []
        for h in range(RET_H):
            lg = lg_ref[:, h:h + 1]
            gc = n_col * lg
            ks = slice(h * RET_DK, (h + 1) * RET_DK)
            vs = slice(h * RET_DV, (h + 1) * RET_DV)
            att = _dot(q[:, ks], k[:, ks], NT) * (dec_scr[h] if chained else decay_matrix(h))
            ge = float(tv) * lg
            qin = q[:, ks] * jnp.exp(gc)
            kout = k[:, ks] * jnp.exp(ge - gc)
            if chained:
                cur = s_scr[h]
                if rows % LANES == 0:
                    oh = _dot(jnp.concatenate([att.astype(BF16), qin.astype(BF16)], axis=1),
                              jnp.concatenate([v[:, vs], cur.astype(BF16)], axis=0))
                else:
                    oh = _dot(att, v[:, vs]) + _dot(qin, cur)
                s_scr[h] = jnp.exp(ge) * cur + _dot(kout, v[:, vs], TN)
            else:
                oh = _dot(att, v[:, vs])
                o_rows = []
                for b in range(nblk):
                    rs = slice(b * lb, (b + 1) * lb)
                    cur = s0_ref[b, h]
                    o_rows.append(_dot(qin[rs], cur))
                    so[b, h] = jnp.exp(ge) * cur + _dot(kout[rs], v[rs, vs], TN)
                oh = oh + jnp.concatenate(o_rows, axis=0)
            outs.append(_rms(oh))
            yield
        y_ref[...] = (_silu(g_ref[...].astype(F32)) * jnp.concatenate(outs, axis=1)).astype(BF16)

    _run_chunks(chunk, row_refs, y_out, s0_ref, so_ref, scr[:1], chained, nsq, out_mode)


def _ret(proj, lg, cos, sin, s0, *, lb, tv, rows, chained, name, nsq=1, **kw):
    kern = functools.partial(_ret_kernel, rows=rows, lb=lb, tv=tv, chained=chained)
    scratch = ([pltpu.VMEM((nsq, RET_H, RET_DK, RET_DV), F32), pltpu.VMEM((RET_H, rows, rows), F32)]
               if chained else [])
    return _rec_call(kern, proj, [("ret_q", 256), ("ret_k", 256), ("ret_v", 512), ("ret_g", 512)],
                     [(lg, 0)], [s0], [MIX], rows=rows, chained=chained, scratch=scratch, name=name,
                     tables=(cos, sin), nsq=nsq, **kw)


def _dense2_kernel(x_ref, gate_ref, yg_ref, ys_ref, yr_ref, wg_ref, ws_ref, wr_ref, wo_ref,
                   l1w_ref, l1b_ref, w1_ref, b1_ref, w2_ref, b2_ref, l2w_ref, l2b_ref, o_ref,
                   *, ff_chunk):
    gate = lambda i: jax.nn.sigmoid(gate_ref[:, i * D_MODEL:(i + 1) * D_MODEL].astype(F32))
    merged = (gate(0) * _dot(yg_ref[...], wg_ref[...]) + gate(1) * _dot(ys_ref[...], ws_ref[...])
              + gate(2) * _dot(yr_ref[...], wr_ref[...]))
    mix = _dot(merged, wo_ref[...])
    h = _layer_norm(ALPHA * x_ref[...] + mix, l1w_ref[...], l1b_ref[...])
    hb = h.astype(BF16)
    ff = jnp.zeros_like(h) + b2_ref[...]
    for c0 in range(0, D_FF, ff_chunk):
        hid = jnp.dot(hb, w1_ref[:, c0:c0 + ff_chunk], preferred_element_type=F32)
        hid = jnp.square(jnp.maximum(hid + b1_ref[:, c0:c0 + ff_chunk], 0.0))
        ff = ff + _dot(hid, w2_ref[c0:c0 + ff_chunk, :])
    o_ref[...] = _layer_norm(ALPHA * h + ff, l2w_ref[...], l2b_ref[...])


def _dense2(x, proj, yg, ys, yr, wl, layer, *, tm, name):
    m = x.shape[0]
    row = lambda w: pl.BlockSpec((tm, w), lambda i: (i, 0))
    const = lambda r, w: pl.BlockSpec((None, r, w), lambda i: (layer, 0, 0),
                                      pipeline_mode=pl.Buffered(1))
    return pl.pallas_call(
        functools.partial(_dense2_kernel, ff_chunk=1024),
        grid=(m // tm,),
        in_specs=[row(D_MODEL), row(3 * D_MODEL), row(MIX), row(MIX), row(MIX),
                  const(MIX, D_MODEL), const(MIX, D_MODEL), const(MIX, D_MODEL),
                  const(D_MODEL, D_MODEL), const(1, D_MODEL), const(1, D_MODEL),
                  const(D_MODEL, D_FF), const(1, D_FF), const(D_FF, D_MODEL), const(1, D_MODEL),
                  const(1, D_MODEL), const(1, D_MODEL)],
        out_specs=row(D_MODEL),
        out_shape=jax.ShapeDtypeStruct((m, D_MODEL), F32),
        compiler_params=pltpu.CompilerParams(
            dimension_semantics=("parallel",), vmem_limit_bytes=VMEM_LIMIT),
        name=name,
    )(x, proj, yg, ys, yr, wl["w_gla_out"], wl["w_ssm_out"], wl["w_ret_out"], wl["w_o"],
      wl["ln1_w"], wl["ln1_b"], wl["w_ff1"], wl["b_ff1"], wl["w_ff2"], wl["b_ff2"],
      wl["ln2_w"], wl["ln2_b"])


def _pick_tile(n, pref):
    t = min(n, pref)
    while n % t or t % SUBLANES:
        t -= 1
    return t


def _rearrange_w_in(w):
    offs = [0]
    for s in SPLIT_SIZES:
        offs.append(offs[-1] + s)
    names = ("gla_q", "gla_k", "gla_v", "gla_r", "gla_a", "ssm_z", "ssm_xbc", "ssm_dt",
             "ret_q", "ret_k", "ret_v", "ret_g", "gates")
    seg = {n: w[..., offs[i]:offs[i + 1]].astype(BF16) for i, n in enumerate(names)}
    pad = lambda a: jnp.pad(a, ((0, 0),) * (a.ndim - 1) + ((0, LANES - a.shape[-1]),))
    order = sorted(COL, key=COL.get)
    parts = [pad(seg[n]) if n in ("gla_a", "ssm_dt") else seg[n] for n in order]
    return jnp.concatenate(parts, axis=-1)


def _rope_tables(pos):
    half = RET_DK // 2
    inv_freq = ROPE_BASE ** (-jnp.arange(half, dtype=F32) / half)
    ang = pos.astype(F32)[:, None] * inv_freq[None, :]
    cos = jnp.tile(jnp.cos(ang), (1, 2 * RET_H))
    sin = jnp.tile(jnp.sin(ang), (1, 2 * RET_H))
    return cos, sin


def kernel(x_prompt, x_sample, state_gla, state_ssm, state_conv, state_ret, meta_tokens,
           ln_in_w, ln_in_b, w_in, w_gla_a2, b_gla_a, w_gla_norm, conv_w, conv_b, dt_bias,
           a_log, d_skip, w_ssm_norm, w_gla_out, w_ssm_out, w_ret_out, w_o, ln1_w, ln1_b,
           w_ff1, b_ff1, w_ff2, b_ff2, ln2_w, ln2_b):
    bp, tp, d = x_prompt.shape
    bs, ts, _ = x_sample.shape
    assert d == D_MODEL and tp % CHUNK == 0 and w_in.shape[0] == DEPTH
    assert SSM_CONV - 1 <= SAMPLE_ROWS - ts
    nchunk = tp // CHUNK
    pad_rows = SAMPLE_ROWS - ts

    x_body = x_prompt.reshape(bp * tp, d)
    n_sample = bs * SAMPLE_ROWS
    rows_s = min(CHUNK, n_sample)
    assert n_sample % rows_s == 0 and n_sample % N_META == 0
    n_small = n_sample + CHUNK
    x_small = jnp.concatenate(
        [jnp.pad(x_sample, ((0, 0), (pad_rows, 0), (0, 0))).reshape(n_sample, d),
         meta_tokens.astype(F32), jnp.zeros((CHUNK - N_META, d), F32)], axis=0)

    cos_b, sin_b = _rope_tables(N_META + jnp.arange(tp, dtype=jnp.int32))
    cos_m, sin_m = _rope_tables(jnp.arange(N_META, dtype=jnp.int32))
    pos_tile = PAST_LEN - pad_rows + jnp.arange(SAMPLE_ROWS, dtype=jnp.int32)
    cos_s, sin_s = _rope_tables(jnp.tile(pos_tile, rows_s // SAMPLE_ROWS))
    lg_ret = jnp.pad(jnp.log1p(-jnp.exp2(-5.0 - jnp.arange(RET_H, dtype=F32))),
                     (0, LANES - RET_H)).reshape(1, 1, LANES)

    zero_gla = (jnp.zeros((1, 1, GLA_H, GLA_DK, GLA_DV), F32), 0)
    zero_ssm = (jnp.zeros((1, 1, SSM_H, SSM_N, SSM_P), F32), 0)
    zero_conv = (jnp.zeros((1, 1, SUBLANES, SSM_CONV_DIM), F32), 0)
    zero_ret = (jnp.zeros((1, 1, RET_H, RET_DK, RET_DV), F32), 0)
    from_meta = lambda a: (a[None], 0)
    conv_in = jnp.pad(state_conv, ((0, 0), (0, 0), (pad_rows - (SSM_CONV - 1), ts), (0, 0)))

    rowvec = lambda a: a.reshape(DEPTH, 1, -1)
    lane_pad = lambda a: jnp.pad(a, ((0, 0), (0, LANES - a.shape[1]))).reshape(DEPTH, 1, LANES)
    w_in_all = _rearrange_w_in(w_in)
    wl = dict(w_gla_out=w_gla_out.astype(BF16), w_ssm_out=w_ssm_out.astype(BF16),
              w_ret_out=w_ret_out.astype(BF16), w_o=w_o.astype(BF16),
              ln1_w=rowvec(ln1_w), ln1_b=rowvec(ln1_b),
              w_ff1=w_ff1.astype(BF16), b_ff1=rowvec(b_ff1),
              w_ff2=w_ff2.astype(BF16), b_ff2=rowvec(b_ff2),
              ln2_w=rowvec(ln2_w), ln2_b=rowvec(ln2_b))
    gla_w = [jnp.pad(w_gla_a2, ((0, 0), (0, LANES - GLA_RANK), (0, 0))), rowvec(b_gla_a),
             rowvec(w_gla_norm)]
    ssd_w = [conv_w, rowvec(conv_b), lane_pad(dt_bias), lane_pad(a_log),
             rowvec(jnp.repeat(d_skip, SSM_P, axis=1)), rowvec(w_ssm_norm)]
    ln_w, ln_b = ln_in_w.reshape(1, -1), ln_in_b.reshape(1, -1)

    tm_in = _pick_tile(bp * tp, 2048)
    tm_d2 = _pick_tile(bp * tp, 512)
    tm_d2s = _pick_tile(n_small, 384)
    tn = 1536
    meta = dict(row0=n_sample, nseq=1, nchunk=1, rows=N_META, lb=N_META, tv=N_META, chained=True,
                per_seq_state=False)
    body = dict(row0=0, nseq=bp, nchunk=nchunk, rows=CHUNK, lb=CHUNK, tv=CHUNK, chained=True,
                per_seq_state=False, nsq=(2 if bp % 2 == 0 else 1))
    samp = dict(row0=0, nseq=bs, nchunk=1, rows=rows_s, lb=SAMPLE_ROWS, tv=ts, chained=False,
                per_seq_state=True)

    names = ("gla_p", "gla_s", "ssm_p", "ssm_s", "conv_p", "conv_s", "ret_p", "ret_s")
    st = {k: None for k in names}
    stk = lambda *keys: [] if st[keys[0]] is None else [st[k] for k in keys]
    xb, xs = x_body, x_small
    for l in range(DEPTH):
        proj_s, xs = _in_proj(xs, ln_w, ln_b, w_in_all, l, apply_ln=(l == 0), tm=n_small, tn=tn,
                              name=f"inproj_small_{l}")
        yg_m, sg_m = _gla(proj_s, gla_w, zero_gla, layer=l, name=f"gla_meta_{l}", **meta)
        ys_m, cv_m, ss_m = _ssd(proj_s, ssd_w, zero_conv, zero_ssm, layer=l,
                                name=f"ssd_meta_{l}", **meta)
        yr_m, sr_m = _ret(proj_s, lg_ret, cos_m, sin_m, zero_ret, layer=l,
                          name=f"ret_meta_{l}", **meta)
        yg_s, st["gla_s"] = _gla(proj_s, gla_w, (state_gla, l), layer=l, stacked=stk("gla_s"),
                                 name=f"gla_sample_{l}", **samp)
        ys_s, st["conv_s"], st["ssm_s"] = _ssd(proj_s, ssd_w, (conv_in, l), (state_ssm, l), layer=l,
                                               stacked=stk("conv_s", "ssm_s"),
                                               name=f"ssd_sample_{l}", **samp)
        yr_s, st["ret_s"] = _ret(proj_s, lg_ret, cos_s, sin_s, (state_ret, l), layer=l,
                                 stacked=stk("ret_s"), name=f"ret_sample_{l}", **samp)
        zpad = jnp.zeros((CHUNK - N_META, MIX), BF16)
        yg = jnp.concatenate([yg_s, yg_m, zpad], axis=0)
        ys = jnp.concatenate([ys_s, ys_m, zpad], axis=0)
        yr = jnp.concatenate([yr_s, yr_m, zpad], axis=0)
        xs = _dense2(xs, proj_s, yg, ys, yr, wl, l, tm=tm_d2s, name=f"dense2_small_{l}")

        proj_b, xb = _in_proj(xb, ln_w, ln_b, w_in_all, l, apply_ln=(l == 0),
                              tm=(tm_in // 2 if l == 0 else tm_in), tn=tn,
                              name=f"inproj_body_{l}")
        yg_b, st["gla_p"] = _gla(proj_b, gla_w, from_meta(sg_m), layer=l, stacked=stk("gla_p"),
                                 name=f"gla_body_{l}", **body)
        ys_b, st["conv_p"], st["ssm_p"] = _ssd(proj_b, ssd_w, from_meta(cv_m), from_meta(ss_m),
                                               layer=l, stacked=stk("conv_p", "ssm_p"),
                                               name=f"ssd_body_{l}", **body)
        yr_b, st["ret_p"] = _ret(proj_b, lg_ret, cos_b, sin_b, from_meta(sr_m), layer=l,
                                 stacked=stk("ret_p"), name=f"ret_body_{l}", **body)
        xb = _dense2(xb, proj_b, yg_b, ys_b, yr_b, wl, l, tm=tm_d2, name=f"dense2_body_{l}")

    y_prompt = xb.reshape(bp, tp, d)
    y_sample = xs[:n_sample].reshape(bs, SAMPLE_ROWS, d)[:, pad_rows:]
    tail3 = lambda c: c[:, :, SUBLANES - (SSM_CONV - 1):, :]
    return (y_prompt, y_sample, st["gla_p"], st["gla_s"], st["ssm_p"], st["ssm_s"],
            tail3(st["conv_p"]), tail3(st["conv_s"]), st["ret_p"], st["ret_s"])
```

```python
import functools

import jax
import jax.numpy as jnp
from jax import lax
from jax.experimental import pallas as pl
from jax.experimental.pallas import tpu as pltpu

F32 = jnp.float32
BF16 = jnp.bfloat16

D_MODEL = 1024
DEPTH = 2
N_META = 16
MIX = 512
GLA_H, GLA_DK, GLA_DV, GLA_RANK = 4, 64, 128, 16
GLA_GATE_NORM = 16.0
SSM_H, SSM_P, SSM_N, SSM_G, SSM_CONV = 8, 64, 64, 2, 4
SSM_CONV_DIM = MIX + 2 * SSM_G * SSM_N
SSM_PAIRS = SSM_H // 2
RET_H, RET_DK, RET_DV = 4, 64, 128
ROPE_BASE = 10000.0
D_FF = 4 * D_MODEL
ALPHA = (2 * DEPTH) ** 0.25
PAST_LEN = 16384
SPLIT_SIZES = (256, 256, 512, 512, 16, 512, 768, 8, 256, 256, 512, 512, 3072)

LANES = 128
SUBLANES = 8
VMEM_LIMIT = 56 * 1024 * 1024

COL = dict(gates=0, gla_v=3072, gla_r=3584, ssm_z=4096, ret_v=4608, ret_g=5120,
           gla_q=5632, gla_k=5888, ssm_xbc=6144, ret_q=6912, ret_k=7168, gla_a=7424, ssm_dt=7552)
N_PROJ = 7680
SAMPLE_ROWS = SUBLANES
GLA_BASE = SUBLANES
CHUNK = 128

NN = (((1,), (0,)), ((), ()))
NT = (((1,), (1,)), ((), ()))
TN = (((0,), (0,)), ((), ()))


def _dot(a, b, dims=NN):
    return lax.dot_general(a.astype(BF16), b.astype(BF16), dims, preferred_element_type=F32)


def _dot_f32(a, b, dims=NN):
    return lax.dot_general(a, b, dims, precision=lax.Precision.HIGHEST,
                           preferred_element_type=F32)


def _dot_sel(sel, x, dims=NN, sel_first=True):
    hi = x.astype(BF16)
    r1 = x - hi.astype(F32)
    mid = r1.astype(BF16)
    lo = (r1 - mid.astype(F32)).astype(BF16)
    sb = sel.astype(BF16)
    out = None
    for part in (hi, mid, lo):
        ops = (sb, part) if sel_first else (part, sb)
        term = lax.dot_general(*ops, dims, preferred_element_type=F32)
        out = term if out is None else out + term
    return out


def _layer_norm(x, w, b):
    mu = jnp.mean(x, axis=-1, keepdims=True)
    xc = x - mu
    var = jnp.mean(xc * xc, axis=-1, keepdims=True)
    return xc * lax.rsqrt(var + 1e-5) * w + b


def _rms(x):
    return x * lax.rsqrt(jnp.mean(x * x, axis=-1, keepdims=True) + 1e-6)


def _silu(x):
    return x * jax.nn.sigmoid(x)


def _iotas(rows):
    return (lax.broadcasted_iota(jnp.int32, (rows, rows), 0),
            lax.broadcasted_iota(jnp.int32, (rows, rows), 1))


def _row_in_block(rows, lb):
    return lax.broadcasted_iota(jnp.int32, (rows, 1), 0) % lb


def _inproj_kernel(x_ref, lnw_ref, lnb_ref, w_ref, *refs, apply_ln):
    if apply_ln:
        proj_ref, xn_ref, xb_scr = refs
    else:
        proj_ref, xb_scr = refs

    @pl.when(pl.program_id(1) == 0)
    def _():
        x = x_ref[...]
        if apply_ln:
            x = _layer_norm(x, lnw_ref[...], lnb_ref[...])
            xn_ref[...] = x
        xb_scr[...] = x.astype(BF16)

    proj_ref[...] = jnp.dot(xb_scr[...], w_ref[...], preferred_element_type=F32).astype(BF16)


def _in_proj(x, lnw, lnb, w, layer, *, apply_ln, tm, tn, name):
    m = x.shape[0]
    grid = (m // tm, N_PROJ // tn)
    out_shape = [jax.ShapeDtypeStruct((m, N_PROJ), BF16)]
    out_specs = [pl.BlockSpec((tm, tn), lambda i, j: (i, j))]
    if apply_ln:
        out_shape.append(jax.ShapeDtypeStruct((m, D_MODEL), F32))
        out_specs.append(pl.BlockSpec((tm, D_MODEL), lambda i, j: (i, 0)))
    res = pl.pallas_call(
        functools.partial(_inproj_kernel, apply_ln=apply_ln),
        grid=grid,
        in_specs=[pl.BlockSpec((tm, D_MODEL), lambda i, j: (i, 0)),
                  pl.BlockSpec((1, D_MODEL), lambda i, j: (0, 0)),
                  pl.BlockSpec((1, D_MODEL), lambda i, j: (0, 0)),
                  pl.BlockSpec((None, D_MODEL, tn), lambda i, j: (layer, 0, j))],
        out_specs=out_specs,
        out_shape=out_shape,
        scratch_shapes=[pltpu.VMEM((tm, D_MODEL), BF16)],
        compiler_params=pltpu.CompilerParams(
            dimension_semantics=("parallel", "arbitrary"), vmem_limit_bytes=VMEM_LIMIT),
        name=name,
    )(x, lnw, lnb, w)
    return (res[0], res[1]) if apply_ln else (res[0], x)


def _rec_call(kernel_fn, proj, segs, consts, states, out_widths, *, layer, row0, nseq, nchunk,
              rows, chained, per_seq_state, scratch, name, tables=(), stacked=None, nsq=1):
    rb = row0 // rows
    nb = nsq if chained else rows // SAMPLE_ROWS
    grid = (nseq // nb, nchunk) if chained else (nseq // nb, 1)
    assert nseq % nb == 0 and (nsq == 1 or (chained and row0 == 0))
    out_mode = "plain" if stacked is None else ("first" if layer == 0 else "later")
    aliased = list(stacked) if out_mode == "later" else []

    def row_idx(b, c):
        return b * nchunk + c if chained else b

    def col(seg, w):
        cbi = COL[seg] // w
        if nsq > 1:
            return pl.BlockSpec((nsq, rows, w), lambda b, c: (b, c, cbi))
        return pl.BlockSpec((rows, w), lambda b, c: (rb + row_idx(b, c), cbi))

    def row_out(w):
        if nsq > 1:
            return pl.BlockSpec((nsq, rows, w), lambda b, c: (b, c, 0))
        return pl.BlockSpec((rows, w), lambda b, c: (row_idx(b, c), 0))

    def const_spec(a, lyr):
        zeros = (0,) * (a.ndim - 1)
        return pl.BlockSpec((None,) + a.shape[1:], lambda b, c: (lyr,) + zeros)

    def state_in_spec(a, lyr):
        zeros = (0,) * (a.ndim - 2)
        if per_seq_state:
            return pl.BlockSpec((None, nb) + a.shape[2:], lambda b, c: (lyr, b) + zeros)
        return pl.BlockSpec((None, 1) + a.shape[2:], lambda b, c: (lyr, 0) + zeros)

    def state_out_spec(a):
        zeros = (0,) * (a.ndim - 2)
        if out_mode == "plain":
            return pl.BlockSpec((nb,) + a.shape[2:], lambda b, c: (b,) + zeros)
        if out_mode == "first":
            return pl.BlockSpec((DEPTH, nb) + a.shape[2:], lambda b, c: (0, b) + zeros)
        return pl.BlockSpec((None, nb) + a.shape[2:], lambda b, c: (layer, b) + zeros)

    def state_out_shape(a):
        lead = (nseq,) if out_mode == "plain" else (DEPTH, nseq)
        return jax.ShapeDtypeStruct(lead + a.shape[2:], F32)

    consts = [c if isinstance(c, tuple) else (c, layer) for c in consts]
    in_specs = ([col(s, w) for s, w in segs] + [const_spec(a, lyr) for a, lyr in consts]
                + [pl.BlockSpec((rows, t.shape[1]), lambda b, c: (c, 0)) for t in tables]
                + [state_in_spec(a, lyr) for a, lyr in states]
                + [pl.BlockSpec(memory_space=pl.ANY) for _ in aliased])
    n_in = len(in_specs) - len(aliased)
    n_rows = nseq * nchunk * rows if chained else nseq * SAMPLE_ROWS
    out_specs = [row_out(w) for w in out_widths] + [state_out_spec(a) for a, _ in states]
    row_shape = (lambda w: (nseq, nchunk * rows, w)) if nsq > 1 else (lambda w: (n_rows, w))
    out_shape = ([jax.ShapeDtypeStruct(row_shape(w), BF16) for w in out_widths]
                 + [state_out_shape(a) for a, _ in states])
    if nsq > 1:
        proj = proj.reshape(nseq, nchunk * rows, proj.shape[-1])
    res = pl.pallas_call(
        functools.partial(kernel_fn, n_alias=len(aliased), out_mode=out_mode, nsq=nsq),
        grid=grid, in_specs=in_specs, out_specs=out_specs, out_shape=out_shape,
        scratch_shapes=scratch,
        input_output_aliases={n_in + i: len(out_widths) + i for i in range(len(aliased))},
        compiler_params=pltpu.CompilerParams(
            dimension_semantics=("parallel", "arbitrary"), vmem_limit_bytes=VMEM_LIMIT),
        name=name,
    )(*([proj] * len(segs)), *[a for a, _ in consts], *tables, *[a for a, _ in states], *aliased)
    if nsq > 1:
        res = ([r.reshape(n_rows, r.shape[-1]) for r in res[:len(out_widths)]]
               + list(res[len(out_widths):]))
    return res


def _state_slot(ref, out_mode):
    if out_mode != "first":
        return ref
    ref[1:] = jnp.zeros((DEPTH - 1,) + ref.shape[1:], F32)
    return ref.at[0]


def _interleave(stages):
    stages = list(stages)
    while stages:
        for gen in list(stages):
            try:
                next(gen)
            except StopIteration:
                stages.remove(gen)


def _run_chunks(chunk, row_refs, y_ref, s0_ref, so_ref, scr, chained, nsq, out_mode):
    if not chained:
        _interleave([chunk(*row_refs, y_ref, None)])
        return
    s_all, = scr

    @pl.when(pl.program_id(1) == 0)
    def _():
        for j in range(nsq):
            s_all[j] = s0_ref[0]

    view = lambda r, j: r.at[j] if nsq > 1 else r
    _interleave([chunk(*[view(r, j) for r in row_refs], view(y_ref, j), s_all.at[j])
                 for j in range(nsq)])

    @pl.when(pl.program_id(1) == pl.num_programs(1) - 1)
    def _():
        dst = _state_slot(so_ref, out_mode)
        for j in range(nsq):
            dst[j] = s_all[j]


def _gla_kernel(*refs, rows, lb, tv, chained, nsq, n_alias, out_mode):
    row_refs = refs[:5]
    w2_ref, ba_ref, wn_ref, eb_ref, sh_ref, s0_ref = refs[5:11]
    y_out, so_ref = refs[11 + n_alias:13 + n_alias]
    scr = refs[13 + n_alias:]
    nblk = rows // lb
    width = GLA_H * GLA_DK

    def chunk(q_ref, k_ref, v_ref, r_ref, a_ref, y_ref, s_scr):
        q = q_ref[...].astype(F32) * (GLA_DK ** -0.5)
        k = k_ref[...].astype(F32)
        vb = v_ref[...]
        a = _dot(a_ref[...], w2_ref[...]) + ba_ref[...]
        g = jax.nn.log_sigmoid(a) * (1.0 / GLA_GATE_NORM)
        t_in = _row_in_block(rows, lb)
        if tv < lb:
            valid = t_in >= lb - tv
            g = jnp.where(valid, g, 0.0)
            k = jnp.where(valid, k, 0.0)
        yield
        r_i, c_i = _iotas(rows)
        same = (r_i // lb) == (c_i // lb)
        sizes = []
        while 2 * GLA_BASE * 2 ** len(sizes) <= lb:
            sizes.append(2 * GLA_BASE * 2 ** len(sizes))
        sums = [same & (r_i >= c_i)] + ([same] if nblk > 1 else [])
        sums += [same & (c_i <= (r_i // sz) * sz + sz // 2 - 1) for sz in sizes]
        gsums = _dot_sel(jnp.concatenate(sums, axis=0), g)
        gcum = gsums[:rows]
        gtot = gcum[rows - 1:rows, :] if nblk == 1 else gsums[rows:2 * rows]
        g_mids = [gsums[(len(sums) - len(sizes) + i) * rows:(len(sums) - len(sizes) + i + 1) * rows]
                  for i in range(len(sizes))]
        qd = q * jnp.exp(gcum)
        kd = k * jnp.exp(gtot - gcum)
        sel = (lax.broadcasted_iota(jnp.int32, (rows, nblk * LANES), 0) // lb
               == lax.broadcasted_iota(jnp.int32, (rows, nblk * LANES), 1) // LANES)
        ds = jnp.exp(_dot_sel(sel, g, TN, sel_first=False))
        yield

        nbase = rows // GLA_BASE
        q3 = q.reshape(nbase, GLA_BASE, width)
        k3 = k.reshape(nbase, GLA_BASE, width)
        g3 = gcum.reshape(nbase, GLA_BASE, width)
        t3 = lax.broadcasted_iota(jnp.int32, (1, GLA_BASE, 1), 1)
        pieces = []
        for s in range(max(0, GLA_BASE - tv), GLA_BASE):
            dd = jnp.minimum(g3 - g3[:, s:s + 1, :], 0.0)
            w = q3 * k3[:, s:s + 1, :] * jnp.exp(dd)
            pieces.append(jnp.where(t3 >= s, w, 0.0).reshape(rows, width).astype(BF16))
            yield
        compact = jnp.dot(jnp.concatenate(pieces, axis=1), eb_ref[...], preferred_element_type=F32)
        spread = jnp.dot(compact.astype(BF16), sh_ref[...], preferred_element_type=F32)
        base_mask = (r_i // GLA_BASE) == (c_i // GLA_BASE)
        att = [jnp.where(base_mask, spread[:, h * LANES:h * LANES + rows], 0.0) for h in range(GLA_H)]
        yield

        for size, g_mid in zip(sizes, g_mids):
            second = (t_in % size) >= size // 2
            ql = jnp.where(second, q * jnp.exp(jnp.minimum(gcum - g_mid, 0.0)), 0.0)
            kl = jnp.where(second, 0.0, k * jnp.exp(jnp.minimum(g_mid - gcum, 0.0)))
            group = (r_i // size) == (c_i // size)
            for h in range(GLA_H):
                ks = slice(h * GLA_DK, (h + 1) * GLA_DK)
                att[h] = att[h] + jnp.where(group, _dot(ql[:, ks], kl[:, ks], NT), 0.0)
            yield

        so = _state_slot(so_ref, out_mode) if not chained else None
        o_heads = []
        for h in range(GLA_H):
            ks = slice(h * GLA_DK, (h + 1) * GLA_DK)
            vs = slice(h * GLA_DV, (h + 1) * GLA_DV)
            if chained:
                cur = s_scr[h]
                if rows % LANES == 0:
                    oh = _dot(jnp.concatenate([att[h].astype(BF16), qd[:, ks].astype(BF16)], axis=1),
                              jnp.concatenate([vb[:, vs], cur.astype(BF16)], axis=0))
                else:
                    oh = _dot(att[h], vb[:, vs]) + _dot(qd[:, ks], cur)
                s_scr[h] = ds[ks, :] * cur + _dot(kd[:, ks], vb[:, vs], TN)
            else:
                o_rows = []
                for b in range(nblk):
                    rs = slice(b * lb, (b + 1) * lb)
                    cur = s0_ref[b, h]
                    o_rows.append(_dot(qd[rs, ks], cur))
                    so[b, h] = ds[ks, b * LANES:(b + 1) * LANES] * cur + _dot(kd[rs, ks], vb[rs, vs], TN)
                oh = _dot(att[h], vb[:, vs]) + jnp.concatenate(o_rows, axis=0)
            o_heads.append(_rms(oh) * wn_ref[...])
            yield
        y_ref[...] = (_silu(r_ref[...].astype(F32)) * jnp.concatenate(o_heads, axis=1)).astype(BF16)

    _run_chunks(chunk, row_refs, y_out, s0_ref, so_ref, scr, chained, nsq, out_mode)


def _gla_consts(tv):
    sources = jnp.arange(max(0, GLA_BASE - tv), GLA_BASE, dtype=jnp.int32)
    row = jnp.arange(sources.shape[0] * GLA_H * GLA_DK, dtype=jnp.int32)
    target = sources[row // (GLA_H * GLA_DK)] * GLA_H + (row % (GLA_H * GLA_DK)) // GLA_DK
    eb = (target[:, None] == jnp.arange(LANES, dtype=jnp.int32)[None, :]).astype(BF16)
    r = jnp.arange(LANES, dtype=jnp.int32)[:, None]
    c = jnp.arange(GLA_H * LANES, dtype=jnp.int32)[None, :]
    sh = ((r < GLA_BASE * GLA_H) & (r % GLA_H == c // LANES)
          & (r // GLA_H == (c % LANES) % GLA_BASE)).astype(BF16)
    return (eb[None], 0), (sh[None], 0)


def _gla(proj, weights, s0, *, lb, tv, rows, chained, name, nsq=1, **kw):
    kern = functools.partial(_gla_kernel, rows=rows, lb=lb, tv=tv, chained=chained)
    scratch = [pltpu.VMEM((nsq, GLA_H, GLA_DK, GLA_DV), F32)] if chained else []
    return _rec_call(kern, proj,
                     [("gla_q", 256), ("gla_k", 256), ("gla_v", 512), ("gla_r", 512), ("gla_a", 128)],
                     list(weights) + list(_gla_consts(tv)), [s0], [MIX], rows=rows,
                     chained=chained, scratch=scratch, name=name, nsq=nsq, **kw)


def _ssd_kernel(*refs, rows, lb, tv, chained, nsq, n_alias, out_mode):
    row_refs = refs[:3]
    cw_ref, cb_ref, dtb_ref, alog_ref, dsk_ref, wn_ref, c0_ref, s0_ref = refs[3:11]
    y_out, co_ref, so_ref = refs[11 + n_alias:14 + n_alias]
    scr = refs[14 + n_alias:]
    nblk = rows // lb
    t_in = _row_in_block(rows, lb)
    valid = (t_in >= lb - tv) if tv < lb else None
    low64 = lax.broadcasted_iota(jnp.int32, (SSM_N, LANES), 1) < SSM_P

    def chunk(z_ref, x_ref, dt_ref, y_ref, s_scr, ext_scr, j):
        co = _state_slot(co_ref, out_mode)
        if chained:
            xb = x_ref[...]
            xin = xb.astype(F32)
            tail = ext_scr[...]
            r_s, c_s = _iotas(rows)
            t8 = lax.broadcasted_iota(jnp.int32, (SUBLANES, 1), 0)
            conv = cb_ref[...] + cw_ref[SSM_CONV - 1:SSM_CONV, :] * xin
            head = jnp.zeros((SUBLANES, SSM_CONV_DIM), F32)
            for d in range(1, SSM_CONV):
                tap = cw_ref[SSM_CONV - 1 - d:SSM_CONV - d, :]
                shifted = jnp.dot((c_s == r_s - d).astype(BF16), xb, preferred_element_type=F32)
                conv = conv + tap * shifted
                head = head + tap * jnp.where(t8 < d, pltpu.roll(tail, d, axis=0), 0.0)
            conv = jnp.concatenate([conv[:SUBLANES] + head, conv[SUBLANES:]], axis=0)
            co[j] = xin[rows - SUBLANES:, :]
            ext_scr[...] = xin[rows - SUBLANES:, :]
        else:
            ext_scr[0:SUBLANES, :] = jnp.zeros((SUBLANES, SSM_CONV_DIM), F32)
            xin = jnp.where(valid, x_ref[...].astype(F32), c0_ref[...].reshape(rows, SSM_CONV_DIM))
            ext_scr[SUBLANES:SUBLANES + rows, :] = xin
            conv = cb_ref[...]
            for i in range(SSM_CONV):
                conv = conv + cw_ref[i:i + 1, :] * ext_scr[pl.ds(SUBLANES - (SSM_CONV - 1) + i, rows), :]
            co[...] = xin.reshape(nblk, lb, SSM_CONV_DIM)
        act = _silu(conv)
        yield

        dt = jax.nn.softplus(dt_ref[...].astype(F32) + dtb_ref[...])
        gdt = dt * (-jnp.exp(alog_ref[...]))
        if valid is not None:
            gdt = jnp.where(valid, gdt, 0.0)
        r_i, c_i = _iotas(rows)
        same = (r_i // lb) == (c_i // lb)
        causal = same & (r_i >= c_i)
        gcum = _dot_sel(causal, gdt)
        if rows < LANES:
            gsq = jnp.concatenate([gcum, jnp.zeros((LANES - rows, LANES), F32)], axis=0)
            gcum_t = gsq.T[:, :rows]
        else:
            gcum_t = gcum.T
        gtot = gcum[rows - 1:rows, :] if nblk == 1 else _dot_sel(same, gdt)

        lane = lax.broadcasted_iota(jnp.int32, (rows, LANES), 1)
        low = lane < SSM_P
        bcol = act[:, MIX:MIX + LANES]
        ccol = act[:, MIX + LANES:MIX + 2 * LANES]
        bswap = pltpu.roll(bcol, SSM_N, axis=1)
        cswap = pltpu.roll(ccol, SSM_N, axis=1)
        b2 = (jnp.where(low, bcol, bswap), jnp.where(low, bswap, bcol))
        c2 = (jnp.where(low, ccol, cswap), jnp.where(low, cswap, ccol))
        cb = (_dot(jnp.where(low, ccol, 0.0), bcol, NT), _dot(jnp.where(low, 0.0, ccol), bcol, NT))

        def pair_lanes(x, p):
            return jnp.where(low[:x.shape[0]], x[:, 2 * p:2 * p + 1], x[:, 2 * p + 1:2 * p + 2])

        yield
        so = _state_slot(so_ref, out_mode) if not chained else None
        y_pairs = []
        for p in range(SSM_PAIRS):
            gi = p // (SSM_PAIRS // SSM_G)
            decs = []
            for h in (2 * p, 2 * p + 1):
                diff = jnp.minimum(gcum[:, h:h + 1] - gcum_t[h:h + 1, :], 0.0)
                decs.append(cb[gi] * jnp.where(causal, jnp.exp(diff), 0.0))
            xp = act[:, p * LANES:(p + 1) * LANES]
            vp = xp * pair_lanes(dt, p)
            if valid is not None:
                vp = jnp.where(valid, vp, 0.0)
            vbd = jnp.concatenate([jnp.where(low, vp, 0.0), jnp.where(low, 0.0, vp)], axis=0)
            g2 = pair_lanes(gcum, p)
            ge2 = pair_lanes(gtot, p)
            cin = c2[gi] * jnp.exp(g2)
            bout = b2[gi] * jnp.exp(ge2 - g2)
            if chained:
                cur = s_scr[p]
                bd = jnp.concatenate([jnp.where(low64, cur, 0.0), jnp.where(low64, 0.0, cur)], axis=0)
                if rows % LANES == 0:
                    o = _dot(jnp.concatenate(decs + [cin], axis=1).astype(BF16),
                             jnp.concatenate([vbd.astype(BF16), bd.astype(BF16)], axis=0))
                else:
                    o = _dot(jnp.concatenate(decs, axis=1), vbd) + _dot(cin, bd)
                u = _dot(bout, vp, TN)
                s_scr[p] = jnp.exp(ge2) * cur + jnp.where(low64, u[:SSM_N, :], u[SSM_N:, :])
            else:
                o = _dot(jnp.concatenate(decs, axis=1), vbd)
                halves = []
                for odd, (ci, bo, vv) in enumerate(
                        ((cin, bout, vp),
                         tuple(pltpu.roll(t, SSM_P, axis=1) for t in (cin, bout, vp)))):
                    h = 2 * p + odd
                    o_rows = []
                    for b in range(nblk):
                        rs = slice(b * lb, (b + 1) * lb)
                        cur = s0_ref[b, h]
                        o_rows.append(_dot(ci[rs, :SSM_N], cur))
                        so[b, h] = (jnp.exp(gtot[b * lb:b * lb + 1, h:h + 1]) * cur
                                    + _dot(bo[rs, :SSM_N], vv[rs, :SSM_P], TN))
                    halves.append(jnp.concatenate(o_rows, axis=0))
                o = o + jnp.concatenate(halves, axis=1)
            y_pairs.append(o + dsk_ref[:, p * LANES:(p + 1) * LANES] * xp)
            yield
        y = jnp.concatenate(y_pairs, axis=1) * _silu(z_ref[...].astype(F32))
        half = MIX // SSM_G
        y = jnp.concatenate([_rms(y[:, gi * half:(gi + 1) * half]) for gi in range(SSM_G)], axis=1)
        y_ref[...] = (y * wn_ref[...]).astype(BF16)

    if not chained:
        _interleave([chunk(*row_refs, y_out, None, scr[0], 0)])
        return
    s_all, ext_all = scr

    @pl.when(pl.program_id(1) == 0)
    def _():
        for j in range(nsq):
            for p in range(SSM_PAIRS):
                s_all[j, p] = jnp.concatenate([s0_ref[0, 2 * p], s0_ref[0, 2 * p + 1]], axis=1)
            ext_all[j, 0:SUBLANES, :] = c0_ref[0]

    view = lambda r, j: r.at[j] if nsq > 1 else r
    _interleave([chunk(*[view(r, j) for r in row_refs], view(y_out, j), s_all.at[j],
                       ext_all.at[j], j) for j in range(nsq)])

    @pl.when(pl.program_id(1) == pl.num_programs(1) - 1)
    def _():
        dst = _state_slot(so_ref, out_mode)
        for j in range(nsq):
            for p in range(SSM_PAIRS):
                dst[j, 2 * p] = s_all[j, p][:, :SSM_P]
                dst[j, 2 * p + 1] = s_all[j, p][:, SSM_P:]


def _ssd(proj, weights, c0, s0, *, lb, tv, rows, chained, name, nsq=1, **kw):
    kern = functools.partial(_ssd_kernel, rows=rows, lb=lb, tv=tv, chained=chained)
    if chained:
        scratch = [pltpu.VMEM((nsq, SSM_PAIRS, SSM_N, LANES), F32),
                   pltpu.VMEM((nsq, SUBLANES, SSM_CONV_DIM), F32)]
    else:
        scratch = [pltpu.VMEM((SUBLANES + rows, SSM_CONV_DIM), F32)]
    return _rec_call(kern, proj, [("ssm_z", 512), ("ssm_xbc", 768), ("ssm_dt", 128)],
                     weights, [c0, s0], [MIX], rows=rows, chained=chained, scratch=scratch,
                     name=name, nsq=nsq, **kw)


def _ret_kernel(*refs, rows, lb, tv, chained, nsq, n_alias, out_mode):
    row_refs = refs[:4]
    lg_ref, cos_ref, sin_ref, s0_ref = refs[4:8]
    y_out, so_ref = refs[8 + n_alias:10 + n_alias]
    scr = refs[10 + n_alias:]
    nblk = rows // lb
    t_col = _row_in_block(rows, lb)
    s_row = lax.broadcasted_iota(jnp.int32, (1, rows), 1) % lb
    n_col = jnp.maximum(t_col - (lb - tv) + 1, 0).astype(F32)
    n_row = jnp.maximum(s_row - (lb - tv) + 1, 0).astype(F32)

    def decay_matrix(h):
        r_i, c_i = _iotas(rows)
        causal = ((r_i // lb) == (c_i // lb)) & (r_i >= c_i)
        diff = (n_col - n_row) * lg_ref[:, h:h + 1]
        return jnp.where(causal, jnp.exp(jnp.minimum(diff, 0.0)), 0.0)

    width = RET_H * RET_DK

    def state_scales():
        head = lax.broadcasted_iota(jnp.int32, (1, width), 1) // RET_DK
        lg_lane = lg_ref[:, 0:1]
        for h in range(1, RET_H):
            lg_lane = jnp.where(head == h, lg_ref[:, h:h + 1], lg_lane)
        gc = n_col * lg_lane
        return jnp.exp(gc), jnp.exp(float(tv) * lg_lane - gc)

    if chained:
        dec_scr, scale_scr = scr[1:]

        @pl.when(pl.program_id(1) == 0)
        def _():
            for h in range(RET_H):
                dec_scr[h] = decay_matrix(h)
            scale_scr[0], scale_scr[1] = state_scales()

    lane = lax.broadcasted_iota(jnp.int32, (rows, width), 1)
    first_half = (lane % RET_DK) < (RET_DK // 2)
    cos = cos_ref[...]
    sin = jnp.where(first_half, -sin_ref[...], sin_ref[...])

    def rope(x):
        partner = jnp.where(first_half, pltpu.roll(x, width - RET_DK // 2, axis=1),
                            pltpu.roll(x, RET_DK // 2, axis=1))
        return x * cos + partner * sin

    def chunk(q_ref, k_ref, v_ref, g_ref, y_ref, s_scr):
        q = rope(q_ref[...].astype(F32))
        k = rope(k_ref[...].astype(F32)) * (RET_DK ** -0.5)
        v = v_ref[...]
        if tv < lb:
            k = jnp.where(t_col >= lb - tv, k, 0.0)
        q_scale, k_scale = (scale_scr[0], scale_scr[1]) if chained else state_scales()
        q_in = q * q_scale
        k_out = k * k_scale
        yield

        so = _state_slot(so_ref, out_mode) if not chained else None
        outs = []
        for h in range(RET_H):
            ks = slice(h * RET_DK, (h + 1) * RET_DK)
            vs = slice(h * RET_DV, (h + 1) * RET_DV)
            att = _dot(q[:, ks], k[:, ks], NT) * (dec_scr[h] if chained else decay_matrix(h))
            ge = float(tv) * lg_ref[:, h:h + 1]
            qin = q_in[:, ks]
            kout = k_out[:, ks]
            if chained:
                cur = s_scr[h]
                if rows % LANES == 0:
                    oh = _dot(jnp.concatenate([att.astype(BF16), qin.astype(BF16)], axis=1),
                              jnp.concatenate([v[:, vs], cur.astype(BF16)], axis=0))
                else:
                    oh = _dot(att, v[:, vs]) + _dot(qin, cur)
                s_scr[h] = jnp.exp(ge) * cur + _dot(kout, v[:, vs], TN)
            else:
                oh = _dot(att, v[:, vs])
                o_rows = []
                for b in range(nblk):
                    rs = slice(b * lb, (b + 1) * lb)
                    cur = s0_ref[b, h]
                    o_rows.append(_dot(qin[rs], cur))
                    so[b, h] = jnp.exp(ge) * cur + _dot(kout[rs], v[rs, vs], TN)
                oh = oh + jnp.concatenate(o_rows, axis=0)
            outs.append(_rms(oh))
            yield
        y_ref[...] = (_silu(g_ref[...].astype(F32)) * jnp.concatenate(outs, axis=1)).astype(BF16)

    _run_chunks(chunk, row_refs, y_out, s0_ref, so_ref, scr[:1], chained, nsq, out_mode)


def _ret(proj, lg, cos, sin, s0, *, lb, tv, rows, chained, name, nsq=1, **kw):
    kern = functools.partial(_ret_kernel, rows=rows, lb=lb, tv=tv, chained=chained)
    scratch = ([pltpu.VMEM((nsq, RET_H, RET_DK, RET_DV), F32), pltpu.VMEM((RET_H, rows, rows), F32),
                pltpu.VMEM((2, rows, RET_H * RET_DK), F32)] if chained else [])
    return _rec_call(kern, proj, [("ret_q", 256), ("ret_k", 256), ("ret_v", 512), ("ret_g", 512)],
                     [(lg, 0)], [s0], [MIX], rows=rows, chained=chained, scratch=scratch, name=name,
                     tables=(cos, sin), nsq=nsq, **kw)


def _dense2_kernel(x_ref, gate_ref, yg_ref, ys_ref, yr_ref, wg_ref, ws_ref, wr_ref, wo_ref,
                   l1w_ref, l1b_ref, w1_ref, b1_ref, w2_ref, b2_ref, l2w_ref, l2b_ref, o_ref,
                   *, ff_chunk, parts):
    tm = x_ref.shape[0]

    def rows_stage(rs):
        branches = [_dot(y_ref[rs, :], w_ref[...])
                    for y_ref, w_ref in ((yg_ref, wg_ref), (ys_ref, ws_ref), (yr_ref, wr_ref))]
        yield
        gate = lambda i: jax.nn.sigmoid(gate_ref[rs, i * D_MODEL:(i + 1) * D_MODEL].astype(F32))
        merged = gate(0) * branches[0] + gate(1) * branches[1] + gate(2) * branches[2]
        yield
        mix = _dot(merged, wo_ref[...])
        yield
        h = _layer_norm(ALPHA * x_ref[rs, :] + mix, l1w_ref[...], l1b_ref[...])
        hb = h.astype(BF16)
        ff = jnp.zeros_like(h) + b2_ref[...]
        yield
        for c0 in range(0, D_FF, ff_chunk):
            hid = jnp.dot(hb, w1_ref[:, c0:c0 + ff_chunk], preferred_element_type=F32)
            hid = jnp.square(jnp.maximum(hid + b1_ref[:, c0:c0 + ff_chunk], 0.0))
            ff = ff + _dot(hid, w2_ref[c0:c0 + ff_chunk, :])
            yield
        o_ref[rs, :] = _layer_norm(ALPHA * h + ff, l2w_ref[...], l2b_ref[...])

    _interleave([rows_stage(pl.ds(i * (tm // parts), tm // parts)) for i in range(parts)])


def _dense2(x, proj, yg, ys, yr, wl, layer, *, tm, name):
    m = x.shape[0]
    row = lambda w: pl.BlockSpec((tm, w), lambda i: (i, 0))
    const = lambda r, w: pl.BlockSpec((None, r, w), lambda i: (layer, 0, 0),
                                      pipeline_mode=pl.Buffered(1))
    return pl.pallas_call(
        functools.partial(_dense2_kernel, ff_chunk=1024, parts=(2 if tm % 32 == 0 else 1)),
        grid=(m // tm,),
        in_specs=[row(D_MODEL), row(3 * D_MODEL), row(MIX), row(MIX), row(MIX),
                  const(MIX, D_MODEL), const(MIX, D_MODEL), const(MIX, D_MODEL),
                  const(D_MODEL, D_MODEL), const(1, D_MODEL), const(1, D_MODEL),
                  const(D_MODEL, D_FF), const(1, D_FF), const(D_FF, D_MODEL), const(1, D_MODEL),
                  const(1, D_MODEL), const(1, D_MODEL)],
        out_specs=row(D_MODEL),
        out_shape=jax.ShapeDtypeStruct((m, D_MODEL), F32),
        compiler_params=pltpu.CompilerParams(
            dimension_semantics=("parallel",), vmem_limit_bytes=VMEM_LIMIT),
        name=name,
    )(x, proj, yg, ys, yr, wl["w_gla_out"], wl["w_ssm_out"], wl["w_ret_out"], wl["w_o"],
      wl["ln1_w"], wl["ln1_b"], wl["w_ff1"], wl["b_ff1"], wl["w_ff2"], wl["b_ff2"],
      wl["ln2_w"], wl["ln2_b"])


def _pick_tile(n, pref):
    t = min(n, pref)
    while n % t or t % SUBLANES:
        t -= 1
    return t


def _rearrange_w_in(w):
    offs = [0]
    for s in SPLIT_SIZES:
        offs.append(offs[-1] + s)
    names = ("gla_q", "gla_k", "gla_v", "gla_r", "gla_a", "ssm_z", "ssm_xbc", "ssm_dt",
             "ret_q", "ret_k", "ret_v", "ret_g", "gates")
    seg = {n: w[..., offs[i]:offs[i + 1]].astype(BF16) for i, n in enumerate(names)}
    pad = lambda a: jnp.pad(a, ((0, 0),) * (a.ndim - 1) + ((0, LANES - a.shape[-1]),))
    order = sorted(COL, key=COL.get)
    parts = [pad(seg[n]) if n in ("gla_a", "ssm_dt") else seg[n] for n in order]
    return jnp.concatenate(parts, axis=-1)


def _rope_tables(pos):
    half = RET_DK // 2
    inv_freq = ROPE_BASE ** (-jnp.arange(half, dtype=F32) / half)
    ang = pos.astype(F32)[:, None] * inv_freq[None, :]
    cos = jnp.tile(jnp.cos(ang), (1, 2 * RET_H))
    sin = jnp.tile(jnp.sin(ang), (1, 2 * RET_H))
    return cos, sin


def kernel(x_prompt, x_sample, state_gla, state_ssm, state_conv, state_ret, meta_tokens,
           ln_in_w, ln_in_b, w_in, w_gla_a2, b_gla_a, w_gla_norm, conv_w, conv_b, dt_bias,
           a_log, d_skip, w_ssm_norm, w_gla_out, w_ssm_out, w_ret_out, w_o, ln1_w, ln1_b,
           w_ff1, b_ff1, w_ff2, b_ff2, ln2_w, ln2_b):
    bp, tp, d = x_prompt.shape
    bs, ts, _ = x_sample.shape
    assert d == D_MODEL and tp % CHUNK == 0 and w_in.shape[0] == DEPTH
    assert SSM_CONV - 1 <= SAMPLE_ROWS - ts
    nchunk = tp // CHUNK
    pad_rows = SAMPLE_ROWS - ts

    x_body = x_prompt.reshape(bp * tp, d)
    n_sample = bs * SAMPLE_ROWS
    rows_s = min(CHUNK, n_sample)
    assert n_sample % rows_s == 0 and n_sample % N_META == 0
    n_small = n_sample + CHUNK
    x_small = jnp.concatenate(
        [jnp.pad(x_sample, ((0, 0), (pad_rows, 0), (0, 0))).reshape(n_sample, d),
         meta_tokens.astype(F32), jnp.zeros((CHUNK - N_META, d), F32)], axis=0)

    cos_b, sin_b = _rope_tables(N_META + jnp.arange(tp, dtype=jnp.int32))
    cos_m, sin_m = _rope_tables(jnp.arange(N_META, dtype=jnp.int32))
    pos_tile = PAST_LEN - pad_rows + jnp.arange(SAMPLE_ROWS, dtype=jnp.int32)
    cos_s, sin_s = _rope_tables(jnp.tile(pos_tile, rows_s // SAMPLE_ROWS))
    lg_ret = jnp.pad(jnp.log1p(-jnp.exp2(-5.0 - jnp.arange(RET_H, dtype=F32))),
                     (0, LANES - RET_H)).reshape(1, 1, LANES)

    zero_gla = (jnp.zeros((1, 1, GLA_H, GLA_DK, GLA_DV), F32), 0)
    zero_ssm = (jnp.zeros((1, 1, SSM_H, SSM_N, SSM_P), F32), 0)
    zero_conv = (jnp.zeros((1, 1, SUBLANES, SSM_CONV_DIM), F32), 0)
    zero_ret = (jnp.zeros((1, 1, RET_H, RET_DK, RET_DV), F32), 0)
    from_meta = lambda a: (a[None], 0)
    conv_in = jnp.pad(state_conv, ((0, 0), (0, 0), (pad_rows - (SSM_CONV - 1), ts), (0, 0)))

    rowvec = lambda a: a.reshape(DEPTH, 1, -1)
    lane_pad = lambda a: jnp.pad(a, ((0, 0), (0, LANES - a.shape[1]))).reshape(DEPTH, 1, LANES)
    w_in_all = _rearrange_w_in(w_in)
    wl = dict(w_gla_out=w_gla_out.astype(BF16), w_ssm_out=w_ssm_out.astype(BF16),
              w_ret_out=w_ret_out.astype(BF16), w_o=w_o.astype(BF16),
              ln1_w=rowvec(ln1_w), ln1_b=rowvec(ln1_b),
              w_ff1=w_ff1.astype(BF16), b_ff1=rowvec(b_ff1),
              w_ff2=w_ff2.astype(BF16), b_ff2=rowvec(b_ff2),
              ln2_w=rowvec(ln2_w), ln2_b=rowvec(ln2_b))
    gla_w = [jnp.pad(w_gla_a2, ((0, 0), (0, LANES - GLA_RANK), (0, 0))), rowvec(b_gla_a),
             rowvec(w_gla_norm)]
    ssd_w = [conv_w, rowvec(conv_b), lane_pad(dt_bias), lane_pad(a_log),
             rowvec(jnp.repeat(d_skip, SSM_P, axis=1)), rowvec(w_ssm_norm)]
    ln_w, ln_b = ln_in_w.reshape(1, -1), ln_in_b.reshape(1, -1)

    tm_in = _pick_tile(bp * tp, 2048)
    tm_d2 = _pick_tile(bp * tp, 512)
    tm_d2s = _pick_tile(n_small, 384)
    tn = 1536
    meta = dict(row0=n_sample, nseq=1, nchunk=1, rows=N_META, lb=N_META, tv=N_META, chained=True,
                per_seq_state=False)
    body = dict(row0=0, nseq=bp, nchunk=nchunk, rows=CHUNK, lb=CHUNK, tv=CHUNK, chained=True,
                per_seq_state=False)
    seqs_per_step = lambda want: max(n for n in (1, 2, 4) if n <= want and bp % n == 0)
    samp = dict(row0=0, nseq=bs, nchunk=1, rows=rows_s, lb=SAMPLE_ROWS, tv=ts, chained=False,
                per_seq_state=True)

    names = ("gla_p", "gla_s", "ssm_p", "ssm_s", "conv_p", "conv_s", "ret_p", "ret_s")
    st = {k: None for k in names}
    stk = lambda *keys: [] if st[keys[0]] is None else [st[k] for k in keys]
    xb, xs = x_body, x_small
    for l in range(DEPTH):
        proj_s, xs = _in_proj(xs, ln_w, ln_b, w_in_all, l, apply_ln=(l == 0), tm=n_small, tn=tn,
                              name=f"inproj_small_{l}")
        yg_m, sg_m = _gla(proj_s, gla_w, zero_gla, layer=l, name=f"gla_meta_{l}", **meta)
        ys_m, cv_m, ss_m = _ssd(proj_s, ssd_w, zero_conv, zero_ssm, layer=l,
                                name=f"ssd_meta_{l}", **meta)
        yr_m, sr_m = _ret(proj_s, lg_ret, cos_m, sin_m, zero_ret, layer=l,
                          name=f"ret_meta_{l}", **meta)
        yg_s, st["gla_s"] = _gla(proj_s, gla_w, (state_gla, l), layer=l, stacked=stk("gla_s"),
                                 name=f"gla_sample_{l}", **samp)
        ys_s, st["conv_s"], st["ssm_s"] = _ssd(proj_s, ssd_w, (conv_in, l), (state_ssm, l), layer=l,
                                               stacked=stk("conv_s", "ssm_s"),
                                               name=f"ssd_sample_{l}", **samp)
        yr_s, st["ret_s"] = _ret(proj_s, lg_ret, cos_s, sin_s, (state_ret, l), layer=l,
                                 stacked=stk("ret_s"), name=f"ret_sample_{l}", **samp)
        zpad = jnp.zeros((CHUNK - N_META, MIX), BF16)
        yg = jnp.concatenate([yg_s, yg_m, zpad], axis=0)
        ys = jnp.concatenate([ys_s, ys_m, zpad], axis=0)
        yr = jnp.concatenate([yr_s, yr_m, zpad], axis=0)
        xs = _dense2(xs, proj_s, yg, ys, yr, wl, l, tm=tm_d2s, name=f"dense2_small_{l}")

        proj_b, xb = _in_proj(xb, ln_w, ln_b, w_in_all, l, apply_ln=(l == 0),
                              tm=(tm_in // 2 if l == 0 else tm_in), tn=tn,
                              name=f"inproj_body_{l}")
        yg_b, st["gla_p"] = _gla(proj_b, gla_w, from_meta(sg_m), layer=l, stacked=stk("gla_p"),
                                 name=f"gla_body_{l}", nsq=seqs_per_step(4), **body)
        ys_b, st["conv_p"], st["ssm_p"] = _ssd(proj_b, ssd_w, from_meta(cv_m), from_meta(ss_m),
                                               layer=l, stacked=stk("conv_p", "ssm_p"),
                                               name=f"ssd_body_{l}", nsq=seqs_per_step(2), **body)
        yr_b, st["ret_p"] = _ret(proj_b, lg_ret, cos_b, sin_b, from_meta(sr_m), layer=l,
                                 stacked=stk("ret_p"), name=f"ret_body_{l}", nsq=seqs_per_step(4),
                                 **body)
        xb = _dense2(xb, proj_b, yg_b, ys_b, yr_b, wl, l, tm=tm_d2, name=f"dense2_body_{l}")

    y_prompt = xb.reshape(bp, tp, d)
    y_sample = xs[:n_sample].reshape(bs, SAMPLE_ROWS, d)[:, pad_rows:]
    tail3 = lambda c: c[:, :, SUBLANES - (SSM_CONV - 1):, :]
    return (y_prompt, y_sample, st["gla_p"], st["gla_s"], st["ssm_p"], st["ssm_s"],
            tail3(st["conv_p"]), tail3(st["conv_s"]), st["ret_p"], st["ret_s"])
```

```python
import functools

import jax
import jax.numpy as jnp
from jax import lax
from jax.experimental import pallas as pl
from jax.experimental.pallas import tpu as pltpu

F32 = jnp.float32
BF16 = jnp.bfloat16

D_MODEL = 1024
DEPTH = 2
N_META = 16
MIX = 512
GLA_H, GLA_DK, GLA_DV, GLA_RANK = 4, 64, 128, 16
GLA_GATE_NORM = 16.0
SSM_H, SSM_P, SSM_N, SSM_G, SSM_CONV = 8, 64, 64, 2, 4
SSM_CONV_DIM = MIX + 2 * SSM_G * SSM_N
SSM_PAIRS = SSM_H // 2
RET_H, RET_DK, RET_DV = 4, 64, 128
ROPE_BASE = 10000.0
D_FF = 4 * D_MODEL
ALPHA = (2 * DEPTH) ** 0.25
PAST_LEN = 16384
SPLIT_SIZES = (256, 256, 512, 512, 16, 512, 768, 8, 256, 256, 512, 512, 3072)

LANES = 128
SUBLANES = 8
VMEM_LIMIT = 56 * 1024 * 1024

COL = dict(gates=0, gla_v=3072, gla_r=3584, ssm_z=4096, ret_v=4608, ret_g=5120,
           gla_q=5632, gla_k=5888, ssm_xbc=6144, ret_q=6912, ret_k=7168, gla_a=7424, ssm_dt=7552)
N_PROJ = 7680
SAMPLE_ROWS = SUBLANES
GLA_BASE = SUBLANES
CHUNK = 128

NN = (((1,), (0,)), ((), ()))
NT = (((1,), (1,)), ((), ()))
TN = (((0,), (0,)), ((), ()))


def _dot(a, b, dims=NN):
    return lax.dot_general(a.astype(BF16), b.astype(BF16), dims, preferred_element_type=F32)


def _dot_sel(sel, x, dims=NN, sel_first=True):
    hi = x.astype(BF16)
    r1 = x - hi.astype(F32)
    mid = r1.astype(BF16)
    lo = (r1 - mid.astype(F32)).astype(BF16)
    sb = sel.astype(BF16)
    out = None
    for part in (hi, mid, lo):
        ops = (sb, part) if sel_first else (part, sb)
        term = lax.dot_general(*ops, dims, preferred_element_type=F32)
        out = term if out is None else out + term
    return out


def _layer_norm(x, w, b):
    mu = jnp.mean(x, axis=-1, keepdims=True)
    xc = x - mu
    var = jnp.mean(xc * xc, axis=-1, keepdims=True)
    return xc * lax.rsqrt(var + 1e-5) * w + b


def _rms(x):
    return x * lax.rsqrt(jnp.mean(x * x, axis=-1, keepdims=True) + 1e-6)


def _silu(x):
    return x * jax.nn.sigmoid(x)


def _iotas(rows):
    return (lax.broadcasted_iota(jnp.int32, (rows, rows), 0),
            lax.broadcasted_iota(jnp.int32, (rows, rows), 1))


def _row_in_block(rows, lb):
    return lax.broadcasted_iota(jnp.int32, (rows, 1), 0) % lb


def _inproj_kernel(x_ref, lnw_ref, lnb_ref, w_ref, proj_ref, xb_scr, *, apply_ln, parts):
    tm = x_ref.shape[0]

    def first_tile(rs):
        x = x_ref[rs, :]
        if apply_ln:
            x = _layer_norm(x, lnw_ref[...], lnb_ref[...])
        xb = x.astype(BF16)
        xb_scr[rs, :] = xb
        yield
        proj_ref[rs, :] = jnp.dot(xb, w_ref[...], preferred_element_type=F32).astype(BF16)

    @pl.when(pl.program_id(1) == 0)
    def _():
        _interleave([first_tile(pl.ds(p * (tm // parts), tm // parts)) for p in range(parts)],
                    stagger=True)

    @pl.when(pl.program_id(1) > 0)
    def _():
        proj_ref[...] = jnp.dot(xb_scr[...], w_ref[...], preferred_element_type=F32).astype(BF16)


def _in_proj(x, lnw, lnb, w, layer, *, apply_ln, tm, tn, name):
    m = x.shape[0]
    grid = (m // tm, N_PROJ // tn)
    return pl.pallas_call(
        functools.partial(_inproj_kernel, apply_ln=apply_ln,
                          parts=(4 if apply_ln and tm % 64 == 0 else 1)),
        grid=grid,
        in_specs=[pl.BlockSpec((tm, D_MODEL), lambda i, j: (i, 0)),
                  pl.BlockSpec((1, D_MODEL), lambda i, j: (0, 0)),
                  pl.BlockSpec((1, D_MODEL), lambda i, j: (0, 0)),
                  pl.BlockSpec((None, D_MODEL, tn), lambda i, j: (layer, 0, j))],
        out_specs=pl.BlockSpec((tm, tn), lambda i, j: (i, j)),
        out_shape=jax.ShapeDtypeStruct((m, N_PROJ), BF16),
        scratch_shapes=[pltpu.VMEM((tm, D_MODEL), BF16)],
        compiler_params=pltpu.CompilerParams(
            dimension_semantics=("parallel", "arbitrary"), vmem_limit_bytes=VMEM_LIMIT),
        name=name,
    )(x, lnw, lnb, w)


def _rec_call(kernel_fn, proj, segs, consts, states, out_widths, *, layer, row0, nseq, nchunk,
              rows, chained, per_seq_state, scratch, name, tables=(), stacked=None, nsq=1):
    rb = row0 // rows
    nb = nsq if chained else rows // SAMPLE_ROWS
    grid = (nseq // nb, nchunk) if chained else (nseq // nb, 1)
    assert nseq % nb == 0 and (nsq == 1 or (chained and row0 == 0))
    out_mode = "plain" if stacked is None else ("first" if layer == 0 else "later")
    aliased = list(stacked) if out_mode == "later" else []

    def row_idx(b, c):
        return b * nchunk + c if chained else b

    def col(seg, w):
        cbi = COL[seg] // w
        if nsq > 1:
            return pl.BlockSpec((nsq, rows, w), lambda b, c: (b, c, cbi))
        return pl.BlockSpec((rows, w), lambda b, c: (rb + row_idx(b, c), cbi))

    def row_out(w):
        if nsq > 1:
            return pl.BlockSpec((nsq, rows, w), lambda b, c: (b, c, 0))
        return pl.BlockSpec((rows, w), lambda b, c: (row_idx(b, c), 0))

    def const_spec(a, lyr):
        zeros = (0,) * (a.ndim - 1)
        return pl.BlockSpec((None,) + a.shape[1:], lambda b, c: (lyr,) + zeros)

    def state_in_spec(a, lyr):
        zeros = (0,) * (a.ndim - 2)
        if per_seq_state:
            return pl.BlockSpec((None, nb) + a.shape[2:], lambda b, c: (lyr, b) + zeros)
        return pl.BlockSpec((None, 1) + a.shape[2:], lambda b, c: (lyr, 0) + zeros)

    def state_out_spec(a):
        zeros = (0,) * (a.ndim - 2)
        if out_mode == "plain":
            return pl.BlockSpec((nb,) + a.shape[2:], lambda b, c: (b,) + zeros)
        if out_mode == "first":
            return pl.BlockSpec((DEPTH, nb) + a.shape[2:], lambda b, c: (0, b) + zeros)
        return pl.BlockSpec((None, nb) + a.shape[2:], lambda b, c: (layer, b) + zeros)

    def state_out_shape(a):
        lead = (nseq,) if out_mode == "plain" else (DEPTH, nseq)
        return jax.ShapeDtypeStruct(lead + a.shape[2:], F32)

    consts = [c if isinstance(c, tuple) else (c, layer) for c in consts]
    in_specs = ([col(s, w) for s, w in segs] + [const_spec(a, lyr) for a, lyr in consts]
                + [pl.BlockSpec((rows, t.shape[1]), lambda b, c: (c, 0)) for t in tables]
                + [state_in_spec(a, lyr) for a, lyr in states]
                + [pl.BlockSpec(memory_space=pl.ANY) for _ in aliased])
    n_in = len(in_specs) - len(aliased)
    n_rows = nseq * nchunk * rows if chained else nseq * SAMPLE_ROWS
    out_specs = [row_out(w) for w in out_widths] + [state_out_spec(a) for a, _ in states]
    row_shape = (lambda w: (nseq, nchunk * rows, w)) if nsq > 1 else (lambda w: (n_rows, w))
    out_shape = ([jax.ShapeDtypeStruct(row_shape(w), BF16) for w in out_widths]
                 + [state_out_shape(a) for a, _ in states])
    if nsq > 1:
        proj = proj.reshape(nseq, nchunk * rows, proj.shape[-1])
    res = pl.pallas_call(
        functools.partial(kernel_fn, n_alias=len(aliased), out_mode=out_mode, nsq=nsq),
        grid=grid, in_specs=in_specs, out_specs=out_specs, out_shape=out_shape,
        scratch_shapes=scratch,
        input_output_aliases={n_in + i: len(out_widths) + i for i in range(len(aliased))},
        compiler_params=pltpu.CompilerParams(
            dimension_semantics=("parallel", "arbitrary"), vmem_limit_bytes=VMEM_LIMIT),
        name=name,
    )(*([proj] * len(segs)), *[a for a, _ in consts], *tables, *[a for a, _ in states], *aliased)
    if nsq > 1:
        res = ([r.reshape(n_rows, r.shape[-1]) for r in res[:len(out_widths)]]
               + list(res[len(out_widths):]))
    return res


def _state_slot(ref, out_mode):
    if out_mode != "first":
        return ref
    ref[1:] = jnp.zeros((DEPTH - 1,) + ref.shape[1:], F32)
    return ref.at[0]


def _interleave(stages, stagger=False):
    stages = list(stages)
    delay = {id(gen): (i if stagger else 0) for i, gen in enumerate(stages)}
    rnd = 0
    while stages:
        for gen in list(stages):
            if delay[id(gen)] > rnd:
                continue
            try:
                next(gen)
            except StopIteration:
                stages.remove(gen)
        rnd += 1


def _run_chunks(chunk, row_refs, y_ref, s0_ref, so_ref, scr, chained, nsq, out_mode):
    if not chained:
        _interleave([chunk(*row_refs, y_ref, None)])
        return
    s_all, = scr

    @pl.when(pl.program_id(1) == 0)
    def _():
        for j in range(nsq):
            s_all[j] = s0_ref[0]

    view = lambda r, j: r.at[j] if nsq > 1 else r
    _interleave([chunk(*[view(r, j) for r in row_refs], view(y_ref, j), s_all.at[j])
                 for j in range(nsq)])

    @pl.when(pl.program_id(1) == pl.num_programs(1) - 1)
    def _():
        dst = _state_slot(so_ref, out_mode)
        for j in range(nsq):
            dst[j] = s_all[j]


def _gla_kernel(*refs, rows, lb, tv, chained, nsq, n_alias, out_mode):
    row_refs = refs[:5]
    w2_ref, ba_ref, wn_ref, eb_ref, sh_ref, s0_ref = refs[5:11]
    y_out, so_ref = refs[11 + n_alias:13 + n_alias]
    scr = refs[13 + n_alias:]
    nblk = rows // lb
    width = GLA_H * GLA_DK

    def chunk(q_ref, k_ref, v_ref, r_ref, a_ref, y_ref, s_scr):
        q = q_ref[...].astype(F32) * (GLA_DK ** -0.5)
        k = k_ref[...].astype(F32)
        vb = v_ref[...]
        a = _dot(a_ref[...], w2_ref[...]) + ba_ref[...]
        g = jax.nn.log_sigmoid(a) * (1.0 / GLA_GATE_NORM)
        t_in = _row_in_block(rows, lb)
        if tv < lb:
            valid = t_in >= lb - tv
            g = jnp.where(valid, g, 0.0)
            k = jnp.where(valid, k, 0.0)
        yield
        r_i, c_i = _iotas(rows)
        same = (r_i // lb) == (c_i // lb)
        sizes = []
        while 2 * GLA_BASE * 2 ** len(sizes) <= lb:
            sizes.append(2 * GLA_BASE * 2 ** len(sizes))
        sums = [same & (r_i >= c_i)] + ([same] if nblk > 1 else [])
        sums += [same & (c_i <= (r_i // sz) * sz + sz // 2 - 1) for sz in sizes]
        gsums = _dot_sel(jnp.concatenate(sums, axis=0), g)
        gcum = gsums[:rows]
        gtot = gcum[rows - 1:rows, :] if nblk == 1 else gsums[rows:2 * rows]
        g_mids = [gsums[(len(sums) - len(sizes) + i) * rows:(len(sums) - len(sizes) + i + 1) * rows]
                  for i in range(len(sizes))]
        qd = q * jnp.exp(gcum)
        kd = k * jnp.exp(gtot - gcum)
        sel = (lax.broadcasted_iota(jnp.int32, (rows, nblk * LANES), 0) // lb
               == lax.broadcasted_iota(jnp.int32, (rows, nblk * LANES), 1) // LANES)
        ds = jnp.exp(_dot_sel(sel, g, TN, sel_first=False))
        yield

        nbase = rows // GLA_BASE
        q3 = q.reshape(nbase, GLA_BASE, width)
        k3 = k.reshape(nbase, GLA_BASE, width)
        g3 = gcum.reshape(nbase, GLA_BASE, width)
        t3 = lax.broadcasted_iota(jnp.int32, (1, GLA_BASE, 1), 1)
        pieces = []
        for s in range(max(0, GLA_BASE - tv), GLA_BASE):
            dd = jnp.minimum(g3 - g3[:, s:s + 1, :], 0.0)
            w = q3 * k3[:, s:s + 1, :] * jnp.exp(dd)
            pieces.append(jnp.where(t3 >= s, w, 0.0).reshape(rows, width).astype(BF16))
            yield
        compact = jnp.dot(jnp.concatenate(pieces, axis=1), eb_ref[...], preferred_element_type=F32)
        spread = jnp.dot(compact.astype(BF16), sh_ref[...], preferred_element_type=F32)
        base_mask = (r_i // GLA_BASE) == (c_i // GLA_BASE)
        att = [jnp.where(base_mask, spread[:, h * LANES:h * LANES + rows], 0.0) for h in range(GLA_H)]
        yield

        for size, g_mid in zip(sizes, g_mids):
            second = (t_in % size) >= size // 2
            ql = jnp.where(second, q * jnp.exp(jnp.minimum(gcum - g_mid, 0.0)), 0.0)
            kl = jnp.where(second, 0.0, k * jnp.exp(jnp.minimum(g_mid - gcum, 0.0)))
            group = (r_i // size) == (c_i // size)
            for h in range(GLA_H):
                ks = slice(h * GLA_DK, (h + 1) * GLA_DK)
                att[h] = att[h] + jnp.where(group, _dot(ql[:, ks], kl[:, ks], NT), 0.0)
            yield

        so = _state_slot(so_ref, out_mode) if not chained else None
        o_heads = []
        for h in range(GLA_H):
            ks = slice(h * GLA_DK, (h + 1) * GLA_DK)
            vs = slice(h * GLA_DV, (h + 1) * GLA_DV)
            if chained:
                cur = s_scr[h]
                if rows % LANES == 0:
                    oh = _dot(jnp.concatenate([att[h].astype(BF16), qd[:, ks].astype(BF16)], axis=1),
                              jnp.concatenate([vb[:, vs], cur.astype(BF16)], axis=0))
                else:
                    oh = _dot(att[h], vb[:, vs]) + _dot(qd[:, ks], cur)
                s_scr[h] = ds[ks, :] * cur + _dot(kd[:, ks], vb[:, vs], TN)
            else:
                o_rows = []
                for b in range(nblk):
                    rs = slice(b * lb, (b + 1) * lb)
                    cur = s0_ref[b, h]
                    o_rows.append(_dot(qd[rs, ks], cur))
                    so[b, h] = ds[ks, b * LANES:(b + 1) * LANES] * cur + _dot(kd[rs, ks], vb[rs, vs], TN)
                oh = _dot(att[h], vb[:, vs]) + jnp.concatenate(o_rows, axis=0)
            o_heads.append(_rms(oh) * wn_ref[...])
            yield
        y_ref[...] = (_silu(r_ref[...].astype(F32)) * jnp.concatenate(o_heads, axis=1)).astype(BF16)

    _run_chunks(chunk, row_refs, y_out, s0_ref, so_ref, scr, chained, nsq, out_mode)


def _gla_consts(tv):
    sources = jnp.arange(max(0, GLA_BASE - tv), GLA_BASE, dtype=jnp.int32)
    row = jnp.arange(sources.shape[0] * GLA_H * GLA_DK, dtype=jnp.int32)
    target = sources[row // (GLA_H * GLA_DK)] * GLA_H + (row % (GLA_H * GLA_DK)) // GLA_DK
    eb = (target[:, None] == jnp.arange(LANES, dtype=jnp.int32)[None, :]).astype(BF16)
    r = jnp.arange(LANES, dtype=jnp.int32)[:, None]
    c = jnp.arange(GLA_H * LANES, dtype=jnp.int32)[None, :]
    sh = ((r < GLA_BASE * GLA_H) & (r % GLA_H == c // LANES)
          & (r // GLA_H == (c % LANES) % GLA_BASE)).astype(BF16)
    return (eb[None], 0), (sh[None], 0)


def _gla(proj, weights, s0, *, lb, tv, rows, chained, name, nsq=1, **kw):
    kern = functools.partial(_gla_kernel, rows=rows, lb=lb, tv=tv, chained=chained)
    scratch = [pltpu.VMEM((nsq, GLA_H, GLA_DK, GLA_DV), F32)] if chained else []
    return _rec_call(kern, proj,
                     [("gla_q", 256), ("gla_k", 256), ("gla_v", 512), ("gla_r", 512), ("gla_a", 128)],
                     list(weights) + list(_gla_consts(tv)), [s0], [MIX], rows=rows,
                     chained=chained, scratch=scratch, name=name, nsq=nsq, **kw)


def _ssd_kernel(*refs, rows, lb, tv, chained, nsq, n_alias, out_mode):
    row_refs = refs[:3]
    cw_ref, cb_ref, dtb_ref, alog_ref, dsk_ref, wn_ref, c0_ref, s0_ref = refs[3:11]
    y_out, co_ref, so_ref = refs[11 + n_alias:14 + n_alias]
    scr = refs[14 + n_alias:]
    nblk = rows // lb
    t_in = _row_in_block(rows, lb)
    valid = (t_in >= lb - tv) if tv < lb else None
    low64 = lax.broadcasted_iota(jnp.int32, (SSM_N, LANES), 1) < SSM_P

    def chunk(z_ref, x_ref, dt_ref, y_ref, s_scr, ext_scr, j):
        co = _state_slot(co_ref, out_mode)
        if chained:
            xb = x_ref[...]
            xin = xb.astype(F32)
            tail = ext_scr[...]
            r_s, c_s = _iotas(rows)
            t8 = lax.broadcasted_iota(jnp.int32, (SUBLANES, 1), 0)
            conv = cb_ref[...] + cw_ref[SSM_CONV - 1:SSM_CONV, :] * xin
            head = jnp.zeros((SUBLANES, SSM_CONV_DIM), F32)
            for d in range(1, SSM_CONV):
                tap = cw_ref[SSM_CONV - 1 - d:SSM_CONV - d, :]
                shifted = jnp.dot((c_s == r_s - d).astype(BF16), xb, preferred_element_type=F32)
                conv = conv + tap * shifted
                head = head + tap * jnp.where(t8 < d, pltpu.roll(tail, d, axis=0), 0.0)
            conv = jnp.concatenate([conv[:SUBLANES] + head, conv[SUBLANES:]], axis=0)
            co[j] = xin[rows - SUBLANES:, :]
            ext_scr[...] = xin[rows - SUBLANES:, :]
        else:
            ext_scr[0:SUBLANES, :] = jnp.zeros((SUBLANES, SSM_CONV_DIM), F32)
            xin = jnp.where(valid, x_ref[...].astype(F32), c0_ref[...].reshape(rows, SSM_CONV_DIM))
            ext_scr[SUBLANES:SUBLANES + rows, :] = xin
            conv = cb_ref[...]
            for i in range(SSM_CONV):
                conv = conv + cw_ref[i:i + 1, :] * ext_scr[pl.ds(SUBLANES - (SSM_CONV - 1) + i, rows), :]
            co[...] = xin.reshape(nblk, lb, SSM_CONV_DIM)
        act = _silu(conv)
        yield

        dt = jax.nn.softplus(dt_ref[...].astype(F32) + dtb_ref[...])
        gdt = dt * (-jnp.exp(alog_ref[...]))
        if valid is not None:
            gdt = jnp.where(valid, gdt, 0.0)
        r_i, c_i = _iotas(rows)
        same = (r_i // lb) == (c_i // lb)
        causal = same & (r_i >= c_i)
        gcum = _dot_sel(causal, gdt)
        if rows < LANES:
            gsq = jnp.concatenate([gcum, jnp.zeros((LANES - rows, LANES), F32)], axis=0)
            gcum_t = gsq.T[:, :rows]
        else:
            gcum_t = gcum.T
        gtot = gcum[rows - 1:rows, :] if nblk == 1 else _dot_sel(same, gdt)

        lane = lax.broadcasted_iota(jnp.int32, (rows, LANES), 1)
        low = lane < SSM_P
        bcol = act[:, MIX:MIX + LANES]
        ccol = act[:, MIX + LANES:MIX + 2 * LANES]
        bswap = pltpu.roll(bcol, SSM_N, axis=1)
        cswap = pltpu.roll(ccol, SSM_N, axis=1)
        b2 = (jnp.where(low, bcol, bswap), jnp.where(low, bswap, bcol))
        c2 = (jnp.where(low, ccol, cswap), jnp.where(low, cswap, ccol))
        cb = (_dot(jnp.where(low, ccol, 0.0), bcol, NT), _dot(jnp.where(low, 0.0, ccol), bcol, NT))

        def pair_lanes(x, p):
            return jnp.where(low[:x.shape[0]], x[:, 2 * p:2 * p + 1], x[:, 2 * p + 1:2 * p + 2])

        yield
        so = _state_slot(so_ref, out_mode) if not chained else None
        y_pairs = []
        for p in range(SSM_PAIRS):
            gi = p // (SSM_PAIRS // SSM_G)
            decs = []
            for h in (2 * p, 2 * p + 1):
                diff = jnp.minimum(gcum[:, h:h + 1] - gcum_t[h:h + 1, :], 0.0)
                decs.append(cb[gi] * jnp.where(causal, jnp.exp(diff), 0.0))
            xp = act[:, p * LANES:(p + 1) * LANES]
            vp = xp * pair_lanes(dt, p)
            if valid is not None:
                vp = jnp.where(valid, vp, 0.0)
            vbd = jnp.concatenate([jnp.where(low, vp, 0.0), jnp.where(low, 0.0, vp)], axis=0)
            g2 = pair_lanes(gcum, p)
            ge2 = pair_lanes(gtot, p)
            cin = c2[gi] * jnp.exp(g2)
            bout = b2[gi] * jnp.exp(ge2 - g2)
            if chained:
                cur = s_scr[p]
                bd = jnp.concatenate([jnp.where(low64, cur, 0.0), jnp.where(low64, 0.0, cur)], axis=0)
                if rows % LANES == 0:
                    o = _dot(jnp.concatenate(decs + [cin], axis=1).astype(BF16),
                             jnp.concatenate([vbd.astype(BF16), bd.astype(BF16)], axis=0))
                else:
                    o = _dot(jnp.concatenate(decs, axis=1), vbd) + _dot(cin, bd)
                u = _dot(bout, vp, TN)
                s_scr[p] = jnp.exp(ge2) * cur + jnp.where(low64, u[:SSM_N, :], u[SSM_N:, :])
            else:
                o = _dot(jnp.concatenate(decs, axis=1), vbd)
                halves = []
                for odd, (ci, bo, vv) in enumerate(
                        ((cin, bout, vp),
                         tuple(pltpu.roll(t, SSM_P, axis=1) for t in (cin, bout, vp)))):
                    h = 2 * p + odd
                    o_rows = []
                    for b in range(nblk):
                        rs = slice(b * lb, (b + 1) * lb)
                        cur = s0_ref[b, h]
                        o_rows.append(_dot(ci[rs, :SSM_N], cur))
                        so[b, h] = (jnp.exp(gtot[b * lb:b * lb + 1, h:h + 1]) * cur
                                    + _dot(bo[rs, :SSM_N], vv[rs, :SSM_P], TN))
                    halves.append(jnp.concatenate(o_rows, axis=0))
                o = o + jnp.concatenate(halves, axis=1)
            y_pairs.append(o + dsk_ref[:, p * LANES:(p + 1) * LANES] * xp)
            yield
        y = jnp.concatenate(y_pairs, axis=1) * _silu(z_ref[...].astype(F32))
        half = MIX // SSM_G
        y = jnp.concatenate([_rms(y[:, gi * half:(gi + 1) * half]) for gi in range(SSM_G)], axis=1)
        y_ref[...] = (y * wn_ref[...]).astype(BF16)

    if not chained:
        _interleave([chunk(*row_refs, y_out, None, scr[0], 0)])
        return
    s_all, ext_all = scr

    @pl.when(pl.program_id(1) == 0)
    def _():
        for j in range(nsq):
            for p in range(SSM_PAIRS):
                s_all[j, p] = jnp.concatenate([s0_ref[0, 2 * p], s0_ref[0, 2 * p + 1]], axis=1)
            ext_all[j, 0:SUBLANES, :] = c0_ref[0]

    view = lambda r, j: r.at[j] if nsq > 1 else r
    _interleave([chunk(*[view(r, j) for r in row_refs], view(y_out, j), s_all.at[j],
                       ext_all.at[j], j) for j in range(nsq)])

    @pl.when(pl.program_id(1) == pl.num_programs(1) - 1)
    def _():
        dst = _state_slot(so_ref, out_mode)
        for j in range(nsq):
            for p in range(SSM_PAIRS):
                dst[j, 2 * p] = s_all[j, p][:, :SSM_P]
                dst[j, 2 * p + 1] = s_all[j, p][:, SSM_P:]


def _ssd(proj, weights, c0, s0, *, lb, tv, rows, chained, name, nsq=1, **kw):
    kern = functools.partial(_ssd_kernel, rows=rows, lb=lb, tv=tv, chained=chained)
    if chained:
        scratch = [pltpu.VMEM((nsq, SSM_PAIRS, SSM_N, LANES), F32),
                   pltpu.VMEM((nsq, SUBLANES, SSM_CONV_DIM), F32)]
    else:
        scratch = [pltpu.VMEM((SUBLANES + rows, SSM_CONV_DIM), F32)]
    return _rec_call(kern, proj, [("ssm_z", 512), ("ssm_xbc", 768), ("ssm_dt", 128)],
                     weights, [c0, s0], [MIX], rows=rows, chained=chained, scratch=scratch,
                     name=name, nsq=nsq, **kw)


def _ret_kernel(*refs, rows, lb, tv, chained, nsq, n_alias, out_mode):
    row_refs = refs[:4]
    lg_ref, cos_ref, sin_ref, s0_ref = refs[4:8]
    y_out, so_ref = refs[8 + n_alias:10 + n_alias]
    scr = refs[10 + n_alias:]
    nblk = rows // lb
    t_col = _row_in_block(rows, lb)
    s_row = lax.broadcasted_iota(jnp.int32, (1, rows), 1) % lb
    n_col = jnp.maximum(t_col - (lb - tv) + 1, 0).astype(F32)
    n_row = jnp.maximum(s_row - (lb - tv) + 1, 0).astype(F32)

    def decay_matrix(h):
        r_i, c_i = _iotas(rows)
        causal = ((r_i // lb) == (c_i // lb)) & (r_i >= c_i)
        diff = (n_col - n_row) * lg_ref[:, h:h + 1]
        return jnp.where(causal, jnp.exp(jnp.minimum(diff, 0.0)), 0.0)

    width = RET_H * RET_DK

    def state_scales():
        head = lax.broadcasted_iota(jnp.int32, (1, width), 1) // RET_DK
        lg_lane = lg_ref[:, 0:1]
        for h in range(1, RET_H):
            lg_lane = jnp.where(head == h, lg_ref[:, h:h + 1], lg_lane)
        gc = n_col * lg_lane
        return jnp.exp(gc), jnp.exp(float(tv) * lg_lane - gc)

    if chained:
        dec_scr, scale_scr = scr[1:]

        @pl.when(pl.program_id(1) == 0)
        def _():
            for h in range(RET_H):
                dec_scr[h] = decay_matrix(h)
            scale_scr[0], scale_scr[1] = state_scales()

    lane = lax.broadcasted_iota(jnp.int32, (rows, width), 1)
    first_half = (lane % RET_DK) < (RET_DK // 2)
    cos = cos_ref[...]
    sin = jnp.where(first_half, -sin_ref[...], sin_ref[...])

    def rope(x):
        partner = jnp.where(first_half, pltpu.roll(x, width - RET_DK // 2, axis=1),
                            pltpu.roll(x, RET_DK // 2, axis=1))
        return x * cos + partner * sin

    def chunk(q_ref, k_ref, v_ref, g_ref, y_ref, s_scr):
        q = rope(q_ref[...].astype(F32))
        k = rope(k_ref[...].astype(F32)) * (RET_DK ** -0.5)
        v = v_ref[...]
        if tv < lb:
            k = jnp.where(t_col >= lb - tv, k, 0.0)
        q_scale, k_scale = (scale_scr[0], scale_scr[1]) if chained else state_scales()
        q_in = q * q_scale
        k_out = k * k_scale
        yield

        so = _state_slot(so_ref, out_mode) if not chained else None
        outs = []
        for h in range(RET_H):
            ks = slice(h * RET_DK, (h + 1) * RET_DK)
            vs = slice(h * RET_DV, (h + 1) * RET_DV)
            att = _dot(q[:, ks], k[:, ks], NT) * (dec_scr[h] if chained else decay_matrix(h))
            ge = float(tv) * lg_ref[:, h:h + 1]
            qin = q_in[:, ks]
            kout = k_out[:, ks]
            if chained:
                cur = s_scr[h]
                if rows % LANES == 0:
                    oh = _dot(jnp.concatenate([att.astype(BF16), qin.astype(BF16)], axis=1),
                              jnp.concatenate([v[:, vs], cur.astype(BF16)], axis=0))
                else:
                    oh = _dot(att, v[:, vs]) + _dot(qin, cur)
                s_scr[h] = jnp.exp(ge) * cur + _dot(kout, v[:, vs], TN)
            else:
                oh = _dot(att, v[:, vs])
                o_rows = []
                for b in range(nblk):
                    rs = slice(b * lb, (b + 1) * lb)
                    cur = s0_ref[b, h]
                    o_rows.append(_dot(qin[rs], cur))
                    so[b, h] = jnp.exp(ge) * cur + _dot(kout[rs], v[rs, vs], TN)
                oh = oh + jnp.concatenate(o_rows, axis=0)
            outs.append(_rms(oh))
            yield
        y_ref[...] = (_silu(g_ref[...].astype(F32)) * jnp.concatenate(outs, axis=1)).astype(BF16)

    _run_chunks(chunk, row_refs, y_out, s0_ref, so_ref, scr[:1], chained, nsq, out_mode)


def _ret(proj, lg, cos, sin, s0, *, lb, tv, rows, chained, name, nsq=1, **kw):
    kern = functools.partial(_ret_kernel, rows=rows, lb=lb, tv=tv, chained=chained)
    scratch = ([pltpu.VMEM((nsq, RET_H, RET_DK, RET_DV), F32), pltpu.VMEM((RET_H, rows, rows), F32),
                pltpu.VMEM((2, rows, RET_H * RET_DK), F32)] if chained else [])
    return _rec_call(kern, proj, [("ret_q", 256), ("ret_k", 256), ("ret_v", 512), ("ret_g", 512)],
                     [(lg, 0)], [s0], [MIX], rows=rows, chained=chained, scratch=scratch, name=name,
                     tables=(cos, sin), nsq=nsq, **kw)


def _dense2_kernel(x_ref, gate_ref, yg_ref, ys_ref, yr_ref, lnw_ref, lnb_ref, wg_ref, ws_ref, wr_ref,
                   wo_ref, l1w_ref, l1b_ref, w1_ref, b1_ref, w2_ref, b2_ref, l2w_ref, l2b_ref, o_ref,
                   *, ff_chunk, parts, input_ln):
    tm = x_ref.shape[0]

    def rows_stage(rs):
        branches = [_dot(y_ref[rs, :], w_ref[...])
                    for y_ref, w_ref in ((yg_ref, wg_ref), (ys_ref, ws_ref), (yr_ref, wr_ref))]
        yield
        gate = lambda i: jax.nn.sigmoid(gate_ref[rs, i * D_MODEL:(i + 1) * D_MODEL].astype(F32))
        merged = gate(0) * branches[0] + gate(1) * branches[1] + gate(2) * branches[2]
        yield
        mix = _dot(merged, wo_ref[...])
        yield
        x = x_ref[rs, :]
        if input_ln:
            x = _layer_norm(x, lnw_ref[...], lnb_ref[...])
        h = _layer_norm(ALPHA * x + mix, l1w_ref[...], l1b_ref[...])
        hb = h.astype(BF16)
        ff = jnp.zeros_like(h) + b2_ref[...]
        yield
        for c0 in range(0, D_FF, ff_chunk):
            hid = jnp.dot(hb, w1_ref[:, c0:c0 + ff_chunk], preferred_element_type=F32)
            hid = jnp.square(jnp.maximum(hid + b1_ref[:, c0:c0 + ff_chunk], 0.0))
            ff = ff + _dot(hid, w2_ref[c0:c0 + ff_chunk, :])
            yield
        o_ref[rs, :] = _layer_norm(ALPHA * h + ff, l2w_ref[...], l2b_ref[...])

    _interleave([rows_stage(pl.ds(i * (tm // parts), tm // parts)) for i in range(parts)],
                stagger=True)


def _dense2(x, proj, yg, ys, yr, lnw, lnb, wl, layer, *, input_ln, tm, name):
    m = x.shape[0]
    row = lambda w: pl.BlockSpec((tm, w), lambda i: (i, 0))
    vec = pl.BlockSpec((1, D_MODEL), lambda i: (0, 0))
    const = lambda r, w: pl.BlockSpec((None, r, w), lambda i: (layer, 0, 0),
                                      pipeline_mode=pl.Buffered(1))
    return pl.pallas_call(
        functools.partial(_dense2_kernel, ff_chunk=1024, parts=(2 if tm % 32 == 0 else 1),
                          input_ln=input_ln),
        grid=(m // tm,),
        in_specs=[row(D_MODEL), row(3 * D_MODEL), row(MIX), row(MIX), row(MIX), vec, vec,
                  const(MIX, D_MODEL), const(MIX, D_MODEL), const(MIX, D_MODEL),
                  const(D_MODEL, D_MODEL), const(1, D_MODEL), const(1, D_MODEL),
                  const(D_MODEL, D_FF), const(1, D_FF), const(D_FF, D_MODEL), const(1, D_MODEL),
                  const(1, D_MODEL), const(1, D_MODEL)],
        out_specs=row(D_MODEL),
        out_shape=jax.ShapeDtypeStruct((m, D_MODEL), F32),
        compiler_params=pltpu.CompilerParams(
            dimension_semantics=("parallel",), vmem_limit_bytes=VMEM_LIMIT),
        name=name,
    )(x, proj, yg, ys, yr, lnw, lnb, wl["w_gla_out"], wl["w_ssm_out"], wl["w_ret_out"], wl["w_o"],
      wl["ln1_w"], wl["ln1_b"], wl["w_ff1"], wl["b_ff1"], wl["w_ff2"], wl["b_ff2"],
      wl["ln2_w"], wl["ln2_b"])


def _pick_tile(n, pref):
    t = min(n, pref)
    while n % t or t % SUBLANES:
        t -= 1
    return t


def _rearrange_w_in(w):
    offs = [0]
    for s in SPLIT_SIZES:
        offs.append(offs[-1] + s)
    names = ("gla_q", "gla_k", "gla_v", "gla_r", "gla_a", "ssm_z", "ssm_xbc", "ssm_dt",
             "ret_q", "ret_k", "ret_v", "ret_g", "gates")
    seg = {n: w[..., offs[i]:offs[i + 1]].astype(BF16) for i, n in enumerate(names)}
    pad = lambda a: jnp.pad(a, ((0, 0),) * (a.ndim - 1) + ((0, LANES - a.shape[-1]),))
    order = sorted(COL, key=COL.get)
    parts = [pad(seg[n]) if n in ("gla_a", "ssm_dt") else seg[n] for n in order]
    return jnp.concatenate(parts, axis=-1)


def _rope_tables(pos):
    half = RET_DK // 2
    inv_freq = ROPE_BASE ** (-jnp.arange(half, dtype=F32) / half)
    ang = pos.astype(F32)[:, None] * inv_freq[None, :]
    cos = jnp.tile(jnp.cos(ang), (1, 2 * RET_H))
    sin = jnp.tile(jnp.sin(ang), (1, 2 * RET_H))
    return cos, sin


def kernel(x_prompt, x_sample, state_gla, state_ssm, state_conv, state_ret, meta_tokens,
           ln_in_w, ln_in_b, w_in, w_gla_a2, b_gla_a, w_gla_norm, conv_w, conv_b, dt_bias,
           a_log, d_skip, w_ssm_norm, w_gla_out, w_ssm_out, w_ret_out, w_o, ln1_w, ln1_b,
           w_ff1, b_ff1, w_ff2, b_ff2, ln2_w, ln2_b):
    bp, tp, d = x_prompt.shape
    bs, ts, _ = x_sample.shape
    assert d == D_MODEL and tp % CHUNK == 0 and w_in.shape[0] == DEPTH
    assert SSM_CONV - 1 <= SAMPLE_ROWS - ts
    nchunk = tp // CHUNK
    pad_rows = SAMPLE_ROWS - ts

    x_body = x_prompt.reshape(bp * tp, d)
    n_sample = bs * SAMPLE_ROWS
    rows_s = min(CHUNK, n_sample)
    assert n_sample % rows_s == 0 and n_sample % N_META == 0
    n_small = n_sample + CHUNK
    x_small = jnp.concatenate(
        [jnp.pad(x_sample, ((0, 0), (pad_rows, 0), (0, 0))).reshape(n_sample, d),
         meta_tokens.astype(F32), jnp.zeros((CHUNK - N_META, d), F32)], axis=0)

    cos_b, sin_b = _rope_tables(N_META + jnp.arange(tp, dtype=jnp.int32))
    cos_m, sin_m = _rope_tables(jnp.arange(N_META, dtype=jnp.int32))
    pos_tile = PAST_LEN - pad_rows + jnp.arange(SAMPLE_ROWS, dtype=jnp.int32)
    cos_s, sin_s = _rope_tables(jnp.tile(pos_tile, rows_s // SAMPLE_ROWS))
    lg_ret = jnp.pad(jnp.log1p(-jnp.exp2(-5.0 - jnp.arange(RET_H, dtype=F32))),
                     (0, LANES - RET_H)).reshape(1, 1, LANES)

    zero_gla = (jnp.zeros((1, 1, GLA_H, GLA_DK, GLA_DV), F32), 0)
    zero_ssm = (jnp.zeros((1, 1, SSM_H, SSM_N, SSM_P), F32), 0)
    zero_conv = (jnp.zeros((1, 1, SUBLANES, SSM_CONV_DIM), F32), 0)
    zero_ret = (jnp.zeros((1, 1, RET_H, RET_DK, RET_DV), F32), 0)
    from_meta = lambda a: (a[None], 0)
    conv_in = jnp.pad(state_conv, ((0, 0), (0, 0), (pad_rows - (SSM_CONV - 1), ts), (0, 0)))

    rowvec = lambda a: a.reshape(DEPTH, 1, -1)
    lane_pad = lambda a: jnp.pad(a, ((0, 0), (0, LANES - a.shape[1]))).reshape(DEPTH, 1, LANES)
    w_in_all = _rearrange_w_in(w_in)
    wl = dict(w_gla_out=w_gla_out.astype(BF16), w_ssm_out=w_ssm_out.astype(BF16),
              w_ret_out=w_ret_out.astype(BF16), w_o=w_o.astype(BF16),
              ln1_w=rowvec(ln1_w), ln1_b=rowvec(ln1_b),
              w_ff1=w_ff1.astype(BF16), b_ff1=rowvec(b_ff1),
              w_ff2=w_ff2.astype(BF16), b_ff2=rowvec(b_ff2),
              ln2_w=rowvec(ln2_w), ln2_b=rowvec(ln2_b))
    gla_w = [jnp.pad(w_gla_a2, ((0, 0), (0, LANES - GLA_RANK), (0, 0))), rowvec(b_gla_a),
             rowvec(w_gla_norm)]
    ssd_w = [conv_w, rowvec(conv_b), lane_pad(dt_bias), lane_pad(a_log),
             rowvec(jnp.repeat(d_skip, SSM_P, axis=1)), rowvec(w_ssm_norm)]
    ln_w, ln_b = ln_in_w.reshape(1, -1), ln_in_b.reshape(1, -1)

    tm_in = _pick_tile(bp * tp, 2048)
    tm_d2 = _pick_tile(bp * tp, 512)
    tm_d2s = _pick_tile(n_small, 384)
    tn = 1536
    meta = dict(row0=n_sample, nseq=1, nchunk=1, rows=N_META, lb=N_META, tv=N_META, chained=True,
                per_seq_state=False)
    body = dict(row0=0, nseq=bp, nchunk=nchunk, rows=CHUNK, lb=CHUNK, tv=CHUNK, chained=True,
                per_seq_state=False)
    seqs_per_step = lambda want: max(n for n in (1, 2, 4) if n <= want and bp % n == 0)
    samp = dict(row0=0, nseq=bs, nchunk=1, rows=rows_s, lb=SAMPLE_ROWS, tv=ts, chained=False,
                per_seq_state=True)

    names = ("gla_p", "gla_s", "ssm_p", "ssm_s", "conv_p", "conv_s", "ret_p", "ret_s")
    st = {k: None for k in names}
    stk = lambda *keys: [] if st[keys[0]] is None else [st[k] for k in keys]
    xb, xs = x_body, x_small
    for l in range(DEPTH):
        proj_s = _in_proj(xs, ln_w, ln_b, w_in_all, l, apply_ln=(l == 0), tm=n_small, tn=tn,
                          name=f"inproj_small_{l}")
        yg_m, sg_m = _gla(proj_s, gla_w, zero_gla, layer=l, name=f"gla_meta_{l}", **meta)
        ys_m, cv_m, ss_m = _ssd(proj_s, ssd_w, zero_conv, zero_ssm, layer=l,
                                name=f"ssd_meta_{l}", **meta)
        yr_m, sr_m = _ret(proj_s, lg_ret, cos_m, sin_m, zero_ret, layer=l,
                          name=f"ret_meta_{l}", **meta)
        yg_s, st["gla_s"] = _gla(proj_s, gla_w, (state_gla, l), layer=l, stacked=stk("gla_s"),
                                 name=f"gla_sample_{l}", **samp)
        ys_s, st["conv_s"], st["ssm_s"] = _ssd(proj_s, ssd_w, (conv_in, l), (state_ssm, l), layer=l,
                                               stacked=stk("conv_s", "ssm_s"),
                                               name=f"ssd_sample_{l}", **samp)
        yr_s, st["ret_s"] = _ret(proj_s, lg_ret, cos_s, sin_s, (state_ret, l), layer=l,
                                 stacked=stk("ret_s"), name=f"ret_sample_{l}", **samp)
        zpad = jnp.zeros((CHUNK - N_META, MIX), BF16)
        yg = jnp.concatenate([yg_s, yg_m, zpad], axis=0)
        ys = jnp.concatenate([ys_s, ys_m, zpad], axis=0)
        yr = jnp.concatenate([yr_s, yr_m, zpad], axis=0)
        xs = _dense2(xs, proj_s, yg, ys, yr, ln_w, ln_b, wl, l, input_ln=(l == 0), tm=tm_d2s,
                     name=f"dense2_small_{l}")

        proj_b = _in_proj(xb, ln_w, ln_b, w_in_all, l, apply_ln=(l == 0), tm=tm_in, tn=tn,
                          name=f"inproj_body_{l}")
        yg_b, st["gla_p"] = _gla(proj_b, gla_w, from_meta(sg_m), layer=l, stacked=stk("gla_p"),
                                 name=f"gla_body_{l}", nsq=seqs_per_step(4), **body)
        ys_b, st["conv_p"], st["ssm_p"] = _ssd(proj_b, ssd_w, from_meta(cv_m), from_meta(ss_m),
                                               layer=l, stacked=stk("conv_p", "ssm_p"),
                                               name=f"ssd_body_{l}", nsq=seqs_per_step(2), **body)
        yr_b, st["ret_p"] = _ret(proj_b, lg_ret, cos_b, sin_b, from_meta(sr_m), layer=l,
                                 stacked=stk("ret_p"), name=f"ret_body_{l}", nsq=seqs_per_step(4),
                                 **body)
        xb = _dense2(xb, proj_b, yg_b, ys_b, yr_b, ln_w, ln_b, wl, l, input_ln=(l == 0), tm=tm_d2,
                     name=f"dense2_body_{l}")

    y_prompt = xb.reshape(bp, tp, d)
    y_sample = xs[:n_sample].reshape(bs, SAMPLE_ROWS, d)[:, pad_rows:]
    tail3 = lambda c: c[:, :, SUBLANES - (SSM_CONV - 1):, :]
    return (y_prompt, y_sample, st["gla_p"], st["gla_s"], st["ssm_p"], st["ssm_s"],
            tail3(st["conv_p"]), tail3(st["conv_s"]), st["ret_p"], st["ret_s"])
```

```python
import functools

import jax
import jax.numpy as jnp
from jax import lax
from jax.experimental import pallas as pl
from jax.experimental.pallas import tpu as pltpu

F32 = jnp.float32
BF16 = jnp.bfloat16

D_MODEL = 1024
DEPTH = 2
N_META = 16
MIX = 512
GLA_H, GLA_DK, GLA_DV, GLA_RANK = 4, 64, 128, 16
GLA_GATE_NORM = 16.0
SSM_H, SSM_P, SSM_N, SSM_G, SSM_CONV = 8, 64, 64, 2, 4
SSM_CONV_DIM = MIX + 2 * SSM_G * SSM_N
SSM_PAIRS = SSM_H // 2
RET_H, RET_DK, RET_DV = 4, 64, 128
ROPE_BASE = 10000.0
D_FF = 4 * D_MODEL
ALPHA = (2 * DEPTH) ** 0.25
PAST_LEN = 16384
SPLIT_SIZES = (256, 256, 512, 512, 16, 512, 768, 8, 256, 256, 512, 512, 3072)

LANES = 128
SUBLANES = 8
VMEM_LIMIT = 56 * 1024 * 1024

COL = dict(gates=0, gla_v=3072, gla_r=3584, ssm_z=4096, ret_v=4608, ret_g=5120,
           gla_q=5632, gla_k=5888, ssm_xbc=6144, ret_q=6912, ret_k=7168, gla_a=7424, ssm_dt=7552)
N_PROJ = 7680
SAMPLE_ROWS = SUBLANES
GLA_BASE = SUBLANES
CHUNK = 128

NN = (((1,), (0,)), ((), ()))
NT = (((1,), (1,)), ((), ()))
TN = (((0,), (0,)), ((), ()))


def _dot(a, b, dims=NN):
    return lax.dot_general(a.astype(BF16), b.astype(BF16), dims, preferred_element_type=F32)


def _dot_sel(sel, x, dims=NN, sel_first=True):
    hi = x.astype(BF16)
    r1 = x - hi.astype(F32)
    mid = r1.astype(BF16)
    lo = (r1 - mid.astype(F32)).astype(BF16)
    sb = sel.astype(BF16)
    out = None
    for part in (hi, mid, lo):
        ops = (sb, part) if sel_first else (part, sb)
        term = lax.dot_general(*ops, dims, preferred_element_type=F32)
        out = term if out is None else out + term
    return out


def _layer_norm(x, w, b):
    mu = jnp.mean(x, axis=-1, keepdims=True)
    xc = x - mu
    var = jnp.mean(xc * xc, axis=-1, keepdims=True)
    return xc * lax.rsqrt(var + 1e-5) * w + b


def _rms(x):
    return x * lax.rsqrt(jnp.mean(x * x, axis=-1, keepdims=True) + 1e-6)


def _silu(x):
    return x * jax.nn.sigmoid(x)


def _iotas(rows):
    return (lax.broadcasted_iota(jnp.int32, (rows, rows), 0),
            lax.broadcasted_iota(jnp.int32, (rows, rows), 1))


def _row_in_block(rows, lb):
    return lax.broadcasted_iota(jnp.int32, (rows, 1), 0) % lb


def _inproj_kernel(x_ref, lnw_ref, lnb_ref, w_ref, proj_ref, xb_scr, *, apply_ln, parts):
    tm = x_ref.shape[0]

    def first_tile(rs):
        x = x_ref[rs, :]
        if apply_ln:
            x = _layer_norm(x, lnw_ref[...], lnb_ref[...])
        xb = x.astype(BF16)
        xb_scr[rs, :] = xb
        yield
        proj_ref[rs, :] = lax.dot_general(xb, w_ref[...], NT, preferred_element_type=F32).astype(BF16)

    @pl.when(pl.program_id(1) == 0)
    def _():
        _interleave([first_tile(pl.ds(p * (tm // parts), tm // parts)) for p in range(parts)],
                    stagger=True)

    @pl.when(pl.program_id(1) > 0)
    def _():
        proj_ref[...] = lax.dot_general(xb_scr[...], w_ref[...], NT,
                                        preferred_element_type=F32).astype(BF16)


def _in_proj(x, lnw, lnb, w, layer, *, apply_ln, tm, tn, name):
    m = x.shape[0]
    grid = (m // tm, N_PROJ // tn)
    return pl.pallas_call(
        functools.partial(_inproj_kernel, apply_ln=apply_ln,
                          parts=(4 if apply_ln and tm % 64 == 0 else 1)),
        grid=grid,
        in_specs=[pl.BlockSpec((tm, D_MODEL), lambda i, j: (i, 0)),
                  pl.BlockSpec((1, D_MODEL), lambda i, j: (0, 0)),
                  pl.BlockSpec((1, D_MODEL), lambda i, j: (0, 0)),
                  pl.BlockSpec((None, tn, D_MODEL), lambda i, j: (layer, j, 0))],
        out_specs=pl.BlockSpec((tm, tn), lambda i, j: (i, j)),
        out_shape=jax.ShapeDtypeStruct((m, N_PROJ), BF16),
        scratch_shapes=[pltpu.VMEM((tm, D_MODEL), BF16)],
        compiler_params=pltpu.CompilerParams(
            dimension_semantics=("parallel", "arbitrary"), vmem_limit_bytes=VMEM_LIMIT),
        name=name,
    )(x, lnw, lnb, w)


def _rec_call(kernel_fn, proj, segs, consts, states, out_widths, *, layer, row0, nseq, nchunk,
              rows, chained, per_seq_state, scratch, name, tables=(), stacked=None, nsq=1):
    rb = row0 // rows
    nb = nsq if chained else rows // SAMPLE_ROWS
    grid = (nseq // nb, nchunk) if chained else (nseq // nb, 1)
    assert nseq % nb == 0 and (nsq == 1 or (chained and row0 == 0))
    out_mode = "plain" if stacked is None else ("first" if layer == 0 else "later")
    aliased = list(stacked) if out_mode == "later" else []

    def row_idx(b, c):
        return b * nchunk + c if chained else b

    def col(seg, w):
        cbi = COL[seg] // w
        if nsq > 1:
            return pl.BlockSpec((nsq, rows, w), lambda b, c: (b, c, cbi))
        return pl.BlockSpec((rows, w), lambda b, c: (rb + row_idx(b, c), cbi))

    def row_out(w):
        if nsq > 1:
            return pl.BlockSpec((nsq, rows, w), lambda b, c: (b, c, 0))
        return pl.BlockSpec((rows, w), lambda b, c: (row_idx(b, c), 0))

    def const_spec(a, lyr):
        zeros = (0,) * (a.ndim - 1)
        return pl.BlockSpec((None,) + a.shape[1:], lambda b, c: (lyr,) + zeros)

    def state_in_spec(a, lyr):
        zeros = (0,) * (a.ndim - 2)
        if per_seq_state:
            return pl.BlockSpec((None, nb) + a.shape[2:], lambda b, c: (lyr, b) + zeros)
        return pl.BlockSpec((None, 1) + a.shape[2:], lambda b, c: (lyr, 0) + zeros)

    def state_out_spec(a):
        zeros = (0,) * (a.ndim - 2)
        if out_mode == "plain":
            return pl.BlockSpec((nb,) + a.shape[2:], lambda b, c: (b,) + zeros)
        if out_mode == "first":
            return pl.BlockSpec((DEPTH, nb) + a.shape[2:], lambda b, c: (0, b) + zeros)
        return pl.BlockSpec((None, nb) + a.shape[2:], lambda b, c: (layer, b) + zeros)

    def state_out_shape(a):
        lead = (nseq,) if out_mode == "plain" else (DEPTH, nseq)
        return jax.ShapeDtypeStruct(lead + a.shape[2:], F32)

    consts = [c if isinstance(c, tuple) else (c, layer) for c in consts]
    in_specs = ([col(s, w) for s, w in segs] + [const_spec(a, lyr) for a, lyr in consts]
                + [pl.BlockSpec((rows, t.shape[1]), lambda b, c: (c, 0)) for t in tables]
                + [state_in_spec(a, lyr) for a, lyr in states]
                + [pl.BlockSpec(memory_space=pl.ANY) for _ in aliased])
    n_in = len(in_specs) - len(aliased)
    n_rows = nseq * nchunk * rows if chained else nseq * SAMPLE_ROWS
    out_specs = [row_out(w) for w in out_widths] + [state_out_spec(a) for a, _ in states]
    row_shape = (lambda w: (nseq, nchunk * rows, w)) if nsq > 1 else (lambda w: (n_rows, w))
    out_shape = ([jax.ShapeDtypeStruct(row_shape(w), BF16) for w in out_widths]
                 + [state_out_shape(a) for a, _ in states])
    if nsq > 1:
        proj = proj.reshape(nseq, nchunk * rows, proj.shape[-1])
    res = pl.pallas_call(
        functools.partial(kernel_fn, n_alias=len(aliased), out_mode=out_mode, nsq=nsq),
        grid=grid, in_specs=in_specs, out_specs=out_specs, out_shape=out_shape,
        scratch_shapes=scratch,
        input_output_aliases={n_in + i: len(out_widths) + i for i in range(len(aliased))},
        compiler_params=pltpu.CompilerParams(
            dimension_semantics=("parallel", "arbitrary"), vmem_limit_bytes=VMEM_LIMIT),
        name=name,
    )(*([proj] * len(segs)), *[a for a, _ in consts], *tables, *[a for a, _ in states], *aliased)
    if nsq > 1:
        res = ([r.reshape(n_rows, r.shape[-1]) for r in res[:len(out_widths)]]
               + list(res[len(out_widths):]))
    return res


def _state_slot(ref, out_mode):
    if out_mode != "first":
        return ref
    ref[1:] = jnp.zeros((DEPTH - 1,) + ref.shape[1:], F32)
    return ref.at[0]


def _interleave(stages, stagger=False):
    stages = list(stages)
    delay = {id(gen): (i if stagger else 0) for i, gen in enumerate(stages)}
    rnd = 0
    while stages:
        for gen in list(stages):
            if delay[id(gen)] > rnd:
                continue
            try:
                next(gen)
            except StopIteration:
                stages.remove(gen)
        rnd += 1


def _run_chunks(chunk, row_refs, y_ref, s0_ref, so_ref, scr, chained, nsq, out_mode):
    if not chained:
        _interleave([chunk(*row_refs, y_ref, None)])
        return
    s_all, = scr

    @pl.when(pl.program_id(1) == 0)
    def _():
        for j in range(nsq):
            s_all[j] = s0_ref[0]

    view = lambda r, j: r.at[j] if nsq > 1 else r
    _interleave([chunk(*[view(r, j) for r in row_refs], view(y_ref, j), s_all.at[j])
                 for j in range(nsq)])

    @pl.when(pl.program_id(1) == pl.num_programs(1) - 1)
    def _():
        dst = _state_slot(so_ref, out_mode)
        for j in range(nsq):
            dst[j] = s_all[j]


def _gla_kernel(*refs, rows, lb, tv, chained, nsq, n_alias, out_mode):
    row_refs = refs[:5]
    w2_ref, ba_ref, wn_ref, eb_ref, sh_ref, s0_ref = refs[5:11]
    y_out, so_ref = refs[11 + n_alias:13 + n_alias]
    scr = refs[13 + n_alias:]
    nblk = rows // lb
    width = GLA_H * GLA_DK

    def chunk(q_ref, k_ref, v_ref, r_ref, a_ref, y_ref, s_scr):
        q = q_ref[...].astype(F32) * (GLA_DK ** -0.5)
        k = k_ref[...].astype(F32)
        vb = v_ref[...]
        a = _dot(a_ref[...], w2_ref[...]) + ba_ref[...]
        g = jax.nn.log_sigmoid(a) * (1.0 / GLA_GATE_NORM)
        t_in = _row_in_block(rows, lb)
        if tv < lb:
            valid = t_in >= lb - tv
            g = jnp.where(valid, g, 0.0)
            k = jnp.where(valid, k, 0.0)
        yield
        r_i, c_i = _iotas(rows)
        same = (r_i // lb) == (c_i // lb)
        sizes = []
        while 2 * GLA_BASE * 2 ** len(sizes) <= lb:
            sizes.append(2 * GLA_BASE * 2 ** len(sizes))
        sums = [same & (r_i >= c_i)] + ([same] if nblk > 1 else [])
        sums += [same & (c_i <= (r_i // sz) * sz + sz // 2 - 1) for sz in sizes]
        gsums = _dot_sel(jnp.concatenate(sums, axis=0), g)
        gcum = gsums[:rows]
        gtot = gcum[rows - 1:rows, :] if nblk == 1 else gsums[rows:2 * rows]
        g_mids = [gsums[(len(sums) - len(sizes) + i) * rows:(len(sums) - len(sizes) + i + 1) * rows]
                  for i in range(len(sizes))]
        qd = q * jnp.exp(gcum)
        kd = k * jnp.exp(gtot - gcum)
        sel = (lax.broadcasted_iota(jnp.int32, (rows, nblk * LANES), 0) // lb
               == lax.broadcasted_iota(jnp.int32, (rows, nblk * LANES), 1) // LANES)
        ds = jnp.exp(_dot_sel(sel, g, TN, sel_first=False))
        yield

        nbase = rows // GLA_BASE
        q3 = q.reshape(nbase, GLA_BASE, width)
        k3 = k.reshape(nbase, GLA_BASE, width)
        g3 = gcum.reshape(nbase, GLA_BASE, width)
        t3 = lax.broadcasted_iota(jnp.int32, (1, GLA_BASE, 1), 1)
        pieces = []
        for s in range(max(0, GLA_BASE - tv), GLA_BASE):
            dd = jnp.minimum(g3 - g3[:, s:s + 1, :], 0.0)
            w = q3 * k3[:, s:s + 1, :] * jnp.exp(dd)
            pieces.append(jnp.where(t3 >= s, w, 0.0).reshape(rows, width).astype(BF16))
            yield
        compact = jnp.dot(jnp.concatenate(pieces, axis=1), eb_ref[...], preferred_element_type=F32)
        spread = jnp.dot(compact.astype(BF16), sh_ref[...], preferred_element_type=F32)
        base_mask = (r_i // GLA_BASE) == (c_i // GLA_BASE)
        att = [jnp.where(base_mask, spread[:, h * LANES:h * LANES + rows], 0.0) for h in range(GLA_H)]
        yield

        for size, g_mid in zip(sizes, g_mids):
            second = (t_in % size) >= size // 2
            ql = jnp.where(second, q * jnp.exp(jnp.minimum(gcum - g_mid, 0.0)), 0.0)
            kl = jnp.where(second, 0.0, k * jnp.exp(jnp.minimum(g_mid - gcum, 0.0)))
            group = (r_i // size) == (c_i // size)
            for h in range(GLA_H):
                ks = slice(h * GLA_DK, (h + 1) * GLA_DK)
                att[h] = att[h] + jnp.where(group, _dot(ql[:, ks], kl[:, ks], NT), 0.0)
            yield

        so = _state_slot(so_ref, out_mode) if not chained else None
        o_heads = []
        for h in range(GLA_H):
            ks = slice(h * GLA_DK, (h + 1) * GLA_DK)
            vs = slice(h * GLA_DV, (h + 1) * GLA_DV)
            if chained:
                cur = s_scr[h]
                if rows % LANES == 0:
                    oh = _dot(jnp.concatenate([att[h].astype(BF16), qd[:, ks].astype(BF16)], axis=1),
                              jnp.concatenate([vb[:, vs], cur.astype(BF16)], axis=0))
                else:
                    oh = _dot(att[h], vb[:, vs]) + _dot(qd[:, ks], cur)
                s_scr[h] = ds[ks, :] * cur + _dot(kd[:, ks], vb[:, vs], TN)
            else:
                o_rows = []
                for b in range(nblk):
                    rs = slice(b * lb, (b + 1) * lb)
                    cur = s0_ref[b, h]
                    o_rows.append(_dot(qd[rs, ks], cur))
                    so[b, h] = ds[ks, b * LANES:(b + 1) * LANES] * cur + _dot(kd[rs, ks], vb[rs, vs], TN)
                oh = _dot(att[h], vb[:, vs]) + jnp.concatenate(o_rows, axis=0)
            o_heads.append(_rms(oh) * wn_ref[...])
            yield
        y_ref[...] = (_silu(r_ref[...].astype(F32)) * jnp.concatenate(o_heads, axis=1)).astype(BF16)

    _run_chunks(chunk, row_refs, y_out, s0_ref, so_ref, scr, chained, nsq, out_mode)


def _gla_consts(tv):
    sources = jnp.arange(max(0, GLA_BASE - tv), GLA_BASE, dtype=jnp.int32)
    row = jnp.arange(sources.shape[0] * GLA_H * GLA_DK, dtype=jnp.int32)
    target = sources[row // (GLA_H * GLA_DK)] * GLA_H + (row % (GLA_H * GLA_DK)) // GLA_DK
    eb = (target[:, None] == jnp.arange(LANES, dtype=jnp.int32)[None, :]).astype(BF16)
    r = jnp.arange(LANES, dtype=jnp.int32)[:, None]
    c = jnp.arange(GLA_H * LANES, dtype=jnp.int32)[None, :]
    sh = ((r < GLA_BASE * GLA_H) & (r % GLA_H == c // LANES)
          & (r // GLA_H == (c % LANES) % GLA_BASE)).astype(BF16)
    return (eb[None], 0), (sh[None], 0)


def _gla(proj, weights, s0, *, lb, tv, rows, chained, name, nsq=1, **kw):
    kern = functools.partial(_gla_kernel, rows=rows, lb=lb, tv=tv, chained=chained)
    scratch = [pltpu.VMEM((nsq, GLA_H, GLA_DK, GLA_DV), F32)] if chained else []
    return _rec_call(kern, proj,
                     [("gla_q", 256), ("gla_k", 256), ("gla_v", 512), ("gla_r", 512), ("gla_a", 128)],
                     list(weights) + list(_gla_consts(tv)), [s0], [MIX], rows=rows,
                     chained=chained, scratch=scratch, name=name, nsq=nsq, **kw)


def _ssd_kernel(*refs, rows, lb, tv, chained, nsq, n_alias, out_mode):
    row_refs = refs[:3]
    cw_ref, cb_ref, dtb_ref, alog_ref, dsk_ref, wn_ref, c0_ref, s0_ref = refs[3:11]
    y_out, co_ref, so_ref = refs[11 + n_alias:14 + n_alias]
    scr = refs[14 + n_alias:]
    nblk = rows // lb
    t_in = _row_in_block(rows, lb)
    valid = (t_in >= lb - tv) if tv < lb else None
    low64 = lax.broadcasted_iota(jnp.int32, (SSM_N, LANES), 1) < SSM_P

    def chunk(z_ref, x_ref, dt_ref, y_ref, s_scr, ext_scr, j):
        co = _state_slot(co_ref, out_mode)
        if chained:
            xb = x_ref[...]
            xin = xb.astype(F32)
            tail = ext_scr[...]
            r_s, c_s = _iotas(rows)
            t8 = lax.broadcasted_iota(jnp.int32, (SUBLANES, 1), 0)
            conv = cb_ref[...] + cw_ref[SSM_CONV - 1:SSM_CONV, :] * xin
            head = jnp.zeros((SUBLANES, SSM_CONV_DIM), F32)
            for d in range(1, SSM_CONV):
                tap = cw_ref[SSM_CONV - 1 - d:SSM_CONV - d, :]
                shifted = jnp.dot((c_s == r_s - d).astype(BF16), xb, preferred_element_type=F32)
                conv = conv + tap * shifted
                head = head + tap * jnp.where(t8 < d, pltpu.roll(tail, d, axis=0), 0.0)
            conv = jnp.concatenate([conv[:SUBLANES] + head, conv[SUBLANES:]], axis=0)
            co[j] = xin[rows - SUBLANES:, :]
            ext_scr[...] = xin[rows - SUBLANES:, :]
        else:
            ext_scr[0:SUBLANES, :] = jnp.zeros((SUBLANES, SSM_CONV_DIM), F32)
            xin = jnp.where(valid, x_ref[...].astype(F32), c0_ref[...].reshape(rows, SSM_CONV_DIM))
            ext_scr[SUBLANES:SUBLANES + rows, :] = xin
            conv = cb_ref[...]
            for i in range(SSM_CONV):
                conv = conv + cw_ref[i:i + 1, :] * ext_scr[pl.ds(SUBLANES - (SSM_CONV - 1) + i, rows), :]
            co[...] = xin.reshape(nblk, lb, SSM_CONV_DIM)
        act = _silu(conv)
        yield

        dt = jax.nn.softplus(dt_ref[...].astype(F32) + dtb_ref[...])
        gdt = dt * (-jnp.exp(alog_ref[...]))
        if valid is not None:
            gdt = jnp.where(valid, gdt, 0.0)
        r_i, c_i = _iotas(rows)
        same = (r_i // lb) == (c_i // lb)
        causal = same & (r_i >= c_i)
        gcum = _dot_sel(causal, gdt)
        if rows < LANES:
            gsq = jnp.concatenate([gcum, jnp.zeros((LANES - rows, LANES), F32)], axis=0)
            gcum_t = gsq.T[:, :rows]
        else:
            gcum_t = gcum.T
        gtot = gcum[rows - 1:rows, :] if nblk == 1 else _dot_sel(same, gdt)

        lane = lax.broadcasted_iota(jnp.int32, (rows, LANES), 1)
        low = lane < SSM_P
        bcol = act[:, MIX:MIX + LANES]
        ccol = act[:, MIX + LANES:MIX + 2 * LANES]
        bswap = pltpu.roll(bcol, SSM_N, axis=1)
        cswap = pltpu.roll(ccol, SSM_N, axis=1)
        b2 = (jnp.where(low, bcol, bswap), jnp.where(low, bswap, bcol))
        c2 = (jnp.where(low, ccol, cswap), jnp.where(low, cswap, ccol))
        cb = (_dot(jnp.where(low, ccol, 0.0), bcol, NT), _dot(jnp.where(low, 0.0, ccol), bcol, NT))

        def pair_lanes(x, p):
            return jnp.where(low[:x.shape[0]], x[:, 2 * p:2 * p + 1], x[:, 2 * p + 1:2 * p + 2])

        yield
        so = _state_slot(so_ref, out_mode) if not chained else None
        y_pairs = []
        for p in range(SSM_PAIRS):
            gi = p // (SSM_PAIRS // SSM_G)
            decs = []
            for h in (2 * p, 2 * p + 1):
                diff = jnp.minimum(gcum[:, h:h + 1] - gcum_t[h:h + 1, :], 0.0)
                decs.append(cb[gi] * jnp.where(causal, jnp.exp(diff), 0.0))
            xp = act[:, p * LANES:(p + 1) * LANES]
            vp = xp * pair_lanes(dt, p)
            if valid is not None:
                vp = jnp.where(valid, vp, 0.0)
            vbd = jnp.concatenate([jnp.where(low, vp, 0.0), jnp.where(low, 0.0, vp)], axis=0)
            g2 = pair_lanes(gcum, p)
            ge2 = pair_lanes(gtot, p)
            cin = c2[gi] * jnp.exp(g2)
            bout = b2[gi] * jnp.exp(ge2 - g2)
            if chained:
                cur = s_scr[p]
                bd = jnp.concatenate([jnp.where(low64, cur, 0.0), jnp.where(low64, 0.0, cur)], axis=0)
                if rows % LANES == 0:
                    o = _dot(jnp.concatenate(decs + [cin], axis=1).astype(BF16),
                             jnp.concatenate([vbd.astype(BF16), bd.astype(BF16)], axis=0))
                else:
                    o = _dot(jnp.concatenate(decs, axis=1), vbd) + _dot(cin, bd)
                u = _dot(bout, vp, TN)
                s_scr[p] = jnp.exp(ge2) * cur + jnp.where(low64, u[:SSM_N, :], u[SSM_N:, :])
            else:
                o = _dot(jnp.concatenate(decs, axis=1), vbd)
                halves = []
                for odd, (ci, bo, vv) in enumerate(
                        ((cin, bout, vp),
                         tuple(pltpu.roll(t, SSM_P, axis=1) for t in (cin, bout, vp)))):
                    h = 2 * p + odd
                    o_rows = []
                    for b in range(nblk):
                        rs = slice(b * lb, (b + 1) * lb)
                        cur = s0_ref[b, h]
                        o_rows.append(_dot(ci[rs, :SSM_N], cur))
                        so[b, h] = (jnp.exp(gtot[b * lb:b * lb + 1, h:h + 1]) * cur
                                    + _dot(bo[rs, :SSM_N], vv[rs, :SSM_P], TN))
                    halves.append(jnp.concatenate(o_rows, axis=0))
                o = o + jnp.concatenate(halves, axis=1)
            y_pairs.append(o + dsk_ref[:, p * LANES:(p + 1) * LANES] * xp)
            yield
        y = jnp.concatenate(y_pairs, axis=1) * _silu(z_ref[...].astype(F32))
        half = MIX // SSM_G
        y = jnp.concatenate([_rms(y[:, gi * half:(gi + 1) * half]) for gi in range(SSM_G)], axis=1)
        y_ref[...] = (y * wn_ref[...]).astype(BF16)

    if not chained:
        _interleave([chunk(*row_refs, y_out, None, scr[0], 0)])
        return
    s_all, ext_all = scr

    @pl.when(pl.program_id(1) == 0)
    def _():
        for j in range(nsq):
            for p in range(SSM_PAIRS):
                s_all[j, p] = jnp.concatenate([s0_ref[0, 2 * p], s0_ref[0, 2 * p + 1]], axis=1)
            ext_all[j, 0:SUBLANES, :] = c0_ref[0]

    view = lambda r, j: r.at[j] if nsq > 1 else r
    _interleave([chunk(*[view(r, j) for r in row_refs], view(y_out, j), s_all.at[j],
                       ext_all.at[j], j) for j in range(nsq)])

    @pl.when(pl.program_id(1) == pl.num_programs(1) - 1)
    def _():
        dst = _state_slot(so_ref, out_mode)
        for j in range(nsq):
            for p in range(SSM_PAIRS):
                dst[j, 2 * p] = s_all[j, p][:, :SSM_P]
                dst[j, 2 * p + 1] = s_all[j, p][:, SSM_P:]


def _ssd(proj, weights, c0, s0, *, lb, tv, rows, chained, name, nsq=1, **kw):
    kern = functools.partial(_ssd_kernel, rows=rows, lb=lb, tv=tv, chained=chained)
    if chained:
        scratch = [pltpu.VMEM((nsq, SSM_PAIRS, SSM_N, LANES), F32),
                   pltpu.VMEM((nsq, SUBLANES, SSM_CONV_DIM), F32)]
    else:
        scratch = [pltpu.VMEM((SUBLANES + rows, SSM_CONV_DIM), F32)]
    return _rec_call(kern, proj, [("ssm_z", 512), ("ssm_xbc", 768), ("ssm_dt", 128)],
                     weights, [c0, s0], [MIX], rows=rows, chained=chained, scratch=scratch,
                     name=name, nsq=nsq, **kw)


def _ret_kernel(*refs, rows, lb, tv, chained, nsq, n_alias, out_mode):
    row_refs = refs[:4]
    lg_ref, cos_ref, sin_ref, s0_ref = refs[4:8]
    y_out, so_ref = refs[8 + n_alias:10 + n_alias]
    scr = refs[10 + n_alias:]
    nblk = rows // lb
    t_col = _row_in_block(rows, lb)
    s_row = lax.broadcasted_iota(jnp.int32, (1, rows), 1) % lb
    n_col = jnp.maximum(t_col - (lb - tv) + 1, 0).astype(F32)
    n_row = jnp.maximum(s_row - (lb - tv) + 1, 0).astype(F32)

    def decay_matrix(h):
        r_i, c_i = _iotas(rows)
        causal = ((r_i // lb) == (c_i // lb)) & (r_i >= c_i)
        diff = (n_col - n_row) * lg_ref[:, h:h + 1]
        return jnp.where(causal, jnp.exp(jnp.minimum(diff, 0.0)), 0.0)

    width = RET_H * RET_DK

    def state_scales():
        head = lax.broadcasted_iota(jnp.int32, (1, width), 1) // RET_DK
        lg_lane = lg_ref[:, 0:1]
        for h in range(1, RET_H):
            lg_lane = jnp.where(head == h, lg_ref[:, h:h + 1], lg_lane)
        gc = n_col * lg_lane
        return jnp.exp(gc), jnp.exp(float(tv) * lg_lane - gc)

    if chained:
        dec_scr, scale_scr = scr[1:]

        @pl.when(pl.program_id(1) == 0)
        def _():
            for h in range(RET_H):
                dec_scr[h] = decay_matrix(h)
            scale_scr[0], scale_scr[1] = state_scales()

    lane = lax.broadcasted_iota(jnp.int32, (rows, width), 1)
    first_half = (lane % RET_DK) < (RET_DK // 2)
    cos = cos_ref[...]
    sin = jnp.where(first_half, -sin_ref[...], sin_ref[...])

    def rope(x):
        partner = jnp.where(first_half, pltpu.roll(x, width - RET_DK // 2, axis=1),
                            pltpu.roll(x, RET_DK // 2, axis=1))
        return x * cos + partner * sin

    def chunk(q_ref, k_ref, v_ref, g_ref, y_ref, s_scr):
        q = rope(q_ref[...].astype(F32))
        k = rope(k_ref[...].astype(F32)) * (RET_DK ** -0.5)
        v = v_ref[...]
        if tv < lb:
            k = jnp.where(t_col >= lb - tv, k, 0.0)
        q_scale, k_scale = (scale_scr[0], scale_scr[1]) if chained else state_scales()
        q_in = q * q_scale
        k_out = k * k_scale
        yield

        so = _state_slot(so_ref, out_mode) if not chained else None
        outs = []
        for h in range(RET_H):
            ks = slice(h * RET_DK, (h + 1) * RET_DK)
            vs = slice(h * RET_DV, (h + 1) * RET_DV)
            att = _dot(q[:, ks], k[:, ks], NT) * (dec_scr[h] if chained else decay_matrix(h))
            ge = float(tv) * lg_ref[:, h:h + 1]
            qin = q_in[:, ks]
            kout = k_out[:, ks]
            if chained:
                cur = s_scr[h]
                if rows % LANES == 0:
                    oh = _dot(jnp.concatenate([att.astype(BF16), qin.astype(BF16)], axis=1),
                              jnp.concatenate([v[:, vs], cur.astype(BF16)], axis=0))
                else:
                    oh = _dot(att, v[:, vs]) + _dot(qin, cur)
                s_scr[h] = jnp.exp(ge) * cur + _dot(kout, v[:, vs], TN)
            else:
                oh = _dot(att, v[:, vs])
                o_rows = []
                for b in range(nblk):
                    rs = slice(b * lb, (b + 1) * lb)
                    cur = s0_ref[b, h]
                    o_rows.append(_dot(qin[rs], cur))
                    so[b, h] = jnp.exp(ge) * cur + _dot(kout[rs], v[rs, vs], TN)
                oh = oh + jnp.concatenate(o_rows, axis=0)
            outs.append(_rms(oh))
            yield
        y_ref[...] = (_silu(g_ref[...].astype(F32)) * jnp.concatenate(outs, axis=1)).astype(BF16)

    _run_chunks(chunk, row_refs, y_out, s0_ref, so_ref, scr[:1], chained, nsq, out_mode)


def _ret(proj, lg, cos, sin, s0, *, lb, tv, rows, chained, name, nsq=1, **kw):
    kern = functools.partial(_ret_kernel, rows=rows, lb=lb, tv=tv, chained=chained)
    scratch = ([pltpu.VMEM((nsq, RET_H, RET_DK, RET_DV), F32), pltpu.VMEM((RET_H, rows, rows), F32),
                pltpu.VMEM((2, rows, RET_H * RET_DK), F32)] if chained else [])
    return _rec_call(kern, proj, [("ret_q", 256), ("ret_k", 256), ("ret_v", 512), ("ret_g", 512)],
                     [(lg, 0)], [s0], [MIX], rows=rows, chained=chained, scratch=scratch, name=name,
                     tables=(cos, sin), nsq=nsq, **kw)


def _dense2_kernel(x_ref, gate_ref, yg_ref, ys_ref, yr_ref, lnw_ref, lnb_ref, wg_ref, ws_ref, wr_ref,
                   wo_ref, l1w_ref, l1b_ref, w1_ref, b1_ref, w2_ref, b2_ref, l2w_ref, l2b_ref, o_ref,
                   *, ff_chunk, parts, input_ln):
    tm = x_ref.shape[0]

    def rows_stage(rs):
        branches = [_dot(y_ref[rs, :], w_ref[...])
                    for y_ref, w_ref in ((yg_ref, wg_ref), (ys_ref, ws_ref), (yr_ref, wr_ref))]
        yield
        gate = lambda i: jax.nn.sigmoid(gate_ref[rs, i * D_MODEL:(i + 1) * D_MODEL].astype(F32))
        merged = gate(0) * branches[0] + gate(1) * branches[1] + gate(2) * branches[2]
        yield
        mix = _dot(merged, wo_ref[...])
        yield
        x = x_ref[rs, :]
        if input_ln:
            x = _layer_norm(x, lnw_ref[...], lnb_ref[...])
        h = _layer_norm(ALPHA * x + mix, l1w_ref[...], l1b_ref[...])
        hb = h.astype(BF16)
        ff = jnp.zeros_like(h) + b2_ref[...]
        yield
        for c0 in range(0, D_FF, ff_chunk):
            hid = jnp.dot(hb, w1_ref[:, c0:c0 + ff_chunk], preferred_element_type=F32)
            hid = jnp.square(jnp.maximum(hid + b1_ref[:, c0:c0 + ff_chunk], 0.0))
            ff = ff + _dot(hid, w2_ref[c0:c0 + ff_chunk, :])
            yield
        o_ref[rs, :] = _layer_norm(ALPHA * h + ff, l2w_ref[...], l2b_ref[...])

    _interleave([rows_stage(pl.ds(i * (tm // parts), tm // parts)) for i in range(parts)],
                stagger=True)


def _dense2(x, proj, yg, ys, yr, lnw, lnb, wl, layer, *, input_ln, tm, name):
    m = x.shape[0]
    row = lambda w: pl.BlockSpec((tm, w), lambda i: (i, 0))
    vec = pl.BlockSpec((1, D_MODEL), lambda i: (0, 0))
    const = lambda r, w: pl.BlockSpec((None, r, w), lambda i: (layer, 0, 0),
                                      pipeline_mode=pl.Buffered(1))
    return pl.pallas_call(
        functools.partial(_dense2_kernel, ff_chunk=1024, parts=(2 if tm % 32 == 0 else 1),
                          input_ln=input_ln),
        grid=(m // tm,),
        in_specs=[row(D_MODEL), row(3 * D_MODEL), row(MIX), row(MIX), row(MIX), vec, vec,
                  const(MIX, D_MODEL), const(MIX, D_MODEL), const(MIX, D_MODEL),
                  const(D_MODEL, D_MODEL), const(1, D_MODEL), const(1, D_MODEL),
                  const(D_MODEL, D_FF), const(1, D_FF), const(D_FF, D_MODEL), const(1, D_MODEL),
                  const(1, D_MODEL), const(1, D_MODEL)],
        out_specs=row(D_MODEL),
        out_shape=jax.ShapeDtypeStruct((m, D_MODEL), F32),
        compiler_params=pltpu.CompilerParams(
            dimension_semantics=("parallel",), vmem_limit_bytes=VMEM_LIMIT),
        name=name,
    )(x, proj, yg, ys, yr, lnw, lnb, wl["w_gla_out"], wl["w_ssm_out"], wl["w_ret_out"], wl["w_o"],
      wl["ln1_w"], wl["ln1_b"], wl["w_ff1"], wl["b_ff1"], wl["w_ff2"], wl["b_ff2"],
      wl["ln2_w"], wl["ln2_b"])


def _pick_tile(n, pref):
    t = min(n, pref)
    while n % t or t % SUBLANES:
        t -= 1
    return t


def _rearrange_w_in(w):
    w = jnp.swapaxes(w, -1, -2)
    offs = [0]
    for s in SPLIT_SIZES:
        offs.append(offs[-1] + s)
    names = ("gla_q", "gla_k", "gla_v", "gla_r", "gla_a", "ssm_z", "ssm_xbc", "ssm_dt",
             "ret_q", "ret_k", "ret_v", "ret_g", "gates")
    seg = {n: w[..., offs[i]:offs[i + 1], :].astype(BF16) for i, n in enumerate(names)}
    pad = lambda a: jnp.pad(a, ((0, 0),) * (a.ndim - 2) + ((0, LANES - a.shape[-2]), (0, 0)))
    order = sorted(COL, key=COL.get)
    parts = [pad(seg[n]) if n in ("gla_a", "ssm_dt") else seg[n] for n in order]
    return jnp.concatenate(parts, axis=-2)


def _rope_tables(pos):
    half = RET_DK // 2
    inv_freq = ROPE_BASE ** (-jnp.arange(half, dtype=F32) / half)
    ang = pos.astype(F32)[:, None] * inv_freq[None, :]
    cos = jnp.tile(jnp.cos(ang), (1, 2 * RET_H))
    sin = jnp.tile(jnp.sin(ang), (1, 2 * RET_H))
    return cos, sin


def kernel(x_prompt, x_sample, state_gla, state_ssm, state_conv, state_ret, meta_tokens,
           ln_in_w, ln_in_b, w_in, w_gla_a2, b_gla_a, w_gla_norm, conv_w, conv_b, dt_bias,
           a_log, d_skip, w_ssm_norm, w_gla_out, w_ssm_out, w_ret_out, w_o, ln1_w, ln1_b,
           w_ff1, b_ff1, w_ff2, b_ff2, ln2_w, ln2_b):
    bp, tp, d = x_prompt.shape
    bs, ts, _ = x_sample.shape
    assert d == D_MODEL and tp % CHUNK == 0 and w_in.shape[0] == DEPTH
    assert SSM_CONV - 1 <= SAMPLE_ROWS - ts
    nchunk = tp // CHUNK
    pad_rows = SAMPLE_ROWS - ts

    x_body = x_prompt.reshape(bp * tp, d)
    n_sample = bs * SAMPLE_ROWS
    rows_s = min(CHUNK, n_sample)
    assert n_sample % rows_s == 0 and n_sample % N_META == 0
    n_small = n_sample + CHUNK
    x_small = jnp.concatenate(
        [jnp.pad(x_sample, ((0, 0), (pad_rows, 0), (0, 0))).reshape(n_sample, d),
         meta_tokens.astype(F32), jnp.zeros((CHUNK - N_META, d), F32)], axis=0)

    cos_b, sin_b = _rope_tables(N_META + jnp.arange(tp, dtype=jnp.int32))
    cos_m, sin_m = _rope_tables(jnp.arange(N_META, dtype=jnp.int32))
    pos_tile = PAST_LEN - pad_rows + jnp.arange(SAMPLE_ROWS, dtype=jnp.int32)
    cos_s, sin_s = _rope_tables(jnp.tile(pos_tile, rows_s // SAMPLE_ROWS))
    lg_ret = jnp.pad(jnp.log1p(-jnp.exp2(-5.0 - jnp.arange(RET_H, dtype=F32))),
                     (0, LANES - RET_H)).reshape(1, 1, LANES)

    zero_gla = (jnp.zeros((1, 1, GLA_H, GLA_DK, GLA_DV), F32), 0)
    zero_ssm = (jnp.zeros((1, 1, SSM_H, SSM_N, SSM_P), F32), 0)
    zero_conv = (jnp.zeros((1, 1, SUBLANES, SSM_CONV_DIM), F32), 0)
    zero_ret = (jnp.zeros((1, 1, RET_H, RET_DK, RET_DV), F32), 0)
    from_meta = lambda a: (a[None], 0)
    conv_in = jnp.pad(state_conv, ((0, 0), (0, 0), (pad_rows - (SSM_CONV - 1), ts), (0, 0)))

    rowvec = lambda a: a.reshape(DEPTH, 1, -1)
    lane_pad = lambda a: jnp.pad(a, ((0, 0), (0, LANES - a.shape[1]))).reshape(DEPTH, 1, LANES)
    w_in_all = _rearrange_w_in(w_in)
    wl = dict(w_gla_out=w_gla_out.astype(BF16), w_ssm_out=w_ssm_out.astype(BF16),
              w_ret_out=w_ret_out.astype(BF16), w_o=w_o.astype(BF16),
              ln1_w=rowvec(ln1_w), ln1_b=rowvec(ln1_b),
              w_ff1=w_ff1.astype(BF16), b_ff1=rowvec(b_ff1),
              w_ff2=w_ff2.astype(BF16), b_ff2=rowvec(b_ff2),
              ln2_w=rowvec(ln2_w), ln2_b=rowvec(ln2_b))
    gla_w = [jnp.pad(w_gla_a2, ((0, 0), (0, LANES - GLA_RANK), (0, 0))), rowvec(b_gla_a),
             rowvec(w_gla_norm)]
    ssd_w = [conv_w, rowvec(conv_b), lane_pad(dt_bias), lane_pad(a_log),
             rowvec(jnp.repeat(d_skip, SSM_P, axis=1)), rowvec(w_ssm_norm)]
    ln_w, ln_b = ln_in_w.reshape(1, -1), ln_in_b.reshape(1, -1)

    tm_in = _pick_tile(bp * tp, 2048)
    tm_d2 = _pick_tile(bp * tp, 512)
    tm_d2s = _pick_tile(n_small, 384)
    tn = 1536
    meta = dict(row0=n_sample, nseq=1, nchunk=1, rows=N_META, lb=N_META, tv=N_META, chained=True,
                per_seq_state=False)
    body = dict(row0=0, nseq=bp, nchunk=nchunk, rows=CHUNK, lb=CHUNK, tv=CHUNK, chained=True,
                per_seq_state=False)
    seqs_per_step = lambda want: max(n for n in (1, 2, 4) if n <= want and bp % n == 0)
    samp = dict(row0=0, nseq=bs, nchunk=1, rows=rows_s, lb=SAMPLE_ROWS, tv=ts, chained=False,
                per_seq_state=True)

    names = ("gla_p", "gla_s", "ssm_p", "ssm_s", "conv_p", "conv_s", "ret_p", "ret_s")
    st = {k: None for k in names}
    stk = lambda *keys: [] if st[keys[0]] is None else [st[k] for k in keys]
    xb, xs = x_body, x_small
    for l in range(DEPTH):
        proj_s = _in_proj(xs, ln_w, ln_b, w_in_all, l, apply_ln=(l == 0), tm=n_small, tn=tn,
                          name=f"inproj_small_{l}")
        yg_m, sg_m = _gla(proj_s, gla_w, zero_gla, layer=l, name=f"gla_meta_{l}", **meta)
        ys_m, cv_m, ss_m = _ssd(proj_s, ssd_w, zero_conv, zero_ssm, layer=l,
                                name=f"ssd_meta_{l}", **meta)
        yr_m, sr_m = _ret(proj_s, lg_ret, cos_m, sin_m, zero_ret, layer=l,
                          name=f"ret_meta_{l}", **meta)
        yg_s, st["gla_s"] = _gla(proj_s, gla_w, (state_gla, l), layer=l, stacked=stk("gla_s"),
                                 name=f"gla_sample_{l}", **samp)
        ys_s, st["conv_s"], st["ssm_s"] = _ssd(proj_s, ssd_w, (conv_in, l), (state_ssm, l), layer=l,
                                               stacked=stk("conv_s", "ssm_s"),
                                               name=f"ssd_sample_{l}", **samp)
        yr_s, st["ret_s"] = _ret(proj_s, lg_ret, cos_s, sin_s, (state_ret, l), layer=l,
                                 stacked=stk("ret_s"), name=f"ret_sample_{l}", **samp)
        zpad = jnp.zeros((CHUNK - N_META, MIX), BF16)
        yg = jnp.concatenate([yg_s, yg_m, zpad], axis=0)
        ys = jnp.concatenate([ys_s, ys_m, zpad], axis=0)
        yr = jnp.concatenate([yr_s, yr_m, zpad], axis=0)
        xs = _dense2(xs, proj_s, yg, ys, yr, ln_w, ln_b, wl, l, input_ln=(l == 0), tm=tm_d2s,
                     name=f"dense2_small_{l}")

        proj_b = _in_proj(xb, ln_w, ln_b, w_in_all, l, apply_ln=(l == 0), tm=tm_in, tn=tn,
                          name=f"inproj_body_{l}")
        yg_b, st["gla_p"] = _gla(proj_b, gla_w, from_meta(sg_m), layer=l, stacked=stk("gla_p"),
                                 name=f"gla_body_{l}", nsq=seqs_per_step(4), **body)
        ys_b, st["conv_p"], st["ssm_p"] = _ssd(proj_b, ssd_w, from_meta(cv_m), from_meta(ss_m),
                                               layer=l, stacked=stk("conv_p", "ssm_p"),
                                               name=f"ssd_body_{l}", nsq=seqs_per_step(2), **body)
        yr_b, st["ret_p"] = _ret(proj_b, lg_ret, cos_b, sin_b, from_meta(sr_m), layer=l,
                                 stacked=stk("ret_p"), name=f"ret_body_{l}", nsq=seqs_per_step(4),
                                 **body)
        xb = _dense2(xb, proj_b, yg_b, ys_b, yr_b, ln_w, ln_b, wl, l, input_ln=(l == 0), tm=tm_d2,
                     name=f"dense2_body_{l}")

    y_prompt = xb.reshape(bp, tp, d)
    y_sample = xs[:n_sample].reshape(bs, SAMPLE_ROWS, d)[:, pad_rows:]
    tail3 = lambda c: c[:, :, SUBLANES - (SSM_CONV - 1):, :]
    return (y_prompt, y_sample, st["gla_p"], st["gla_s"], st["ssm_p"], st["ssm_s"],
            tail3(st["conv_p"]), tail3(st["conv_s"]), st["ret_p"], st["ret_s"])
```

```python
import functools

import jax
import jax.numpy as jnp
from jax import lax
from jax.experimental import pallas as pl
from jax.experimental.pallas import tpu as pltpu

F32 = jnp.float32
BF16 = jnp.bfloat16

D_MODEL = 1024
DEPTH = 2
N_META = 16
MIX = 512
GLA_H, GLA_DK, GLA_DV, GLA_RANK = 4, 64, 128, 16
GLA_GATE_NORM = 16.0
SSM_H, SSM_P, SSM_N, SSM_G, SSM_CONV = 8, 64, 64, 2, 4
SSM_CONV_DIM = MIX + 2 * SSM_G * SSM_N
SSM_PAIRS = SSM_H // 2
RET_H, RET_DK, RET_DV = 4, 64, 128
ROPE_BASE = 10000.0
D_FF = 4 * D_MODEL
ALPHA = (2 * DEPTH) ** 0.25
PAST_LEN = 16384
SPLIT_SIZES = (256, 256, 512, 512, 16, 512, 768, 8, 256, 256, 512, 512, 3072)

LANES = 128
SUBLANES = 8
VMEM_LIMIT = 56 * 1024 * 1024

COL = dict(gates=0, gla_v=3072, gla_r=3584, ssm_z=4096, ret_v=4608, ret_g=5120,
           gla_q=5632, gla_k=5888, ssm_xbc=6144, ret_q=6912, ret_k=7168, gla_a=7424, ssm_dt=7552)
N_PROJ = 7680
SAMPLE_ROWS = SUBLANES
GLA_BASE = SUBLANES
CHUNK = 128

NN = (((1,), (0,)), ((), ()))
NT = (((1,), (1,)), ((), ()))
TN = (((0,), (0,)), ((), ()))


def _dot(a, b, dims=NN):
    return lax.dot_general(a.astype(BF16), b.astype(BF16), dims, preferred_element_type=F32)


def _dot_sel(sel, x, dims=NN, sel_first=True):
    hi = x.astype(BF16)
    r1 = x - hi.astype(F32)
    mid = r1.astype(BF16)
    lo = (r1 - mid.astype(F32)).astype(BF16)
    sb = sel.astype(BF16)
    out = None
    for part in (hi, mid, lo):
        ops = (sb, part) if sel_first else (part, sb)
        term = lax.dot_general(*ops, dims, preferred_element_type=F32)
        out = term if out is None else out + term
    return out


def _layer_norm(x, w, b):
    mu = jnp.mean(x, axis=-1, keepdims=True)
    xc = x - mu
    var = jnp.mean(xc * xc, axis=-1, keepdims=True)
    return xc * lax.rsqrt(var + 1e-5) * w + b


def _rms(x):
    return x * lax.rsqrt(jnp.mean(x * x, axis=-1, keepdims=True) + 1e-6)


def _silu(x):
    return x * jax.nn.sigmoid(x)


def _iotas(rows):
    return (lax.broadcasted_iota(jnp.int32, (rows, rows), 0),
            lax.broadcasted_iota(jnp.int32, (rows, rows), 1))


def _row_in_block(rows, lb):
    return lax.broadcasted_iota(jnp.int32, (rows, 1), 0) % lb


def _inproj_kernel(x_ref, lnw_ref, lnb_ref, w_ref, proj_ref, xb_scr, *, apply_ln, parts):
    tm = x_ref.shape[0]

    def first_tile(rs):
        x = x_ref[rs, :]
        if apply_ln:
            x = _layer_norm(x, lnw_ref[...], lnb_ref[...])
        xb = x.astype(BF16)
        xb_scr[rs, :] = xb
        yield
        proj_ref[rs, :] = lax.dot_general(xb, w_ref[...], NT, preferred_element_type=F32).astype(BF16)

    @pl.when(pl.program_id(1) == 0)
    def _():
        _interleave([first_tile(pl.ds(p * (tm // parts), tm // parts)) for p in range(parts)],
                    stagger=True)

    @pl.when(pl.program_id(1) > 0)
    def _():
        proj_ref[...] = lax.dot_general(xb_scr[...], w_ref[...], NT,
                                        preferred_element_type=F32).astype(BF16)


def _in_proj(x, lnw, lnb, w, layer, *, apply_ln, tm, tn, name):
    m = x.shape[0]
    grid = (m // tm, N_PROJ // tn)
    return pl.pallas_call(
        functools.partial(_inproj_kernel, apply_ln=apply_ln,
                          parts=(4 if apply_ln and tm % 64 == 0 else 1)),
        grid=grid,
        in_specs=[pl.BlockSpec((tm, D_MODEL), lambda i, j: (i, 0)),
                  pl.BlockSpec((1, D_MODEL), lambda i, j: (0, 0)),
                  pl.BlockSpec((1, D_MODEL), lambda i, j: (0, 0)),
                  pl.BlockSpec((None, tn, D_MODEL), lambda i, j: (layer, j, 0))],
        out_specs=pl.BlockSpec((tm, tn), lambda i, j: (i, j)),
        out_shape=jax.ShapeDtypeStruct((m, N_PROJ), BF16),
        scratch_shapes=[pltpu.VMEM((tm, D_MODEL), BF16)],
        compiler_params=pltpu.CompilerParams(
            dimension_semantics=("parallel", "arbitrary"), vmem_limit_bytes=VMEM_LIMIT),
        name=name,
    )(x, lnw, lnb, w)


def _rec_call(kernel_fn, proj, segs, consts, states, out_widths, *, layer, row0, nseq, nchunk,
              rows, chained, per_seq_state, scratch, name, tables=(), stacked=None, nsq=1):
    rb = row0 // rows
    nb = nsq if chained else rows // SAMPLE_ROWS
    grid = (nseq // nb, nchunk) if chained else (nseq // nb, 1)
    assert nseq % nb == 0 and (nsq == 1 or (chained and row0 == 0))
    out_mode = "plain" if stacked is None else ("first" if layer == 0 else "later")
    aliased = list(stacked) if out_mode == "later" else []

    def row_idx(b, c):
        return b * nchunk + c if chained else b

    def col(seg, w):
        cbi = COL[seg] // w
        if nsq > 1:
            return pl.BlockSpec((nsq, rows, w), lambda b, c: (b, c, cbi))
        return pl.BlockSpec((rows, w), lambda b, c: (rb + row_idx(b, c), cbi))

    def row_out(w):
        if nsq > 1:
            return pl.BlockSpec((nsq, rows, w), lambda b, c: (b, c, 0))
        return pl.BlockSpec((rows, w), lambda b, c: (row_idx(b, c), 0))

    def const_spec(a, lyr):
        zeros = (0,) * (a.ndim - 1)
        return pl.BlockSpec((None,) + a.shape[1:], lambda b, c: (lyr,) + zeros)

    def state_in_spec(a, lyr):
        zeros = (0,) * (a.ndim - 2)
        if per_seq_state:
            return pl.BlockSpec((None, nb) + a.shape[2:], lambda b, c: (lyr, b) + zeros)
        return pl.BlockSpec((None, 1) + a.shape[2:], lambda b, c: (lyr, 0) + zeros)

    def state_out_spec(a):
        zeros = (0,) * (a.ndim - 2)
        if out_mode == "plain":
            return pl.BlockSpec((nb,) + a.shape[2:], lambda b, c: (b,) + zeros)
        if out_mode == "first":
            return pl.BlockSpec((DEPTH, nb) + a.shape[2:], lambda b, c: (0, b) + zeros)
        return pl.BlockSpec((None, nb) + a.shape[2:], lambda b, c: (layer, b) + zeros)

    def state_out_shape(a):
        lead = (nseq,) if out_mode == "plain" else (DEPTH, nseq)
        return jax.ShapeDtypeStruct(lead + a.shape[2:], F32)

    consts = [c if isinstance(c, tuple) else (c, layer) for c in consts]
    in_specs = ([col(s, w) for s, w in segs] + [const_spec(a, lyr) for a, lyr in consts]
                + [pl.BlockSpec((rows, t.shape[1]), lambda b, c: (c, 0)) for t in tables]
                + [state_in_spec(a, lyr) for a, lyr in states]
                + [pl.BlockSpec(memory_space=pl.ANY) for _ in aliased])
    n_in = len(in_specs) - len(aliased)
    n_rows = nseq * nchunk * rows if chained else nseq * SAMPLE_ROWS
    out_specs = [row_out(w) for w in out_widths] + [state_out_spec(a) for a, _ in states]
    row_shape = (lambda w: (nseq, nchunk * rows, w)) if nsq > 1 else (lambda w: (n_rows, w))
    out_shape = ([jax.ShapeDtypeStruct(row_shape(w), BF16) for w in out_widths]
                 + [state_out_shape(a) for a, _ in states])
    if nsq > 1:
        proj = proj.reshape(nseq, nchunk * rows, proj.shape[-1])
    res = pl.pallas_call(
        functools.partial(kernel_fn, n_alias=len(aliased), out_mode=out_mode, nsq=nsq),
        grid=grid, in_specs=in_specs, out_specs=out_specs, out_shape=out_shape,
        scratch_shapes=scratch,
        input_output_aliases={n_in + i: len(out_widths) + i for i in range(len(aliased))},
        compiler_params=pltpu.CompilerParams(
            dimension_semantics=("parallel", "arbitrary"), vmem_limit_bytes=VMEM_LIMIT),
        name=name,
    )(*([proj] * len(segs)), *[a for a, _ in consts], *tables, *[a for a, _ in states], *aliased)
    if nsq > 1:
        res = ([r.reshape(n_rows, r.shape[-1]) for r in res[:len(out_widths)]]
               + list(res[len(out_widths):]))
    return res


def _state_slot(ref, out_mode):
    if out_mode != "first":
        return ref
    ref[1:] = jnp.zeros((DEPTH - 1,) + ref.shape[1:], F32)
    return ref.at[0]


def _interleave(stages, stagger=False):
    stages = list(stages)
    delay = {id(gen): (i if stagger else 0) for i, gen in enumerate(stages)}
    rnd = 0
    while stages:
        for gen in list(stages):
            if delay[id(gen)] > rnd:
                continue
            try:
                next(gen)
            except StopIteration:
                stages.remove(gen)
        rnd += 1


def _run_chunks(chunk, row_refs, y_ref, s0_ref, so_ref, scr, chained, nsq, out_mode):
    if not chained:
        _interleave([chunk(*row_refs, y_ref, None)])
        return
    s_all, = scr

    @pl.when(pl.program_id(1) == 0)
    def _():
        for j in range(nsq):
            s_all[j] = s0_ref[0]

    view = lambda r, j: r.at[j] if nsq > 1 else r
    _interleave([chunk(*[view(r, j) for r in row_refs], view(y_ref, j), s_all.at[j])
                 for j in range(nsq)])

    @pl.when(pl.program_id(1) == pl.num_programs(1) - 1)
    def _():
        dst = _state_slot(so_ref, out_mode)
        for j in range(nsq):
            dst[j] = s_all[j]


def _gla_kernel(*refs, rows, lb, tv, chained, nsq, n_alias, out_mode):
    row_refs = refs[:5]
    w2_ref, ba_ref, wn_ref, eb_ref, sh_ref, s0_ref = refs[5:11]
    y_out, so_ref = refs[11 + n_alias:13 + n_alias]
    scr = refs[13 + n_alias:]
    nblk = rows // lb
    width = GLA_H * GLA_DK

    def chunk(q_ref, k_ref, v_ref, r_ref, a_ref, y_ref, s_scr):
        q = q_ref[...].astype(F32) * (GLA_DK ** -0.5)
        k = k_ref[...].astype(F32)
        vb = v_ref[...]
        a = _dot(a_ref[...], w2_ref[...]) + ba_ref[...]
        g = jax.nn.log_sigmoid(a) * (1.0 / GLA_GATE_NORM)
        t_in = _row_in_block(rows, lb)
        if tv < lb:
            valid = t_in >= lb - tv
            g = jnp.where(valid, g, 0.0)
            k = jnp.where(valid, k, 0.0)
        yield
        r_i, c_i = _iotas(rows)
        same = (r_i // lb) == (c_i // lb)
        sizes = []
        while 2 * GLA_BASE * 2 ** len(sizes) <= lb:
            sizes.append(2 * GLA_BASE * 2 ** len(sizes))
        sums = [same & (r_i >= c_i)] + ([same] if nblk > 1 else [])
        sums += [same & (c_i <= (r_i // sz) * sz + sz // 2 - 1) for sz in sizes]
        gsums = _dot_sel(jnp.concatenate(sums, axis=0), g)
        gcum = gsums[:rows]
        gtot = gcum[rows - 1:rows, :] if nblk == 1 else gsums[rows:2 * rows]
        g_mids = [gsums[(len(sums) - len(sizes) + i) * rows:(len(sums) - len(sizes) + i + 1) * rows]
                  for i in range(len(sizes))]
        qd = q * jnp.exp(gcum)
        kd = k * jnp.exp(gtot - gcum)
        sel = (lax.broadcasted_iota(jnp.int32, (rows, nblk * LANES), 0) // lb
               == lax.broadcasted_iota(jnp.int32, (rows, nblk * LANES), 1) // LANES)
        ds = jnp.exp(_dot_sel(sel, g, TN, sel_first=False))
        yield

        nbase = rows // GLA_BASE
        q3 = q.reshape(nbase, GLA_BASE, width)
        k3 = k.reshape(nbase, GLA_BASE, width)
        g3 = gcum.reshape(nbase, GLA_BASE, width)
        t3 = lax.broadcasted_iota(jnp.int32, (1, GLA_BASE, 1), 1)
        pieces = []
        for s in range(max(0, GLA_BASE - tv), GLA_BASE):
            dd = jnp.minimum(g3 - g3[:, s:s + 1, :], 0.0)
            w = q3 * k3[:, s:s + 1, :] * jnp.exp(dd)
            pieces.append(jnp.where(t3 >= s, w, 0.0).reshape(rows, width).astype(BF16))
            yield
        compact = jnp.dot(jnp.concatenate(pieces, axis=1), eb_ref[...], preferred_element_type=F32)
        spread = jnp.dot(compact.astype(BF16), sh_ref[...], preferred_element_type=F32)
        base_mask = (r_i // GLA_BASE) == (c_i // GLA_BASE)
        att = [jnp.where(base_mask, spread[:, h * LANES:h * LANES + rows], 0.0) for h in range(GLA_H)]
        yield

        for size, g_mid in zip(sizes, g_mids):
            second = (t_in % size) >= size // 2
            ql = jnp.where(second, q * jnp.exp(jnp.minimum(gcum - g_mid, 0.0)), 0.0)
            kl = jnp.where(second, 0.0, k * jnp.exp(jnp.minimum(g_mid - gcum, 0.0)))
            group = (r_i // size) == (c_i // size)
            for h in range(GLA_H):
                ks = slice(h * GLA_DK, (h + 1) * GLA_DK)
                att[h] = att[h] + jnp.where(group, _dot(ql[:, ks], kl[:, ks], NT), 0.0)
            yield

        so = _state_slot(so_ref, out_mode) if not chained else None
        o_heads = []
        for h in range(GLA_H):
            ks = slice(h * GLA_DK, (h + 1) * GLA_DK)
            vs = slice(h * GLA_DV, (h + 1) * GLA_DV)
            if chained:
                cur = s_scr[h]
                if rows % LANES == 0:
                    oh = _dot(jnp.concatenate([att[h].astype(BF16), qd[:, ks].astype(BF16)], axis=1),
                              jnp.concatenate([vb[:, vs], cur.astype(BF16)], axis=0))
                else:
                    oh = _dot(att[h], vb[:, vs]) + _dot(qd[:, ks], cur)
                s_scr[h] = ds[ks, :] * cur + _dot(kd[:, ks], vb[:, vs], TN)
            else:
                o_rows = []
                for b in range(nblk):
                    rs = slice(b * lb, (b + 1) * lb)
                    cur = s0_ref[b, h]
                    o_rows.append(_dot(qd[rs, ks], cur))
                    so[b, h] = ds[ks, b * LANES:(b + 1) * LANES] * cur + _dot(kd[rs, ks], vb[rs, vs], TN)
                oh = _dot(att[h], vb[:, vs]) + jnp.concatenate(o_rows, axis=0)
            o_heads.append(_rms(oh) * wn_ref[...])
            yield
        y_ref[...] = (_silu(r_ref[...].astype(F32)) * jnp.concatenate(o_heads, axis=1)).astype(BF16)

    _run_chunks(chunk, row_refs, y_out, s0_ref, so_ref, scr, chained, nsq, out_mode)


def _gla_consts(tv):
    sources = jnp.arange(max(0, GLA_BASE - tv), GLA_BASE, dtype=jnp.int32)
    row = jnp.arange(sources.shape[0] * GLA_H * GLA_DK, dtype=jnp.int32)
    target = sources[row // (GLA_H * GLA_DK)] * GLA_H + (row % (GLA_H * GLA_DK)) // GLA_DK
    eb = (target[:, None] == jnp.arange(LANES, dtype=jnp.int32)[None, :]).astype(BF16)
    r = jnp.arange(LANES, dtype=jnp.int32)[:, None]
    c = jnp.arange(GLA_H * LANES, dtype=jnp.int32)[None, :]
    sh = ((r < GLA_BASE * GLA_H) & (r % GLA_H == c // LANES)
          & (r // GLA_H == (c % LANES) % GLA_BASE)).astype(BF16)
    return (eb[None], 0), (sh[None], 0)


def _gla(proj, weights, s0, *, lb, tv, rows, chained, name, nsq=1, **kw):
    kern = functools.partial(_gla_kernel, rows=rows, lb=lb, tv=tv, chained=chained)
    scratch = [pltpu.VMEM((nsq, GLA_H, GLA_DK, GLA_DV), F32)] if chained else []
    return _rec_call(kern, proj,
                     [("gla_q", 256), ("gla_k", 256), ("gla_v", 512), ("gla_r", 512), ("gla_a", 128)],
                     list(weights) + list(_gla_consts(tv)), [s0], [MIX], rows=rows,
                     chained=chained, scratch=scratch, name=name, nsq=nsq, **kw)


def _ssd_kernel(*refs, rows, lb, tv, chained, nsq, n_alias, out_mode):
    row_refs = refs[:3]
    cw_ref, cb_ref, dtb_ref, alog_ref, dsk_ref, wn_ref, c0_ref, s0_ref = refs[3:11]
    y_out, co_ref, so_ref = refs[11 + n_alias:14 + n_alias]
    scr = refs[14 + n_alias:]
    nblk = rows // lb
    t_in = _row_in_block(rows, lb)
    valid = (t_in >= lb - tv) if tv < lb else None
    low64 = lax.broadcasted_iota(jnp.int32, (SSM_N, LANES), 1) < SSM_P

    def chunk(z_ref, x_ref, dt_ref, y_ref, s_scr, ext_scr, j):
        co = _state_slot(co_ref, out_mode)
        if chained:
            xb = x_ref[...]
            xin = xb.astype(F32)
            tail = ext_scr[...]
            r_s, c_s = _iotas(rows)
            t8 = lax.broadcasted_iota(jnp.int32, (SUBLANES, 1), 0)
            conv = cb_ref[...] + cw_ref[SSM_CONV - 1:SSM_CONV, :] * xin
            head = jnp.zeros((SUBLANES, SSM_CONV_DIM), F32)
            for d in range(1, SSM_CONV):
                tap = cw_ref[SSM_CONV - 1 - d:SSM_CONV - d, :]
                shifted = jnp.dot((c_s == r_s - d).astype(BF16), xb, preferred_element_type=F32)
                conv = conv + tap * shifted
                head = head + tap * jnp.where(t8 < d, pltpu.roll(tail, d, axis=0), 0.0)
            conv = jnp.concatenate([conv[:SUBLANES] + head, conv[SUBLANES:]], axis=0)
            co[j] = xin[rows - SUBLANES:, :]
            ext_scr[...] = xin[rows - SUBLANES:, :]
        else:
            ext_scr[0:SUBLANES, :] = jnp.zeros((SUBLANES, SSM_CONV_DIM), F32)
            xin = jnp.where(valid, x_ref[...].astype(F32), c0_ref[...].reshape(rows, SSM_CONV_DIM))
            ext_scr[SUBLANES:SUBLANES + rows, :] = xin
            conv = cb_ref[...]
            for i in range(SSM_CONV):
                conv = conv + cw_ref[i:i + 1, :] * ext_scr[pl.ds(SUBLANES - (SSM_CONV - 1) + i, rows), :]
            co[...] = xin.reshape(nblk, lb, SSM_CONV_DIM)
        act = _silu(conv)
        yield

        dt = jax.nn.softplus(dt_ref[...].astype(F32) + dtb_ref[...])
        gdt = dt * (-jnp.exp(alog_ref[...]))
        if valid is not None:
            gdt = jnp.where(valid, gdt, 0.0)
        r_i, c_i = _iotas(rows)
        same = (r_i // lb) == (c_i // lb)
        causal = same & (r_i >= c_i)
        gcum = _dot_sel(causal, gdt)
        if rows < LANES:
            gsq = jnp.concatenate([gcum, jnp.zeros((LANES - rows, LANES), F32)], axis=0)
            gcum_t = gsq.T[:, :rows]
        else:
            gcum_t = gcum.T
        gtot = gcum[rows - 1:rows, :] if nblk == 1 else _dot_sel(same, gdt)

        lane = lax.broadcasted_iota(jnp.int32, (rows, LANES), 1)
        low = lane < SSM_P
        bcol = act[:, MIX:MIX + LANES]
        ccol = act[:, MIX + LANES:MIX + 2 * LANES]
        bswap = pltpu.roll(bcol, SSM_N, axis=1)
        cswap = pltpu.roll(ccol, SSM_N, axis=1)
        b2 = (jnp.where(low, bcol, bswap), jnp.where(low, bswap, bcol))
        c2 = (jnp.where(low, ccol, cswap), jnp.where(low, cswap, ccol))
        cb = (_dot(jnp.where(low, ccol, 0.0), bcol, NT), _dot(jnp.where(low, 0.0, ccol), bcol, NT))

        def pair_lanes(x, p):
            return jnp.where(low[:x.shape[0]], x[:, 2 * p:2 * p + 1], x[:, 2 * p + 1:2 * p + 2])

        yield
        so = _state_slot(so_ref, out_mode) if not chained else None
        y_pairs = []
        for p in range(SSM_PAIRS):
            gi = p // (SSM_PAIRS // SSM_G)
            decs = []
            for h in (2 * p, 2 * p + 1):
                diff = jnp.minimum(gcum[:, h:h + 1] - gcum_t[h:h + 1, :], 0.0)
                decs.append(cb[gi] * jnp.where(causal, jnp.exp(diff), 0.0))
            xp = act[:, p * LANES:(p + 1) * LANES]
            vp = xp * pair_lanes(dt, p)
            if valid is not None:
                vp = jnp.where(valid, vp, 0.0)
            vbd = jnp.concatenate([jnp.where(low, vp, 0.0), jnp.where(low, 0.0, vp)], axis=0)
            g2 = pair_lanes(gcum, p)
            ge2 = pair_lanes(gtot, p)
            cin = c2[gi] * jnp.exp(g2)
            bout = b2[gi] * jnp.exp(ge2 - g2)
            if chained:
                cur = s_scr[p]
                bd = jnp.concatenate([jnp.where(low64, cur, 0.0), jnp.where(low64, 0.0, cur)], axis=0)
                if rows % LANES == 0:
                    o = _dot(jnp.concatenate(decs + [cin], axis=1).astype(BF16),
                             jnp.concatenate([vbd.astype(BF16), bd.astype(BF16)], axis=0))
                else:
                    o = _dot(jnp.concatenate(decs, axis=1), vbd) + _dot(cin, bd)
                u = _dot(bout, vp, TN)
                s_scr[p] = jnp.exp(ge2) * cur + jnp.where(low64, u[:SSM_N, :], u[SSM_N:, :])
            else:
                o = _dot(jnp.concatenate(decs, axis=1), vbd)
                halves = []
                for odd, (ci, bo, vv) in enumerate(
                        ((cin, bout, vp),
                         tuple(pltpu.roll(t, SSM_P, axis=1) for t in (cin, bout, vp)))):
                    h = 2 * p + odd
                    o_rows = []
                    for b in range(nblk):
                        rs = slice(b * lb, (b + 1) * lb)
                        cur = s0_ref[b, h]
                        o_rows.append(_dot(ci[rs, :SSM_N], cur))
                        so[b, h] = (jnp.exp(gtot[b * lb:b * lb + 1, h:h + 1]) * cur
                                    + _dot(bo[rs, :SSM_N], vv[rs, :SSM_P], TN))
                    halves.append(jnp.concatenate(o_rows, axis=0))
                o = o + jnp.concatenate(halves, axis=1)
            y_pairs.append(o + dsk_ref[:, p * LANES:(p + 1) * LANES] * xp)
            yield
        y = jnp.concatenate(y_pairs, axis=1) * _silu(z_ref[...].astype(F32))
        half = MIX // SSM_G
        y = jnp.concatenate([_rms(y[:, gi * half:(gi + 1) * half]) for gi in range(SSM_G)], axis=1)
        y_ref[...] = (y * wn_ref[...]).astype(BF16)

    if not chained:
        _interleave([chunk(*row_refs, y_out, None, scr[0], 0)])
        return
    s_all, ext_all = scr

    @pl.when(pl.program_id(1) == 0)
    def _():
        for j in range(nsq):
            for p in range(SSM_PAIRS):
                s_all[j, p] = jnp.concatenate([s0_ref[0, 2 * p], s0_ref[0, 2 * p + 1]], axis=1)
            ext_all[j, 0:SUBLANES, :] = c0_ref[0]

    view = lambda r, j: r.at[j] if nsq > 1 else r
    _interleave([chunk(*[view(r, j) for r in row_refs], view(y_out, j), s_all.at[j],
                       ext_all.at[j], j) for j in range(nsq)])

    @pl.when(pl.program_id(1) == pl.num_programs(1) - 1)
    def _():
        dst = _state_slot(so_ref, out_mode)
        for j in range(nsq):
            for p in range(SSM_PAIRS):
                dst[j, 2 * p] = s_all[j, p][:, :SSM_P]
                dst[j, 2 * p + 1] = s_all[j, p][:, SSM_P:]


def _ssd(proj, weights, c0, s0, *, lb, tv, rows, chained, name, nsq=1, **kw):
    kern = functools.partial(_ssd_kernel, rows=rows, lb=lb, tv=tv, chained=chained)
    if chained:
        scratch = [pltpu.VMEM((nsq, SSM_PAIRS, SSM_N, LANES), F32),
                   pltpu.VMEM((nsq, SUBLANES, SSM_CONV_DIM), F32)]
    else:
        scratch = [pltpu.VMEM((SUBLANES + rows, SSM_CONV_DIM), F32)]
    return _rec_call(kern, proj, [("ssm_z", 512), ("ssm_xbc", 768), ("ssm_dt", 128)],
                     weights, [c0, s0], [MIX], rows=rows, chained=chained, scratch=scratch,
                     name=name, nsq=nsq, **kw)


def _ssd_lanes_kernel(*refs, nb, ts, n_alias, out_mode):
    (z_ref, x_ref, dt_ref, cw_ref, cb_ref, dtb_ref, alog_ref, dsk_ref, wn_ref,
     c0_ref, s0_ref) = refs[:11]
    y_ref, co_ref, so_ref = refs[11 + n_alias:14 + n_alias]
    act_scr, z_scr, dt_scr, g_scr, cbs_scr, y_scr = refs[14 + n_alias:]
    h = pl.program_id(0)
    rows = nb * SAMPLE_ROWS
    first_row = SAMPLE_ROWS - ts
    b_off, c_off = MIX, MIX + SSM_G * SSM_N

    @pl.when(h == 0)
    def _():
        seq = lax.broadcasted_iota(jnp.int32, (nb, rows), 0)
        row = lax.broadcasted_iota(jnp.int32, (nb, rows), 1)
        plain = [c0_ref[r] for r in range(SSM_CONV - 1)]
        for t in range(ts):
            pick = (row == seq * SAMPLE_ROWS + first_row + t).astype(BF16)
            plain.append(jnp.dot(pick, x_ref[...], preferred_element_type=F32))
            z_scr[t] = jnp.dot(pick, z_ref[...], preferred_element_type=F32).T
            dt_scr[t] = jnp.dot(pick, dt_ref[...], preferred_element_type=F32).T
        co = _state_slot(co_ref, out_mode)
        for r in range(SSM_CONV - 1):
            co[r] = plain[ts + r]
        lanes = [p.T for p in plain]
        for t in range(ts):
            conv = cb_ref[...]
            for i in range(SSM_CONV):
                conv = conv + cw_ref[:, i:i + 1] * lanes[t + i]
            act_scr[t] = _silu(conv)
        a_neg = -jnp.exp(alog_ref[...])
        gsum = jnp.zeros((LANES, nb), F32)
        for t in range(ts):
            dt = jax.nn.softplus(dt_scr[t] + dtb_ref[...])
            dt_scr[t] = dt
            gsum = gsum + dt * a_neg
            g_scr[t] = gsum
        for gi in range(SSM_G):
            for t in range(ts):
                cm = act_scr[t, c_off + gi * SSM_N:c_off + (gi + 1) * SSM_N, :]
                for s in range(t + 1):
                    bm = act_scr[s, b_off + gi * SSM_N:b_off + (gi + 1) * SSM_N, :]
                    idx = (gi * ts + t) * ts + s
                    cbs_scr[idx:idx + 1, :] = jnp.sum(cm * bm, axis=0, keepdims=True)

    gi = h // (SSM_H // SSM_G)
    x_row = pl.multiple_of(h * SSM_P, SSM_P)
    xs = [act_scr[t, pl.ds(x_row, SSM_P), :] for t in range(ts)]
    dts = [dt_scr[t, pl.ds(h, 1), :] for t in range(ts)]
    gs = [g_scr[t, pl.ds(h, 1), :] for t in range(ts)]
    gtot = gs[-1]
    decay = jnp.exp(gtot)
    into_state = [jnp.exp(gtot - gs[t]) * dts[t] for t in range(ts)]
    so = _state_slot(so_ref, out_mode)

    def state_row(n, acc):
        s_n = s0_ref[n]
        new = decay * s_n
        out = []
        for t in range(ts):
            out.append(acc[t] + act_scr[t, pl.ds(c_off + gi * SSM_N + n, 1), :] * s_n)
            new = new + (act_scr[t, pl.ds(b_off + gi * SSM_N + n, 1), :] * into_state[t]) * xs[t]
        so[n] = new
        return tuple(out)

    acc = lax.fori_loop(0, SSM_N, state_row,
                        tuple(jnp.zeros((SSM_P, nb), F32) for _ in range(ts)))
    for t in range(ts):
        o = jnp.exp(gs[t]) * acc[t]
        for s in range(t + 1):
            cb = cbs_scr[pl.ds((gi * ts + t) * ts + s, 1), :]
            o = o + (cb * jnp.exp(gs[t] - gs[s]) * dts[s]) * xs[s]
        y_scr[t, pl.ds(x_row, SSM_P), :] = o + dsk_ref[pl.ds(x_row, SSM_P), :] * xs[t]

    @pl.when(h == SSM_H - 1)
    def _():
        row = lax.broadcasted_iota(jnp.int32, (rows, nb), 0)
        seq = lax.broadcasted_iota(jnp.int32, (rows, nb), 1)
        half = MIX // SSM_G
        out = jnp.zeros((rows, MIX), F32)
        for t in range(ts):
            y = y_scr[t] * _silu(z_scr[t])
            normed = []
            for g2 in range(SSM_G):
                blk = y[g2 * half:(g2 + 1) * half, :]
                ms = jnp.mean(blk * blk, axis=0, keepdims=True)
                normed.append(blk * lax.rsqrt(ms + 1e-6))
            y = (jnp.concatenate(normed, axis=0) * wn_ref[...]).T.astype(BF16)
            put = (row == seq * SAMPLE_ROWS + first_row + t).astype(BF16)
            out = out + jnp.dot(put, y, preferred_element_type=F32)
        y_ref[...] = out.astype(BF16)


def _ssd_lanes(proj, weights, c0, s0, *, layer, nb, ts, stacked, name):
    rows = nb * SAMPLE_ROWS
    out_mode = "first" if layer == 0 else "later"
    aliased = list(stacked) if out_mode == "later" else []
    seg = lambda name_, w: pl.BlockSpec((rows, w), lambda h: (0, COL[name_] // w))
    const = lambda a: pl.BlockSpec((None,) + a.shape[1:], lambda h: (layer,) + (0,) * (a.ndim - 1))
    if out_mode == "first":
        co_spec = pl.BlockSpec((DEPTH,) + c0.shape[1:], lambda h: (0, 0, 0, 0))
        so_spec = pl.BlockSpec((DEPTH, None) + s0.shape[2:], lambda h: (0, h, 0, 0, 0))
    else:
        co_spec = pl.BlockSpec((None,) + c0.shape[1:], lambda h: (layer, 0, 0, 0))
        so_spec = pl.BlockSpec((None, None) + s0.shape[2:], lambda h: (layer, h, 0, 0, 0))
    n_in = 3 + len(weights) + 2
    return pl.pallas_call(
        functools.partial(_ssd_lanes_kernel, nb=nb, ts=ts, n_alias=len(aliased), out_mode=out_mode),
        grid=(SSM_H,),
        in_specs=([seg("ssm_z", 512), seg("ssm_xbc", 768), seg("ssm_dt", 128)]
                  + [const(a) for a in weights] + [const(c0)]
                  + [pl.BlockSpec((None, None) + s0.shape[2:], lambda h: (layer, h, 0, 0, 0))]
                  + [pl.BlockSpec(memory_space=pl.ANY) for _ in aliased]),
        out_specs=[pl.BlockSpec((rows, MIX), lambda h: (0, 0)), co_spec, so_spec],
        out_shape=[jax.ShapeDtypeStruct((rows, MIX), BF16),
                   jax.ShapeDtypeStruct(c0.shape, F32), jax.ShapeDtypeStruct(s0.shape, F32)],
        scratch_shapes=[pltpu.VMEM((ts, SSM_CONV_DIM, nb), F32), pltpu.VMEM((ts, MIX, nb), F32),
                        pltpu.VMEM((ts, LANES, nb), F32), pltpu.VMEM((ts, LANES, nb), F32),
                        pltpu.VMEM((SSM_G * ts * ts, nb), F32), pltpu.VMEM((ts, MIX, nb), F32)],
        input_output_aliases={n_in + i: 1 + i for i in range(len(aliased))},
        compiler_params=pltpu.CompilerParams(
            dimension_semantics=("arbitrary",), vmem_limit_bytes=VMEM_LIMIT),
        name=name,
    )(proj, proj, proj, *weights, c0, s0, *aliased)


def _ret_kernel(*refs, rows, lb, tv, chained, nsq, n_alias, out_mode):
    row_refs = refs[:4]
    lg_ref, cos_ref, sin_ref, s0_ref = refs[4:8]
    y_out, so_ref = refs[8 + n_alias:10 + n_alias]
    scr = refs[10 + n_alias:]
    nblk = rows // lb
    t_col = _row_in_block(rows, lb)
    s_row = lax.broadcasted_iota(jnp.int32, (1, rows), 1) % lb
    n_col = jnp.maximum(t_col - (lb - tv) + 1, 0).astype(F32)
    n_row = jnp.maximum(s_row - (lb - tv) + 1, 0).astype(F32)

    def decay_matrix(h):
        r_i, c_i = _iotas(rows)
        causal = ((r_i // lb) == (c_i // lb)) & (r_i >= c_i)
        diff = (n_col - n_row) * lg_ref[:, h:h + 1]
        return jnp.where(causal, jnp.exp(jnp.minimum(diff, 0.0)), 0.0)

    width = RET_H * RET_DK

    def state_scales():
        head = lax.broadcasted_iota(jnp.int32, (1, width), 1) // RET_DK
        lg_lane = lg_ref[:, 0:1]
        for h in range(1, RET_H):
            lg_lane = jnp.where(head == h, lg_ref[:, h:h + 1], lg_lane)
        gc = n_col * lg_lane
        return jnp.exp(gc), jnp.exp(float(tv) * lg_lane - gc)

    if chained:
        dec_scr, scale_scr = scr[1:]

        @pl.when(pl.program_id(1) == 0)
        def _():
            for h in range(RET_H):
                dec_scr[h] = decay_matrix(h)
            scale_scr[0], scale_scr[1] = state_scales()

    lane = lax.broadcasted_iota(jnp.int32, (rows, width), 1)
    first_half = (lane % RET_DK) < (RET_DK // 2)
    cos = cos_ref[...]
    sin = jnp.where(first_half, -sin_ref[...], sin_ref[...])

    def rope(x):
        partner = jnp.where(first_half, pltpu.roll(x, width - RET_DK // 2, axis=1),
                            pltpu.roll(x, RET_DK // 2, axis=1))
        return x * cos + partner * sin

    def chunk(q_ref, k_ref, v_ref, g_ref, y_ref, s_scr):
        q = rope(q_ref[...].astype(F32))
        k = rope(k_ref[...].astype(F32)) * (RET_DK ** -0.5)
        v = v_ref[...]
        if tv < lb:
            k = jnp.where(t_col >= lb - tv, k, 0.0)
        q_scale, k_scale = (scale_scr[0], scale_scr[1]) if chained else state_scales()
        q_in = q * q_scale
        k_out = k * k_scale
        yield

        so = _state_slot(so_ref, out_mode) if not chained else None
        outs = []
        for h in range(RET_H):
            ks = slice(h * RET_DK, (h + 1) * RET_DK)
            vs = slice(h * RET_DV, (h + 1) * RET_DV)
            att = _dot(q[:, ks], k[:, ks], NT) * (dec_scr[h] if chained else decay_matrix(h))
            ge = float(tv) * lg_ref[:, h:h + 1]
            qin = q_in[:, ks]
            kout = k_out[:, ks]
            if chained:
                cur = s_scr[h]
                if rows % LANES == 0:
                    oh = _dot(jnp.concatenate([att.astype(BF16), qin.astype(BF16)], axis=1),
                              jnp.concatenate([v[:, vs], cur.astype(BF16)], axis=0))
                else:
                    oh = _dot(att, v[:, vs]) + _dot(qin, cur)
                s_scr[h] = jnp.exp(ge) * cur + _dot(kout, v[:, vs], TN)
            else:
                oh = _dot(att, v[:, vs])
                o_rows = []
                for b in range(nblk):
                    rs = slice(b * lb, (b + 1) * lb)
                    cur = s0_ref[b, h]
                    o_rows.append(_dot(qin[rs], cur))
                    so[b, h] = jnp.exp(ge) * cur + _dot(kout[rs], v[rs, vs], TN)
                oh = oh + jnp.concatenate(o_rows, axis=0)
            outs.append(_rms(oh))
            yield
        y_ref[...] = (_silu(g_ref[...].astype(F32)) * jnp.concatenate(outs, axis=1)).astype(BF16)

    _run_chunks(chunk, row_refs, y_out, s0_ref, so_ref, scr[:1], chained, nsq, out_mode)


def _ret(proj, lg, cos, sin, s0, *, lb, tv, rows, chained, name, nsq=1, **kw):
    kern = functools.partial(_ret_kernel, rows=rows, lb=lb, tv=tv, chained=chained)
    scratch = ([pltpu.VMEM((nsq, RET_H, RET_DK, RET_DV), F32), pltpu.VMEM((RET_H, rows, rows), F32),
                pltpu.VMEM((2, rows, RET_H * RET_DK), F32)] if chained else [])
    return _rec_call(kern, proj, [("ret_q", 256), ("ret_k", 256), ("ret_v", 512), ("ret_g", 512)],
                     [(lg, 0)], [s0], [MIX], rows=rows, chained=chained, scratch=scratch, name=name,
                     tables=(cos, sin), nsq=nsq, **kw)


def _dense2_kernel(x_ref, gate_ref, yg_ref, ys_ref, yr_ref, lnw_ref, lnb_ref, wg_ref, ws_ref, wr_ref,
                   wo_ref, l1w_ref, l1b_ref, w1_ref, b1_ref, w2_ref, b2_ref, l2w_ref, l2b_ref, o_ref,
                   *, ff_chunk, parts, input_ln):
    tm = x_ref.shape[0]

    def rows_stage(rs):
        branches = [_dot(y_ref[rs, :], w_ref[...])
                    for y_ref, w_ref in ((yg_ref, wg_ref), (ys_ref, ws_ref), (yr_ref, wr_ref))]
        yield
        gate = lambda i: jax.nn.sigmoid(gate_ref[rs, i * D_MODEL:(i + 1) * D_MODEL].astype(F32))
        merged = gate(0) * branches[0] + gate(1) * branches[1] + gate(2) * branches[2]
        yield
        mix = _dot(merged, wo_ref[...])
        yield
        x = x_ref[rs, :]
        if input_ln:
            x = _layer_norm(x, lnw_ref[...], lnb_ref[...])
        h = _layer_norm(ALPHA * x + mix, l1w_ref[...], l1b_ref[...])
        hb = h.astype(BF16)
        ff = jnp.zeros_like(h) + b2_ref[...]
        yield
        for c0 in range(0, D_FF, ff_chunk):
            hid = jnp.dot(hb, w1_ref[:, c0:c0 + ff_chunk], preferred_element_type=F32)
            hid = jnp.square(jnp.maximum(hid + b1_ref[:, c0:c0 + ff_chunk], 0.0))
            ff = ff + _dot(hid, w2_ref[c0:c0 + ff_chunk, :])
            yield
        o_ref[rs, :] = _layer_norm(ALPHA * h + ff, l2w_ref[...], l2b_ref[...])

    _interleave([rows_stage(pl.ds(i * (tm // parts), tm // parts)) for i in range(parts)],
                stagger=True)


def _dense2(x, proj, yg, ys, yr, lnw, lnb, wl, layer, *, input_ln, tm, name):
    m = x.shape[0]
    row = lambda w: pl.BlockSpec((tm, w), lambda i: (i, 0))
    vec = pl.BlockSpec((1, D_MODEL), lambda i: (0, 0))
    const = lambda r, w: pl.BlockSpec((None, r, w), lambda i: (layer, 0, 0),
                                      pipeline_mode=pl.Buffered(1))
    return pl.pallas_call(
        functools.partial(_dense2_kernel, ff_chunk=1024, parts=(2 if tm % 32 == 0 else 1),
                          input_ln=input_ln),
        grid=(m // tm,),
        in_specs=[row(D_MODEL), row(3 * D_MODEL), row(MIX), row(MIX), row(MIX), vec, vec,
                  const(MIX, D_MODEL), const(MIX, D_MODEL), const(MIX, D_MODEL),
                  const(D_MODEL, D_MODEL), const(1, D_MODEL), const(1, D_MODEL),
                  const(D_MODEL, D_FF), const(1, D_FF), const(D_FF, D_MODEL), const(1, D_MODEL),
                  const(1, D_MODEL), const(1, D_MODEL)],
        out_specs=row(D_MODEL),
        out_shape=jax.ShapeDtypeStruct((m, D_MODEL), F32),
        compiler_params=pltpu.CompilerParams(
            dimension_semantics=("parallel",), vmem_limit_bytes=VMEM_LIMIT),
        name=name,
    )(x, proj, yg, ys, yr, lnw, lnb, wl["w_gla_out"], wl["w_ssm_out"], wl["w_ret_out"], wl["w_o"],
      wl["ln1_w"], wl["ln1_b"], wl["w_ff1"], wl["b_ff1"], wl["w_ff2"], wl["b_ff2"],
      wl["ln2_w"], wl["ln2_b"])


def _pick_tile(n, pref):
    t = min(n, pref)
    while n % t or t % SUBLANES:
        t -= 1
    return t


def _rearrange_w_in(w):
    w = jnp.swapaxes(w, -1, -2)
    offs = [0]
    for s in SPLIT_SIZES:
        offs.append(offs[-1] + s)
    names = ("gla_q", "gla_k", "gla_v", "gla_r", "gla_a", "ssm_z", "ssm_xbc", "ssm_dt",
             "ret_q", "ret_k", "ret_v", "ret_g", "gates")
    seg = {n: w[..., offs[i]:offs[i + 1], :].astype(BF16) for i, n in enumerate(names)}
    pad = lambda a: jnp.pad(a, ((0, 0),) * (a.ndim - 2) + ((0, LANES - a.shape[-2]), (0, 0)))
    order = sorted(COL, key=COL.get)
    parts = [pad(seg[n]) if n in ("gla_a", "ssm_dt") else seg[n] for n in order]
    return jnp.concatenate(parts, axis=-2)


def _rope_tables(pos):
    half = RET_DK // 2
    inv_freq = ROPE_BASE ** (-jnp.arange(half, dtype=F32) / half)
    ang = pos.astype(F32)[:, None] * inv_freq[None, :]
    cos = jnp.tile(jnp.cos(ang), (1, 2 * RET_H))
    sin = jnp.tile(jnp.sin(ang), (1, 2 * RET_H))
    return cos, sin


def kernel(x_prompt, x_sample, state_gla, state_ssm, state_conv, state_ret, meta_tokens,
           ln_in_w, ln_in_b, w_in, w_gla_a2, b_gla_a, w_gla_norm, conv_w, conv_b, dt_bias,
           a_log, d_skip, w_ssm_norm, w_gla_out, w_ssm_out, w_ret_out, w_o, ln1_w, ln1_b,
           w_ff1, b_ff1, w_ff2, b_ff2, ln2_w, ln2_b):
    bp, tp, d = x_prompt.shape
    bs, ts, _ = x_sample.shape
    assert d == D_MODEL and tp % CHUNK == 0 and w_in.shape[0] == DEPTH
    assert SSM_CONV - 1 <= SAMPLE_ROWS - ts
    nchunk = tp // CHUNK
    pad_rows = SAMPLE_ROWS - ts

    x_body = x_prompt.reshape(bp * tp, d)
    n_sample = bs * SAMPLE_ROWS
    rows_s = min(CHUNK, n_sample)
    assert n_sample % rows_s == 0 and n_sample % N_META == 0
    n_small = n_sample + CHUNK
    x_small = jnp.concatenate(
        [jnp.pad(x_sample, ((0, 0), (pad_rows, 0), (0, 0))).reshape(n_sample, d),
         meta_tokens.astype(F32), jnp.zeros((CHUNK - N_META, d), F32)], axis=0)

    cos_b, sin_b = _rope_tables(N_META + jnp.arange(tp, dtype=jnp.int32))
    cos_m, sin_m = _rope_tables(jnp.arange(N_META, dtype=jnp.int32))
    pos_tile = PAST_LEN - pad_rows + jnp.arange(SAMPLE_ROWS, dtype=jnp.int32)
    cos_s, sin_s = _rope_tables(jnp.tile(pos_tile, rows_s // SAMPLE_ROWS))
    lg_ret = jnp.pad(jnp.log1p(-jnp.exp2(-5.0 - jnp.arange(RET_H, dtype=F32))),
                     (0, LANES - RET_H)).reshape(1, 1, LANES)

    zero_gla = (jnp.zeros((1, 1, GLA_H, GLA_DK, GLA_DV), F32), 0)
    zero_ssm = (jnp.zeros((1, 1, SSM_H, SSM_N, SSM_P), F32), 0)
    zero_conv = (jnp.zeros((1, 1, SUBLANES, SSM_CONV_DIM), F32), 0)
    zero_ret = (jnp.zeros((1, 1, RET_H, RET_DK, RET_DV), F32), 0)
    from_meta = lambda a: (a[None], 0)
    ssm_lanes = state_ssm.transpose(0, 2, 3, 4, 1)
    conv_lanes = state_conv.transpose(0, 2, 1, 3)

    rowvec = lambda a: a.reshape(DEPTH, 1, -1)
    lane_pad = lambda a: jnp.pad(a, ((0, 0), (0, LANES - a.shape[1]))).reshape(DEPTH, 1, LANES)
    w_in_all = _rearrange_w_in(w_in)
    wl = dict(w_gla_out=w_gla_out.astype(BF16), w_ssm_out=w_ssm_out.astype(BF16),
              w_ret_out=w_ret_out.astype(BF16), w_o=w_o.astype(BF16),
              ln1_w=rowvec(ln1_w), ln1_b=rowvec(ln1_b),
              w_ff1=w_ff1.astype(BF16), b_ff1=rowvec(b_ff1),
              w_ff2=w_ff2.astype(BF16), b_ff2=rowvec(b_ff2),
              ln2_w=rowvec(ln2_w), ln2_b=rowvec(ln2_b))
    gla_w = [jnp.pad(w_gla_a2, ((0, 0), (0, LANES - GLA_RANK), (0, 0))), rowvec(b_gla_a),
             rowvec(w_gla_norm)]
    ssd_w = [conv_w, rowvec(conv_b), lane_pad(dt_bias), lane_pad(a_log),
             rowvec(jnp.repeat(d_skip, SSM_P, axis=1)), rowvec(w_ssm_norm)]
    colvec = lambda a: a.reshape(DEPTH, -1, 1)
    ssd_cols = [conv_w.transpose(0, 2, 1)] + [colvec(a) for a in ssd_w[1:]]
    ln_w, ln_b = ln_in_w.reshape(1, -1), ln_in_b.reshape(1, -1)

    tm_in = _pick_tile(bp * tp, 2048)
    tm_d2 = _pick_tile(bp * tp, 512)
    tm_d2s = _pick_tile(n_small, 384)
    tn = 1536
    meta = dict(row0=n_sample, nseq=1, nchunk=1, rows=N_META, lb=N_META, tv=N_META, chained=True,
                per_seq_state=False)
    body = dict(row0=0, nseq=bp, nchunk=nchunk, rows=CHUNK, lb=CHUNK, tv=CHUNK, chained=True,
                per_seq_state=False)
    seqs_per_step = lambda want: max(n for n in (1, 2, 4) if n <= want and bp % n == 0)
    samp = dict(row0=0, nseq=bs, nchunk=1, rows=rows_s, lb=SAMPLE_ROWS, tv=ts, chained=False,
                per_seq_state=True)

    names = ("gla_p", "gla_s", "ssm_p", "ssm_s", "conv_p", "conv_s", "ret_p", "ret_s")
    st = {k: None for k in names}
    stk = lambda *keys: [] if st[keys[0]] is None else [st[k] for k in keys]
    xb, xs = x_body, x_small
    for l in range(DEPTH):
        proj_s = _in_proj(xs, ln_w, ln_b, w_in_all, l, apply_ln=(l == 0), tm=n_small, tn=tn,
                          name=f"inproj_small_{l}")
        yg_m, sg_m = _gla(proj_s, gla_w, zero_gla, layer=l, name=f"gla_meta_{l}", **meta)
        ys_m, cv_m, ss_m = _ssd(proj_s, ssd_w, zero_conv, zero_ssm, layer=l,
                                name=f"ssd_meta_{l}", **meta)
        yr_m, sr_m = _ret(proj_s, lg_ret, cos_m, sin_m, zero_ret, layer=l,
                          name=f"ret_meta_{l}", **meta)
        yg_s, st["gla_s"] = _gla(proj_s, gla_w, (state_gla, l), layer=l, stacked=stk("gla_s"),
                                 name=f"gla_sample_{l}", **samp)
        ys_s, st["conv_s"], st["ssm_s"] = _ssd_lanes(proj_s, ssd_cols, conv_lanes, ssm_lanes, layer=l,
                                                     nb=bs, ts=ts, stacked=stk("conv_s", "ssm_s"),
                                                     name=f"ssd_sample_{l}")
        yr_s, st["ret_s"] = _ret(proj_s, lg_ret, cos_s, sin_s, (state_ret, l), layer=l,
                                 stacked=stk("ret_s"), name=f"ret_sample_{l}", **samp)
        zpad = jnp.zeros((CHUNK - N_META, MIX), BF16)
        yg = jnp.concatenate([yg_s, yg_m, zpad], axis=0)
        ys = jnp.concatenate([ys_s, ys_m, zpad], axis=0)
        yr = jnp.concatenate([yr_s, yr_m, zpad], axis=0)
        xs = _dense2(xs, proj_s, yg, ys, yr, ln_w, ln_b, wl, l, input_ln=(l == 0), tm=tm_d2s,
                     name=f"dense2_small_{l}")

        proj_b = _in_proj(xb, ln_w, ln_b, w_in_all, l, apply_ln=(l == 0), tm=tm_in, tn=tn,
                          name=f"inproj_body_{l}")
        yg_b, st["gla_p"] = _gla(proj_b, gla_w, from_meta(sg_m), layer=l, stacked=stk("gla_p"),
                                 name=f"gla_body_{l}", nsq=seqs_per_step(4), **body)
        ys_b, st["conv_p"], st["ssm_p"] = _ssd(proj_b, ssd_w, from_meta(cv_m), from_meta(ss_m),
                                               layer=l, stacked=stk("conv_p", "ssm_p"),
                                               name=f"ssd_body_{l}", nsq=seqs_per_step(2), **body)
        yr_b, st["ret_p"] = _ret(proj_b, lg_ret, cos_b, sin_b, from_meta(sr_m), layer=l,
                                 stacked=stk("ret_p"), name=f"ret_body_{l}", nsq=seqs_per_step(4),
                                 **body)
        xb = _dense2(xb, proj_b, yg_b, ys_b, yr_b, ln_w, ln_b, wl, l, input_ln=(l == 0), tm=tm_d2,
                     name=f"dense2_body_{l}")

    y_prompt = xb.reshape(bp, tp, d)
    y_sample = xs[:n_sample].reshape(bs, SAMPLE_ROWS, d)[:, pad_rows:]
    tail3 = lambda c: c[:, :, SUBLANES - (SSM_CONV - 1):, :]
    return (y_prompt, y_sample, st["gla_p"], st["gla_s"], st["ssm_p"],
            st["ssm_s"].transpose(0, 4, 1, 2, 3), tail3(st["conv_p"]),
            st["conv_s"].transpose(0, 2, 1, 3), st["ret_p"], st["ret_s"])
```

```python
import functools

import jax
import jax.numpy as jnp
from jax import lax
from jax.experimental import pallas as pl
from jax.experimental.pallas import tpu as pltpu

F32 = jnp.float32
BF16 = jnp.bfloat16

D_MODEL = 1024
DEPTH = 2
N_META = 16
MIX = 512
GLA_H, GLA_DK, GLA_DV, GLA_RANK = 4, 64, 128, 16
GLA_GATE_NORM = 16.0
SSM_H, SSM_P, SSM_N, SSM_G, SSM_CONV = 8, 64, 64, 2, 4
SSM_CONV_DIM = MIX + 2 * SSM_G * SSM_N
SSM_PAIRS = SSM_H // 2
RET_H, RET_DK, RET_DV = 4, 64, 128
ROPE_BASE = 10000.0
D_FF = 4 * D_MODEL
ALPHA = (2 * DEPTH) ** 0.25
PAST_LEN = 16384
SPLIT_SIZES = (256, 256, 512, 512, 16, 512, 768, 8, 256, 256, 512, 512, 3072)

LANES = 128
SUBLANES = 8
VMEM_LIMIT = 56 * 1024 * 1024

COL = dict(gates=0, gla_v=3072, gla_r=3584, ssm_z=4096, ret_v=4608, ret_g=5120,
           gla_q=5632, gla_k=5888, ssm_xbc=6144, ret_q=6912, ret_k=7168, gla_a=7424, ssm_dt=7552)
N_PROJ = 7680
SAMPLE_ROWS = SUBLANES
GLA_BASE = SUBLANES
CHUNK = 128

NN = (((1,), (0,)), ((), ()))
NT = (((1,), (1,)), ((), ()))
TN = (((0,), (0,)), ((), ()))


def _dot(a, b, dims=NN):
    return lax.dot_general(a.astype(BF16), b.astype(BF16), dims, preferred_element_type=F32)


def _dot_sel(sel, x, dims=NN, sel_first=True):
    hi = x.astype(BF16)
    r1 = x - hi.astype(F32)
    mid = r1.astype(BF16)
    lo = (r1 - mid.astype(F32)).astype(BF16)
    sb = sel.astype(BF16)
    out = None
    for part in (hi, mid, lo):
        ops = (sb, part) if sel_first else (part, sb)
        term = lax.dot_general(*ops, dims, preferred_element_type=F32)
        out = term if out is None else out + term
    return out


def _layer_norm(x, w, b):
    mu = jnp.mean(x, axis=-1, keepdims=True)
    xc = x - mu
    var = jnp.mean(xc * xc, axis=-1, keepdims=True)
    return xc * lax.rsqrt(var + 1e-5) * w + b


def _rms(x):
    return x * lax.rsqrt(jnp.mean(x * x, axis=-1, keepdims=True) + 1e-6)


def _silu(x):
    return x * jax.nn.sigmoid(x)


def _iotas(rows):
    return (lax.broadcasted_iota(jnp.int32, (rows, rows), 0),
            lax.broadcasted_iota(jnp.int32, (rows, rows), 1))


def _row_in_block(rows, lb):
    return lax.broadcasted_iota(jnp.int32, (rows, 1), 0) % lb


def _inproj_kernel(x_ref, lnw_ref, lnb_ref, w_ref, proj_ref, xb_scr, *, apply_ln, parts):
    tm = x_ref.shape[0]

    def first_tile(rs):
        x = x_ref[rs, :]
        if apply_ln:
            x = _layer_norm(x, lnw_ref[...], lnb_ref[...])
        xb = x.astype(BF16)
        xb_scr[rs, :] = xb
        yield
        proj_ref[rs, :] = lax.dot_general(xb, w_ref[...], NT, preferred_element_type=F32).astype(BF16)

    @pl.when(pl.program_id(1) == 0)
    def _():
        _interleave([first_tile(pl.ds(p * (tm // parts), tm // parts)) for p in range(parts)],
                    stagger=True)

    @pl.when(pl.program_id(1) > 0)
    def _():
        proj_ref[...] = lax.dot_general(xb_scr[...], w_ref[...], NT,
                                        preferred_element_type=F32).astype(BF16)


def _in_proj(x, lnw, lnb, w, layer, *, apply_ln, tm, tn, name):
    m = x.shape[0]
    grid = (m // tm, N_PROJ // tn)
    return pl.pallas_call(
        functools.partial(_inproj_kernel, apply_ln=apply_ln,
                          parts=(4 if apply_ln and tm % 64 == 0 else 1)),
        grid=grid,
        in_specs=[pl.BlockSpec((tm, D_MODEL), lambda i, j: (i, 0)),
                  pl.BlockSpec((1, D_MODEL), lambda i, j: (0, 0)),
                  pl.BlockSpec((1, D_MODEL), lambda i, j: (0, 0)),
                  pl.BlockSpec((None, tn, D_MODEL), lambda i, j: (layer, j, 0))],
        out_specs=pl.BlockSpec((tm, tn), lambda i, j: (i, j)),
        out_shape=jax.ShapeDtypeStruct((m, N_PROJ), BF16),
        scratch_shapes=[pltpu.VMEM((tm, D_MODEL), BF16)],
        compiler_params=pltpu.CompilerParams(
            dimension_semantics=("parallel", "arbitrary"), vmem_limit_bytes=VMEM_LIMIT),
        name=name,
    )(x, lnw, lnb, w)


def _rec_call(kernel_fn, proj, segs, consts, states, out_widths, *, layer, row0, nseq, nchunk,
              rows, chained, per_seq_state, scratch, name, tables=(), stacked=None, nsq=1):
    rb = row0 // rows
    nb = nsq if chained else rows // SAMPLE_ROWS
    grid = (nseq // nb, nchunk) if chained else (nseq // nb, 1)
    assert nseq % nb == 0 and (nsq == 1 or (chained and row0 == 0))
    out_mode = "plain" if stacked is None else ("first" if layer == 0 else "later")
    aliased = list(stacked) if out_mode == "later" else []

    def row_idx(b, c):
        return b * nchunk + c if chained else b

    def col(seg, w):
        cbi = COL[seg] // w
        if nsq > 1:
            return pl.BlockSpec((nsq, rows, w), lambda b, c: (b, c, cbi))
        return pl.BlockSpec((rows, w), lambda b, c: (rb + row_idx(b, c), cbi))

    def row_out(w):
        if nsq > 1:
            return pl.BlockSpec((nsq, rows, w), lambda b, c: (b, c, 0))
        return pl.BlockSpec((rows, w), lambda b, c: (row_idx(b, c), 0))

    def const_spec(a, lyr):
        zeros = (0,) * (a.ndim - 1)
        return pl.BlockSpec((None,) + a.shape[1:], lambda b, c: (lyr,) + zeros)

    def state_in_spec(a, lyr):
        zeros = (0,) * (a.ndim - 2)
        if per_seq_state:
            return pl.BlockSpec((None, nb) + a.shape[2:], lambda b, c: (lyr, b) + zeros)
        return pl.BlockSpec((None, 1) + a.shape[2:], lambda b, c: (lyr, 0) + zeros)

    def state_out_spec(a):
        zeros = (0,) * (a.ndim - 2)
        if out_mode == "plain":
            return pl.BlockSpec((nb,) + a.shape[2:], lambda b, c: (b,) + zeros)
        if out_mode == "first":
            return pl.BlockSpec((DEPTH, nb) + a.shape[2:], lambda b, c: (0, b) + zeros)
        return pl.BlockSpec((None, nb) + a.shape[2:], lambda b, c: (layer, b) + zeros)

    def state_out_shape(a):
        lead = (nseq,) if out_mode == "plain" else (DEPTH, nseq)
        return jax.ShapeDtypeStruct(lead + a.shape[2:], F32)

    consts = [c if isinstance(c, tuple) else (c, layer) for c in consts]
    in_specs = ([col(s, w) for s, w in segs] + [const_spec(a, lyr) for a, lyr in consts]
                + [pl.BlockSpec((rows, t.shape[1]), lambda b, c: (c, 0)) for t in tables]
                + [state_in_spec(a, lyr) for a, lyr in states]
                + [pl.BlockSpec(memory_space=pl.ANY) for _ in aliased])
    n_in = len(in_specs) - len(aliased)
    n_rows = nseq * nchunk * rows if chained else nseq * SAMPLE_ROWS
    out_specs = [row_out(w) for w in out_widths] + [state_out_spec(a) for a, _ in states]
    row_shape = (lambda w: (nseq, nchunk * rows, w)) if nsq > 1 else (lambda w: (n_rows, w))
    out_shape = ([jax.ShapeDtypeStruct(row_shape(w), BF16) for w in out_widths]
                 + [state_out_shape(a) for a, _ in states])
    if nsq > 1:
        proj = proj.reshape(nseq, nchunk * rows, proj.shape[-1])
    res = pl.pallas_call(
        functools.partial(kernel_fn, n_alias=len(aliased), out_mode=out_mode, nsq=nsq),
        grid=grid, in_specs=in_specs, out_specs=out_specs, out_shape=out_shape,
        scratch_shapes=scratch,
        input_output_aliases={n_in + i: len(out_widths) + i for i in range(len(aliased))},
        compiler_params=pltpu.CompilerParams(
            dimension_semantics=("parallel", "arbitrary"), vmem_limit_bytes=VMEM_LIMIT),
        name=name,
    )(*([proj] * len(segs)), *[a for a, _ in consts], *tables, *[a for a, _ in states], *aliased)
    if nsq > 1:
        res = ([r.reshape(n_rows, r.shape[-1]) for r in res[:len(out_widths)]]
               + list(res[len(out_widths):]))
    return res


def _state_slot(ref, out_mode):
    if out_mode != "first":
        return ref
    ref[1:] = jnp.zeros((DEPTH - 1,) + ref.shape[1:], F32)
    return ref.at[0]


def _interleave(stages, stagger=False):
    stages = list(stages)
    delay = {id(gen): (i if stagger else 0) for i, gen in enumerate(stages)}
    rnd = 0
    while stages:
        for gen in list(stages):
            if delay[id(gen)] > rnd:
                continue
            try:
                next(gen)
            except StopIteration:
                stages.remove(gen)
        rnd += 1


def _run_chunks(chunk, row_refs, y_ref, s0_ref, so_ref, scr, chained, nsq, out_mode):
    if not chained:
        _interleave([chunk(*row_refs, y_ref, None)])
        return
    s_all, = scr

    @pl.when(pl.program_id(1) == 0)
    def _():
        for j in range(nsq):
            s_all[j] = s0_ref[0]

    view = lambda r, j: r.at[j] if nsq > 1 else r
    _interleave([chunk(*[view(r, j) for r in row_refs], view(y_ref, j), s_all.at[j])
                 for j in range(nsq)])

    @pl.when(pl.program_id(1) == pl.num_programs(1) - 1)
    def _():
        dst = _state_slot(so_ref, out_mode)
        for j in range(nsq):
            dst[j] = s_all[j]


def _gla_kernel(*refs, rows, lb, tv, chained, nsq, n_alias, out_mode):
    row_refs = refs[:5]
    w2_ref, ba_ref, wn_ref, eb_ref, sh_ref, s0_ref = refs[5:11]
    y_out, so_ref = refs[11 + n_alias:13 + n_alias]
    scr = refs[13 + n_alias:]
    nblk = rows // lb
    width = GLA_H * GLA_DK

    def chunk(q_ref, k_ref, v_ref, r_ref, a_ref, y_ref, s_scr):
        states = [s_scr[h] for h in range(GLA_H)] if chained else None
        q = q_ref[...].astype(F32) * (GLA_DK ** -0.5)
        k = k_ref[...].astype(F32)
        vb = v_ref[...]
        a = _dot(a_ref[...], w2_ref[...]) + ba_ref[...]
        g = jax.nn.log_sigmoid(a) * (1.0 / GLA_GATE_NORM)
        t_in = _row_in_block(rows, lb)
        if tv < lb:
            valid = t_in >= lb - tv
            g = jnp.where(valid, g, 0.0)
            k = jnp.where(valid, k, 0.0)
        yield
        r_i, c_i = _iotas(rows)
        same = (r_i // lb) == (c_i // lb)
        sizes = []
        while 2 * GLA_BASE * 2 ** len(sizes) <= lb:
            sizes.append(2 * GLA_BASE * 2 ** len(sizes))
        sums = [same & (r_i >= c_i)] + ([same] if nblk > 1 else [])
        sums += [same & (c_i <= (r_i // sz) * sz + sz // 2 - 1) for sz in sizes]
        gsums = _dot_sel(jnp.concatenate(sums, axis=0), g)
        gcum = gsums[:rows]
        gtot = gcum[rows - 1:rows, :] if nblk == 1 else gsums[rows:2 * rows]
        g_mids = [gsums[(len(sums) - len(sizes) + i) * rows:(len(sums) - len(sizes) + i + 1) * rows]
                  for i in range(len(sizes))]
        qd = q * jnp.exp(gcum)
        kd = k * jnp.exp(gtot - gcum)
        sel = (lax.broadcasted_iota(jnp.int32, (rows, nblk * LANES), 0) // lb
               == lax.broadcasted_iota(jnp.int32, (rows, nblk * LANES), 1) // LANES)
        ds = jnp.exp(_dot_sel(sel, g, TN, sel_first=False))
        yield

        nbase = rows // GLA_BASE
        q3 = q.reshape(nbase, GLA_BASE, width)
        k3 = k.reshape(nbase, GLA_BASE, width)
        g3 = gcum.reshape(nbase, GLA_BASE, width)
        t3 = lax.broadcasted_iota(jnp.int32, (1, GLA_BASE, 1), 1)
        pieces = []
        for s in range(max(0, GLA_BASE - tv), GLA_BASE):
            dd = jnp.minimum(g3 - g3[:, s:s + 1, :], 0.0)
            w = q3 * k3[:, s:s + 1, :] * jnp.exp(dd)
            pieces.append(jnp.where(t3 >= s, w, 0.0).reshape(rows, width).astype(BF16))
            yield
        compact = jnp.dot(jnp.concatenate(pieces, axis=1), eb_ref[...], preferred_element_type=F32)
        spread = jnp.dot(compact.astype(BF16), sh_ref[...], preferred_element_type=F32)
        base_mask = (r_i // GLA_BASE) == (c_i // GLA_BASE)
        att = [jnp.where(base_mask, spread[:, h * LANES:h * LANES + rows], 0.0) for h in range(GLA_H)]
        yield

        for size, g_mid in zip(sizes, g_mids):
            second = (t_in % size) >= size // 2
            ql = jnp.where(second, q * jnp.exp(jnp.minimum(gcum - g_mid, 0.0)), 0.0)
            kl = jnp.where(second, 0.0, k * jnp.exp(jnp.minimum(g_mid - gcum, 0.0)))
            group = (r_i // size) == (c_i // size)
            for h in range(GLA_H):
                ks = slice(h * GLA_DK, (h + 1) * GLA_DK)
                att[h] = att[h] + jnp.where(group, _dot(ql[:, ks], kl[:, ks], NT), 0.0)
            yield

        so = _state_slot(so_ref, out_mode) if not chained else None
        o_heads = []
        for h in range(GLA_H):
            ks = slice(h * GLA_DK, (h + 1) * GLA_DK)
            vs = slice(h * GLA_DV, (h + 1) * GLA_DV)
            if chained:
                cur = states[h]
                if rows % LANES == 0:
                    oh = _dot(jnp.concatenate([att[h].astype(BF16), qd[:, ks].astype(BF16)], axis=1),
                              jnp.concatenate([vb[:, vs], cur.astype(BF16)], axis=0))
                else:
                    oh = _dot(att[h], vb[:, vs]) + _dot(qd[:, ks], cur)
                states[h] = ds[ks, :] * cur + _dot(kd[:, ks], vb[:, vs], TN)
            else:
                o_rows = []
                for b in range(nblk):
                    rs = slice(b * lb, (b + 1) * lb)
                    cur = s0_ref[b, h]
                    o_rows.append(_dot(qd[rs, ks], cur))
                    so[b, h] = ds[ks, b * LANES:(b + 1) * LANES] * cur + _dot(kd[rs, ks], vb[rs, vs], TN)
                oh = _dot(att[h], vb[:, vs]) + jnp.concatenate(o_rows, axis=0)
            o_heads.append(_rms(oh) * wn_ref[...])
            yield
        y_ref[...] = (_silu(r_ref[...].astype(F32)) * jnp.concatenate(o_heads, axis=1)).astype(BF16)
        if chained:
            for h in range(GLA_H):
                s_scr[h] = states[h]

    _run_chunks(chunk, row_refs, y_out, s0_ref, so_ref, scr, chained, nsq, out_mode)


def _gla_consts(tv):
    sources = jnp.arange(max(0, GLA_BASE - tv), GLA_BASE, dtype=jnp.int32)
    row = jnp.arange(sources.shape[0] * GLA_H * GLA_DK, dtype=jnp.int32)
    target = sources[row // (GLA_H * GLA_DK)] * GLA_H + (row % (GLA_H * GLA_DK)) // GLA_DK
    eb = (target[:, None] == jnp.arange(LANES, dtype=jnp.int32)[None, :]).astype(BF16)
    r = jnp.arange(LANES, dtype=jnp.int32)[:, None]
    c = jnp.arange(GLA_H * LANES, dtype=jnp.int32)[None, :]
    sh = ((r < GLA_BASE * GLA_H) & (r % GLA_H == c // LANES)
          & (r // GLA_H == (c % LANES) % GLA_BASE)).astype(BF16)
    return (eb[None], 0), (sh[None], 0)


def _gla(proj, weights, s0, *, lb, tv, rows, chained, name, nsq=1, **kw):
    kern = functools.partial(_gla_kernel, rows=rows, lb=lb, tv=tv, chained=chained)
    scratch = [pltpu.VMEM((nsq, GLA_H, GLA_DK, GLA_DV), F32)] if chained else []
    return _rec_call(kern, proj,
                     [("gla_q", 256), ("gla_k", 256), ("gla_v", 512), ("gla_r", 512), ("gla_a", 128)],
                     list(weights) + list(_gla_consts(tv)), [s0], [MIX], rows=rows,
                     chained=chained, scratch=scratch, name=name, nsq=nsq, **kw)


def _ssd_kernel(*refs, rows, lb, tv, chained, nsq, n_alias, out_mode):
    row_refs = refs[:3]
    cw_ref, cb_ref, dtb_ref, alog_ref, dsk_ref, wn_ref, c0_ref, s0_ref = refs[3:11]
    y_out, co_ref, so_ref = refs[11 + n_alias:14 + n_alias]
    scr = refs[14 + n_alias:]
    nblk = rows // lb
    t_in = _row_in_block(rows, lb)
    valid = (t_in >= lb - tv) if tv < lb else None
    low64 = lax.broadcasted_iota(jnp.int32, (SSM_N, LANES), 1) < SSM_P

    def chunk(z_ref, x_ref, dt_ref, y_ref, s_scr, ext_scr, j):
        states = [s_scr[p] for p in range(SSM_PAIRS)] if chained else None
        co = _state_slot(co_ref, out_mode)
        if chained:
            xb = x_ref[...]
            xin = xb.astype(F32)
            tail = ext_scr[...]
            r_s, c_s = _iotas(rows)
            t8 = lax.broadcasted_iota(jnp.int32, (SUBLANES, 1), 0)
            conv = cb_ref[...] + cw_ref[SSM_CONV - 1:SSM_CONV, :] * xin
            head = jnp.zeros((SUBLANES, SSM_CONV_DIM), F32)
            for d in range(1, SSM_CONV):
                tap = cw_ref[SSM_CONV - 1 - d:SSM_CONV - d, :]
                shifted = jnp.dot((c_s == r_s - d).astype(BF16), xb, preferred_element_type=F32)
                conv = conv + tap * shifted
                head = head + tap * jnp.where(t8 < d, pltpu.roll(tail, d, axis=0), 0.0)
            conv = jnp.concatenate([conv[:SUBLANES] + head, conv[SUBLANES:]], axis=0)
            co[j] = xin[rows - SUBLANES:, :]
            ext_scr[...] = xin[rows - SUBLANES:, :]
        else:
            ext_scr[0:SUBLANES, :] = jnp.zeros((SUBLANES, SSM_CONV_DIM), F32)
            xin = jnp.where(valid, x_ref[...].astype(F32), c0_ref[...].reshape(rows, SSM_CONV_DIM))
            ext_scr[SUBLANES:SUBLANES + rows, :] = xin
            conv = cb_ref[...]
            for i in range(SSM_CONV):
                conv = conv + cw_ref[i:i + 1, :] * ext_scr[pl.ds(SUBLANES - (SSM_CONV - 1) + i, rows), :]
            co[...] = xin.reshape(nblk, lb, SSM_CONV_DIM)
        act = _silu(conv)
        yield

        dt = jax.nn.softplus(dt_ref[...].astype(F32) + dtb_ref[...])
        gdt = dt * (-jnp.exp(alog_ref[...]))
        if valid is not None:
            gdt = jnp.where(valid, gdt, 0.0)
        r_i, c_i = _iotas(rows)
        same = (r_i // lb) == (c_i // lb)
        causal = same & (r_i >= c_i)
        gcum = _dot_sel(causal, gdt)
        if rows < LANES:
            gsq = jnp.concatenate([gcum, jnp.zeros((LANES - rows, LANES), F32)], axis=0)
            gcum_t = gsq.T[:, :rows]
        else:
            gcum_t = gcum.T
        gtot = gcum[rows - 1:rows, :] if nblk == 1 else _dot_sel(same, gdt)

        lane = lax.broadcasted_iota(jnp.int32, (rows, LANES), 1)
        low = lane < SSM_P
        bcol = act[:, MIX:MIX + LANES]
        ccol = act[:, MIX + LANES:MIX + 2 * LANES]
        bswap = pltpu.roll(bcol, SSM_N, axis=1)
        cswap = pltpu.roll(ccol, SSM_N, axis=1)
        b2 = (jnp.where(low, bcol, bswap), jnp.where(low, bswap, bcol))
        c2 = (jnp.where(low, ccol, cswap), jnp.where(low, cswap, ccol))
        cb = (_dot(jnp.where(low, ccol, 0.0), bcol, NT), _dot(jnp.where(low, 0.0, ccol), bcol, NT))

        def pair_lanes(x, p):
            return jnp.where(low[:x.shape[0]], x[:, 2 * p:2 * p + 1], x[:, 2 * p + 1:2 * p + 2])

        yield
        so = _state_slot(so_ref, out_mode) if not chained else None
        y_pairs = []
        for p in range(SSM_PAIRS):
            gi = p // (SSM_PAIRS // SSM_G)
            decs = []
            for h in (2 * p, 2 * p + 1):
                diff = jnp.minimum(gcum[:, h:h + 1] - gcum_t[h:h + 1, :], 0.0)
                decs.append(cb[gi] * jnp.where(causal, jnp.exp(diff), 0.0))
            xp = act[:, p * LANES:(p + 1) * LANES]
            vp = xp * pair_lanes(dt, p)
            if valid is not None:
                vp = jnp.where(valid, vp, 0.0)
            vbd = jnp.concatenate([jnp.where(low, vp, 0.0), jnp.where(low, 0.0, vp)], axis=0)
            g2 = pair_lanes(gcum, p)
            ge2 = pair_lanes(gtot, p)
            cin = c2[gi] * jnp.exp(g2)
            bout = b2[gi] * jnp.exp(ge2 - g2)
            if chained:
                cur = states[p]
                bd = jnp.concatenate([jnp.where(low64, cur, 0.0), jnp.where(low64, 0.0, cur)], axis=0)
                if rows % LANES == 0:
                    o = _dot(jnp.concatenate(decs + [cin], axis=1).astype(BF16),
                             jnp.concatenate([vbd.astype(BF16), bd.astype(BF16)], axis=0))
                else:
                    o = _dot(jnp.concatenate(decs, axis=1), vbd) + _dot(cin, bd)
                u = _dot(bout, vp, TN)
                states[p] = jnp.exp(ge2) * cur + jnp.where(low64, u[:SSM_N, :], u[SSM_N:, :])
            else:
                o = _dot(jnp.concatenate(decs, axis=1), vbd)
                halves = []
                for odd, (ci, bo, vv) in enumerate(
                        ((cin, bout, vp),
                         tuple(pltpu.roll(t, SSM_P, axis=1) for t in (cin, bout, vp)))):
                    h = 2 * p + odd
                    o_rows = []
                    for b in range(nblk):
                        rs = slice(b * lb, (b + 1) * lb)
                        cur = s0_ref[b, h]
                        o_rows.append(_dot(ci[rs, :SSM_N], cur))
                        so[b, h] = (jnp.exp(gtot[b * lb:b * lb + 1, h:h + 1]) * cur
                                    + _dot(bo[rs, :SSM_N], vv[rs, :SSM_P], TN))
                    halves.append(jnp.concatenate(o_rows, axis=0))
                o = o + jnp.concatenate(halves, axis=1)
            y_pairs.append(o + dsk_ref[:, p * LANES:(p + 1) * LANES] * xp)
            yield
        y = jnp.concatenate(y_pairs, axis=1) * _silu(z_ref[...].astype(F32))
        half = MIX // SSM_G
        y = jnp.concatenate([_rms(y[:, gi * half:(gi + 1) * half]) for gi in range(SSM_G)], axis=1)
        y_ref[...] = (y * wn_ref[...]).astype(BF16)
        if chained:
            for p in range(SSM_PAIRS):
                s_scr[p] = states[p]

    if not chained:
        _interleave([chunk(*row_refs, y_out, None, scr[0], 0)])
        return
    s_all, ext_all = scr

    @pl.when(pl.program_id(1) == 0)
    def _():
        for j in range(nsq):
            for p in range(SSM_PAIRS):
                s_all[j, p] = jnp.concatenate([s0_ref[0, 2 * p], s0_ref[0, 2 * p + 1]], axis=1)
            ext_all[j, 0:SUBLANES, :] = c0_ref[0]

    view = lambda r, j: r.at[j] if nsq > 1 else r
    _interleave([chunk(*[view(r, j) for r in row_refs], view(y_out, j), s_all.at[j],
                       ext_all.at[j], j) for j in range(nsq)])

    @pl.when(pl.program_id(1) == pl.num_programs(1) - 1)
    def _():
        dst = _state_slot(so_ref, out_mode)
        for j in range(nsq):
            for p in range(SSM_PAIRS):
                dst[j, 2 * p] = s_all[j, p][:, :SSM_P]
                dst[j, 2 * p + 1] = s_all[j, p][:, SSM_P:]


def _ssd(proj, weights, c0, s0, *, lb, tv, rows, chained, name, nsq=1, **kw):
    kern = functools.partial(_ssd_kernel, rows=rows, lb=lb, tv=tv, chained=chained)
    if chained:
        scratch = [pltpu.VMEM((nsq, SSM_PAIRS, SSM_N, LANES), F32),
                   pltpu.VMEM((nsq, SUBLANES, SSM_CONV_DIM), F32)]
    else:
        scratch = [pltpu.VMEM((SUBLANES + rows, SSM_CONV_DIM), F32)]
    return _rec_call(kern, proj, [("ssm_z", 512), ("ssm_xbc", 768), ("ssm_dt", 128)],
                     weights, [c0, s0], [MIX], rows=rows, chained=chained, scratch=scratch,
                     name=name, nsq=nsq, **kw)


def _ssd_lanes_kernel(*refs, nb, ts, n_alias, out_mode):
    (z_ref, x_ref, dt_ref, cw_ref, cb_ref, dtb_ref, alog_ref, dsk_ref, wn_ref,
     c0_ref, s0_ref) = refs[:11]
    y_ref, co_ref, so_ref = refs[11 + n_alias:14 + n_alias]
    act_scr, z_scr, dt_scr, g_scr, cbs_scr, y_scr = refs[14 + n_alias:]
    h = pl.program_id(0)
    rows = nb * SAMPLE_ROWS
    first_row = SAMPLE_ROWS - ts
    b_off, c_off = MIX, MIX + SSM_G * SSM_N

    @pl.when(h == 0)
    def _():
        seq = lax.broadcasted_iota(jnp.int32, (nb, rows), 0)
        row = lax.broadcasted_iota(jnp.int32, (nb, rows), 1)
        plain = [c0_ref[r] for r in range(SSM_CONV - 1)]
        for t in range(ts):
            pick = (row == seq * SAMPLE_ROWS + first_row + t).astype(BF16)
            plain.append(jnp.dot(pick, x_ref[...], preferred_element_type=F32))
            z_scr[t] = jnp.dot(pick, z_ref[...], preferred_element_type=F32).T
            dt_scr[t] = jnp.dot(pick, dt_ref[...], preferred_element_type=F32).T
        co = _state_slot(co_ref, out_mode)
        for r in range(SSM_CONV - 1):
            co[r] = plain[ts + r]
        lanes = [p.T for p in plain]
        for t in range(ts):
            conv = cb_ref[...]
            for i in range(SSM_CONV):
                conv = conv + cw_ref[:, i:i + 1] * lanes[t + i]
            act_scr[t] = _silu(conv)
        a_neg = -jnp.exp(alog_ref[...])
        gsum = jnp.zeros((LANES, nb), F32)
        for t in range(ts):
            dt = jax.nn.softplus(dt_scr[t] + dtb_ref[...])
            dt_scr[t] = dt
            gsum = gsum + dt * a_neg
            g_scr[t] = gsum
        for gi in range(SSM_G):
            for t in range(ts):
                cm = act_scr[t, c_off + gi * SSM_N:c_off + (gi + 1) * SSM_N, :]
                for s in range(t + 1):
                    bm = act_scr[s, b_off + gi * SSM_N:b_off + (gi + 1) * SSM_N, :]
                    idx = (gi * ts + t) * ts + s
                    cbs_scr[idx:idx + 1, :] = jnp.sum(cm * bm, axis=0, keepdims=True)

    gi = h // (SSM_H // SSM_G)
    x_row = pl.multiple_of(h * SSM_P, SSM_P)
    xs = [act_scr[t, pl.ds(x_row, SSM_P), :] for t in range(ts)]
    dts = [dt_scr[t, pl.ds(h, 1), :] for t in range(ts)]
    gs = [g_scr[t, pl.ds(h, 1), :] for t in range(ts)]
    gtot = gs[-1]
    decay = jnp.exp(gtot)
    into_state = [jnp.exp(gtot - gs[t]) * dts[t] for t in range(ts)]
    so = _state_slot(so_ref, out_mode)

    def state_row(n, acc):
        s_n = s0_ref[n]
        new = decay * s_n
        out = []
        for t in range(ts):
            out.append(acc[t] + act_scr[t, pl.ds(c_off + gi * SSM_N + n, 1), :] * s_n)
            new = new + (act_scr[t, pl.ds(b_off + gi * SSM_N + n, 1), :] * into_state[t]) * xs[t]
        so[n] = new
        return tuple(out)

    acc = lax.fori_loop(0, SSM_N, state_row,
                        tuple(jnp.zeros((SSM_P, nb), F32) for _ in range(ts)))
    for t in range(ts):
        o = jnp.exp(gs[t]) * acc[t]
        for s in range(t + 1):
            cb = cbs_scr[pl.ds((gi * ts + t) * ts + s, 1), :]
            o = o + (cb * jnp.exp(gs[t] - gs[s]) * dts[s]) * xs[s]
        y_scr[t, pl.ds(x_row, SSM_P), :] = o + dsk_ref[pl.ds(x_row, SSM_P), :] * xs[t]

    @pl.when(h == SSM_H - 1)
    def _():
        row = lax.broadcasted_iota(jnp.int32, (rows, nb), 0)
        seq = lax.broadcasted_iota(jnp.int32, (rows, nb), 1)
        half = MIX // SSM_G
        out = jnp.zeros((rows, MIX), F32)
        for t in range(ts):
            y = y_scr[t] * _silu(z_scr[t])
            normed = []
            for g2 in range(SSM_G):
                blk = y[g2 * half:(g2 + 1) * half, :]
                ms = jnp.mean(blk * blk, axis=0, keepdims=True)
                normed.append(blk * lax.rsqrt(ms + 1e-6))
            y = (jnp.concatenate(normed, axis=0) * wn_ref[...]).T.astype(BF16)
            put = (row == seq * SAMPLE_ROWS + first_row + t).astype(BF16)
            out = out + jnp.dot(put, y, preferred_element_type=F32)
        y_ref[...] = out.astype(BF16)


def _ssd_lanes(proj, weights, c0, s0, *, layer, nb, ts, stacked, name):
    rows = nb * SAMPLE_ROWS
    out_mode = "first" if layer == 0 else "later"
    aliased = list(stacked) if out_mode == "later" else []
    seg = lambda name_, w: pl.BlockSpec((rows, w), lambda h: (0, COL[name_] // w))
    const = lambda a: pl.BlockSpec((None,) + a.shape[1:], lambda h: (layer,) + (0,) * (a.ndim - 1))
    if out_mode == "first":
        co_spec = pl.BlockSpec((DEPTH,) + c0.shape[1:], lambda h: (0, 0, 0, 0))
        so_spec = pl.BlockSpec((DEPTH, None) + s0.shape[2:], lambda h: (0, h, 0, 0, 0))
    else:
        co_spec = pl.BlockSpec((None,) + c0.shape[1:], lambda h: (layer, 0, 0, 0))
        so_spec = pl.BlockSpec((None, None) + s0.shape[2:], lambda h: (layer, h, 0, 0, 0))
    n_in = 3 + len(weights) + 2
    return pl.pallas_call(
        functools.partial(_ssd_lanes_kernel, nb=nb, ts=ts, n_alias=len(aliased), out_mode=out_mode),
        grid=(SSM_H,),
        in_specs=([seg("ssm_z", 512), seg("ssm_xbc", 768), seg("ssm_dt", 128)]
                  + [const(a) for a in weights] + [const(c0)]
                  + [pl.BlockSpec((None, None) + s0.shape[2:], lambda h: (layer, h, 0, 0, 0))]
                  + [pl.BlockSpec(memory_space=pl.ANY) for _ in aliased]),
        out_specs=[pl.BlockSpec((rows, MIX), lambda h: (0, 0)), co_spec, so_spec],
        out_shape=[jax.ShapeDtypeStruct((rows, MIX), BF16),
                   jax.ShapeDtypeStruct(c0.shape, F32), jax.ShapeDtypeStruct(s0.shape, F32)],
        scratch_shapes=[pltpu.VMEM((ts, SSM_CONV_DIM, nb), F32), pltpu.VMEM((ts, MIX, nb), F32),
                        pltpu.VMEM((ts, LANES, nb), F32), pltpu.VMEM((ts, LANES, nb), F32),
                        pltpu.VMEM((SSM_G * ts * ts, nb), F32), pltpu.VMEM((ts, MIX, nb), F32)],
        input_output_aliases={n_in + i: 1 + i for i in range(len(aliased))},
        compiler_params=pltpu.CompilerParams(
            dimension_semantics=("arbitrary",), vmem_limit_bytes=VMEM_LIMIT),
        name=name,
    )(proj, proj, proj, *weights, c0, s0, *aliased)


def _ret_kernel(*refs, rows, lb, tv, chained, nsq, n_alias, out_mode):
    row_refs = refs[:4]
    lg_ref, swap_ref, cos_ref, sin_ref, s0_ref = refs[4:9]
    y_out, so_ref = refs[9 + n_alias:11 + n_alias]
    scr = refs[11 + n_alias:]
    nblk = rows // lb
    t_col = _row_in_block(rows, lb)
    s_row = lax.broadcasted_iota(jnp.int32, (1, rows), 1) % lb
    n_col = jnp.maximum(t_col - (lb - tv) + 1, 0).astype(F32)
    n_row = jnp.maximum(s_row - (lb - tv) + 1, 0).astype(F32)

    def decay_matrix(h):
        r_i, c_i = _iotas(rows)
        causal = ((r_i // lb) == (c_i // lb)) & (r_i >= c_i)
        diff = (n_col - n_row) * lg_ref[:, h:h + 1]
        return jnp.where(causal, jnp.exp(jnp.minimum(diff, 0.0)), 0.0)

    width = RET_H * RET_DK

    def state_scales():
        head = lax.broadcasted_iota(jnp.int32, (1, width), 1) // RET_DK
        lg_lane = lg_ref[:, 0:1]
        for h in range(1, RET_H):
            lg_lane = jnp.where(head == h, lg_ref[:, h:h + 1], lg_lane)
        gc = n_col * lg_lane
        return jnp.exp(gc), jnp.exp(float(tv) * lg_lane - gc)

    if chained:
        dec_scr, scale_scr = scr[1:]

        @pl.when(pl.program_id(1) == 0)
        def _():
            for h in range(RET_H):
                dec_scr[h] = decay_matrix(h)
            scale_scr[0], scale_scr[1] = state_scales()

    lane = lax.broadcasted_iota(jnp.int32, (rows, width), 1)
    first_half = (lane % RET_DK) < (RET_DK // 2)
    cos = cos_ref[...]
    sin = jnp.where(first_half, -sin_ref[...], sin_ref[...])

    def rope(x):
        partner = jnp.dot(x, swap_ref[...], preferred_element_type=F32)
        return x.astype(F32) * cos + partner * sin

    def chunk(q_ref, k_ref, v_ref, g_ref, y_ref, s_scr):
        states = [s_scr[h] for h in range(RET_H)] if chained else None
        q = rope(q_ref[...])
        k = rope(k_ref[...]) * (RET_DK ** -0.5)
        v = v_ref[...]
        if tv < lb:
            k = jnp.where(t_col >= lb - tv, k, 0.0)
        q_scale, k_scale = (scale_scr[0], scale_scr[1]) if chained else state_scales()
        q_in = q * q_scale
        k_out = k * k_scale
        yield

        so = _state_slot(so_ref, out_mode) if not chained else None
        outs = []
        for h in range(RET_H):
            ks = slice(h * RET_DK, (h + 1) * RET_DK)
            vs = slice(h * RET_DV, (h + 1) * RET_DV)
            att = _dot(q[:, ks], k[:, ks], NT) * (dec_scr[h] if chained else decay_matrix(h))
            ge = float(tv) * lg_ref[:, h:h + 1]
            qin = q_in[:, ks]
            kout = k_out[:, ks]
            if chained:
                cur = states[h]
                if rows % LANES == 0:
                    oh = _dot(jnp.concatenate([att.astype(BF16), qin.astype(BF16)], axis=1),
                              jnp.concatenate([v[:, vs], cur.astype(BF16)], axis=0))
                else:
                    oh = _dot(att, v[:, vs]) + _dot(qin, cur)
                states[h] = jnp.exp(ge) * cur + _dot(kout, v[:, vs], TN)
            else:
                oh = _dot(att, v[:, vs])
                o_rows = []
                for b in range(nblk):
                    rs = slice(b * lb, (b + 1) * lb)
                    cur = s0_ref[b, h]
                    o_rows.append(_dot(qin[rs], cur))
                    so[b, h] = jnp.exp(ge) * cur + _dot(kout[rs], v[rs, vs], TN)
                oh = oh + jnp.concatenate(o_rows, axis=0)
            outs.append(_rms(oh))
            yield
        y_ref[...] = (_silu(g_ref[...].astype(F32)) * jnp.concatenate(outs, axis=1)).astype(BF16)
        if chained:
            for h in range(RET_H):
                s_scr[h] = states[h]

    _run_chunks(chunk, row_refs, y_out, s0_ref, so_ref, scr[:1], chained, nsq, out_mode)


def _rope_swap():
    width = RET_H * RET_DK
    r = jnp.arange(width, dtype=jnp.int32)[:, None]
    c = jnp.arange(width, dtype=jnp.int32)[None, :]
    same_head = (r // RET_DK) == (c // RET_DK)
    return (same_head & (r % RET_DK == (c % RET_DK + RET_DK // 2) % RET_DK)).astype(BF16)[None]


def _ret(proj, lg, cos, sin, s0, *, lb, tv, rows, chained, name, nsq=1, **kw):
    kern = functools.partial(_ret_kernel, rows=rows, lb=lb, tv=tv, chained=chained)
    scratch = ([pltpu.VMEM((nsq, RET_H, RET_DK, RET_DV), F32), pltpu.VMEM((RET_H, rows, rows), F32),
                pltpu.VMEM((2, rows, RET_H * RET_DK), F32)] if chained else [])
    return _rec_call(kern, proj, [("ret_q", 256), ("ret_k", 256), ("ret_v", 512), ("ret_g", 512)],
                     [(lg, 0), (_rope_swap(), 0)], [s0], [MIX], rows=rows, chained=chained,
                     scratch=scratch, name=name,
                     tables=(cos, sin), nsq=nsq, **kw)


def _dense2_kernel(x_ref, gate_ref, yg_ref, ys_ref, yr_ref, lnw_ref, lnb_ref, wg_ref, ws_ref, wr_ref,
                   wo_ref, l1w_ref, l1b_ref, w1_ref, b1_ref, w2_ref, b2_ref, l2w_ref, l2b_ref, o_ref,
                   *, ff_chunk, parts, input_ln):
    tm = x_ref.shape[0]

    def rows_stage(rs):
        branches = [_dot(y_ref[rs, :], w_ref[...])
                    for y_ref, w_ref in ((yg_ref, wg_ref), (ys_ref, ws_ref), (yr_ref, wr_ref))]
        yield
        gate = lambda i: jax.nn.sigmoid(gate_ref[rs, i * D_MODEL:(i + 1) * D_MODEL].astype(F32))
        merged = gate(0) * branches[0] + gate(1) * branches[1] + gate(2) * branches[2]
        yield
        mix = _dot(merged, wo_ref[...])
        yield
        x = x_ref[rs, :]
        if input_ln:
            x = _layer_norm(x, lnw_ref[...], lnb_ref[...])
        h = _layer_norm(ALPHA * x + mix, l1w_ref[...], l1b_ref[...])
        hb = h.astype(BF16)
        ff = jnp.zeros_like(h) + b2_ref[...]
        yield
        for c0 in range(0, D_FF, ff_chunk):
            hid = jnp.dot(hb, w1_ref[:, c0:c0 + ff_chunk], preferred_element_type=F32)
            hid = jnp.square(jnp.maximum(hid + b1_ref[:, c0:c0 + ff_chunk], 0.0))
            ff = ff + _dot(hid, w2_ref[c0:c0 + ff_chunk, :])
            yield
        o_ref[rs, :] = _layer_norm(ALPHA * h + ff, l2w_ref[...], l2b_ref[...])

    _interleave([rows_stage(pl.ds(i * (tm // parts), tm // parts)) for i in range(parts)],
                stagger=True)


def _dense2(x, proj, yg, ys, yr, lnw, lnb, wl, layer, *, input_ln, tm, name):
    m = x.shape[0]
    row = lambda w: pl.BlockSpec((tm, w), lambda i: (i, 0))
    vec = pl.BlockSpec((1, D_MODEL), lambda i: (0, 0))
    const = lambda r, w: pl.BlockSpec((None, r, w), lambda i: (layer, 0, 0),
                                      pipeline_mode=pl.Buffered(1))
    return pl.pallas_call(
        functools.partial(_dense2_kernel, ff_chunk=1024, parts=(2 if tm % 32 == 0 else 1),
                          input_ln=input_ln),
        grid=(m // tm,),
        in_specs=[row(D_MODEL), row(3 * D_MODEL), row(MIX), row(MIX), row(MIX), vec, vec,
                  const(MIX, D_MODEL), const(MIX, D_MODEL), const(MIX, D_MODEL),
                  const(D_MODEL, D_MODEL), const(1, D_MODEL), const(1, D_MODEL),
                  const(D_MODEL, D_FF), const(1, D_FF), const(D_FF, D_MODEL), const(1, D_MODEL),
                  const(1, D_MODEL), const(1, D_MODEL)],
        out_specs=row(D_MODEL),
        out_shape=jax.ShapeDtypeStruct((m, D_MODEL), F32),
        compiler_params=pltpu.CompilerParams(
            dimension_semantics=("parallel",), vmem_limit_bytes=VMEM_LIMIT),
        name=name,
    )(x, proj, yg, ys, yr, lnw, lnb, wl["w_gla_out"], wl["w_ssm_out"], wl["w_ret_out"], wl["w_o"],
      wl["ln1_w"], wl["ln1_b"], wl["w_ff1"], wl["b_ff1"], wl["w_ff2"], wl["b_ff2"],
      wl["ln2_w"], wl["ln2_b"])


def _pick_tile(n, pref):
    t = min(n, pref)
    while n % t or t % SUBLANES:
        t -= 1
    return t


def _rearrange_w_in(w):
    w = jnp.swapaxes(w, -1, -2)
    offs = [0]
    for s in SPLIT_SIZES:
        offs.append(offs[-1] + s)
    names = ("gla_q", "gla_k", "gla_v", "gla_r", "gla_a", "ssm_z", "ssm_xbc", "ssm_dt",
             "ret_q", "ret_k", "ret_v", "ret_g", "gates")
    seg = {n: w[..., offs[i]:offs[i + 1], :].astype(BF16) for i, n in enumerate(names)}
    pad = lambda a: jnp.pad(a, ((0, 0),) * (a.ndim - 2) + ((0, LANES - a.shape[-2]), (0, 0)))
    order = sorted(COL, key=COL.get)
    parts = [pad(seg[n]) if n in ("gla_a", "ssm_dt") else seg[n] for n in order]
    return jnp.concatenate(parts, axis=-2)


def _rope_tables(pos):
    half = RET_DK // 2
    inv_freq = ROPE_BASE ** (-jnp.arange(half, dtype=F32) / half)
    ang = pos.astype(F32)[:, None] * inv_freq[None, :]
    cos = jnp.tile(jnp.cos(ang), (1, 2 * RET_H))
    sin = jnp.tile(jnp.sin(ang), (1, 2 * RET_H))
    return cos, sin


def kernel(x_prompt, x_sample, state_gla, state_ssm, state_conv, state_ret, meta_tokens,
           ln_in_w, ln_in_b, w_in, w_gla_a2, b_gla_a, w_gla_norm, conv_w, conv_b, dt_bias,
           a_log, d_skip, w_ssm_norm, w_gla_out, w_ssm_out, w_ret_out, w_o, ln1_w, ln1_b,
           w_ff1, b_ff1, w_ff2, b_ff2, ln2_w, ln2_b):
    bp, tp, d = x_prompt.shape
    bs, ts, _ = x_sample.shape
    assert d == D_MODEL and tp % CHUNK == 0 and w_in.shape[0] == DEPTH
    assert SSM_CONV - 1 <= SAMPLE_ROWS - ts
    nchunk = tp // CHUNK
    pad_rows = SAMPLE_ROWS - ts

    x_body = x_prompt.reshape(bp * tp, d)
    n_sample = bs * SAMPLE_ROWS
    rows_s = min(CHUNK, n_sample)
    assert n_sample % rows_s == 0 and n_sample % N_META == 0
    n_small = n_sample + CHUNK
    x_small = jnp.concatenate(
        [jnp.pad(x_sample, ((0, 0), (pad_rows, 0), (0, 0))).reshape(n_sample, d),
         meta_tokens.astype(F32), jnp.zeros((CHUNK - N_META, d), F32)], axis=0)

    cos_b, sin_b = _rope_tables(N_META + jnp.arange(tp, dtype=jnp.int32))
    cos_m, sin_m = _rope_tables(jnp.arange(N_META, dtype=jnp.int32))
    pos_tile = PAST_LEN - pad_rows + jnp.arange(SAMPLE_ROWS, dtype=jnp.int32)
    cos_s, sin_s = _rope_tables(jnp.tile(pos_tile, rows_s // SAMPLE_ROWS))
    lg_ret = jnp.pad(jnp.log1p(-jnp.exp2(-5.0 - jnp.arange(RET_H, dtype=F32))),
                     (0, LANES - RET_H)).reshape(1, 1, LANES)

    zero_gla = (jnp.zeros((1, 1, GLA_H, GLA_DK, GLA_DV), F32), 0)
    zero_ssm = (jnp.zeros((1, 1, SSM_H, SSM_N, SSM_P), F32), 0)
    zero_conv = (jnp.zeros((1, 1, SUBLANES, SSM_CONV_DIM), F32), 0)
    zero_ret = (jnp.zeros((1, 1, RET_H, RET_DK, RET_DV), F32), 0)
    from_meta = lambda a: (a[None], 0)
    ssm_lanes = state_ssm.transpose(0, 2, 3, 4, 1)
    conv_lanes = state_conv.transpose(0, 2, 1, 3)

    rowvec = lambda a: a.reshape(DEPTH, 1, -1)
    lane_pad = lambda a: jnp.pad(a, ((0, 0), (0, LANES - a.shape[1]))).reshape(DEPTH, 1, LANES)
    w_in_all = _rearrange_w_in(w_in)
    wl = dict(w_gla_out=w_gla_out.astype(BF16), w_ssm_out=w_ssm_out.astype(BF16),
              w_ret_out=w_ret_out.astype(BF16), w_o=w_o.astype(BF16),
              ln1_w=rowvec(ln1_w), ln1_b=rowvec(ln1_b),
              w_ff1=w_ff1.astype(BF16), b_ff1=rowvec(b_ff1),
              w_ff2=w_ff2.astype(BF16), b_ff2=rowvec(b_ff2),
              ln2_w=rowvec(ln2_w), ln2_b=rowvec(ln2_b))
    gla_w = [jnp.pad(w_gla_a2, ((0, 0), (0, LANES - GLA_RANK), (0, 0))), rowvec(b_gla_a),
             rowvec(w_gla_norm)]
    ssd_w = [conv_w, rowvec(conv_b), lane_pad(dt_bias), lane_pad(a_log),
             rowvec(jnp.repeat(d_skip, SSM_P, axis=1)), rowvec(w_ssm_norm)]
    colvec = lambda a: a.reshape(DEPTH, -1, 1)
    ssd_cols = [conv_w.transpose(0, 2, 1)] + [colvec(a) for a in ssd_w[1:]]
    ln_w, ln_b = ln_in_w.reshape(1, -1), ln_in_b.reshape(1, -1)

    tm_in = _pick_tile(bp * tp, 2048)
    tm_d2 = _pick_tile(bp * tp, 512)
    tm_d2s = _pick_tile(n_small, 384)
    tn = 1536
    meta = dict(row0=n_sample, nseq=1, nchunk=1, rows=N_META, lb=N_META, tv=N_META, chained=True,
                per_seq_state=False)
    body = dict(row0=0, nseq=bp, nchunk=nchunk, rows=CHUNK, lb=CHUNK, tv=CHUNK, chained=True,
                per_seq_state=False)
    ret_rows = 4 * CHUNK if tp % (4 * CHUNK) == 0 else CHUNK
    ret_body = dict(body, nchunk=tp // ret_rows, rows=ret_rows, lb=ret_rows, tv=ret_rows)
    seqs_per_step = lambda want: max(n for n in (1, 2, 4) if n <= want and bp % n == 0)
    samp = dict(row0=0, nseq=bs, nchunk=1, rows=rows_s, lb=SAMPLE_ROWS, tv=ts, chained=False,
                per_seq_state=True)

    names = ("gla_p", "gla_s", "ssm_p", "ssm_s", "conv_p", "conv_s", "ret_p", "ret_s")
    st = {k: None for k in names}
    stk = lambda *keys: [] if st[keys[0]] is None else [st[k] for k in keys]
    xb, xs = x_body, x_small
    for l in range(DEPTH):
        proj_s = _in_proj(xs, ln_w, ln_b, w_in_all, l, apply_ln=(l == 0), tm=n_small, tn=tn,
                          name=f"inproj_small_{l}")
        yg_m, sg_m = _gla(proj_s, gla_w, zero_gla, layer=l, name=f"gla_meta_{l}", **meta)
        ys_m, cv_m, ss_m = _ssd(proj_s, ssd_w, zero_conv, zero_ssm, layer=l,
                                name=f"ssd_meta_{l}", **meta)
        yr_m, sr_m = _ret(proj_s, lg_ret, cos_m, sin_m, zero_ret, layer=l,
                          name=f"ret_meta_{l}", **meta)
        yg_s, st["gla_s"] = _gla(proj_s, gla_w, (state_gla, l), layer=l, stacked=stk("gla_s"),
                                 name=f"gla_sample_{l}", **samp)
        ys_s, st["conv_s"], st["ssm_s"] = _ssd_lanes(proj_s, ssd_cols, conv_lanes, ssm_lanes, layer=l,
                                                     nb=bs, ts=ts, stacked=stk("conv_s", "ssm_s"),
                                                     name=f"ssd_sample_{l}")
        yr_s, st["ret_s"] = _ret(proj_s, lg_ret, cos_s, sin_s, (state_ret, l), layer=l,
                                 stacked=stk("ret_s"), name=f"ret_sample_{l}", **samp)
        zpad = jnp.zeros((CHUNK - N_META, MIX), BF16)
        yg = jnp.concatenate([yg_s, yg_m, zpad], axis=0)
        ys = jnp.concatenate([ys_s, ys_m, zpad], axis=0)
        yr = jnp.concatenate([yr_s, yr_m, zpad], axis=0)
        xs = _dense2(xs, proj_s, yg, ys, yr, ln_w, ln_b, wl, l, input_ln=(l == 0), tm=tm_d2s,
                     name=f"dense2_small_{l}")

        proj_b = _in_proj(xb, ln_w, ln_b, w_in_all, l, apply_ln=(l == 0), tm=tm_in, tn=tn,
                          name=f"inproj_body_{l}")
        yg_b, st["gla_p"] = _gla(proj_b, gla_w, from_meta(sg_m), layer=l, stacked=stk("gla_p"),
                                 name=f"gla_body_{l}", nsq=seqs_per_step(4), **body)
        ys_b, st["conv_p"], st["ssm_p"] = _ssd(proj_b, ssd_w, from_meta(cv_m), from_meta(ss_m),
                                               layer=l, stacked=stk("conv_p", "ssm_p"),
                                               name=f"ssd_body_{l}", nsq=seqs_per_step(4), **body)
        yr_b, st["ret_p"] = _ret(proj_b, lg_ret, cos_b, sin_b, from_meta(sr_m), layer=l,
                                 stacked=stk("ret_p"), name=f"ret_body_{l}", nsq=seqs_per_step(4),
                                 **ret_body)
        xb = _dense2(xb, proj_b, yg_b, ys_b, yr_b, ln_w, ln_b, wl, l, input_ln=(l == 0), tm=tm_d2,
                     name=f"dense2_body_{l}")

    y_prompt = xb.reshape(bp, tp, d)
    y_sample = xs[:n_sample].reshape(bs, SAMPLE_ROWS, d)[:, pad_rows:]
    tail3 = lambda c: c[:, :, SUBLANES - (SSM_CONV - 1):, :]
    return (y_prompt, y_sample, st["gla_p"], st["gla_s"], st["ssm_p"],
            st["ssm_s"].transpose(0, 4, 1, 2, 3), tail3(st["conv_p"]),
            st["conv_s"].transpose(0, 2, 1, 3), st["ret_p"], st["ret_s"])
```

```python
import functools

import jax
import jax.numpy as jnp
from jax import lax
from jax.experimental import pallas as pl
from jax.experimental.pallas import tpu as pltpu

F32 = jnp.float32
BF16 = jnp.bfloat16

D_MODEL = 1024
DEPTH = 2
N_META = 16
MIX = 512
GLA_H, GLA_DK, GLA_DV, GLA_RANK = 4, 64, 128, 16
GLA_GATE_NORM = 16.0
SSM_H, SSM_P, SSM_N, SSM_G, SSM_CONV = 8, 64, 64, 2, 4
SSM_CONV_DIM = MIX + 2 * SSM_G * SSM_N
SSM_PAIRS = SSM_H // 2
RET_H, RET_DK, RET_DV = 4, 64, 128
ROPE_BASE = 10000.0
D_FF = 4 * D_MODEL
ALPHA = (2 * DEPTH) ** 0.25
PAST_LEN = 16384
SPLIT_SIZES = (256, 256, 512, 512, 16, 512, 768, 8, 256, 256, 512, 512, 3072)

LANES = 128
SUBLANES = 8
VMEM_LIMIT = 56 * 1024 * 1024

COL = dict(gates=0, gla_v=3072, gla_r=3584, ssm_z=4096, ret_v=4608, ret_g=5120,
           gla_q=5632, gla_k=5888, ssm_xbc=6144, ret_q=6912, ret_k=7168, gla_a=7424, ssm_dt=7552)
N_PROJ = 7680
SAMPLE_ROWS = SUBLANES
GLA_BASE = SUBLANES
CHUNK = 128

NN = (((1,), (0,)), ((), ()))
NT = (((1,), (1,)), ((), ()))
TN = (((0,), (0,)), ((), ()))


def _dot(a, b, dims=NN):
    return lax.dot_general(a.astype(BF16), b.astype(BF16), dims, preferred_element_type=F32)


def _dot_sel(sel, x, dims=NN, sel_first=True):
    hi = x.astype(BF16)
    lo = (x - hi.astype(F32)).astype(BF16)
    sb = sel.astype(BF16)
    out = None
    for part in (hi, lo):
        ops = (sb, part) if sel_first else (part, sb)
        term = lax.dot_general(*ops, dims, preferred_element_type=F32)
        out = term if out is None else out + term
    return out


def _layer_norm(x, w, b):
    mu = jnp.mean(x, axis=-1, keepdims=True)
    xc = x - mu
    var = jnp.mean(xc * xc, axis=-1, keepdims=True)
    return xc * lax.rsqrt(var + 1e-5) * w + b


def _rms(x):
    return x * lax.rsqrt(jnp.mean(x * x, axis=-1, keepdims=True) + 1e-6)


def _silu(x):
    return x * jax.nn.sigmoid(x)


def _iotas(rows):
    return (lax.broadcasted_iota(jnp.int32, (rows, rows), 0),
            lax.broadcasted_iota(jnp.int32, (rows, rows), 1))


def _tile_spread(rows, rows_io, lb, tv, transpose=False):
    shape = (rows_io, rows) if transpose else (rows, rows_io)
    r = lax.broadcasted_iota(jnp.int32, shape, 1 if transpose else 0)
    c = lax.broadcasted_iota(jnp.int32, shape, 0 if transpose else 1)
    return ((r // lb == c // tv) & (r % lb - (lb - tv) == c % tv)).astype(BF16)


def _row_in_block(rows, lb):
    return lax.broadcasted_iota(jnp.int32, (rows, 1), 0) % lb


def _inproj_kernel(x_ref, lnw_ref, lnb_ref, w_ref, proj_ref, xb_scr, *, apply_ln, parts):
    tm = x_ref.shape[0]

    def first_tile(rs):
        x = x_ref[rs, :]
        if apply_ln:
            x = _layer_norm(x, lnw_ref[...], lnb_ref[...])
        xb = x.astype(BF16)
        xb_scr[rs, :] = xb
        yield
        proj_ref[rs, :] = lax.dot_general(xb, w_ref[...], NT, preferred_element_type=F32).astype(BF16)

    @pl.when(pl.program_id(1) == 0)
    def _():
        _interleave([first_tile(pl.ds(p * (tm // parts), tm // parts)) for p in range(parts)],
                    stagger=True)

    @pl.when(pl.program_id(1) > 0)
    def _():
        proj_ref[...] = lax.dot_general(xb_scr[...], w_ref[...], NT,
                                        preferred_element_type=F32).astype(BF16)


def _in_proj(x, lnw, lnb, w, layer, *, apply_ln, tm, tn, name):
    m = x.shape[0]
    grid = (m // tm, N_PROJ // tn)
    return pl.pallas_call(
        functools.partial(_inproj_kernel, apply_ln=apply_ln,
                          parts=(4 if apply_ln and tm % 64 == 0 else 1)),
        grid=grid,
        in_specs=[pl.BlockSpec((tm, D_MODEL), lambda i, j: (i, 0)),
                  pl.BlockSpec((1, D_MODEL), lambda i, j: (0, 0)),
                  pl.BlockSpec((1, D_MODEL), lambda i, j: (0, 0)),
                  pl.BlockSpec((None, tn, D_MODEL), lambda i, j: (layer, j, 0))],
        out_specs=pl.BlockSpec((tm, tn), lambda i, j: (i, j)),
        out_shape=jax.ShapeDtypeStruct((m, N_PROJ), BF16),
        scratch_shapes=[pltpu.VMEM((tm, D_MODEL), BF16)],
        compiler_params=pltpu.CompilerParams(
            dimension_semantics=("parallel", "arbitrary"), vmem_limit_bytes=VMEM_LIMIT),
        name=name,
    )(x, lnw, lnb, w)


def _rec_call(kernel_fn, proj, segs, consts, states, out_widths, *, layer, row0, nseq, nchunk,
              rows, chained, per_seq_state, scratch, name, tables=(), stacked=None, nsq=1,
              rows_io=None):
    rows_io = rows_io or rows
    rb = row0 // rows_io
    nb = nsq if chained else rows // SAMPLE_ROWS
    grid = (nseq // nb, nchunk) if chained else (nseq // nb, 1)
    assert nseq % nb == 0 and (nsq == 1 or (chained and row0 == 0))
    out_mode = "plain" if stacked is None else ("first" if layer == 0 else "later")
    aliased = list(stacked) if out_mode == "later" else []

    def row_idx(b, c):
        return b * nchunk + c if chained else b

    def col(seg, w):
        cbi = COL[seg] // w
        if nsq > 1:
            return pl.BlockSpec((nsq, rows, w), lambda b, c: (b, c, cbi))
        return pl.BlockSpec((rows_io, w), lambda b, c: (rb + row_idx(b, c), cbi))

    def row_out(w):
        if nsq > 1:
            return pl.BlockSpec((nsq, rows, w), lambda b, c: (b, c, 0))
        return pl.BlockSpec((rows_io, w), lambda b, c: (row_idx(b, c), 0))

    def const_spec(a, lyr):
        zeros = (0,) * (a.ndim - 1)
        return pl.BlockSpec((None,) + a.shape[1:], lambda b, c: (lyr,) + zeros)

    def state_in_spec(a, lyr):
        zeros = (0,) * (a.ndim - 2)
        if per_seq_state:
            return pl.BlockSpec((None, nb) + a.shape[2:], lambda b, c: (lyr, b) + zeros)
        return pl.BlockSpec((None, 1) + a.shape[2:], lambda b, c: (lyr, 0) + zeros)

    def state_out_spec(a):
        zeros = (0,) * (a.ndim - 2)
        if out_mode == "plain":
            return pl.BlockSpec((nb,) + a.shape[2:], lambda b, c: (b,) + zeros)
        if out_mode == "first":
            return pl.BlockSpec((DEPTH, nb) + a.shape[2:], lambda b, c: (0, b) + zeros)
        return pl.BlockSpec((None, nb) + a.shape[2:], lambda b, c: (layer, b) + zeros)

    def state_out_shape(a):
        lead = (nseq,) if out_mode == "plain" else (DEPTH, nseq)
        return jax.ShapeDtypeStruct(lead + a.shape[2:], F32)

    consts = [c if isinstance(c, tuple) else (c, layer) for c in consts]
    in_specs = ([col(s, w) for s, w in segs] + [const_spec(a, lyr) for a, lyr in consts]
                + [pl.BlockSpec((rows, t.shape[1]), lambda b, c: (c, 0)) for t in tables]
                + [state_in_spec(a, lyr) for a, lyr in states]
                + [pl.BlockSpec(memory_space=pl.ANY) for _ in aliased])
    n_in = len(in_specs) - len(aliased)
    n_rows = nseq * nchunk * rows if chained else nseq // nb * rows_io
    out_specs = [row_out(w) for w in out_widths] + [state_out_spec(a) for a, _ in states]
    row_shape = (lambda w: (nseq, nchunk * rows, w)) if nsq > 1 else (lambda w: (n_rows, w))
    out_shape = ([jax.ShapeDtypeStruct(row_shape(w), BF16) for w in out_widths]
                 + [state_out_shape(a) for a, _ in states])
    if nsq > 1:
        proj = proj.reshape(nseq, nchunk * rows, proj.shape[-1])
    res = pl.pallas_call(
        functools.partial(kernel_fn, n_alias=len(aliased), out_mode=out_mode, nsq=nsq),
        grid=grid, in_specs=in_specs, out_specs=out_specs, out_shape=out_shape,
        scratch_shapes=scratch,
        input_output_aliases={n_in + i: len(out_widths) + i for i in range(len(aliased))},
        compiler_params=pltpu.CompilerParams(
            dimension_semantics=("parallel", "arbitrary"), vmem_limit_bytes=VMEM_LIMIT),
        name=name,
    )(*([proj] * len(segs)), *[a for a, _ in consts], *tables, *[a for a, _ in states], *aliased)
    if nsq > 1:
        res = ([r.reshape(n_rows, r.shape[-1]) for r in res[:len(out_widths)]]
               + list(res[len(out_widths):]))
    return res


def _state_slot(ref, out_mode):
    if out_mode != "first":
        return ref
    ref[1:] = jnp.zeros((DEPTH - 1,) + ref.shape[1:], F32)
    return ref.at[0]


def _interleave(stages, stagger=False):
    stages = list(stages)
    delay = {id(gen): (i if stagger else 0) for i, gen in enumerate(stages)}
    rnd = 0
    while stages:
        for gen in list(stages):
            if delay[id(gen)] > rnd:
                continue
            try:
                next(gen)
            except StopIteration:
                stages.remove(gen)
        rnd += 1


def _run_chunks(chunk, row_refs, y_ref, s0_ref, so_ref, scr, chained, nsq, out_mode):
    if not chained:
        _interleave([chunk(*row_refs, y_ref, None)])
        return
    s_all, = scr

    @pl.when(pl.program_id(1) == 0)
    def _():
        for j in range(nsq):
            s_all[j] = s0_ref[0]

    view = lambda r, j: r.at[j] if nsq > 1 else r
    _interleave([chunk(*[view(r, j) for r in row_refs], view(y_ref, j), s_all.at[j])
                 for j in range(nsq)])

    @pl.when(pl.program_id(1) == pl.num_programs(1) - 1)
    def _():
        dst = _state_slot(so_ref, out_mode)
        for j in range(nsq):
            dst[j] = s_all[j]


def _gla_kernel(*refs, rows, lb, tv, chained, nsq, n_alias, out_mode):
    row_refs = refs[:5]
    w2_ref, ba_ref, wn_ref, eb_ref, sh_ref, s0_ref = refs[5:11]
    y_out, so_ref = refs[11 + n_alias:13 + n_alias]
    scr = refs[13 + n_alias:]
    nblk = rows // lb
    width = GLA_H * GLA_DK

    def chunk(q_ref, k_ref, v_ref, r_ref, a_ref, y_ref, s_scr):
        states = [s_scr[h] for h in range(GLA_H)] if chained else None
        if q_ref.shape[0] != rows:
            to_tiles = _tile_spread(rows, q_ref.shape[0], lb, tv)
            load = lambda ref: jnp.dot(to_tiles, ref[...], preferred_element_type=F32)
            vb = load(v_ref).astype(BF16)
        else:
            load = lambda ref: ref[...].astype(F32)
            vb = v_ref[...]
        q = load(q_ref) * (GLA_DK ** -0.5)
        k = load(k_ref)
        a = _dot(load(a_ref), w2_ref[...]) + ba_ref[...]
        g = jax.nn.log_sigmoid(a) * (1.0 / GLA_GATE_NORM)
        t_in = _row_in_block(rows, lb)
        if tv < lb:
            valid = t_in >= lb - tv
            g = jnp.where(valid, g, 0.0)
            k = jnp.where(valid, k, 0.0)
        yield
        r_i, c_i = _iotas(rows)
        same = (r_i // lb) == (c_i // lb)
        sizes = []
        while 2 * GLA_BASE * 2 ** len(sizes) <= lb:
            sizes.append(2 * GLA_BASE * 2 ** len(sizes))
        sums = [same & (r_i >= c_i)] + ([same] if nblk > 1 else [])
        sums += [same & (c_i <= (r_i // sz) * sz + sz // 2 - 1) for sz in sizes]
        gsums = _dot_sel(jnp.concatenate(sums, axis=0), g)
        gcum = gsums[:rows]
        gtot = gcum[rows - 1:rows, :] if nblk == 1 else gsums[rows:2 * rows]
        g_mids = [gsums[(len(sums) - len(sizes) + i) * rows:(len(sums) - len(sizes) + i + 1) * rows]
                  for i in range(len(sizes))]
        qd = q * jnp.exp(gcum)
        kd = k * jnp.exp(gtot - gcum)
        sel = (lax.broadcasted_iota(jnp.int32, (rows, nblk * LANES), 0) // lb
               == lax.broadcasted_iota(jnp.int32, (rows, nblk * LANES), 1) // LANES)
        ds = jnp.exp(_dot_sel(sel, g, TN, sel_first=False))
        yield

        nbase = rows // GLA_BASE
        q3 = q.reshape(nbase, GLA_BASE, width)
        k3 = k.reshape(nbase, GLA_BASE, width)
        g3 = gcum.reshape(nbase, GLA_BASE, width)
        t3 = lax.broadcasted_iota(jnp.int32, (1, GLA_BASE, 1), 1)
        pieces = []
        for s in range(max(0, GLA_BASE - tv), GLA_BASE):
            dd = jnp.minimum(g3 - g3[:, s:s + 1, :], 0.0)
            w = q3 * k3[:, s:s + 1, :] * jnp.exp(dd)
            pieces.append(jnp.where(t3 >= s, w, 0.0).reshape(rows, width).astype(BF16))
            yield
        compact = jnp.dot(jnp.concatenate(pieces, axis=1), eb_ref[...], preferred_element_type=F32)
        spread = jnp.dot(compact.astype(BF16), sh_ref[...], preferred_element_type=F32)
        base_mask = (r_i // GLA_BASE) == (c_i // GLA_BASE)
        att = [jnp.where(base_mask, spread[:, h * LANES:h * LANES + rows], 0.0) for h in range(GLA_H)]
        yield

        for size, g_mid in zip(sizes, g_mids):
            second = (t_in % size) >= size // 2
            ql = jnp.where(second, q * jnp.exp(jnp.minimum(gcum - g_mid, 0.0)), 0.0)
            kl = jnp.where(second, 0.0, k * jnp.exp(jnp.minimum(g_mid - gcum, 0.0)))
            group = (r_i // size) == (c_i // size)
            for h in range(GLA_H):
                ks = slice(h * GLA_DK, (h + 1) * GLA_DK)
                att[h] = att[h] + jnp.where(group, _dot(ql[:, ks], kl[:, ks], NT), 0.0)
            yield

        so = _state_slot(so_ref, out_mode) if not chained else None
        o_heads = []
        for h in range(GLA_H):
            ks = slice(h * GLA_DK, (h + 1) * GLA_DK)
            vs = slice(h * GLA_DV, (h + 1) * GLA_DV)
            if chained:
                cur = states[h]
                if rows % LANES == 0:
                    oh = _dot(jnp.concatenate([att[h].astype(BF16), qd[:, ks].astype(BF16)], axis=1),
                              jnp.concatenate([vb[:, vs], cur.astype(BF16)], axis=0))
                else:
                    oh = _dot(att[h], vb[:, vs]) + _dot(qd[:, ks], cur)
                states[h] = ds[ks, :] * cur + _dot(kd[:, ks], vb[:, vs], TN)
            else:
                o_rows = []
                for b in range(nblk):
                    rs = slice(b * lb, (b + 1) * lb)
                    cur = s0_ref[b, h]
                    o_rows.append(_dot(qd[rs, ks], cur))
                    so[b, h] = ds[ks, b * LANES:(b + 1) * LANES] * cur + _dot(kd[rs, ks], vb[rs, vs], TN)
                oh = _dot(att[h], vb[:, vs]) + jnp.concatenate(o_rows, axis=0)
            o_heads.append(_rms(oh) * wn_ref[...])
            yield
        y = (_silu(load(r_ref)) * jnp.concatenate(o_heads, axis=1)).astype(BF16)
        if y_ref.shape[0] != rows:
            y = jnp.dot(_tile_spread(rows, y_ref.shape[0], lb, tv, transpose=True), y,
                        preferred_element_type=F32).astype(BF16)
        y_ref[...] = y
        if chained:
            for h in range(GLA_H):
                s_scr[h] = states[h]

    _run_chunks(chunk, row_refs, y_out, s0_ref, so_ref, scr, chained, nsq, out_mode)


def _gla_consts(tv):
    sources = jnp.arange(max(0, GLA_BASE - tv), GLA_BASE, dtype=jnp.int32)
    row = jnp.arange(sources.shape[0] * GLA_H * GLA_DK, dtype=jnp.int32)
    target = sources[row // (GLA_H * GLA_DK)] * GLA_H + (row % (GLA_H * GLA_DK)) // GLA_DK
    eb = (target[:, None] == jnp.arange(LANES, dtype=jnp.int32)[None, :]).astype(BF16)
    r = jnp.arange(LANES, dtype=jnp.int32)[:, None]
    c = jnp.arange(GLA_H * LANES, dtype=jnp.int32)[None, :]
    sh = ((r < GLA_BASE * GLA_H) & (r % GLA_H == c // LANES)
          & (r // GLA_H == (c % LANES) % GLA_BASE)).astype(BF16)
    return (eb[None], 0), (sh[None], 0)


def _gla(proj, weights, s0, *, lb, tv, rows, chained, name, nsq=1, **kw):
    kern = functools.partial(_gla_kernel, rows=rows, lb=lb, tv=tv, chained=chained)
    scratch = [pltpu.VMEM((nsq, GLA_H, GLA_DK, GLA_DV), F32)] if chained else []
    return _rec_call(kern, proj,
                     [("gla_q", 256), ("gla_k", 256), ("gla_v", 512), ("gla_r", 512), ("gla_a", 128)],
                     list(weights) + list(_gla_consts(tv)), [s0], [MIX], rows=rows,
                     chained=chained, scratch=scratch, name=name, nsq=nsq, **kw)


def _ssd_kernel(*refs, rows, lb, tv, chained, nsq, n_alias, out_mode):
    row_refs = refs[:3]
    cw_ref, cb_ref, dtb_ref, alog_ref, dsk_ref, wn_ref, c0_ref, s0_ref = refs[3:11]
    y_out, co_ref, so_ref = refs[11 + n_alias:14 + n_alias]
    scr = refs[14 + n_alias:]
    nblk = rows // lb
    t_in = _row_in_block(rows, lb)
    valid = (t_in >= lb - tv) if tv < lb else None
    low64 = lax.broadcasted_iota(jnp.int32, (SSM_N, LANES), 1) < SSM_P

    def chunk(z_ref, x_ref, dt_ref, y_ref, s_scr, ext_scr, j):
        states = [s_scr[p] for p in range(SSM_PAIRS)] if chained else None
        co = _state_slot(co_ref, out_mode)
        if chained:
            xb = x_ref[...]
            xin = xb.astype(F32)
            tail = ext_scr[...]
            r_s, c_s = _iotas(rows)
            t8 = lax.broadcasted_iota(jnp.int32, (SUBLANES, 1), 0)
            conv = cb_ref[...] + cw_ref[SSM_CONV - 1:SSM_CONV, :] * xin
            head = jnp.zeros((SUBLANES, SSM_CONV_DIM), F32)
            for d in range(1, SSM_CONV):
                tap = cw_ref[SSM_CONV - 1 - d:SSM_CONV - d, :]
                shifted = jnp.dot((c_s == r_s - d).astype(BF16), xb, preferred_element_type=F32)
                conv = conv + tap * shifted
                head = head + tap * jnp.where(t8 < d, pltpu.roll(tail, d, axis=0), 0.0)
            conv = jnp.concatenate([conv[:SUBLANES] + head, conv[SUBLANES:]], axis=0)
            co[j] = xin[rows - SUBLANES:, :]
            ext_scr[...] = xin[rows - SUBLANES:, :]
        else:
            ext_scr[0:SUBLANES, :] = jnp.zeros((SUBLANES, SSM_CONV_DIM), F32)
            xin = jnp.where(valid, x_ref[...].astype(F32), c0_ref[...].reshape(rows, SSM_CONV_DIM))
            ext_scr[SUBLANES:SUBLANES + rows, :] = xin
            conv = cb_ref[...]
            for i in range(SSM_CONV):
                conv = conv + cw_ref[i:i + 1, :] * ext_scr[pl.ds(SUBLANES - (SSM_CONV - 1) + i, rows), :]
            co[...] = xin.reshape(nblk, lb, SSM_CONV_DIM)
        act = _silu(conv)
        yield

        dt = jax.nn.softplus(dt_ref[...].astype(F32) + dtb_ref[...])
        gdt = dt * (-jnp.exp(alog_ref[...]))
        if valid is not None:
            gdt = jnp.where(valid, gdt, 0.0)
        r_i, c_i = _iotas(rows)
        same = (r_i // lb) == (c_i // lb)
        causal = same & (r_i >= c_i)
        gcum = _dot_sel(causal, gdt)
        if rows < LANES:
            gsq = jnp.concatenate([gcum, jnp.zeros((LANES - rows, LANES), F32)], axis=0)
            gcum_t = gsq.T[:, :rows]
        else:
            gcum_t = gcum.T
        gtot = gcum[rows - 1:rows, :] if nblk == 1 else _dot_sel(same, gdt)

        lane = lax.broadcasted_iota(jnp.int32, (rows, LANES), 1)
        low = lane < SSM_P
        bcol = act[:, MIX:MIX + LANES]
        ccol = act[:, MIX + LANES:MIX + 2 * LANES]
        bswap = pltpu.roll(bcol, SSM_N, axis=1)
        cswap = pltpu.roll(ccol, SSM_N, axis=1)
        b2 = (jnp.where(low, bcol, bswap), jnp.where(low, bswap, bcol))
        c2 = (jnp.where(low, ccol, cswap), jnp.where(low, cswap, ccol))
        cb = (_dot(jnp.where(low, ccol, 0.0), bcol, NT), _dot(jnp.where(low, 0.0, ccol), bcol, NT))

        def pair_lanes(x, p):
            return jnp.where(low[:x.shape[0]], x[:, 2 * p:2 * p + 1], x[:, 2 * p + 1:2 * p + 2])

        yield
        so = _state_slot(so_ref, out_mode) if not chained else None
        y_pairs = []
        for p in range(SSM_PAIRS):
            gi = p // (SSM_PAIRS // SSM_G)
            decs = []
            for h in (2 * p, 2 * p + 1):
                diff = jnp.minimum(gcum[:, h:h + 1] - gcum_t[h:h + 1, :], 0.0)
                decs.append(cb[gi] * jnp.where(causal, jnp.exp(diff), 0.0))
            xp = act[:, p * LANES:(p + 1) * LANES]
            vp = xp * pair_lanes(dt, p)
            if valid is not None:
                vp = jnp.where(valid, vp, 0.0)
            vbd = jnp.concatenate([jnp.where(low, vp, 0.0), jnp.where(low, 0.0, vp)], axis=0)
            g2 = pair_lanes(gcum, p)
            ge2 = pair_lanes(gtot, p)
            cin = c2[gi] * jnp.exp(g2)
            bout = b2[gi] * jnp.exp(ge2 - g2)
            if chained:
                cur = states[p]
                bd = jnp.concatenate([jnp.where(low64, cur, 0.0), jnp.where(low64, 0.0, cur)], axis=0)
                if rows % LANES == 0:
                    o = _dot(jnp.concatenate(decs + [cin], axis=1).astype(BF16),
                             jnp.concatenate([vbd.astype(BF16), bd.astype(BF16)], axis=0))
                else:
                    o = _dot(jnp.concatenate(decs, axis=1), vbd) + _dot(cin, bd)
                u = _dot(bout, vp, TN)
                states[p] = jnp.exp(ge2) * cur + jnp.where(low64, u[:SSM_N, :], u[SSM_N:, :])
            else:
                o = _dot(jnp.concatenate(decs, axis=1), vbd)
                halves = []
                for odd, (ci, bo, vv) in enumerate(
                        ((cin, bout, vp),
                         tuple(pltpu.roll(t, SSM_P, axis=1) for t in (cin, bout, vp)))):
                    h = 2 * p + odd
                    o_rows = []
                    for b in range(nblk):
                        rs = slice(b * lb, (b + 1) * lb)
                        cur = s0_ref[b, h]
                        o_rows.append(_dot(ci[rs, :SSM_N], cur))
                        so[b, h] = (jnp.exp(gtot[b * lb:b * lb + 1, h:h + 1]) * cur
                                    + _dot(bo[rs, :SSM_N], vv[rs, :SSM_P], TN))
                    halves.append(jnp.concatenate(o_rows, axis=0))
                o = o + jnp.concatenate(halves, axis=1)
            y_pairs.append(o + dsk_ref[:, p * LANES:(p + 1) * LANES] * xp)
            yield
        y = jnp.concatenate(y_pairs, axis=1) * _silu(z_ref[...].astype(F32))
        half = MIX // SSM_G
        y = jnp.concatenate([_rms(y[:, gi * half:(gi + 1) * half]) for gi in range(SSM_G)], axis=1)
        y_ref[...] = (y * wn_ref[...]).astype(BF16)
        if chained:
            for p in range(SSM_PAIRS):
                s_scr[p] = states[p]

    if not chained:
        _interleave([chunk(*row_refs, y_out, None, scr[0], 0)])
        return
    s_all, ext_all = scr

    @pl.when(pl.program_id(1) == 0)
    def _():
        for j in range(nsq):
            for p in range(SSM_PAIRS):
                s_all[j, p] = jnp.concatenate([s0_ref[0, 2 * p], s0_ref[0, 2 * p + 1]], axis=1)
            ext_all[j, 0:SUBLANES, :] = c0_ref[0]

    view = lambda r, j: r.at[j] if nsq > 1 else r
    _interleave([chunk(*[view(r, j) for r in row_refs], view(y_out, j), s_all.at[j],
                       ext_all.at[j], j) for j in range(nsq)])

    @pl.when(pl.program_id(1) == pl.num_programs(1) - 1)
    def _():
        dst = _state_slot(so_ref, out_mode)
        for j in range(nsq):
            for p in range(SSM_PAIRS):
                dst[j, 2 * p] = s_all[j, p][:, :SSM_P]
                dst[j, 2 * p + 1] = s_all[j, p][:, SSM_P:]


def _ssd(proj, weights, c0, s0, *, lb, tv, rows, chained, name, nsq=1, **kw):
    kern = functools.partial(_ssd_kernel, rows=rows, lb=lb, tv=tv, chained=chained)
    if chained:
        scratch = [pltpu.VMEM((nsq, SSM_PAIRS, SSM_N, LANES), F32),
                   pltpu.VMEM((nsq, SUBLANES, SSM_CONV_DIM), F32)]
    else:
        scratch = [pltpu.VMEM((SUBLANES + rows, SSM_CONV_DIM), F32)]
    return _rec_call(kern, proj, [("ssm_z", 512), ("ssm_xbc", 768), ("ssm_dt", 128)],
                     weights, [c0, s0], [MIX], rows=rows, chained=chained, scratch=scratch,
                     name=name, nsq=nsq, **kw)


def _ssd_lanes_kernel(*refs, nb, ts, n_alias, out_mode):
    (z_ref, x_ref, dt_ref, cw_ref, cb_ref, dtb_ref, alog_ref, dsk_ref, wn_ref,
     c0_ref, s0_ref) = refs[:11]
    y_ref, co_ref, so_ref = refs[11 + n_alias:14 + n_alias]
    act_scr, z_scr, dt_scr, g_scr, cbs_scr, y_scr = refs[14 + n_alias:]
    h = pl.program_id(0)
    rows = nb * ts
    b_off, c_off = MIX, MIX + SSM_G * SSM_N

    @pl.when(h == 0)
    def _():
        seq = lax.broadcasted_iota(jnp.int32, (nb, rows), 0)
        row = lax.broadcasted_iota(jnp.int32, (nb, rows), 1)
        plain = [c0_ref[r] for r in range(SSM_CONV - 1)]
        for t in range(ts):
            pick = (row == seq * ts + t).astype(BF16)
            plain.append(jnp.dot(pick, x_ref[...], preferred_element_type=F32))
            z_scr[t] = jnp.dot(pick, z_ref[...], preferred_element_type=F32).T
            dt_scr[t] = jnp.dot(pick, dt_ref[...], preferred_element_type=F32).T
        co = _state_slot(co_ref, out_mode)
        for r in range(SSM_CONV - 1):
            co[r] = plain[ts + r]
        lanes = [p.T for p in plain]
        for t in range(ts):
            conv = cb_ref[...]
            for i in range(SSM_CONV):
                conv = conv + cw_ref[:, i:i + 1] * lanes[t + i]
            act_scr[t] = _silu(conv)
        a_neg = -jnp.exp(alog_ref[...])
        gsum = jnp.zeros((LANES, nb), F32)
        for t in range(ts):
            dt = jax.nn.softplus(dt_scr[t] + dtb_ref[...])
            dt_scr[t] = dt
            gsum = gsum + dt * a_neg
            g_scr[t] = gsum
        for gi in range(SSM_G):
            for t in range(ts):
                cm = act_scr[t, c_off + gi * SSM_N:c_off + (gi + 1) * SSM_N, :]
                for s in range(t + 1):
                    bm = act_scr[s, b_off + gi * SSM_N:b_off + (gi + 1) * SSM_N, :]
                    idx = (gi * ts + t) * ts + s
                    cbs_scr[idx:idx + 1, :] = jnp.sum(cm * bm, axis=0, keepdims=True)

    gi = h // (SSM_H // SSM_G)
    x_row = pl.multiple_of(h * SSM_P, SSM_P)
    xs = [act_scr[t, pl.ds(x_row, SSM_P), :] for t in range(ts)]
    dts = [dt_scr[t, pl.ds(h, 1), :] for t in range(ts)]
    gs = [g_scr[t, pl.ds(h, 1), :] for t in range(ts)]
    gtot = gs[-1]
    decay = jnp.exp(gtot)
    into_state = [jnp.exp(gtot - gs[t]) * dts[t] for t in range(ts)]
    so = _state_slot(so_ref, out_mode)

    def state_row(n, acc):
        s_n = s0_ref[n]
        new = decay * s_n
        out = []
        for t in range(ts):
            out.append(acc[t] + act_scr[t, pl.ds(c_off + gi * SSM_N + n, 1), :] * s_n)
            new = new + (act_scr[t, pl.ds(b_off + gi * SSM_N + n, 1), :] * into_state[t]) * xs[t]
        so[n] = new
        return tuple(out)

    acc = lax.fori_loop(0, SSM_N, state_row,
                        tuple(jnp.zeros((SSM_P, nb), F32) for _ in range(ts)))
    for t in range(ts):
        o = jnp.exp(gs[t]) * acc[t]
        for s in range(t + 1):
            cb = cbs_scr[pl.ds((gi * ts + t) * ts + s, 1), :]
            o = o + (cb * jnp.exp(gs[t] - gs[s]) * dts[s]) * xs[s]
        y_scr[t, pl.ds(x_row, SSM_P), :] = o + dsk_ref[pl.ds(x_row, SSM_P), :] * xs[t]

    @pl.when(h == SSM_H - 1)
    def _():
        row = lax.broadcasted_iota(jnp.int32, (rows, nb), 0)
        seq = lax.broadcasted_iota(jnp.int32, (rows, nb), 1)
        half = MIX // SSM_G
        out = jnp.zeros((rows, MIX), F32)
        for t in range(ts):
            y = y_scr[t] * _silu(z_scr[t])
            normed = []
            for g2 in range(SSM_G):
                blk = y[g2 * half:(g2 + 1) * half, :]
                ms = jnp.mean(blk * blk, axis=0, keepdims=True)
                normed.append(blk * lax.rsqrt(ms + 1e-6))
            y = (jnp.concatenate(normed, axis=0) * wn_ref[...]).T.astype(BF16)
            put = (row == seq * ts + t).astype(BF16)
            out = out + jnp.dot(put, y, preferred_element_type=F32)
        y_ref[...] = out.astype(BF16)


def _ssd_lanes(proj, weights, c0, s0, *, layer, nb, ts, stacked, name):
    rows = nb * ts
    out_mode = "first" if layer == 0 else "later"
    aliased = list(stacked) if out_mode == "later" else []
    seg = lambda name_, w: pl.BlockSpec((rows, w), lambda h: (0, COL[name_] // w))
    const = lambda a: pl.BlockSpec((None,) + a.shape[1:], lambda h: (layer,) + (0,) * (a.ndim - 1))
    if out_mode == "first":
        co_spec = pl.BlockSpec((DEPTH,) + c0.shape[1:], lambda h: (0, 0, 0, 0))
        so_spec = pl.BlockSpec((DEPTH, None) + s0.shape[2:], lambda h: (0, h, 0, 0, 0))
    else:
        co_spec = pl.BlockSpec((None,) + c0.shape[1:], lambda h: (layer, 0, 0, 0))
        so_spec = pl.BlockSpec((None, None) + s0.shape[2:], lambda h: (layer, h, 0, 0, 0))
    n_in = 3 + len(weights) + 2
    return pl.pallas_call(
        functools.partial(_ssd_lanes_kernel, nb=nb, ts=ts, n_alias=len(aliased), out_mode=out_mode),
        grid=(SSM_H,),
        in_specs=([seg("ssm_z", 512), seg("ssm_xbc", 768), seg("ssm_dt", 128)]
                  + [const(a) for a in weights] + [const(c0)]
                  + [pl.BlockSpec((None, None) + s0.shape[2:], lambda h: (layer, h, 0, 0, 0))]
                  + [pl.BlockSpec(memory_space=pl.ANY) for _ in aliased]),
        out_specs=[pl.BlockSpec((rows, MIX), lambda h: (0, 0)), co_spec, so_spec],
        out_shape=[jax.ShapeDtypeStruct((rows, MIX), BF16),
                   jax.ShapeDtypeStruct(c0.shape, F32), jax.ShapeDtypeStruct(s0.shape, F32)],
        scratch_shapes=[pltpu.VMEM((ts, SSM_CONV_DIM, nb), F32), pltpu.VMEM((ts, MIX, nb), F32),
                        pltpu.VMEM((ts, LANES, nb), F32), pltpu.VMEM((ts, LANES, nb), F32),
                        pltpu.VMEM((SSM_G * ts * ts, nb), F32), pltpu.VMEM((ts, MIX, nb), F32)],
        input_output_aliases={n_in + i: 1 + i for i in range(len(aliased))},
        compiler_params=pltpu.CompilerParams(
            dimension_semantics=("arbitrary",), vmem_limit_bytes=VMEM_LIMIT),
        name=name,
    )(proj, proj, proj, *weights, c0, s0, *aliased)


def _ret_kernel(*refs, rows, lb, tv, chained, nsq, n_alias, out_mode):
    row_refs = refs[:4]
    lg_ref, swap_ref, cos_ref, sin_ref, s0_ref = refs[4:9]
    y_out, so_ref = refs[9 + n_alias:11 + n_alias]
    scr = refs[11 + n_alias:]
    nblk = rows // lb
    t_col = _row_in_block(rows, lb)
    s_row = lax.broadcasted_iota(jnp.int32, (1, rows), 1) % lb
    n_col = jnp.maximum(t_col - (lb - tv) + 1, 0).astype(F32)
    n_row = jnp.maximum(s_row - (lb - tv) + 1, 0).astype(F32)

    def decay_matrix(h):
        r_i, c_i = _iotas(rows)
        causal = ((r_i // lb) == (c_i // lb)) & (r_i >= c_i)
        diff = (n_col - n_row) * lg_ref[:, h:h + 1]
        return jnp.where(causal, jnp.exp(jnp.minimum(diff, 0.0)), 0.0)

    width = RET_H * RET_DK

    def state_scales():
        head = lax.broadcasted_iota(jnp.int32, (1, width), 1) // RET_DK
        lg_lane = lg_ref[:, 0:1]
        for h in range(1, RET_H):
            lg_lane = jnp.where(head == h, lg_ref[:, h:h + 1], lg_lane)
        gc = n_col * lg_lane
        return jnp.exp(gc), jnp.exp(float(tv) * lg_lane - gc)

    if chained:
        dec_scr, scale_scr = scr[1:]

        @pl.when(pl.program_id(1) == 0)
        def _():
            for h in range(RET_H):
                dec_scr[h] = decay_matrix(h)
            scale_scr[0], scale_scr[1] = state_scales()

    lane = lax.broadcasted_iota(jnp.int32, (rows, width), 1)
    first_half = (lane % RET_DK) < (RET_DK // 2)
    cos = cos_ref[...]
    sin = jnp.where(first_half, -sin_ref[...], sin_ref[...])

    def rope(x):
        partner = jnp.dot(x, swap_ref[...], preferred_element_type=F32)
        return x.astype(F32) * cos + partner * sin

    def chunk(q_ref, k_ref, v_ref, g_ref, y_ref, s_scr):
        states = [s_scr[h] for h in range(RET_H)] if chained else None
        if q_ref.shape[0] != rows:
            to_tiles = _tile_spread(rows, q_ref.shape[0], lb, tv)
            load = lambda ref: jnp.dot(to_tiles, ref[...], preferred_element_type=F32).astype(BF16)
        else:
            load = lambda ref: ref[...]
        q = rope(load(q_ref))
        k = rope(load(k_ref)) * (RET_DK ** -0.5)
        v = load(v_ref)
        if tv < lb:
            k = jnp.where(t_col >= lb - tv, k, 0.0)
        q_scale, k_scale = (scale_scr[0], scale_scr[1]) if chained else state_scales()
        q_in = q * q_scale
        k_out = k * k_scale
        yield

        so = _state_slot(so_ref, out_mode) if not chained else None
        outs = []
        for h in range(RET_H):
            ks = slice(h * RET_DK, (h + 1) * RET_DK)
            vs = slice(h * RET_DV, (h + 1) * RET_DV)
            att = _dot(q[:, ks], k[:, ks], NT) * (dec_scr[h] if chained else decay_matrix(h))
            ge = float(tv) * lg_ref[:, h:h + 1]
            qin = q_in[:, ks]
            kout = k_out[:, ks]
            if chained:
                cur = states[h]
                if rows % LANES == 0:
                    oh = _dot(jnp.concatenate([att.astype(BF16), qin.astype(BF16)], axis=1),
                              jnp.concatenate([v[:, vs], cur.astype(BF16)], axis=0))
                else:
                    oh = _dot(att, v[:, vs]) + _dot(qin, cur)
                states[h] = jnp.exp(ge) * cur + _dot(kout, v[:, vs], TN)
            else:
                oh = _dot(att, v[:, vs])
                o_rows = []
                for b in range(nblk):
                    rs = slice(b * lb, (b + 1) * lb)
                    cur = s0_ref[b, h]
                    o_rows.append(_dot(qin[rs], cur))
                    so[b, h] = jnp.exp(ge) * cur + _dot(kout[rs], v[rs, vs], TN)
                oh = oh + jnp.concatenate(o_rows, axis=0)
            outs.append(_rms(oh))
            yield
        y = (_silu(load(g_ref).astype(F32)) * jnp.concatenate(outs, axis=1)).astype(BF16)
        if y_ref.shape[0] != rows:
            y = jnp.dot(_tile_spread(rows, y_ref.shape[0], lb, tv, transpose=True), y,
                        preferred_element_type=F32).astype(BF16)
        y_ref[...] = y
        if chained:
            for h in range(RET_H):
                s_scr[h] = states[h]

    _run_chunks(chunk, row_refs, y_out, s0_ref, so_ref, scr[:1], chained, nsq, out_mode)


def _rope_swap():
    width = RET_H * RET_DK
    r = jnp.arange(width, dtype=jnp.int32)[:, None]
    c = jnp.arange(width, dtype=jnp.int32)[None, :]
    same_head = (r // RET_DK) == (c // RET_DK)
    return (same_head & (r % RET_DK == (c % RET_DK + RET_DK // 2) % RET_DK)).astype(BF16)[None]


def _ret(proj, lg, cos, sin, s0, *, lb, tv, rows, chained, name, nsq=1, **kw):
    kern = functools.partial(_ret_kernel, rows=rows, lb=lb, tv=tv, chained=chained)
    scratch = ([pltpu.VMEM((nsq, RET_H, RET_DK, RET_DV), F32), pltpu.VMEM((RET_H, rows, rows), F32),
                pltpu.VMEM((2, rows, RET_H * RET_DK), F32)] if chained else [])
    return _rec_call(kern, proj, [("ret_q", 256), ("ret_k", 256), ("ret_v", 512), ("ret_g", 512)],
                     [(lg, 0), (_rope_swap(), 0)], [s0], [MIX], rows=rows, chained=chained,
                     scratch=scratch, name=name,
                     tables=(cos, sin), nsq=nsq, **kw)


def _dense2_kernel(x_ref, gate_ref, yg_ref, ys_ref, yr_ref, lnw_ref, lnb_ref, wg_ref, ws_ref, wr_ref,
                   wo_ref, l1w_ref, l1b_ref, w1_ref, b1_ref, w2_ref, b2_ref, l2w_ref, l2b_ref, o_ref,
                   *, ff_chunk, parts, input_ln):
    tm = x_ref.shape[0]

    def rows_stage(rs):
        branches = [_dot(y_ref[rs, :], w_ref[...])
                    for y_ref, w_ref in ((yg_ref, wg_ref), (ys_ref, ws_ref), (yr_ref, wr_ref))]
        yield
        gate = lambda i: jax.nn.sigmoid(gate_ref[rs, i * D_MODEL:(i + 1) * D_MODEL].astype(F32))
        merged = gate(0) * branches[0] + gate(1) * branches[1] + gate(2) * branches[2]
        yield
        mix = _dot(merged, wo_ref[...])
        yield
        x = x_ref[rs, :]
        if input_ln:
            x = _layer_norm(x, lnw_ref[...], lnb_ref[...])
        h = _layer_norm(ALPHA * x + mix, l1w_ref[...], l1b_ref[...])
        hb = h.astype(BF16)
        ff = jnp.zeros_like(h) + b2_ref[...]
        yield
        for c0 in range(0, D_FF, ff_chunk):
            hid = jnp.dot(hb, w1_ref[:, c0:c0 + ff_chunk], preferred_element_type=F32)
            hid = jnp.square(jnp.maximum(hid + b1_ref[:, c0:c0 + ff_chunk], 0.0))
            ff = ff + _dot(hid, w2_ref[c0:c0 + ff_chunk, :])
            yield
        o_ref[rs, :] = _layer_norm(ALPHA * h + ff, l2w_ref[...], l2b_ref[...])

    _interleave([rows_stage(pl.ds(i * (tm // parts), tm // parts)) for i in range(parts)],
                stagger=True)


def _dense2(x, proj, yg, ys, yr, lnw, lnb, wl, layer, *, input_ln, tm, name):
    m = x.shape[0]
    row = lambda w: pl.BlockSpec((tm, w), lambda i: (i, 0))
    vec = pl.BlockSpec((1, D_MODEL), lambda i: (0, 0))
    const = lambda r, w: pl.BlockSpec((None, r, w), lambda i: (layer, 0, 0),
                                      pipeline_mode=pl.Buffered(1))
    return pl.pallas_call(
        functools.partial(_dense2_kernel, ff_chunk=1024, parts=(2 if tm % 32 == 0 else 1),
                          input_ln=input_ln),
        grid=(m // tm,),
        in_specs=[row(D_MODEL), row(3 * D_MODEL), row(MIX), row(MIX), row(MIX), vec, vec,
                  const(MIX, D_MODEL), const(MIX, D_MODEL), const(MIX, D_MODEL),
                  const(D_MODEL, D_MODEL), const(1, D_MODEL), const(1, D_MODEL),
                  const(D_MODEL, D_FF), const(1, D_FF), const(D_FF, D_MODEL), const(1, D_MODEL),
                  const(1, D_MODEL), const(1, D_MODEL)],
        out_specs=row(D_MODEL),
        out_shape=jax.ShapeDtypeStruct((m, D_MODEL), F32),
        compiler_params=pltpu.CompilerParams(
            dimension_semantics=("parallel",), vmem_limit_bytes=VMEM_LIMIT),
        name=name,
    )(x, proj, yg, ys, yr, lnw, lnb, wl["w_gla_out"], wl["w_ssm_out"], wl["w_ret_out"], wl["w_o"],
      wl["ln1_w"], wl["ln1_b"], wl["w_ff1"], wl["b_ff1"], wl["w_ff2"], wl["b_ff2"],
      wl["ln2_w"], wl["ln2_b"])


def _pick_tile(n, pref):
    t = min(n, pref)
    while n % t or t % SUBLANES:
        t -= 1
    return t


def _rearrange_w_in(w):
    w = jnp.swapaxes(w, -1, -2)
    offs = [0]
    for s in SPLIT_SIZES:
        offs.append(offs[-1] + s)
    names = ("gla_q", "gla_k", "gla_v", "gla_r", "gla_a", "ssm_z", "ssm_xbc", "ssm_dt",
             "ret_q", "ret_k", "ret_v", "ret_g", "gates")
    seg = {n: w[..., offs[i]:offs[i + 1], :].astype(BF16) for i, n in enumerate(names)}
    pad = lambda a: jnp.pad(a, ((0, 0),) * (a.ndim - 2) + ((0, LANES - a.shape[-2]), (0, 0)))
    order = sorted(COL, key=COL.get)
    parts = [pad(seg[n]) if n in ("gla_a", "ssm_dt") else seg[n] for n in order]
    return jnp.concatenate(parts, axis=-2)


def _rope_tables(pos):
    half = RET_DK // 2
    inv_freq = ROPE_BASE ** (-jnp.arange(half, dtype=F32) / half)
    ang = pos.astype(F32)[:, None] * inv_freq[None, :]
    cos = jnp.tile(jnp.cos(ang), (1, 2 * RET_H))
    sin = jnp.tile(jnp.sin(ang), (1, 2 * RET_H))
    return cos, sin


def kernel(x_prompt, x_sample, state_gla, state_ssm, state_conv, state_ret, meta_tokens,
           ln_in_w, ln_in_b, w_in, w_gla_a2, b_gla_a, w_gla_norm, conv_w, conv_b, dt_bias,
           a_log, d_skip, w_ssm_norm, w_gla_out, w_ssm_out, w_ret_out, w_o, ln1_w, ln1_b,
           w_ff1, b_ff1, w_ff2, b_ff2, ln2_w, ln2_b):
    bp, tp, d = x_prompt.shape
    bs, ts, _ = x_sample.shape
    assert d == D_MODEL and tp % CHUNK == 0 and w_in.shape[0] == DEPTH
    assert SAMPLE_ROWS % ts == 0
    nchunk = tp // CHUNK
    pad_rows = SAMPLE_ROWS - ts

    x_body = x_prompt.reshape(bp * tp, d)
    n_sample = bs * ts
    rows_s = min(CHUNK, bs * SAMPLE_ROWS)
    assert (bs * SAMPLE_ROWS) % rows_s == 0 and n_sample % N_META == 0
    n_small = n_sample + CHUNK
    x_small = jnp.concatenate(
        [x_sample.reshape(n_sample, d), meta_tokens.astype(F32),
         jnp.zeros((CHUNK - N_META, d), F32)], axis=0)

    cos_b, sin_b = _rope_tables(N_META + jnp.arange(tp, dtype=jnp.int32))
    cos_m, sin_m = _rope_tables(jnp.arange(N_META, dtype=jnp.int32))
    pos_tile = PAST_LEN - pad_rows + jnp.arange(SAMPLE_ROWS, dtype=jnp.int32)
    cos_s, sin_s = _rope_tables(jnp.tile(pos_tile, rows_s // SAMPLE_ROWS))
    lg_ret = jnp.pad(jnp.log1p(-jnp.exp2(-5.0 - jnp.arange(RET_H, dtype=F32))),
                     (0, LANES - RET_H)).reshape(1, 1, LANES)

    zero_gla = (jnp.zeros((1, 1, GLA_H, GLA_DK, GLA_DV), F32), 0)
    zero_ssm = (jnp.zeros((1, 1, SSM_H, SSM_N, SSM_P), F32), 0)
    zero_conv = (jnp.zeros((1, 1, SUBLANES, SSM_CONV_DIM), F32), 0)
    zero_ret = (jnp.zeros((1, 1, RET_H, RET_DK, RET_DV), F32), 0)
    from_meta = lambda a: (a[None], 0)
    ssm_lanes = state_ssm.transpose(0, 2, 3, 4, 1)
    conv_lanes = state_conv.transpose(0, 2, 1, 3)

    rowvec = lambda a: a.reshape(DEPTH, 1, -1)
    lane_pad = lambda a: jnp.pad(a, ((0, 0), (0, LANES - a.shape[1]))).reshape(DEPTH, 1, LANES)
    w_in_all = _rearrange_w_in(w_in)
    wl = dict(w_gla_out=w_gla_out.astype(BF16), w_ssm_out=w_ssm_out.astype(BF16),
              w_ret_out=w_ret_out.astype(BF16), w_o=w_o.astype(BF16),
              ln1_w=rowvec(ln1_w), ln1_b=rowvec(ln1_b),
              w_ff1=w_ff1.astype(BF16), b_ff1=rowvec(b_ff1),
              w_ff2=w_ff2.astype(BF16), b_ff2=rowvec(b_ff2),
              ln2_w=rowvec(ln2_w), ln2_b=rowvec(ln2_b))
    gla_w = [jnp.pad(w_gla_a2, ((0, 0), (0, LANES - GLA_RANK), (0, 0))), rowvec(b_gla_a),
             rowvec(w_gla_norm)]
    ssd_w = [conv_w, rowvec(conv_b), lane_pad(dt_bias), lane_pad(a_log),
             rowvec(jnp.repeat(d_skip, SSM_P, axis=1)), rowvec(w_ssm_norm)]
    colvec = lambda a: a.reshape(DEPTH, -1, 1)
    ssd_cols = [conv_w.transpose(0, 2, 1)] + [colvec(a) for a in ssd_w[1:]]
    ln_w, ln_b = ln_in_w.reshape(1, -1), ln_in_b.reshape(1, -1)

    tm_in = _pick_tile(bp * tp, 2048)
    tm_d2 = _pick_tile(bp * tp, 512)
    tm_d2s = _pick_tile(n_small, 384)
    tn = 1536
    meta = dict(row0=n_sample, nseq=1, nchunk=1, rows=N_META, lb=N_META, tv=N_META, chained=True,
                per_seq_state=False)
    body = dict(row0=0, nseq=bp, nchunk=nchunk, rows=CHUNK, lb=CHUNK, tv=CHUNK, chained=True,
                per_seq_state=False)
    ret_rows = 4 * CHUNK if tp % (4 * CHUNK) == 0 else CHUNK
    ret_body = dict(body, nchunk=tp // ret_rows, rows=ret_rows, lb=ret_rows, tv=ret_rows)
    seqs_per_step = lambda want: max(n for n in (1, 2, 4) if n <= want and bp % n == 0)
    samp = dict(row0=0, nseq=bs, nchunk=1, rows=rows_s, lb=SAMPLE_ROWS, tv=ts, chained=False,
                per_seq_state=True, rows_io=rows_s // SAMPLE_ROWS * ts)

    names = ("gla_p", "gla_s", "ssm_p", "ssm_s", "conv_p", "conv_s", "ret_p", "ret_s")
    st = {k: None for k in names}
    stk = lambda *keys: [] if st[keys[0]] is None else [st[k] for k in keys]
    xb, xs = x_body, x_small
    for l in range(DEPTH):
        proj_s = _in_proj(xs, ln_w, ln_b, w_in_all, l, apply_ln=(l == 0), tm=n_small, tn=tn,
                          name=f"inproj_small_{l}")
        yg_m, sg_m = _gla(proj_s, gla_w, zero_gla, layer=l, name=f"gla_meta_{l}", **meta)
        ys_m, cv_m, ss_m = _ssd(proj_s, ssd_w, zero_conv, zero_ssm, layer=l,
                                name=f"ssd_meta_{l}", **meta)
        yr_m, sr_m = _ret(proj_s, lg_ret, cos_m, sin_m, zero_ret, layer=l,
                          name=f"ret_meta_{l}", **meta)
        yg_s, st["gla_s"] = _gla(proj_s, gla_w, (state_gla, l), layer=l, stacked=stk("gla_s"),
                                 name=f"gla_sample_{l}", **samp)
        ys_s, st["conv_s"], st["ssm_s"] = _ssd_lanes(proj_s, ssd_cols, conv_lanes, ssm_lanes, layer=l,
                                                     nb=bs, ts=ts, stacked=stk("conv_s", "ssm_s"),
                                                     name=f"ssd_sample_{l}")
        yr_s, st["ret_s"] = _ret(proj_s, lg_ret, cos_s, sin_s, (state_ret, l), layer=l,
                                 stacked=stk("ret_s"), name=f"ret_sample_{l}", **samp)
        zpad = jnp.zeros((CHUNK - N_META, MIX), BF16)
        yg = jnp.concatenate([yg_s, yg_m, zpad], axis=0)
        ys = jnp.concatenate([ys_s, ys_m, zpad], axis=0)
        yr = jnp.concatenate([yr_s, yr_m, zpad], axis=0)
        xs = _dense2(xs, proj_s, yg, ys, yr, ln_w, ln_b, wl, l, input_ln=(l == 0), tm=tm_d2s,
                     name=f"dense2_small_{l}")

        proj_b = _in_proj(xb, ln_w, ln_b, w_in_all, l, apply_ln=(l == 0), tm=tm_in, tn=tn,
                          name=f"inproj_body_{l}")
        yg_b, st["gla_p"] = _gla(proj_b, gla_w, from_meta(sg_m), layer=l, stacked=stk("gla_p"),
                                 name=f"gla_body_{l}", nsq=seqs_per_step(4), **body)
        ys_b, st["conv_p"], st["ssm_p"] = _ssd(proj_b, ssd_w, from_meta(cv_m), from_meta(ss_m),
                                               layer=l, stacked=stk("conv_p", "ssm_p"),
                                               name=f"ssd_body_{l}", nsq=seqs_per_step(4), **body)
        yr_b, st["ret_p"] = _ret(proj_b, lg_ret, cos_b, sin_b, from_meta(sr_m), layer=l,
                                 stacked=stk("ret_p"), name=f"ret_body_{l}", nsq=seqs_per_step(4),
                                 **ret_body)
        xb = _dense2(xb, proj_b, yg_b, ys_b, yr_b, ln_w, ln_b, wl, l, input_ln=(l == 0), tm=tm_d2,
                     name=f"dense2_body_{l}")

    y_prompt = xb.reshape(bp, tp, d)
    y_sample = xs[:n_sample].reshape(bs, ts, d)
    tail3 = lambda c: c[:, :, SUBLANES - (SSM_CONV - 1):, :]
    return (y_prompt, y_sample, st["gla_p"], st["gla_s"], st["ssm_p"],
            st["ssm_s"].transpose(0, 4, 1, 2, 3), tail3(st["conv_p"]),
            st["conv_s"].transpose(0, 2, 1, 3), st["ret_p"], st["ret_s"])
```

```python
import functools

import jax
import jax.numpy as jnp
from jax import lax
from jax.experimental import pallas as pl
from jax.experimental.pallas import tpu as pltpu

F32 = jnp.float32
BF16 = jnp.bfloat16

D_MODEL = 1024
DEPTH = 2
N_META = 16
MIX = 512
GLA_H, GLA_DK, GLA_DV, GLA_RANK = 4, 64, 128, 16
GLA_GATE_NORM = 16.0
SSM_H, SSM_P, SSM_N, SSM_G, SSM_CONV = 8, 64, 64, 2, 4
SSM_CONV_DIM = MIX + 2 * SSM_G * SSM_N
SSM_PAIRS = SSM_H // 2
RET_H, RET_DK, RET_DV = 4, 64, 128
ROPE_BASE = 10000.0
D_FF = 4 * D_MODEL
ALPHA = (2 * DEPTH) ** 0.25
PAST_LEN = 16384
SPLIT_SIZES = (256, 256, 512, 512, 16, 512, 768, 8, 256, 256, 512, 512, 3072)

LANES = 128
SUBLANES = 8
VMEM_LIMIT = 56 * 1024 * 1024

COL = dict(gates=0, gla_v=3072, gla_r=3584, ssm_z=4096, ret_v=4608, ret_g=5120,
           gla_q=5632, gla_k=5888, ssm_xbc=6144, ret_q=6912, ret_k=7168, gla_a=7424, ssm_dt=7552)
N_PROJ = 7680
SAMPLE_ROWS = SUBLANES
GLA_BASE = SUBLANES
CHUNK = 128

NN = (((1,), (0,)), ((), ()))
NT = (((1,), (1,)), ((), ()))
TN = (((0,), (0,)), ((), ()))


def _dot(a, b, dims=NN):
    return lax.dot_general(a.astype(BF16), b.astype(BF16), dims, preferred_element_type=F32)


def _dot_sel(sel, x, dims=NN, sel_first=True):
    hi = x.astype(BF16)
    lo = (x - hi.astype(F32)).astype(BF16)
    sb = sel.astype(BF16)
    out = None
    for part in (hi, lo):
        ops = (sb, part) if sel_first else (part, sb)
        term = lax.dot_general(*ops, dims, preferred_element_type=F32)
        out = term if out is None else out + term
    return out


def _layer_norm(x, w, b):
    mu = jnp.mean(x, axis=-1, keepdims=True)
    xc = x - mu
    var = jnp.mean(xc * xc, axis=-1, keepdims=True)
    return xc * lax.rsqrt(var + 1e-5) * w + b


def _rms(x):
    return x * lax.rsqrt(jnp.mean(x * x, axis=-1, keepdims=True) + 1e-6)


def _silu(x):
    return x * jax.nn.sigmoid(x)


def _iotas(rows):
    return (lax.broadcasted_iota(jnp.int32, (rows, rows), 0),
            lax.broadcasted_iota(jnp.int32, (rows, rows), 1))


def _tile_spread(rows, rows_io, lb, tv, transpose=False):
    shape = (rows_io, rows) if transpose else (rows, rows_io)
    r = lax.broadcasted_iota(jnp.int32, shape, 1 if transpose else 0)
    c = lax.broadcasted_iota(jnp.int32, shape, 0 if transpose else 1)
    return ((r // lb == c // tv) & (r % lb - (lb - tv) == c % tv)).astype(BF16)


def _row_in_block(rows, lb):
    return lax.broadcasted_iota(jnp.int32, (rows, 1), 0) % lb


def _inproj_kernel(x_ref, lnw_ref, lnb_ref, w_ref, proj_ref, xb_scr, *, apply_ln, parts):
    tm = x_ref.shape[0]

    def first_tile(rs):
        x = x_ref[rs, :]
        if apply_ln:
            x = _layer_norm(x, lnw_ref[...], lnb_ref[...])
        xb = x.astype(BF16)
        xb_scr[rs, :] = xb
        yield
        proj_ref[rs, :] = lax.dot_general(xb, w_ref[...], NT, preferred_element_type=F32).astype(BF16)

    @pl.when(pl.program_id(1) == 0)
    def _():
        _interleave([first_tile(pl.ds(p * (tm // parts), tm // parts)) for p in range(parts)],
                    stagger=True)

    @pl.when(pl.program_id(1) > 0)
    def _():
        proj_ref[...] = lax.dot_general(xb_scr[...], w_ref[...], NT,
                                        preferred_element_type=F32).astype(BF16)


def _in_proj(x, lnw, lnb, w, layer, *, apply_ln, tm, tn, name):
    m = x.shape[0]
    grid = (m // tm, N_PROJ // tn)
    return pl.pallas_call(
        functools.partial(_inproj_kernel, apply_ln=apply_ln,
                          parts=(4 if apply_ln and tm % 64 == 0 else 1)),
        grid=grid,
        in_specs=[pl.BlockSpec((tm, D_MODEL), lambda i, j: (i, 0)),
                  pl.BlockSpec((1, D_MODEL), lambda i, j: (0, 0)),
                  pl.BlockSpec((1, D_MODEL), lambda i, j: (0, 0)),
                  pl.BlockSpec((None, tn, D_MODEL), lambda i, j: (layer, j, 0))],
        out_specs=pl.BlockSpec((tm, tn), lambda i, j: (i, j)),
        out_shape=jax.ShapeDtypeStruct((m, N_PROJ), BF16),
        scratch_shapes=[pltpu.VMEM((tm, D_MODEL), BF16)],
        compiler_params=pltpu.CompilerParams(
            dimension_semantics=("parallel", "arbitrary"), vmem_limit_bytes=VMEM_LIMIT),
        name=name,
    )(x, lnw, lnb, w)


def _rec_call(kernel_fn, proj, segs, consts, states, out_widths, *, layer, row0, nseq, nchunk,
              rows, chained, per_seq_state, scratch, name, tables=(), stacked=None, nsq=1,
              rows_io=None):
    rows_io = rows_io or rows
    rb = row0 // rows_io
    nb = nsq if chained else rows // SAMPLE_ROWS
    grid = (nseq // nb, nchunk) if chained else (nseq // nb, 1)
    assert nseq % nb == 0 and (nsq == 1 or (chained and row0 == 0))
    out_mode = "plain" if stacked is None else ("first" if layer == 0 else "later")
    aliased = list(stacked) if out_mode == "later" else []

    def row_idx(b, c):
        return b * nchunk + c if chained else b

    def col(seg, w):
        cbi = COL[seg] // w
        if nsq > 1:
            return pl.BlockSpec((nsq, rows, w), lambda b, c: (b, c, cbi))
        return pl.BlockSpec((rows_io, w), lambda b, c: (rb + row_idx(b, c), cbi))

    def row_out(w):
        if nsq > 1:
            return pl.BlockSpec((nsq, rows, w), lambda b, c: (b, c, 0))
        return pl.BlockSpec((rows_io, w), lambda b, c: (row_idx(b, c), 0))

    def const_spec(a, lyr):
        zeros = (0,) * (a.ndim - 1)
        return pl.BlockSpec((None,) + a.shape[1:], lambda b, c: (lyr,) + zeros)

    def state_in_spec(a, lyr):
        zeros = (0,) * (a.ndim - 2)
        if per_seq_state:
            return pl.BlockSpec((None, nb) + a.shape[2:], lambda b, c: (lyr, b) + zeros)
        return pl.BlockSpec((None, 1) + a.shape[2:], lambda b, c: (lyr, 0) + zeros)

    def state_out_spec(a):
        zeros = (0,) * (a.ndim - 2)
        if out_mode == "plain":
            return pl.BlockSpec((nb,) + a.shape[2:], lambda b, c: (b,) + zeros)
        if out_mode == "first":
            return pl.BlockSpec((DEPTH, nb) + a.shape[2:], lambda b, c: (0, b) + zeros)
        return pl.BlockSpec((None, nb) + a.shape[2:], lambda b, c: (layer, b) + zeros)

    def state_out_shape(a):
        lead = (nseq,) if out_mode == "plain" else (DEPTH, nseq)
        return jax.ShapeDtypeStruct(lead + a.shape[2:], F32)

    consts = [c if isinstance(c, tuple) else (c, layer) for c in consts]
    in_specs = ([col(s, w) for s, w in segs] + [const_spec(a, lyr) for a, lyr in consts]
                + [pl.BlockSpec((rows, t.shape[1]), lambda b, c: (c, 0)) for t in tables]
                + [state_in_spec(a, lyr) for a, lyr in states]
                + [pl.BlockSpec(memory_space=pl.ANY) for _ in aliased])
    n_in = len(in_specs) - len(aliased)
    n_rows = nseq * nchunk * rows if chained else nseq // nb * rows_io
    out_specs = [row_out(w) for w in out_widths] + [state_out_spec(a) for a, _ in states]
    row_shape = (lambda w: (nseq, nchunk * rows, w)) if nsq > 1 else (lambda w: (n_rows, w))
    out_shape = ([jax.ShapeDtypeStruct(row_shape(w), BF16) for w in out_widths]
                 + [state_out_shape(a) for a, _ in states])
    if nsq > 1:
        proj = proj.reshape(nseq, nchunk * rows, proj.shape[-1])
    res = pl.pallas_call(
        functools.partial(kernel_fn, n_alias=len(aliased), out_mode=out_mode, nsq=nsq),
        grid=grid, in_specs=in_specs, out_specs=out_specs, out_shape=out_shape,
        scratch_shapes=scratch,
        input_output_aliases={n_in + i: len(out_widths) + i for i in range(len(aliased))},
        compiler_params=pltpu.CompilerParams(
            dimension_semantics=("parallel", "arbitrary"), vmem_limit_bytes=VMEM_LIMIT),
        name=name,
    )(*([proj] * len(segs)), *[a for a, _ in consts], *tables, *[a for a, _ in states], *aliased)
    if nsq > 1:
        res = ([r.reshape(n_rows, r.shape[-1]) for r in res[:len(out_widths)]]
               + list(res[len(out_widths):]))
    return res


def _state_slot(ref, out_mode):
    if out_mode != "first":
        return ref
    ref[1:] = jnp.zeros((DEPTH - 1,) + ref.shape[1:], F32)
    return ref.at[0]


def _interleave(stages, stagger=False):
    stages = list(stages)
    delay = {id(gen): (i if stagger else 0) for i, gen in enumerate(stages)}
    rnd = 0
    while stages:
        for gen in list(stages):
            if delay[id(gen)] > rnd:
                continue
            try:
                next(gen)
            except StopIteration:
                stages.remove(gen)
        rnd += 1


def _run_chunks(chunk, row_refs, y_ref, s0_ref, so_ref, scr, chained, nsq, out_mode):
    if not chained:
        _interleave([chunk(*row_refs, y_ref, None)])
        return
    s_all, = scr

    @pl.when(pl.program_id(1) == 0)
    def _():
        for j in range(nsq):
            s_all[j] = s0_ref[0]

    view = lambda r, j: r.at[j] if nsq > 1 else r
    _interleave([chunk(*[view(r, j) for r in row_refs], view(y_ref, j), s_all.at[j])
                 for j in range(nsq)])

    @pl.when(pl.program_id(1) == pl.num_programs(1) - 1)
    def _():
        dst = _state_slot(so_ref, out_mode)
        for j in range(nsq):
            dst[j] = s_all[j]


def _gla_kernel(*refs, rows, lb, tv, chained, nsq, n_alias, out_mode):
    row_refs = refs[:5]
    w2_ref, ba_ref, wn_ref, eb_ref, sh_ref, s0_ref = refs[5:11]
    y_out, so_ref = refs[11 + n_alias:13 + n_alias]
    scr = refs[13 + n_alias:]
    nblk = rows // lb
    width = GLA_H * GLA_DK

    def chunk(q_ref, k_ref, v_ref, r_ref, a_ref, y_ref, s_scr):
        states = [s_scr[h] for h in range(GLA_H)] if chained else None
        if q_ref.shape[0] != rows:
            to_tiles = _tile_spread(rows, q_ref.shape[0], lb, tv)
            load = lambda ref: jnp.dot(to_tiles, ref[...], preferred_element_type=F32)
            vb = load(v_ref).astype(BF16)
        else:
            load = lambda ref: ref[...].astype(F32)
            vb = v_ref[...]
        q = load(q_ref) * (GLA_DK ** -0.5)
        k = load(k_ref)
        a = _dot(load(a_ref), w2_ref[...]) + ba_ref[...]
        g = jax.nn.log_sigmoid(a) * (1.0 / GLA_GATE_NORM)
        t_in = _row_in_block(rows, lb)
        if tv < lb:
            valid = t_in >= lb - tv
            g = jnp.where(valid, g, 0.0)
            k = jnp.where(valid, k, 0.0)
        yield
        r_i, c_i = _iotas(rows)
        same = (r_i // lb) == (c_i // lb)
        sizes = []
        while 2 * GLA_BASE * 2 ** len(sizes) <= lb:
            sizes.append(2 * GLA_BASE * 2 ** len(sizes))
        sums = [same & (r_i >= c_i)] + ([same] if nblk > 1 else [])
        sums += [same & (c_i <= (r_i // sz) * sz + sz // 2 - 1) for sz in sizes]
        gsums = _dot_sel(jnp.concatenate(sums, axis=0), g)
        gcum = gsums[:rows]
        gtot = gcum[rows - 1:rows, :] if nblk == 1 else gsums[rows:2 * rows]
        g_mids = [gsums[(len(sums) - len(sizes) + i) * rows:(len(sums) - len(sizes) + i + 1) * rows]
                  for i in range(len(sizes))]
        qd = q * jnp.exp(gcum)
        kd = k * jnp.exp(gtot - gcum)
        sel = (lax.broadcasted_iota(jnp.int32, (rows, nblk * LANES), 0) // lb
               == lax.broadcasted_iota(jnp.int32, (rows, nblk * LANES), 1) // LANES)
        ds = jnp.exp(_dot_sel(sel, g, TN, sel_first=False))
        yield

        nbase = rows // GLA_BASE
        q3 = q.reshape(nbase, GLA_BASE, width)
        k3 = k.reshape(nbase, GLA_BASE, width)
        g3 = gcum.reshape(nbase, GLA_BASE, width)
        t3 = lax.broadcasted_iota(jnp.int32, (1, GLA_BASE, 1), 1)
        pieces = []
        for s in range(max(0, GLA_BASE - tv), GLA_BASE):
            dd = jnp.minimum(g3 - g3[:, s:s + 1, :], 0.0)
            w = q3 * k3[:, s:s + 1, :] * jnp.exp(dd)
            pieces.append(jnp.where(t3 >= s, w, 0.0).reshape(rows, width).astype(BF16))
            yield
        compact = jnp.dot(jnp.concatenate(pieces, axis=1), eb_ref[...], preferred_element_type=F32)
        spread = jnp.dot(compact.astype(BF16), sh_ref[...], preferred_element_type=F32)
        base_mask = (r_i // GLA_BASE) == (c_i // GLA_BASE)
        att = [jnp.where(base_mask, spread[:, h * LANES:h * LANES + rows], 0.0) for h in range(GLA_H)]
        yield

        for size, g_mid in zip(sizes, g_mids):
            second = (t_in % size) >= size // 2
            ql = jnp.where(second, q * jnp.exp(jnp.minimum(gcum - g_mid, 0.0)), 0.0)
            kl = jnp.where(second, 0.0, k * jnp.exp(jnp.minimum(g_mid - gcum, 0.0)))
            group = (r_i // size) == (c_i // size)
            for h in range(GLA_H):
                ks = slice(h * GLA_DK, (h + 1) * GLA_DK)
                att[h] = att[h] + jnp.where(group, _dot(ql[:, ks], kl[:, ks], NT), 0.0)
            yield

        so = _state_slot(so_ref, out_mode) if not chained else None
        o_heads = []
        for h in range(GLA_H):
            ks = slice(h * GLA_DK, (h + 1) * GLA_DK)
            vs = slice(h * GLA_DV, (h + 1) * GLA_DV)
            if chained:
                cur = states[h]
                if rows % LANES == 0:
                    oh = _dot(jnp.concatenate([att[h].astype(BF16), qd[:, ks].astype(BF16)], axis=1),
                              jnp.concatenate([vb[:, vs], cur.astype(BF16)], axis=0))
                else:
                    oh = _dot(att[h], vb[:, vs]) + _dot(qd[:, ks], cur)
                states[h] = ds[ks, :] * cur + _dot(kd[:, ks], vb[:, vs], TN)
            else:
                o_rows = []
                for b in range(nblk):
                    rs = slice(b * lb, (b + 1) * lb)
                    cur = s0_ref[b, h]
                    o_rows.append(_dot(qd[rs, ks], cur))
                    so[b, h] = ds[ks, b * LANES:(b + 1) * LANES] * cur + _dot(kd[rs, ks], vb[rs, vs], TN)
                oh = _dot(att[h], vb[:, vs]) + jnp.concatenate(o_rows, axis=0)
            o_heads.append(_rms(oh) * wn_ref[...])
            yield
        y = (_silu(load(r_ref)) * jnp.concatenate(o_heads, axis=1)).astype(BF16)
        if y_ref.shape[0] != rows:
            y = jnp.dot(_tile_spread(rows, y_ref.shape[0], lb, tv, transpose=True), y,
                        preferred_element_type=F32).astype(BF16)
        y_ref[...] = y
        if chained:
            for h in range(GLA_H):
                s_scr[h] = states[h]

    _run_chunks(chunk, row_refs, y_out, s0_ref, so_ref, scr, chained, nsq, out_mode)


def _gla_consts(tv):
    sources = jnp.arange(max(0, GLA_BASE - tv), GLA_BASE, dtype=jnp.int32)
    row = jnp.arange(sources.shape[0] * GLA_H * GLA_DK, dtype=jnp.int32)
    target = sources[row // (GLA_H * GLA_DK)] * GLA_H + (row % (GLA_H * GLA_DK)) // GLA_DK
    eb = (target[:, None] == jnp.arange(LANES, dtype=jnp.int32)[None, :]).astype(BF16)
    r = jnp.arange(LANES, dtype=jnp.int32)[:, None]
    c = jnp.arange(GLA_H * LANES, dtype=jnp.int32)[None, :]
    sh = ((r < GLA_BASE * GLA_H) & (r % GLA_H == c // LANES)
          & (r // GLA_H == (c % LANES) % GLA_BASE)).astype(BF16)
    return (eb[None], 0), (sh[None], 0)


def _gla(proj, weights, s0, *, lb, tv, rows, chained, name, nsq=1, **kw):
    kern = functools.partial(_gla_kernel, rows=rows, lb=lb, tv=tv, chained=chained)
    scratch = [pltpu.VMEM((nsq, GLA_H, GLA_DK, GLA_DV), F32)] if chained else []
    return _rec_call(kern, proj,
                     [("gla_q", 256), ("gla_k", 256), ("gla_v", 512), ("gla_r", 512), ("gla_a", 128)],
                     list(weights) + list(_gla_consts(tv)), [s0], [MIX], rows=rows,
                     chained=chained, scratch=scratch, name=name, nsq=nsq, **kw)


def _ssd_kernel(*refs, rows, nsq, n_alias, out_mode):
    row_refs = refs[:3]
    cw_ref, cb_ref, dtb_ref, alog_ref, dsk_ref, wn_ref, c0_ref, s0_ref = refs[3:11]
    y_out, co_ref, so_ref = refs[11 + n_alias:14 + n_alias]
    s_all, ext_all = refs[14 + n_alias:]
    low64 = lax.broadcasted_iota(jnp.int32, (SSM_N, LANES), 1) < SSM_P

    def chunk(z_ref, x_ref, dt_ref, y_ref, s_scr, ext_scr, j):
        states = [s_scr[p] for p in range(SSM_PAIRS)]
        xb = x_ref[...]
        xin = xb.astype(F32)
        tail = ext_scr[...]
        r_i, c_i = _iotas(rows)
        t8 = lax.broadcasted_iota(jnp.int32, (SUBLANES, 1), 0)
        conv = cb_ref[...] + cw_ref[SSM_CONV - 1:SSM_CONV, :] * xin
        head = jnp.zeros((SUBLANES, SSM_CONV_DIM), F32)
        for d in range(1, SSM_CONV):
            tap = cw_ref[SSM_CONV - 1 - d:SSM_CONV - d, :]
            shifted = jnp.dot((c_i == r_i - d).astype(BF16), xb, preferred_element_type=F32)
            conv = conv + tap * shifted
            head = head + tap * jnp.where(t8 < d, pltpu.roll(tail, d, axis=0), 0.0)
        conv = jnp.concatenate([conv[:SUBLANES] + head, conv[SUBLANES:]], axis=0)
        _state_slot(co_ref, out_mode)[j] = xin[rows - SUBLANES:, :]
        ext_scr[...] = xin[rows - SUBLANES:, :]
        act = _silu(conv)
        yield

        dt = jax.nn.softplus(dt_ref[...].astype(F32) + dtb_ref[...])
        gdt = dt * (-jnp.exp(alog_ref[...]))
        causal = r_i >= c_i
        gcum = _dot_sel(causal, gdt)
        if rows < LANES:
            gsq = jnp.concatenate([gcum, jnp.zeros((LANES - rows, LANES), F32)], axis=0)
            gcum_t = gsq.T[:, :rows]
        else:
            gcum_t = gcum.T
        gtot = gcum[rows - 1:rows, :]

        lane = lax.broadcasted_iota(jnp.int32, (rows, LANES), 1)
        low = lane < SSM_P
        bcol = act[:, MIX:MIX + LANES]
        ccol = act[:, MIX + LANES:MIX + 2 * LANES]
        bswap = pltpu.roll(bcol, SSM_N, axis=1)
        cswap = pltpu.roll(ccol, SSM_N, axis=1)
        b2 = (jnp.where(low, bcol, bswap), jnp.where(low, bswap, bcol))
        c2 = (jnp.where(low, ccol, cswap), jnp.where(low, cswap, ccol))
        cb = (_dot(jnp.where(low, ccol, 0.0), bcol, NT), _dot(jnp.where(low, 0.0, ccol), bcol, NT))

        def pair_lanes(x, p):
            return jnp.where(low[:x.shape[0]], x[:, 2 * p:2 * p + 1], x[:, 2 * p + 1:2 * p + 2])

        yield
        y_pairs = []
        for p in range(SSM_PAIRS):
            gi = p // (SSM_PAIRS // SSM_G)
            decs = []
            for h in (2 * p, 2 * p + 1):
                diff = jnp.minimum(gcum[:, h:h + 1] - gcum_t[h:h + 1, :], 0.0)
                decs.append(cb[gi] * jnp.where(causal, jnp.exp(diff), 0.0))
            xp = act[:, p * LANES:(p + 1) * LANES]
            vp = xp * pair_lanes(dt, p)
            vbd = jnp.concatenate([jnp.where(low, vp, 0.0), jnp.where(low, 0.0, vp)], axis=0)
            g2 = pair_lanes(gcum, p)
            ge2 = pair_lanes(gtot, p)
            cin = c2[gi] * jnp.exp(g2)
            bout = b2[gi] * jnp.exp(ge2 - g2)
            cur = states[p]
            bd = jnp.concatenate([jnp.where(low64, cur, 0.0), jnp.where(low64, 0.0, cur)], axis=0)
            if rows % LANES == 0:
                o = _dot(jnp.concatenate(decs + [cin], axis=1).astype(BF16),
                         jnp.concatenate([vbd.astype(BF16), bd.astype(BF16)], axis=0))
            else:
                o = _dot(jnp.concatenate(decs, axis=1), vbd) + _dot(cin, bd)
            u = _dot(bout, vp, TN)
            states[p] = jnp.exp(ge2) * cur + jnp.where(low64, u[:SSM_N, :], u[SSM_N:, :])
            y_pairs.append(o + dsk_ref[:, p * LANES:(p + 1) * LANES] * xp)
            yield
        y = jnp.concatenate(y_pairs, axis=1) * _silu(z_ref[...].astype(F32))
        half = MIX // SSM_G
        y = jnp.concatenate([_rms(y[:, gi * half:(gi + 1) * half]) for gi in range(SSM_G)], axis=1)
        y_ref[...] = (y * wn_ref[...]).astype(BF16)
        for p in range(SSM_PAIRS):
            s_scr[p] = states[p]

    @pl.when(pl.program_id(1) == 0)
    def _():
        for j in range(nsq):
            for p in range(SSM_PAIRS):
                s_all[j, p] = jnp.concatenate([s0_ref[0, 2 * p], s0_ref[0, 2 * p + 1]], axis=1)
            ext_all[j, 0:SUBLANES, :] = c0_ref[0]

    view = lambda r, j: r.at[j] if nsq > 1 else r
    _interleave([chunk(*[view(r, j) for r in row_refs], view(y_out, j), s_all.at[j],
                       ext_all.at[j], j) for j in range(nsq)])

    @pl.when(pl.program_id(1) == pl.num_programs(1) - 1)
    def _():
        dst = _state_slot(so_ref, out_mode)
        for j in range(nsq):
            for p in range(SSM_PAIRS):
                dst[j, 2 * p] = s_all[j, p][:, :SSM_P]
                dst[j, 2 * p + 1] = s_all[j, p][:, SSM_P:]


def _ssd(proj, weights, c0, s0, *, lb, tv, rows, chained, name, nsq=1, **kw):
    assert chained and lb == tv == rows
    kern = functools.partial(_ssd_kernel, rows=rows)
    scratch = [pltpu.VMEM((nsq, SSM_PAIRS, SSM_N, LANES), F32),
               pltpu.VMEM((nsq, SUBLANES, SSM_CONV_DIM), F32)]
    return _rec_call(kern, proj, [("ssm_z", 512), ("ssm_xbc", 768), ("ssm_dt", 128)],
                     weights, [c0, s0], [MIX], rows=rows, chained=chained, scratch=scratch,
                     name=name, nsq=nsq, **kw)


def _ssd_lanes_kernel(*refs, nb, ts, n_alias, out_mode):
    (z_ref, x_ref, dt_ref, cw_ref, cb_ref, dtb_ref, alog_ref, dsk_ref, wn_ref,
     c0_ref, s0_ref) = refs[:11]
    y_ref, co_ref, so_ref = refs[11 + n_alias:14 + n_alias]
    act_scr, z_scr, dt_scr, g_scr, cbs_scr, y_scr = refs[14 + n_alias:]
    h = pl.program_id(0)
    rows = nb * ts
    b_off, c_off = MIX, MIX + SSM_G * SSM_N

    @pl.when(h == 0)
    def _():
        seq = lax.broadcasted_iota(jnp.int32, (nb, rows), 0)
        row = lax.broadcasted_iota(jnp.int32, (nb, rows), 1)
        plain = [c0_ref[r] for r in range(SSM_CONV - 1)]
        for t in range(ts):
            pick = (row == seq * ts + t).astype(BF16)
            plain.append(jnp.dot(pick, x_ref[...], preferred_element_type=F32))
            z_scr[t] = jnp.dot(pick, z_ref[...], preferred_element_type=F32).T
            dt_scr[t] = jnp.dot(pick, dt_ref[...], preferred_element_type=F32).T
        co = _state_slot(co_ref, out_mode)
        for r in range(SSM_CONV - 1):
            co[r] = plain[ts + r]
        lanes = [p.T for p in plain]
        for t in range(ts):
            conv = cb_ref[...]
            for i in range(SSM_CONV):
                conv = conv + cw_ref[:, i:i + 1] * lanes[t + i]
            act_scr[t] = _silu(conv)
        a_neg = -jnp.exp(alog_ref[...])
        gsum = jnp.zeros((LANES, nb), F32)
        for t in range(ts):
            dt = jax.nn.softplus(dt_scr[t] + dtb_ref[...])
            dt_scr[t] = dt
            gsum = gsum + dt * a_neg
            g_scr[t] = gsum
        for gi in range(SSM_G):
            for t in range(ts):
                cm = act_scr[t, c_off + gi * SSM_N:c_off + (gi + 1) * SSM_N, :]
                for s in range(t + 1):
                    bm = act_scr[s, b_off + gi * SSM_N:b_off + (gi + 1) * SSM_N, :]
                    idx = (gi * ts + t) * ts + s
                    cbs_scr[idx:idx + 1, :] = jnp.sum(cm * bm, axis=0, keepdims=True)

    gi = h // (SSM_H // SSM_G)
    x_row = pl.multiple_of(h * SSM_P, SSM_P)
    xs = [act_scr[t, pl.ds(x_row, SSM_P), :] for t in range(ts)]
    dts = [dt_scr[t, pl.ds(h, 1), :] for t in range(ts)]
    gs = [g_scr[t, pl.ds(h, 1), :] for t in range(ts)]
    gtot = gs[-1]
    decay = jnp.exp(gtot)
    into_state = [jnp.exp(gtot - gs[t]) * dts[t] for t in range(ts)]
    so = _state_slot(so_ref, out_mode)

    def state_row(n, acc):
        s_n = s0_ref[n]
        new = decay * s_n
        out = []
        for t in range(ts):
            out.append(acc[t] + act_scr[t, pl.ds(c_off + gi * SSM_N + n, 1), :] * s_n)
            new = new + (act_scr[t, pl.ds(b_off + gi * SSM_N + n, 1), :] * into_state[t]) * xs[t]
        so[n] = new
        return tuple(out)

    acc = lax.fori_loop(0, SSM_N, state_row,
                        tuple(jnp.zeros((SSM_P, nb), F32) for _ in range(ts)), unroll=2)
    for t in range(ts):
        o = jnp.exp(gs[t]) * acc[t]
        for s in range(t + 1):
            cb = cbs_scr[pl.ds((gi * ts + t) * ts + s, 1), :]
            o = o + (cb * jnp.exp(gs[t] - gs[s]) * dts[s]) * xs[s]
        y_scr[t, pl.ds(x_row, SSM_P), :] = o + dsk_ref[pl.ds(x_row, SSM_P), :] * xs[t]

    @pl.when(h == SSM_H - 1)
    def _():
        row = lax.broadcasted_iota(jnp.int32, (rows, nb), 0)
        seq = lax.broadcasted_iota(jnp.int32, (rows, nb), 1)
        half = MIX // SSM_G
        out = jnp.zeros((rows, MIX), F32)
        for t in range(ts):
            y = y_scr[t] * _silu(z_scr[t])
            normed = []
            for g2 in range(SSM_G):
                blk = y[g2 * half:(g2 + 1) * half, :]
                ms = jnp.mean(blk * blk, axis=0, keepdims=True)
                normed.append(blk * lax.rsqrt(ms + 1e-6))
            y = (jnp.concatenate(normed, axis=0) * wn_ref[...]).T.astype(BF16)
            put = (row == seq * ts + t).astype(BF16)
            out = out + jnp.dot(put, y, preferred_element_type=F32)
        y_ref[...] = out.astype(BF16)


def _ssd_lanes(proj, weights, c0, s0, *, layer, nb, ts, stacked, name):
    rows = nb * ts
    out_mode = "first" if layer == 0 else "later"
    aliased = list(stacked) if out_mode == "later" else []
    seg = lambda name_, w: pl.BlockSpec((rows, w), lambda h: (0, COL[name_] // w))
    const = lambda a: pl.BlockSpec((None,) + a.shape[1:], lambda h: (layer,) + (0,) * (a.ndim - 1))
    if out_mode == "first":
        co_spec = pl.BlockSpec((DEPTH,) + c0.shape[1:], lambda h: (0, 0, 0, 0))
        so_spec = pl.BlockSpec((DEPTH, None) + s0.shape[2:], lambda h: (0, h, 0, 0, 0))
    else:
        co_spec = pl.BlockSpec((None,) + c0.shape[1:], lambda h: (layer, 0, 0, 0))
        so_spec = pl.BlockSpec((None, None) + s0.shape[2:], lambda h: (layer, h, 0, 0, 0))
    n_in = 3 + len(weights) + 2
    return pl.pallas_call(
        functools.partial(_ssd_lanes_kernel, nb=nb, ts=ts, n_alias=len(aliased), out_mode=out_mode),
        grid=(SSM_H,),
        in_specs=([seg("ssm_z", 512), seg("ssm_xbc", 768), seg("ssm_dt", 128)]
                  + [const(a) for a in weights] + [const(c0)]
                  + [pl.BlockSpec((None, None) + s0.shape[2:], lambda h: (layer, h, 0, 0, 0))]
                  + [pl.BlockSpec(memory_space=pl.ANY) for _ in aliased]),
        out_specs=[pl.BlockSpec((rows, MIX), lambda h: (0, 0)), co_spec, so_spec],
        out_shape=[jax.ShapeDtypeStruct((rows, MIX), BF16),
                   jax.ShapeDtypeStruct(c0.shape, F32), jax.ShapeDtypeStruct(s0.shape, F32)],
        scratch_shapes=[pltpu.VMEM((ts, SSM_CONV_DIM, nb), F32), pltpu.VMEM((ts, MIX, nb), F32),
                        pltpu.VMEM((ts, LANES, nb), F32), pltpu.VMEM((ts, LANES, nb), F32),
                        pltpu.VMEM((SSM_G * ts * ts, nb), F32), pltpu.VMEM((ts, MIX, nb), F32)],
        input_output_aliases={n_in + i: 1 + i for i in range(len(aliased))},
        compiler_params=pltpu.CompilerParams(
            dimension_semantics=("arbitrary",), vmem_limit_bytes=VMEM_LIMIT),
        name=name,
    )(proj, proj, proj, *weights, c0, s0, *aliased)


def _ret_kernel(*refs, rows, lb, tv, chained, nsq, n_alias, out_mode):
    row_refs = refs[:4]
    lg_ref, swap_ref, cos_ref, sin_ref, s0_ref = refs[4:9]
    y_out, so_ref = refs[9 + n_alias:11 + n_alias]
    scr = refs[11 + n_alias:]
    nblk = rows // lb
    t_col = _row_in_block(rows, lb)
    s_row = lax.broadcasted_iota(jnp.int32, (1, rows), 1) % lb
    n_col = jnp.maximum(t_col - (lb - tv) + 1, 0).astype(F32)
    n_row = jnp.maximum(s_row - (lb - tv) + 1, 0).astype(F32)

    def decay_matrix(h):
        r_i, c_i = _iotas(rows)
        causal = ((r_i // lb) == (c_i // lb)) & (r_i >= c_i)
        diff = (n_col - n_row) * lg_ref[:, h:h + 1]
        return jnp.where(causal, jnp.exp(jnp.minimum(diff, 0.0)), 0.0)

    width = RET_H * RET_DK

    def state_scales():
        head = lax.broadcasted_iota(jnp.int32, (1, width), 1) // RET_DK
        lg_lane = lg_ref[:, 0:1]
        for h in range(1, RET_H):
            lg_lane = jnp.where(head == h, lg_ref[:, h:h + 1], lg_lane)
        gc = n_col * lg_lane
        return jnp.exp(gc), jnp.exp(float(tv) * lg_lane - gc)

    if chained:
        dec_scr, scale_scr = scr[1:]

        @pl.when(pl.program_id(1) == 0)
        def _():
            for h in range(RET_H):
                dec_scr[h] = decay_matrix(h)
            scale_scr[0], scale_scr[1] = state_scales()

    lane = lax.broadcasted_iota(jnp.int32, (rows, width), 1)
    first_half = (lane % RET_DK) < (RET_DK // 2)
    cos = cos_ref[...]
    sin = jnp.where(first_half, -sin_ref[...], sin_ref[...])

    def rope(x):
        partner = jnp.dot(x, swap_ref[...], preferred_element_type=F32)
        return x.astype(F32) * cos + partner * sin

    def chunk(q_ref, k_ref, v_ref, g_ref, y_ref, s_scr):
        states = [s_scr[h] for h in range(RET_H)] if chained else None
        if q_ref.shape[0] != rows:
            to_tiles = _tile_spread(rows, q_ref.shape[0], lb, tv)
            load = lambda ref: jnp.dot(to_tiles, ref[...], preferred_element_type=F32).astype(BF16)
        else:
            load = lambda ref: ref[...]
        q = rope(load(q_ref))
        k = rope(load(k_ref)) * (RET_DK ** -0.5)
        v = load(v_ref)
        if tv < lb:
            k = jnp.where(t_col >= lb - tv, k, 0.0)
        q_scale, k_scale = (scale_scr[0], scale_scr[1]) if chained else state_scales()
        q_in = q * q_scale
        k_out = k * k_scale
        yield

        so = _state_slot(so_ref, out_mode) if not chained else None
        outs = []
        for h in range(RET_H):
            ks = slice(h * RET_DK, (h + 1) * RET_DK)
            vs = slice(h * RET_DV, (h + 1) * RET_DV)
            att = _dot(q[:, ks], k[:, ks], NT) * (dec_scr[h] if chained else decay_matrix(h))
            ge = float(tv) * lg_ref[:, h:h + 1]
            qin = q_in[:, ks]
            kout = k_out[:, ks]
            if chained:
                cur = states[h]
                if rows % LANES == 0:
                    oh = _dot(jnp.concatenate([att.astype(BF16), qin.astype(BF16)], axis=1),
                              jnp.concatenate([v[:, vs], cur.astype(BF16)], axis=0))
                else:
                    oh = _dot(att, v[:, vs]) + _dot(qin, cur)
                states[h] = jnp.exp(ge) * cur + _dot(kout, v[:, vs], TN)
            else:
                oh = _dot(att, v[:, vs])
                o_rows = []
                for b in range(nblk):
                    rs = slice(b * lb, (b + 1) * lb)
                    cur = s0_ref[b, h]
                    o_rows.append(_dot(qin[rs], cur))
                    so[b, h] = jnp.exp(ge) * cur + _dot(kout[rs], v[rs, vs], TN)
                oh = oh + jnp.concatenate(o_rows, axis=0)
            outs.append(_rms(oh))
            yield
        y = (_silu(load(g_ref).astype(F32)) * jnp.concatenate(outs, axis=1)).astype(BF16)
        if y_ref.shape[0] != rows:
            y = jnp.dot(_tile_spread(rows, y_ref.shape[0], lb, tv, transpose=True), y,
                        preferred_element_type=F32).astype(BF16)
        y_ref[...] = y
        if chained:
            for h in range(RET_H):
                s_scr[h] = states[h]

    _run_chunks(chunk, row_refs, y_out, s0_ref, so_ref, scr[:1], chained, nsq, out_mode)


def _rope_swap():
    width = RET_H * RET_DK
    r = jnp.arange(width, dtype=jnp.int32)[:, None]
    c = jnp.arange(width, dtype=jnp.int32)[None, :]
    same_head = (r // RET_DK) == (c // RET_DK)
    return (same_head & (r % RET_DK == (c % RET_DK + RET_DK // 2) % RET_DK)).astype(BF16)[None]


def _ret(proj, lg, cos, sin, s0, *, lb, tv, rows, chained, name, nsq=1, **kw):
    kern = functools.partial(_ret_kernel, rows=rows, lb=lb, tv=tv, chained=chained)
    scratch = ([pltpu.VMEM((nsq, RET_H, RET_DK, RET_DV), F32), pltpu.VMEM((RET_H, rows, rows), F32),
                pltpu.VMEM((2, rows, RET_H * RET_DK), F32)] if chained else [])
    return _rec_call(kern, proj, [("ret_q", 256), ("ret_k", 256), ("ret_v", 512), ("ret_g", 512)],
                     [(lg, 0), (_rope_swap(), 0)], [s0], [MIX], rows=rows, chained=chained,
                     scratch=scratch, name=name,
                     tables=(cos, sin), nsq=nsq, **kw)


def _dense2_kernel(x_ref, gate_ref, yg_ref, ys_ref, yr_ref, lnw_ref, lnb_ref, wg_ref, ws_ref, wr_ref,
                   wo_ref, l1w_ref, l1b_ref, w1_ref, b1_ref, w2_ref, b2_ref, l2w_ref, l2b_ref, o_ref,
                   *, ff_chunk, parts, input_ln):
    tm = x_ref.shape[0]

    def rows_stage(rs):
        branches = [_dot(y_ref[rs, :], w_ref[...])
                    for y_ref, w_ref in ((yg_ref, wg_ref), (ys_ref, ws_ref), (yr_ref, wr_ref))]
        yield
        gate = lambda i: jax.nn.sigmoid(gate_ref[rs, i * D_MODEL:(i + 1) * D_MODEL].astype(F32))
        merged = gate(0) * branches[0] + gate(1) * branches[1] + gate(2) * branches[2]
        yield
        mix = _dot(merged, wo_ref[...])
        yield
        x = x_ref[rs, :]
        if input_ln:
            x = _layer_norm(x, lnw_ref[...], lnb_ref[...])
        h = _layer_norm(ALPHA * x + mix, l1w_ref[...], l1b_ref[...])
        hb = h.astype(BF16)
        ff = jnp.zeros_like(h) + b2_ref[...]
        yield
        for c0 in range(0, D_FF, ff_chunk):
            hid = jnp.dot(hb, w1_ref[:, c0:c0 + ff_chunk], preferred_element_type=F32)
            hid = jnp.square(jnp.maximum(hid + b1_ref[:, c0:c0 + ff_chunk], 0.0))
            ff = ff + _dot(hid, w2_ref[c0:c0 + ff_chunk, :])
            yield
        o_ref[rs, :] = _layer_norm(ALPHA * h + ff, l2w_ref[...], l2b_ref[...])

    _interleave([rows_stage(pl.ds(i * (tm // parts), tm // parts)) for i in range(parts)],
                stagger=True)


def _dense2(x, proj, yg, ys, yr, lnw, lnb, wl, layer, *, input_ln, tm, name):
    m = x.shape[0]
    row = lambda w: pl.BlockSpec((tm, w), lambda i: (i, 0))
    vec = pl.BlockSpec((1, D_MODEL), lambda i: (0, 0))
    const = lambda r, w: pl.BlockSpec((None, r, w), lambda i: (layer, 0, 0),
                                      pipeline_mode=pl.Buffered(1))
    return pl.pallas_call(
        functools.partial(_dense2_kernel, ff_chunk=1024, parts=(2 if tm % 32 == 0 else 1),
                          input_ln=input_ln),
        grid=(m // tm,),
        in_specs=[row(D_MODEL), row(3 * D_MODEL), row(MIX), row(MIX), row(MIX), vec, vec,
                  const(MIX, D_MODEL), const(MIX, D_MODEL), const(MIX, D_MODEL),
                  const(D_MODEL, D_MODEL), const(1, D_MODEL), const(1, D_MODEL),
                  const(D_MODEL, D_FF), const(1, D_FF), const(D_FF, D_MODEL), const(1, D_MODEL),
                  const(1, D_MODEL), const(1, D_MODEL)],
        out_specs=row(D_MODEL),
        out_shape=jax.ShapeDtypeStruct((m, D_MODEL), F32),
        compiler_params=pltpu.CompilerParams(
            dimension_semantics=("parallel",), vmem_limit_bytes=VMEM_LIMIT),
        name=name,
    )(x, proj, yg, ys, yr, lnw, lnb, wl["w_gla_out"], wl["w_ssm_out"], wl["w_ret_out"], wl["w_o"],
      wl["ln1_w"], wl["ln1_b"], wl["w_ff1"], wl["b_ff1"], wl["w_ff2"], wl["b_ff2"],
      wl["ln2_w"], wl["ln2_b"])


def _pick_tile(n, pref):
    t = min(n, pref)
    while n % t or t % SUBLANES:
        t -= 1
    return t


def _rearrange_w_in(w):
    w = jnp.swapaxes(w, -1, -2)
    offs = [0]
    for s in SPLIT_SIZES:
        offs.append(offs[-1] + s)
    names = ("gla_q", "gla_k", "gla_v", "gla_r", "gla_a", "ssm_z", "ssm_xbc", "ssm_dt",
             "ret_q", "ret_k", "ret_v", "ret_g", "gates")
    seg = {n: w[..., offs[i]:offs[i + 1], :].astype(BF16) for i, n in enumerate(names)}
    pad = lambda a: jnp.pad(a, ((0, 0),) * (a.ndim - 2) + ((0, LANES - a.shape[-2]), (0, 0)))
    order = sorted(COL, key=COL.get)
    parts = [pad(seg[n]) if n in ("gla_a", "ssm_dt") else seg[n] for n in order]
    return jnp.concatenate(parts, axis=-2)


def _rope_tables(pos):
    half = RET_DK // 2
    inv_freq = ROPE_BASE ** (-jnp.arange(half, dtype=F32) / half)
    ang = pos.astype(F32)[:, None] * inv_freq[None, :]
    cos = jnp.tile(jnp.cos(ang), (1, 2 * RET_H))
    sin = jnp.tile(jnp.sin(ang), (1, 2 * RET_H))
    return cos, sin


def kernel(x_prompt, x_sample, state_gla, state_ssm, state_conv, state_ret, meta_tokens,
           ln_in_w, ln_in_b, w_in, w_gla_a2, b_gla_a, w_gla_norm, conv_w, conv_b, dt_bias,
           a_log, d_skip, w_ssm_norm, w_gla_out, w_ssm_out, w_ret_out, w_o, ln1_w, ln1_b,
           w_ff1, b_ff1, w_ff2, b_ff2, ln2_w, ln2_b):
    bp, tp, d = x_prompt.shape
    bs, ts, _ = x_sample.shape
    assert d == D_MODEL and tp % CHUNK == 0 and w_in.shape[0] == DEPTH
    assert SAMPLE_ROWS % ts == 0
    nchunk = tp // CHUNK
    pad_rows = SAMPLE_ROWS - ts

    x_body = x_prompt.reshape(bp * tp, d)
    n_sample = bs * ts
    rows_s = min(CHUNK, bs * SAMPLE_ROWS)
    assert (bs * SAMPLE_ROWS) % rows_s == 0 and n_sample % N_META == 0
    n_small = n_sample + CHUNK
    x_small = jnp.concatenate(
        [x_sample.reshape(n_sample, d), meta_tokens.astype(F32),
         jnp.zeros((CHUNK - N_META, d), F32)], axis=0)

    cos_b, sin_b = _rope_tables(N_META + jnp.arange(tp, dtype=jnp.int32))
    cos_m, sin_m = _rope_tables(jnp.arange(N_META, dtype=jnp.int32))
    pos_tile = PAST_LEN - pad_rows + jnp.arange(SAMPLE_ROWS, dtype=jnp.int32)
    cos_s, sin_s = _rope_tables(jnp.tile(pos_tile, rows_s // SAMPLE_ROWS))
    lg_ret = jnp.pad(jnp.log1p(-jnp.exp2(-5.0 - jnp.arange(RET_H, dtype=F32))),
                     (0, LANES - RET_H)).reshape(1, 1, LANES)

    zero_gla = (jnp.zeros((1, 1, GLA_H, GLA_DK, GLA_DV), F32), 0)
    zero_ssm = (jnp.zeros((1, 1, SSM_H, SSM_N, SSM_P), F32), 0)
    zero_conv = (jnp.zeros((1, 1, SUBLANES, SSM_CONV_DIM), F32), 0)
    zero_ret = (jnp.zeros((1, 1, RET_H, RET_DK, RET_DV), F32), 0)
    from_meta = lambda a: (a[None], 0)
    ssm_lanes = state_ssm.transpose(0, 2, 3, 4, 1)
    conv_lanes = state_conv.transpose(0, 2, 1, 3)

    rowvec = lambda a: a.reshape(DEPTH, 1, -1)
    lane_pad = lambda a: jnp.pad(a, ((0, 0), (0, LANES - a.shape[1]))).reshape(DEPTH, 1, LANES)
    w_in_all = _rearrange_w_in(w_in)
    wl = dict(w_gla_out=w_gla_out.astype(BF16), w_ssm_out=w_ssm_out.astype(BF16),
              w_ret_out=w_ret_out.astype(BF16), w_o=w_o.astype(BF16),
              ln1_w=rowvec(ln1_w), ln1_b=rowvec(ln1_b),
              w_ff1=w_ff1.astype(BF16), b_ff1=rowvec(b_ff1),
              w_ff2=w_ff2.astype(BF16), b_ff2=rowvec(b_ff2),
              ln2_w=rowvec(ln2_w), ln2_b=rowvec(ln2_b))
    gla_w = [jnp.pad(w_gla_a2, ((0, 0), (0, LANES - GLA_RANK), (0, 0))), rowvec(b_gla_a),
             rowvec(w_gla_norm)]
    ssd_w = [conv_w, rowvec(conv_b), lane_pad(dt_bias), lane_pad(a_log),
             rowvec(jnp.repeat(d_skip, SSM_P, axis=1)), rowvec(w_ssm_norm)]
    colvec = lambda a: a.reshape(DEPTH, -1, 1)
    ssd_cols = [conv_w.transpose(0, 2, 1)] + [colvec(a) for a in ssd_w[1:]]
    ln_w, ln_b = ln_in_w.reshape(1, -1), ln_in_b.reshape(1, -1)

    tm_in = _pick_tile(bp * tp, 2048)
    tm_d2 = _pick_tile(bp * tp, 512)
    tm_d2s = _pick_tile(n_small, 384)
    tn = 1536
    meta = dict(row0=n_sample, nseq=1, nchunk=1, rows=N_META, lb=N_META, tv=N_META, chained=True,
                per_seq_state=False)
    body = dict(row0=0, nseq=bp, nchunk=nchunk, rows=CHUNK, lb=CHUNK, tv=CHUNK, chained=True,
                per_seq_state=False)
    ret_rows = 4 * CHUNK if tp % (4 * CHUNK) == 0 else CHUNK
    ret_body = dict(body, nchunk=tp // ret_rows, rows=ret_rows, lb=ret_rows, tv=ret_rows)
    seqs_per_step = lambda want: max(n for n in (1, 2, 4, 8) if n <= want and bp % n == 0)
    samp = dict(row0=0, nseq=bs, nchunk=1, rows=rows_s, lb=SAMPLE_ROWS, tv=ts, chained=False,
                per_seq_state=True, rows_io=rows_s // SAMPLE_ROWS * ts)

    names = ("gla_p", "gla_s", "ssm_p", "ssm_s", "conv_p", "conv_s", "ret_p", "ret_s")
    st = {k: None for k in names}
    stk = lambda *keys: [] if st[keys[0]] is None else [st[k] for k in keys]
    xb, xs = x_body, x_small
    for l in range(DEPTH):
        proj_s = _in_proj(xs, ln_w, ln_b, w_in_all, l, apply_ln=(l == 0), tm=n_small, tn=tn,
                          name=f"inproj_small_{l}")
        yg_m, sg_m = _gla(proj_s, gla_w, zero_gla, layer=l, name=f"gla_meta_{l}", **meta)
        ys_m, cv_m, ss_m = _ssd(proj_s, ssd_w, zero_conv, zero_ssm, layer=l,
                                name=f"ssd_meta_{l}", **meta)
        yr_m, sr_m = _ret(proj_s, lg_ret, cos_m, sin_m, zero_ret, layer=l,
                          name=f"ret_meta_{l}", **meta)
        yg_s, st["gla_s"] = _gla(proj_s, gla_w, (state_gla, l), layer=l, stacked=stk("gla_s"),
                                 name=f"gla_sample_{l}", **samp)
        ys_s, st["conv_s"], st["ssm_s"] = _ssd_lanes(proj_s, ssd_cols, conv_lanes, ssm_lanes, layer=l,
                                                     nb=bs, ts=ts, stacked=stk("conv_s", "ssm_s"),
                                                     name=f"ssd_sample_{l}")
        yr_s, st["ret_s"] = _ret(proj_s, lg_ret, cos_s, sin_s, (state_ret, l), layer=l,
                                 stacked=stk("ret_s"), name=f"ret_sample_{l}", **samp)
        zpad = jnp.zeros((CHUNK - N_META, MIX), BF16)
        yg = jnp.concatenate([yg_s, yg_m, zpad], axis=0)
        ys = jnp.concatenate([ys_s, ys_m, zpad], axis=0)
        yr = jnp.concatenate([yr_s, yr_m, zpad], axis=0)
        xs = _dense2(xs, proj_s, yg, ys, yr, ln_w, ln_b, wl, l, input_ln=(l == 0), tm=tm_d2s,
                     name=f"dense2_small_{l}")

        proj_b = _in_proj(xb, ln_w, ln_b, w_in_all, l, apply_ln=(l == 0), tm=tm_in, tn=tn,
                          name=f"inproj_body_{l}")
        yg_b, st["gla_p"] = _gla(proj_b, gla_w, from_meta(sg_m), layer=l, stacked=stk("gla_p"),
                                 name=f"gla_body_{l}", nsq=seqs_per_step(8), **body)
        ys_b, st["conv_p"], st["ssm_p"] = _ssd(proj_b, ssd_w, from_meta(cv_m), from_meta(ss_m),
                                               layer=l, stacked=stk("conv_p", "ssm_p"),
                                               name=f"ssd_body_{l}", nsq=seqs_per_step(8), **body)
        yr_b, st["ret_p"] = _ret(proj_b, lg_ret, cos_b, sin_b, from_meta(sr_m), layer=l,
                                 stacked=stk("ret_p"), name=f"ret_body_{l}", nsq=seqs_per_step(4),
                                 **ret_body)
        xb = _dense2(xb, proj_b, yg_b, ys_b, yr_b, ln_w, ln_b, wl, l, input_ln=(l == 0), tm=tm_d2,
                     name=f"dense2_body_{l}")

    y_prompt = xb.reshape(bp, tp, d)
    y_sample = xs[:n_sample].reshape(bs, ts, d)
    tail3 = lambda c: c[:, :, SUBLANES - (SSM_CONV - 1):, :]
    return (y_prompt, y_sample, st["gla_p"], st["gla_s"], st["ssm_p"],
            st["ssm_s"].transpose(0, 4, 1, 2, 3), tail3(st["conv_p"]),
            st["conv_s"].transpose(0, 2, 1, 3), st["ret_p"], st["ret_s"])
```

```python
import functools

import jax
import jax.numpy as jnp
from jax import lax
from jax.experimental import pallas as pl
from jax.experimental.pallas import tpu as pltpu

F32 = jnp.float32
BF16 = jnp.bfloat16

D_MODEL = 1024
DEPTH = 2
N_META = 16
MIX = 512
GLA_H, GLA_DK, GLA_DV, GLA_RANK = 4, 64, 128, 16
GLA_GATE_NORM = 16.0
SSM_H, SSM_P, SSM_N, SSM_G, SSM_CONV = 8, 64, 64, 2, 4
SSM_CONV_DIM = MIX + 2 * SSM_G * SSM_N
SSM_PAIRS = SSM_H // 2
RET_H, RET_DK, RET_DV = 4, 64, 128
ROPE_BASE = 10000.0
D_FF = 4 * D_MODEL
ALPHA = (2 * DEPTH) ** 0.25
PAST_LEN = 16384
SPLIT_SIZES = (256, 256, 512, 512, 16, 512, 768, 8, 256, 256, 512, 512, 3072)

LANES = 128
SUBLANES = 8
VMEM_LIMIT = 56 * 1024 * 1024

COL = dict(gates=0, gla_v=3072, gla_r=3584, ssm_z=4096, ret_v=4608, ret_g=5120,
           gla_q=5632, gla_k=5888, ssm_xbc=6144, ret_q=6912, ret_k=7168, gla_a=7424, ssm_dt=7552)
N_PROJ = 7680
SAMPLE_ROWS = SUBLANES
GLA_BASE = SUBLANES
CHUNK = 128

def _spans(*widths):
    edges = [0]
    for w in widths:
        edges.append(edges[-1] + w)
    return tuple(zip(edges[:-1], edges[1:]))


GLA_VEC = _spans(GLA_H * GLA_DK, GLA_DV)
SSD_VEC = _spans(SSM_CONV_DIM, LANES, LANES, MIX, MIX)
D2_VEC = _spans(*(D_MODEL,) * 5, D_FF)

NN = (((1,), (0,)), ((), ()))
NT = (((1,), (1,)), ((), ()))
TN = (((0,), (0,)), ((), ()))


def _dot(a, b, dims=NN):
    return lax.dot_general(a.astype(BF16), b.astype(BF16), dims, preferred_element_type=F32)


def _dot_sel(sel, x, dims=NN, sel_first=True):
    hi = x.astype(BF16)
    lo = (x - hi.astype(F32)).astype(BF16)
    sb = sel.astype(BF16)
    out = None
    for part in (hi, lo):
        ops = (sb, part) if sel_first else (part, sb)
        term = lax.dot_general(*ops, dims, preferred_element_type=F32)
        out = term if out is None else out + term
    return out


def _layer_norm(x, w, b):
    mu = jnp.mean(x, axis=-1, keepdims=True)
    xc = x - mu
    var = jnp.mean(xc * xc, axis=-1, keepdims=True)
    return xc * lax.rsqrt(var + 1e-5) * w + b


def _rms(x):
    return x * lax.rsqrt(jnp.mean(x * x, axis=-1, keepdims=True) + 1e-6)


def _silu(x):
    return x * jax.nn.sigmoid(x)


def _iotas(rows):
    return (lax.broadcasted_iota(jnp.int32, (rows, rows), 0),
            lax.broadcasted_iota(jnp.int32, (rows, rows), 1))


def _tile_spread(rows, rows_io, lb, tv, transpose=False):
    shape = (rows_io, rows) if transpose else (rows, rows_io)
    r = lax.broadcasted_iota(jnp.int32, shape, 1 if transpose else 0)
    c = lax.broadcasted_iota(jnp.int32, shape, 0 if transpose else 1)
    return ((r // lb == c // tv) & (r % lb - (lb - tv) == c % tv)).astype(BF16)


def _row_in_block(rows, lb):
    return lax.broadcasted_iota(jnp.int32, (rows, 1), 0) % lb


def _inproj_kernel(x_ref, lnw_ref, lnb_ref, w_ref, proj_ref, xb_scr, *, apply_ln, parts):
    tm = x_ref.shape[0]

    def first_tile(rs):
        x = x_ref[rs, :]
        if apply_ln:
            x = _layer_norm(x, lnw_ref[...], lnb_ref[...])
        xb = x.astype(BF16)
        xb_scr[rs, :] = xb
        yield
        proj_ref[rs, :] = lax.dot_general(xb, w_ref[...], NT, preferred_element_type=F32).astype(BF16)

    @pl.when(pl.program_id(1) == 0)
    def _():
        _interleave([first_tile(pl.ds(p * (tm // parts), tm // parts)) for p in range(parts)],
                    stagger=True)

    @pl.when(pl.program_id(1) > 0)
    def _():
        proj_ref[...] = lax.dot_general(xb_scr[...], w_ref[...], NT,
                                        preferred_element_type=F32).astype(BF16)


def _in_proj(x, lnw, lnb, w, layer, *, apply_ln, tm, tn, name):
    m = x.shape[0]
    grid = (m // tm, N_PROJ // tn)
    return pl.pallas_call(
        functools.partial(_inproj_kernel, apply_ln=apply_ln,
                          parts=(4 if apply_ln and tm % 64 == 0 else 1)),
        grid=grid,
        in_specs=[pl.BlockSpec((tm, D_MODEL), lambda i, j: (i, 0)),
                  pl.BlockSpec((1, D_MODEL), lambda i, j: (0, 0)),
                  pl.BlockSpec((1, D_MODEL), lambda i, j: (0, 0)),
                  pl.BlockSpec((None, tn, D_MODEL), lambda i, j: (layer, j, 0))],
        out_specs=pl.BlockSpec((tm, tn), lambda i, j: (i, j)),
        out_shape=jax.ShapeDtypeStruct((m, N_PROJ), BF16),
        scratch_shapes=[pltpu.VMEM((tm, D_MODEL), BF16)],
        compiler_params=pltpu.CompilerParams(
            dimension_semantics=("parallel", "arbitrary"), vmem_limit_bytes=VMEM_LIMIT),
        name=name,
    )(x, lnw, lnb, w)


def _rec_call(kernel_fn, proj, segs, consts, states, out_widths, *, layer, row0, nseq, nchunk,
              rows, chained, per_seq_state, scratch, name, tables=(), stacked=None, nsq=1,
              rows_io=None):
    rows_io = rows_io or rows
    rb = row0 // rows_io
    nb = nsq if chained else rows // SAMPLE_ROWS
    grid = (nseq // nb, nchunk) if chained else (nseq // nb, 1)
    assert nseq % nb == 0 and (nsq == 1 or (chained and row0 == 0))
    out_mode = "plain" if stacked is None else ("first" if layer == 0 else "later")
    aliased = list(stacked) if out_mode == "later" else []

    def row_idx(b, c):
        return b * nchunk + c if chained else b

    def col(seg, w):
        cbi = COL[seg] // w
        if nsq > 1:
            return pl.BlockSpec((nsq, rows, w), lambda b, c: (b, c, cbi))
        return pl.BlockSpec((rows_io, w), lambda b, c: (rb + row_idx(b, c), cbi))

    def row_out(w):
        if nsq > 1:
            return pl.BlockSpec((nsq, rows, w), lambda b, c: (b, c, 0))
        return pl.BlockSpec((rows_io, w), lambda b, c: (row_idx(b, c), 0))

    def const_spec(a, lyr):
        zeros = (0,) * (a.ndim - 1)
        return pl.BlockSpec((None,) + a.shape[1:], lambda b, c: (lyr,) + zeros)

    def state_in_spec(a, lyr):
        zeros = (0,) * (a.ndim - 2)
        if per_seq_state:
            return pl.BlockSpec((None, nb) + a.shape[2:], lambda b, c: (lyr, b) + zeros)
        return pl.BlockSpec((None, 1) + a.shape[2:], lambda b, c: (lyr, 0) + zeros)

    def state_out_spec(a):
        zeros = (0,) * (a.ndim - 2)
        if out_mode == "plain":
            return pl.BlockSpec((nb,) + a.shape[2:], lambda b, c: (b,) + zeros)
        if out_mode == "first":
            return pl.BlockSpec((DEPTH, nb) + a.shape[2:], lambda b, c: (0, b) + zeros)
        return pl.BlockSpec((None, nb) + a.shape[2:], lambda b, c: (layer, b) + zeros)

    def state_out_shape(a):
        lead = (nseq,) if out_mode == "plain" else (DEPTH, nseq)
        return jax.ShapeDtypeStruct(lead + a.shape[2:], F32)

    consts = [c if isinstance(c, tuple) else (c, layer) for c in consts]
    in_specs = ([col(s, w) for s, w in segs] + [const_spec(a, lyr) for a, lyr in consts]
                + [pl.BlockSpec((rows, t.shape[1]), lambda b, c: (c, 0)) for t in tables]
                + [state_in_spec(a, lyr) for a, lyr in states]
                + [pl.BlockSpec(memory_space=pl.ANY) for _ in aliased])
    n_in = len(in_specs) - len(aliased)
    n_rows = nseq * nchunk * rows if chained else nseq // nb * rows_io
    out_specs = [row_out(w) for w in out_widths] + [state_out_spec(a) for a, _ in states]
    row_shape = (lambda w: (nseq, nchunk * rows, w)) if nsq > 1 else (lambda w: (n_rows, w))
    out_shape = ([jax.ShapeDtypeStruct(row_shape(w), BF16) for w in out_widths]
                 + [state_out_shape(a) for a, _ in states])
    if nsq > 1:
        proj = proj.reshape(nseq, nchunk * rows, proj.shape[-1])
    res = pl.pallas_call(
        functools.partial(kernel_fn, n_alias=len(aliased), out_mode=out_mode, nsq=nsq),
        grid=grid, in_specs=in_specs, out_specs=out_specs, out_shape=out_shape,
        scratch_shapes=scratch,
        input_output_aliases={n_in + i: len(out_widths) + i for i in range(len(aliased))},
        compiler_params=pltpu.CompilerParams(
            dimension_semantics=("parallel", "arbitrary"), vmem_limit_bytes=VMEM_LIMIT),
        name=name,
    )(*([proj] * len(segs)), *[a for a, _ in consts], *tables, *[a for a, _ in states], *aliased)
    if nsq > 1:
        res = ([r.reshape(n_rows, r.shape[-1]) for r in res[:len(out_widths)]]
               + list(res[len(out_widths):]))
    return res


def _state_slot(ref, out_mode):
    if out_mode != "first":
        return ref
    ref[1:] = jnp.zeros((DEPTH - 1,) + ref.shape[1:], F32)
    return ref.at[0]


def _interleave(stages, stagger=False):
    stages = list(stages)
    delay = {id(gen): (i if stagger else 0) for i, gen in enumerate(stages)}
    rnd = 0
    while stages:
        for gen in list(stages):
            if delay[id(gen)] > rnd:
                continue
            try:
                next(gen)
            except StopIteration:
                stages.remove(gen)
        rnd += 1


def _run_chunks(chunk, row_refs, y_ref, s0_ref, so_ref, scr, chained, nsq, out_mode):
    if not chained:
        _interleave([chunk(*row_refs, y_ref, None)])
        return
    s_all, = scr

    @pl.when(pl.program_id(1) == 0)
    def _():
        for j in range(nsq):
            s_all[j] = s0_ref[0]

    view = lambda r, j: r.at[j] if nsq > 1 else r
    _interleave([chunk(*[view(r, j) for r in row_refs], view(y_ref, j), s_all.at[j])
                 for j in range(nsq)])

    @pl.when(pl.program_id(1) == pl.num_programs(1) - 1)
    def _():
        dst = _state_slot(so_ref, out_mode)
        for j in range(nsq):
            dst[j] = s_all[j]


def _gla_kernel(*refs, rows, lb, tv, chained, nsq, n_alias, out_mode):
    row_refs = refs[:5]
    w2_ref, vec_ref, eb_ref, sh_ref, s0_ref = refs[5:10]
    ba_ref, wn_ref = (vec_ref.at[:, a:b] for a, b in GLA_VEC)
    y_out, so_ref = refs[10 + n_alias:12 + n_alias]
    scr = refs[12 + n_alias:]
    nblk = rows // lb
    width = GLA_H * GLA_DK

    def chunk(q_ref, k_ref, v_ref, r_ref, a_ref, y_ref, s_scr):
        states = [s_scr[h] for h in range(GLA_H)] if chained else None
        if q_ref.shape[0] != rows:
            to_tiles = _tile_spread(rows, q_ref.shape[0], lb, tv)
            load = lambda ref: jnp.dot(to_tiles, ref[...], preferred_element_type=F32)
            vb = load(v_ref).astype(BF16)
        else:
            load = lambda ref: ref[...].astype(F32)
            vb = v_ref[...]
        q = load(q_ref) * (GLA_DK ** -0.5)
        k = load(k_ref)
        a = _dot(load(a_ref), w2_ref[...]) + ba_ref[...]
        g = jax.nn.log_sigmoid(a) * (1.0 / GLA_GATE_NORM)
        t_in = _row_in_block(rows, lb)
        if tv < lb:
            valid = t_in >= lb - tv
            g = jnp.where(valid, g, 0.0)
            k = jnp.where(valid, k, 0.0)
        yield
        r_i, c_i = _iotas(rows)
        same = (r_i // lb) == (c_i // lb)
        sizes = []
        while 2 * GLA_BASE * 2 ** len(sizes) <= lb:
            sizes.append(2 * GLA_BASE * 2 ** len(sizes))
        sums = [same & (r_i >= c_i)] + ([same] if nblk > 1 else [])
        sums += [same & (c_i <= (r_i // sz) * sz + sz // 2 - 1) for sz in sizes]
        gsums = _dot_sel(jnp.concatenate(sums, axis=0), g)
        gcum = gsums[:rows]
        gtot = gcum[rows - 1:rows, :] if nblk == 1 else gsums[rows:2 * rows]
        g_mids = [gsums[(len(sums) - len(sizes) + i) * rows:(len(sums) - len(sizes) + i + 1) * rows]
                  for i in range(len(sizes))]
        qd = q * jnp.exp(gcum)
        kd = k * jnp.exp(gtot - gcum)
        sel = (lax.broadcasted_iota(jnp.int32, (rows, nblk * LANES), 0) // lb
               == lax.broadcasted_iota(jnp.int32, (rows, nblk * LANES), 1) // LANES)
        ds = jnp.exp(_dot_sel(sel, g, TN, sel_first=False))
        yield

        nbase = rows // GLA_BASE
        q3 = q.reshape(nbase, GLA_BASE, width)
        k3 = k.reshape(nbase, GLA_BASE, width)
        g3 = gcum.reshape(nbase, GLA_BASE, width)
        t3 = lax.broadcasted_iota(jnp.int32, (1, GLA_BASE, 1), 1)
        pieces = []
        for s in range(max(0, GLA_BASE - tv), GLA_BASE):
            dd = jnp.minimum(g3 - g3[:, s:s + 1, :], 0.0)
            w = q3 * k3[:, s:s + 1, :] * jnp.exp(dd)
            pieces.append(jnp.where(t3 >= s, w, 0.0).reshape(rows, width).astype(BF16))
            yield
        compact = jnp.dot(jnp.concatenate(pieces, axis=1), eb_ref[...], preferred_element_type=F32)
        spread = jnp.dot(compact.astype(BF16), sh_ref[...], preferred_element_type=F32)
        base_mask = (r_i // GLA_BASE) == (c_i // GLA_BASE)
        att = [jnp.where(base_mask, spread[:, h * LANES:h * LANES + rows], 0.0) for h in range(GLA_H)]
        yield

        for size, g_mid in zip(sizes, g_mids):
            second = (t_in % size) >= size // 2
            ql = jnp.where(second, q * jnp.exp(jnp.minimum(gcum - g_mid, 0.0)), 0.0)
            kl = jnp.where(second, 0.0, k * jnp.exp(jnp.minimum(g_mid - gcum, 0.0)))
            group = (r_i // size) == (c_i // size)
            for h in range(GLA_H):
                ks = slice(h * GLA_DK, (h + 1) * GLA_DK)
                att[h] = att[h] + jnp.where(group, _dot(ql[:, ks], kl[:, ks], NT), 0.0)
            yield

        so = _state_slot(so_ref, out_mode) if not chained else None
        o_heads = []
        for h in range(GLA_H):
            ks = slice(h * GLA_DK, (h + 1) * GLA_DK)
            vs = slice(h * GLA_DV, (h + 1) * GLA_DV)
            if chained:
                cur = states[h]
                if rows % LANES == 0:
                    oh = _dot(jnp.concatenate([att[h].astype(BF16), qd[:, ks].astype(BF16)], axis=1),
                              jnp.concatenate([vb[:, vs], cur.astype(BF16)], axis=0))
                else:
                    oh = _dot(att[h], vb[:, vs]) + _dot(qd[:, ks], cur)
                states[h] = ds[ks, :] * cur + _dot(kd[:, ks], vb[:, vs], TN)
            else:
                o_rows = []
                for b in range(nblk):
                    rs = slice(b * lb, (b + 1) * lb)
                    cur = s0_ref[b, h]
                    o_rows.append(_dot(qd[rs, ks], cur))
                    so[b, h] = ds[ks, b * LANES:(b + 1) * LANES] * cur + _dot(kd[rs, ks], vb[rs, vs], TN)
                oh = _dot(att[h], vb[:, vs]) + jnp.concatenate(o_rows, axis=0)
            o_heads.append(_rms(oh) * wn_ref[...])
            yield
        y = (_silu(load(r_ref)) * jnp.concatenate(o_heads, axis=1)).astype(BF16)
        if y_ref.shape[0] != rows:
            y = jnp.dot(_tile_spread(rows, y_ref.shape[0], lb, tv, transpose=True), y,
                        preferred_element_type=F32).astype(BF16)
        y_ref[...] = y
        if chained:
            for h in range(GLA_H):
                s_scr[h] = states[h]

    _run_chunks(chunk, row_refs, y_out, s0_ref, so_ref, scr, chained, nsq, out_mode)


def _gla_consts(tv):
    sources = jnp.arange(max(0, GLA_BASE - tv), GLA_BASE, dtype=jnp.int32)
    row = jnp.arange(sources.shape[0] * GLA_H * GLA_DK, dtype=jnp.int32)
    target = sources[row // (GLA_H * GLA_DK)] * GLA_H + (row % (GLA_H * GLA_DK)) // GLA_DK
    eb = (target[:, None] == jnp.arange(LANES, dtype=jnp.int32)[None, :]).astype(BF16)
    r = jnp.arange(LANES, dtype=jnp.int32)[:, None]
    c = jnp.arange(GLA_H * LANES, dtype=jnp.int32)[None, :]
    sh = ((r < GLA_BASE * GLA_H) & (r % GLA_H == c // LANES)
          & (r // GLA_H == (c % LANES) % GLA_BASE)).astype(BF16)
    return (eb[None], 0), (sh[None], 0)


def _gla(proj, weights, s0, *, lb, tv, rows, chained, name, nsq=1, **kw):
    kern = functools.partial(_gla_kernel, rows=rows, lb=lb, tv=tv, chained=chained)
    scratch = [pltpu.VMEM((nsq, GLA_H, GLA_DK, GLA_DV), F32)] if chained else []
    return _rec_call(kern, proj,
                     [("gla_q", 256), ("gla_k", 256), ("gla_v", 512), ("gla_r", 512), ("gla_a", 128)],
                     list(weights) + list(_gla_consts(tv)), [s0], [MIX], rows=rows,
                     chained=chained, scratch=scratch, name=name, nsq=nsq, **kw)


def _ssd_kernel(*refs, rows, nsq, n_alias, out_mode):
    row_refs = refs[:3]
    cw_ref, vec_ref, c0_ref, s0_ref = refs[3:7]
    cb_ref, dtb_ref, alog_ref, dsk_ref, wn_ref = (vec_ref.at[:, a:b] for a, b in SSD_VEC)
    y_out, co_ref, so_ref = refs[7 + n_alias:10 + n_alias]
    s_all, ext_all = refs[10 + n_alias:]
    low64 = lax.broadcasted_iota(jnp.int32, (SSM_N, LANES), 1) < SSM_P

    def chunk(z_ref, x_ref, dt_ref, y_ref, s_scr, ext_scr, j):
        states = [s_scr[p] for p in range(SSM_PAIRS)]
        xb = x_ref[...]
        xin = xb.astype(F32)
        tail = ext_scr[...]
        r_i, c_i = _iotas(rows)
        t8 = lax.broadcasted_iota(jnp.int32, (SUBLANES, 1), 0)
        conv = cb_ref[...] + cw_ref[SSM_CONV - 1:SSM_CONV, :] * xin
        head = jnp.zeros((SUBLANES, SSM_CONV_DIM), F32)
        for d in range(1, SSM_CONV):
            tap = cw_ref[SSM_CONV - 1 - d:SSM_CONV - d, :]
            shifted = jnp.dot((c_i == r_i - d).astype(BF16), xb, preferred_element_type=F32)
            conv = conv + tap * shifted
            head = head + tap * jnp.where(t8 < d, pltpu.roll(tail, d, axis=0), 0.0)
        conv = jnp.concatenate([conv[:SUBLANES] + head, conv[SUBLANES:]], axis=0)
        _state_slot(co_ref, out_mode)[j] = xin[rows - SUBLANES:, :]
        ext_scr[...] = xin[rows - SUBLANES:, :]
        act = _silu(conv)
        yield

        dt = jax.nn.softplus(dt_ref[...].astype(F32) + dtb_ref[...])
        gdt = dt * (-jnp.exp(alog_ref[...]))
        causal = r_i >= c_i
        gcum = _dot_sel(causal, gdt)
        if rows < LANES:
            gsq = jnp.concatenate([gcum, jnp.zeros((LANES - rows, LANES), F32)], axis=0)
            gcum_t = gsq.T[:, :rows]
        else:
            gcum_t = gcum.T
        gtot = gcum[rows - 1:rows, :]

        lane = lax.broadcasted_iota(jnp.int32, (rows, LANES), 1)
        low = lane < SSM_P
        bcol = act[:, MIX:MIX + LANES]
        ccol = act[:, MIX + LANES:MIX + 2 * LANES]
        bswap = pltpu.roll(bcol, SSM_N, axis=1)
        cswap = pltpu.roll(ccol, SSM_N, axis=1)
        b2 = (jnp.where(low, bcol, bswap), jnp.where(low, bswap, bcol))
        c2 = (jnp.where(low, ccol, cswap), jnp.where(low, cswap, ccol))
        cb = (_dot(jnp.where(low, ccol, 0.0), bcol, NT), _dot(jnp.where(low, 0.0, ccol), bcol, NT))

        def pair_lanes(x, p):
            return jnp.where(low[:x.shape[0]], x[:, 2 * p:2 * p + 1], x[:, 2 * p + 1:2 * p + 2])

        yield
        y_pairs = []
        for p in range(SSM_PAIRS):
            gi = p // (SSM_PAIRS // SSM_G)
            decs = []
            for h in (2 * p, 2 * p + 1):
                diff = jnp.minimum(gcum[:, h:h + 1] - gcum_t[h:h + 1, :], 0.0)
                decs.append(cb[gi] * jnp.where(causal, jnp.exp(diff), 0.0))
            xp = act[:, p * LANES:(p + 1) * LANES]
            vp = xp * pair_lanes(dt, p)
            vbd = jnp.concatenate([jnp.where(low, vp, 0.0), jnp.where(low, 0.0, vp)], axis=0)
            g2 = pair_lanes(gcum, p)
            ge2 = pair_lanes(gtot, p)
            cin = c2[gi] * jnp.exp(g2)
            bout = b2[gi] * jnp.exp(ge2 - g2)
            cur = states[p]
            bd = jnp.concatenate([jnp.where(low64, cur, 0.0), jnp.where(low64, 0.0, cur)], axis=0)
            if rows % LANES == 0:
                o = _dot(jnp.concatenate(decs + [cin], axis=1).astype(BF16),
                         jnp.concatenate([vbd.astype(BF16), bd.astype(BF16)], axis=0))
            else:
                o = _dot(jnp.concatenate(decs, axis=1), vbd) + _dot(cin, bd)
            u = _dot(bout, vp, TN)
            states[p] = jnp.exp(ge2) * cur + jnp.where(low64, u[:SSM_N, :], u[SSM_N:, :])
            y_pairs.append(o + dsk_ref[:, p * LANES:(p + 1) * LANES] * xp)
            yield
        y = jnp.concatenate(y_pairs, axis=1) * _silu(z_ref[...].astype(F32))
        half = MIX // SSM_G
        y = jnp.concatenate([_rms(y[:, gi * half:(gi + 1) * half]) for gi in range(SSM_G)], axis=1)
        y_ref[...] = (y * wn_ref[...]).astype(BF16)
        for p in range(SSM_PAIRS):
            s_scr[p] = states[p]

    @pl.when(pl.program_id(1) == 0)
    def _():
        for j in range(nsq):
            for p in range(SSM_PAIRS):
                s_all[j, p] = jnp.concatenate([s0_ref[0, 2 * p], s0_ref[0, 2 * p + 1]], axis=1)
            ext_all[j, 0:SUBLANES, :] = c0_ref[0]

    view = lambda r, j: r.at[j] if nsq > 1 else r
    _interleave([chunk(*[view(r, j) for r in row_refs], view(y_out, j), s_all.at[j],
                       ext_all.at[j], j) for j in range(nsq)])

    @pl.when(pl.program_id(1) == pl.num_programs(1) - 1)
    def _():
        dst = _state_slot(so_ref, out_mode)
        for j in range(nsq):
            for p in range(SSM_PAIRS):
                dst[j, 2 * p] = s_all[j, p][:, :SSM_P]
                dst[j, 2 * p + 1] = s_all[j, p][:, SSM_P:]


def _ssd(proj, weights, c0, s0, *, lb, tv, rows, chained, name, nsq=1, **kw):
    assert chained and lb == tv == rows
    kern = functools.partial(_ssd_kernel, rows=rows)
    scratch = [pltpu.VMEM((nsq, SSM_PAIRS, SSM_N, LANES), F32),
               pltpu.VMEM((nsq, SUBLANES, SSM_CONV_DIM), F32)]
    return _rec_call(kern, proj, [("ssm_z", 512), ("ssm_xbc", 768), ("ssm_dt", 128)],
                     weights, [c0, s0], [MIX], rows=rows, chained=chained, scratch=scratch,
                     name=name, nsq=nsq, **kw)


def _ssd_lanes_kernel(*refs, nb, ts, n_alias, out_mode):
    z_ref, x_ref, dt_ref, cw_ref, vec_ref, c0_ref, s0_ref = refs[:7]
    cb_ref, dtb_ref, alog_ref, dsk_ref, wn_ref = (vec_ref.at[a:b, :] for a, b in SSD_VEC)
    y_ref, co_ref, so_ref = refs[7 + n_alias:10 + n_alias]
    act_scr, z_scr, dt_scr, g_scr, cbs_scr, y_scr = refs[10 + n_alias:]
    h = pl.program_id(0)
    rows = nb * ts
    b_off, c_off = MIX, MIX + SSM_G * SSM_N

    @pl.when(h == 0)
    def _():
        seq = lax.broadcasted_iota(jnp.int32, (nb, rows), 0)
        row = lax.broadcasted_iota(jnp.int32, (nb, rows), 1)
        plain = [c0_ref[r] for r in range(SSM_CONV - 1)]
        for t in range(ts):
            pick = (row == seq * ts + t).astype(BF16)
            plain.append(jnp.dot(pick, x_ref[...], preferred_element_type=F32))
            z_scr[t] = jnp.dot(pick, z_ref[...], preferred_element_type=F32).T
            dt_scr[t] = jnp.dot(pick, dt_ref[...], preferred_element_type=F32).T
        co = _state_slot(co_ref, out_mode)
        for r in range(SSM_CONV - 1):
            co[r] = plain[ts + r]
        lanes = [p.T for p in plain]
        for t in range(ts):
            conv = cb_ref[...]
            for i in range(SSM_CONV):
                conv = conv + cw_ref[:, i:i + 1] * lanes[t + i]
            act_scr[t] = _silu(conv)
        a_neg = -jnp.exp(alog_ref[...])
        gsum = jnp.zeros((LANES, nb), F32)
        for t in range(ts):
            dt = jax.nn.softplus(dt_scr[t] + dtb_ref[...])
            dt_scr[t] = dt
            gsum = gsum + dt * a_neg
            g_scr[t] = gsum
        for gi in range(SSM_G):
            for t in range(ts):
                cm = act_scr[t, c_off + gi * SSM_N:c_off + (gi + 1) * SSM_N, :]
                for s in range(t + 1):
                    bm = act_scr[s, b_off + gi * SSM_N:b_off + (gi + 1) * SSM_N, :]
                    idx = (gi * ts + t) * ts + s
                    cbs_scr[idx:idx + 1, :] = jnp.sum(cm * bm, axis=0, keepdims=True)

    gi = h // (SSM_H // SSM_G)
    x_row = pl.multiple_of(h * SSM_P, SSM_P)
    xs = [act_scr[t, pl.ds(x_row, SSM_P), :] for t in range(ts)]
    dts = [dt_scr[t, pl.ds(h, 1), :] for t in range(ts)]
    gs = [g_scr[t, pl.ds(h, 1), :] for t in range(ts)]
    gtot = gs[-1]
    decay = jnp.exp(gtot)
    into_state = [jnp.exp(gtot - gs[t]) * dts[t] for t in range(ts)]
    so = _state_slot(so_ref, out_mode)

    def state_row(n, acc):
        s_n = s0_ref[n]
        new = decay * s_n
        out = []
        for t in range(ts):
            out.append(acc[t] + act_scr[t, pl.ds(c_off + gi * SSM_N + n, 1), :] * s_n)
            new = new + (act_scr[t, pl.ds(b_off + gi * SSM_N + n, 1), :] * into_state[t]) * xs[t]
        so[n] = new
        return tuple(out)

    acc = lax.fori_loop(0, SSM_N, state_row,
                        tuple(jnp.zeros((SSM_P, nb), F32) for _ in range(ts)), unroll=2)
    for t in range(ts):
        o = jnp.exp(gs[t]) * acc[t]
        for s in range(t + 1):
            cb = cbs_scr[pl.ds((gi * ts + t) * ts + s, 1), :]
            o = o + (cb * jnp.exp(gs[t] - gs[s]) * dts[s]) * xs[s]
        y_scr[t, pl.ds(x_row, SSM_P), :] = o + dsk_ref[pl.ds(x_row, SSM_P), :] * xs[t]

    @pl.when(h == SSM_H - 1)
    def _():
        row = lax.broadcasted_iota(jnp.int32, (rows, nb), 0)
        seq = lax.broadcasted_iota(jnp.int32, (rows, nb), 1)
        half = MIX // SSM_G
        out = jnp.zeros((rows, MIX), F32)
        for t in range(ts):
            y = y_scr[t] * _silu(z_scr[t])
            normed = []
            for g2 in range(SSM_G):
                blk = y[g2 * half:(g2 + 1) * half, :]
                ms = jnp.mean(blk * blk, axis=0, keepdims=True)
                normed.append(blk * lax.rsqrt(ms + 1e-6))
            y = (jnp.concatenate(normed, axis=0) * wn_ref[...]).T.astype(BF16)
            put = (row == seq * ts + t).astype(BF16)
            out = out + jnp.dot(put, y, preferred_element_type=F32)
        y_ref[...] = out.astype(BF16)


def _ssd_lanes(proj, weights, c0, s0, *, layer, nb, ts, stacked, name):
    rows = nb * ts
    out_mode = "first" if layer == 0 else "later"
    aliased = list(stacked) if out_mode == "later" else []
    seg = lambda name_, w: pl.BlockSpec((rows, w), lambda h: (0, COL[name_] // w))
    const = lambda a: pl.BlockSpec((None,) + a.shape[1:], lambda h: (layer,) + (0,) * (a.ndim - 1))
    if out_mode == "first":
        co_spec = pl.BlockSpec((DEPTH,) + c0.shape[1:], lambda h: (0, 0, 0, 0))
        so_spec = pl.BlockSpec((DEPTH, None) + s0.shape[2:], lambda h: (0, h, 0, 0, 0))
    else:
        co_spec = pl.BlockSpec((None,) + c0.shape[1:], lambda h: (layer, 0, 0, 0))
        so_spec = pl.BlockSpec((None, None) + s0.shape[2:], lambda h: (layer, h, 0, 0, 0))
    n_in = 3 + len(weights) + 2
    assert len(weights) == 2
    return pl.pallas_call(
        functools.partial(_ssd_lanes_kernel, nb=nb, ts=ts, n_alias=len(aliased), out_mode=out_mode),
        grid=(SSM_H,),
        in_specs=([seg("ssm_z", 512), seg("ssm_xbc", 768), seg("ssm_dt", 128)]
                  + [const(a) for a in weights] + [const(c0)]
                  + [pl.BlockSpec((None, None) + s0.shape[2:], lambda h: (layer, h, 0, 0, 0))]
                  + [pl.BlockSpec(memory_space=pl.ANY) for _ in aliased]),
        out_specs=[pl.BlockSpec((rows, MIX), lambda h: (0, 0)), co_spec, so_spec],
        out_shape=[jax.ShapeDtypeStruct((rows, MIX), BF16),
                   jax.ShapeDtypeStruct(c0.shape, F32), jax.ShapeDtypeStruct(s0.shape, F32)],
        scratch_shapes=[pltpu.VMEM((ts, SSM_CONV_DIM, nb), F32), pltpu.VMEM((ts, MIX, nb), F32),
                        pltpu.VMEM((ts, LANES, nb), F32), pltpu.VMEM((ts, LANES, nb), F32),
                        pltpu.VMEM((SSM_G * ts * ts, nb), F32), pltpu.VMEM((ts, MIX, nb), F32)],
        input_output_aliases={n_in + i: 1 + i for i in range(len(aliased))},
        compiler_params=pltpu.CompilerParams(
            dimension_semantics=("arbitrary",), vmem_limit_bytes=VMEM_LIMIT),
        name=name,
    )(proj, proj, proj, *weights, c0, s0, *aliased)


def _ret_kernel(*refs, rows, lb, tv, chained, nsq, n_alias, out_mode):
    row_refs = refs[:4]
    lg_ref, swap_ref, cos_ref, sin_ref, s0_ref = refs[4:9]
    y_out, so_ref = refs[9 + n_alias:11 + n_alias]
    scr = refs[11 + n_alias:]
    nblk = rows // lb
    t_col = _row_in_block(rows, lb)
    s_row = lax.broadcasted_iota(jnp.int32, (1, rows), 1) % lb
    n_col = jnp.maximum(t_col - (lb - tv) + 1, 0).astype(F32)
    n_row = jnp.maximum(s_row - (lb - tv) + 1, 0).astype(F32)

    def decay_matrix(h):
        r_i, c_i = _iotas(rows)
        causal = ((r_i // lb) == (c_i // lb)) & (r_i >= c_i)
        diff = (n_col - n_row) * lg_ref[:, h:h + 1]
        return jnp.where(causal, jnp.exp(jnp.minimum(diff, 0.0)), 0.0)

    width = RET_H * RET_DK

    def state_scales():
        head = lax.broadcasted_iota(jnp.int32, (1, width), 1) // RET_DK
        lg_lane = lg_ref[:, 0:1]
        for h in range(1, RET_H):
            lg_lane = jnp.where(head == h, lg_ref[:, h:h + 1], lg_lane)
        gc = n_col * lg_lane
        return jnp.exp(gc), jnp.exp(float(tv) * lg_lane - gc)

    if chained:
        dec_scr, scale_scr = scr[1:]

        @pl.when(pl.program_id(1) == 0)
        def _():
            for h in range(RET_H):
                dec_scr[h] = decay_matrix(h)
            scale_scr[0], scale_scr[1] = state_scales()

    lane = lax.broadcasted_iota(jnp.int32, (rows, width), 1)
    first_half = (lane % RET_DK) < (RET_DK // 2)
    cos = cos_ref[...]
    sin = jnp.where(first_half, -sin_ref[...], sin_ref[...])

    def rope(x):
        partner = jnp.dot(x, swap_ref[...], preferred_element_type=F32)
        return x.astype(F32) * cos + partner * sin

    def chunk(q_ref, k_ref, v_ref, g_ref, y_ref, s_scr):
        states = [s_scr[h] for h in range(RET_H)] if chained else None
        if q_ref.shape[0] != rows:
            to_tiles = _tile_spread(rows, q_ref.shape[0], lb, tv)
            load = lambda ref: jnp.dot(to_tiles, ref[...], preferred_element_type=F32).astype(BF16)
        else:
            load = lambda ref: ref[...]
        q = rope(load(q_ref))
        k = rope(load(k_ref)) * (RET_DK ** -0.5)
        v = load(v_ref)
        if tv < lb:
            k = jnp.where(t_col >= lb - tv, k, 0.0)
        q_scale, k_scale = (scale_scr[0], scale_scr[1]) if chained else state_scales()
        q_in = q * q_scale
        k_out = k * k_scale
        yield

        so = _state_slot(so_ref, out_mode) if not chained else None
        outs = []
        for h in range(RET_H):
            ks = slice(h * RET_DK, (h + 1) * RET_DK)
            vs = slice(h * RET_DV, (h + 1) * RET_DV)
            att = _dot(q[:, ks], k[:, ks], NT) * (dec_scr[h] if chained else decay_matrix(h))
            ge = float(tv) * lg_ref[:, h:h + 1]
            qin = q_in[:, ks]
            kout = k_out[:, ks]
            if chained:
                cur = states[h]
                if rows % LANES == 0:
                    oh = _dot(jnp.concatenate([att.astype(BF16), qin.astype(BF16)], axis=1),
                              jnp.concatenate([v[:, vs], cur.astype(BF16)], axis=0))
                else:
                    oh = _dot(att, v[:, vs]) + _dot(qin, cur)
                states[h] = jnp.exp(ge) * cur + _dot(kout, v[:, vs], TN)
            else:
                oh = _dot(att, v[:, vs])
                o_rows = []
                for b in range(nblk):
                    rs = slice(b * lb, (b + 1) * lb)
                    cur = s0_ref[b, h]
                    o_rows.append(_dot(qin[rs], cur))
                    so[b, h] = jnp.exp(ge) * cur + _dot(kout[rs], v[rs, vs], TN)
                oh = oh + jnp.concatenate(o_rows, axis=0)
            outs.append(_rms(oh))
            yield
        y = (_silu(load(g_ref).astype(F32)) * jnp.concatenate(outs, axis=1)).astype(BF16)
        if y_ref.shape[0] != rows:
            y = jnp.dot(_tile_spread(rows, y_ref.shape[0], lb, tv, transpose=True), y,
                        preferred_element_type=F32).astype(BF16)
        y_ref[...] = y
        if chained:
            for h in range(RET_H):
                s_scr[h] = states[h]

    _run_chunks(chunk, row_refs, y_out, s0_ref, so_ref, scr[:1], chained, nsq, out_mode)


def _rope_swap():
    width = RET_H * RET_DK
    r = jnp.arange(width, dtype=jnp.int32)[:, None]
    c = jnp.arange(width, dtype=jnp.int32)[None, :]
    same_head = (r // RET_DK) == (c // RET_DK)
    return (same_head & (r % RET_DK == (c % RET_DK + RET_DK // 2) % RET_DK)).astype(BF16)[None]


def _ret(proj, lg, cos, sin, s0, *, lb, tv, rows, chained, name, nsq=1, **kw):
    kern = functools.partial(_ret_kernel, rows=rows, lb=lb, tv=tv, chained=chained)
    scratch = ([pltpu.VMEM((nsq, RET_H, RET_DK, RET_DV), F32), pltpu.VMEM((RET_H, rows, rows), F32),
                pltpu.VMEM((2, rows, RET_H * RET_DK), F32)] if chained else [])
    return _rec_call(kern, proj, [("ret_q", 256), ("ret_k", 256), ("ret_v", 512), ("ret_g", 512)],
                     [(lg, 0), (_rope_swap(), 0)], [s0], [MIX], rows=rows, chained=chained,
                     scratch=scratch, name=name,
                     tables=(cos, sin), nsq=nsq, **kw)


def _dense2_kernel(x_ref, gate_ref, yg_ref, ys_ref, yr_ref, lnw_ref, lnb_ref, wg_ref, ws_ref, wr_ref,
                   wo_ref, w1_ref, w2_ref, vec_ref, o_ref, *, ff_chunk, parts, input_ln):
    tm = x_ref.shape[0]
    l1w_ref, l1b_ref, b2_ref, l2w_ref, l2b_ref, b1_ref = (vec_ref.at[:, a:b] for a, b in D2_VEC)

    def rows_stage(rs):
        branches = [_dot(y_ref[rs, :], w_ref[...])
                    for y_ref, w_ref in ((yg_ref, wg_ref), (ys_ref, ws_ref), (yr_ref, wr_ref))]
        yield
        gate = lambda i: jax.nn.sigmoid(gate_ref[rs, i * D_MODEL:(i + 1) * D_MODEL].astype(F32))
        merged = gate(0) * branches[0] + gate(1) * branches[1] + gate(2) * branches[2]
        yield
        mix = _dot(merged, wo_ref[...])
        yield
        x = x_ref[rs, :]
        if input_ln:
            x = _layer_norm(x, lnw_ref[...], lnb_ref[...])
        h = _layer_norm(ALPHA * x + mix, l1w_ref[...], l1b_ref[...])
        hb = h.astype(BF16)
        ff = jnp.zeros_like(h) + b2_ref[...]
        yield
        for c0 in range(0, D_FF, ff_chunk):
            hid = jnp.dot(hb, w1_ref[:, c0:c0 + ff_chunk], preferred_element_type=F32)
            hid = jnp.square(jnp.maximum(hid + b1_ref[:, c0:c0 + ff_chunk], 0.0))
            ff = ff + _dot(hid, w2_ref[c0:c0 + ff_chunk, :])
            yield
        o_ref[rs, :] = _layer_norm(ALPHA * h + ff, l2w_ref[...], l2b_ref[...])

    _interleave([rows_stage(pl.ds(i * (tm // parts), tm // parts)) for i in range(parts)],
                stagger=True)


def _dense2(x, proj, yg, ys, yr, lnw, lnb, wl, layer, *, input_ln, tm, name):
    m = x.shape[0]
    row = lambda w: pl.BlockSpec((tm, w), lambda i: (i, 0))
    vec = pl.BlockSpec((1, D_MODEL), lambda i: (0, 0))
    const = lambda r, w: pl.BlockSpec((None, r, w), lambda i: (layer, 0, 0),
                                      pipeline_mode=pl.Buffered(1))
    return pl.pallas_call(
        functools.partial(_dense2_kernel, ff_chunk=1024, parts=(2 if tm % 32 == 0 else 1),
                          input_ln=input_ln),
        grid=(m // tm,),
        in_specs=[row(D_MODEL), row(3 * D_MODEL), row(MIX), row(MIX), row(MIX), vec, vec,
                  const(MIX, D_MODEL), const(MIX, D_MODEL), const(MIX, D_MODEL),
                  const(D_MODEL, D_MODEL), const(D_MODEL, D_FF), const(D_FF, D_MODEL),
                  const(1, D2_VEC[-1][1])],
        out_specs=row(D_MODEL),
        out_shape=jax.ShapeDtypeStruct((m, D_MODEL), F32),
        compiler_params=pltpu.CompilerParams(
            dimension_semantics=("parallel",), vmem_limit_bytes=VMEM_LIMIT),
        name=name,
    )(x, proj, yg, ys, yr, lnw, lnb, wl["w_gla_out"], wl["w_ssm_out"], wl["w_ret_out"], wl["w_o"],
      wl["w_ff1"], wl["w_ff2"], wl["vec"])


def _pick_tile(n, pref):
    t = min(n, pref)
    while n % t or t % SUBLANES:
        t -= 1
    return t


def _rearrange_w_in(w):
    w = jnp.swapaxes(w, -1, -2)
    offs = [0]
    for s in SPLIT_SIZES:
        offs.append(offs[-1] + s)
    names = ("gla_q", "gla_k", "gla_v", "gla_r", "gla_a", "ssm_z", "ssm_xbc", "ssm_dt",
             "ret_q", "ret_k", "ret_v", "ret_g", "gates")
    seg = {n: w[..., offs[i]:offs[i + 1], :].astype(BF16) for i, n in enumerate(names)}
    pad = lambda a: jnp.pad(a, ((0, 0),) * (a.ndim - 2) + ((0, LANES - a.shape[-2]), (0, 0)))
    order = sorted(COL, key=COL.get)
    parts = [pad(seg[n]) if n in ("gla_a", "ssm_dt") else seg[n] for n in order]
    return jnp.concatenate(parts, axis=-2)


def _rope_tables(pos):
    half = RET_DK // 2
    inv_freq = ROPE_BASE ** (-jnp.arange(half, dtype=F32) / half)
    ang = pos.astype(F32)[:, None] * inv_freq[None, :]
    cos = jnp.tile(jnp.cos(ang), (1, 2 * RET_H))
    sin = jnp.tile(jnp.sin(ang), (1, 2 * RET_H))
    return cos, sin


def kernel(x_prompt, x_sample, state_gla, state_ssm, state_conv, state_ret, meta_tokens,
           ln_in_w, ln_in_b, w_in, w_gla_a2, b_gla_a, w_gla_norm, conv_w, conv_b, dt_bias,
           a_log, d_skip, w_ssm_norm, w_gla_out, w_ssm_out, w_ret_out, w_o, ln1_w, ln1_b,
           w_ff1, b_ff1, w_ff2, b_ff2, ln2_w, ln2_b):
    bp, tp, d = x_prompt.shape
    bs, ts, _ = x_sample.shape
    assert d == D_MODEL and tp % CHUNK == 0 and w_in.shape[0] == DEPTH
    assert SAMPLE_ROWS % ts == 0
    nchunk = tp // CHUNK
    pad_rows = SAMPLE_ROWS - ts

    x_body = x_prompt.reshape(bp * tp, d)
    n_sample = bs * ts
    rows_s = min(CHUNK, bs * SAMPLE_ROWS)
    assert (bs * SAMPLE_ROWS) % rows_s == 0 and n_sample % N_META == 0
    n_small = n_sample + CHUNK
    x_small = jnp.concatenate(
        [x_sample.reshape(n_sample, d), meta_tokens.astype(F32),
         jnp.zeros((CHUNK - N_META, d), F32)], axis=0)

    cos_b, sin_b = _rope_tables(N_META + jnp.arange(tp, dtype=jnp.int32))
    cos_m, sin_m = _rope_tables(jnp.arange(N_META, dtype=jnp.int32))
    pos_tile = PAST_LEN - pad_rows + jnp.arange(SAMPLE_ROWS, dtype=jnp.int32)
    cos_s, sin_s = _rope_tables(jnp.tile(pos_tile, rows_s // SAMPLE_ROWS))
    lg_ret = jnp.pad(jnp.log1p(-jnp.exp2(-5.0 - jnp.arange(RET_H, dtype=F32))),
                     (0, LANES - RET_H)).reshape(1, 1, LANES)

    zero_gla = (jnp.zeros((1, 1, GLA_H, GLA_DK, GLA_DV), F32), 0)
    zero_ssm = (jnp.zeros((1, 1, SSM_H, SSM_N, SSM_P), F32), 0)
    zero_conv = (jnp.zeros((1, 1, SUBLANES, SSM_CONV_DIM), F32), 0)
    zero_ret = (jnp.zeros((1, 1, RET_H, RET_DK, RET_DV), F32), 0)
    from_meta = lambda a: (a[None], 0)
    ssm_lanes = state_ssm.transpose(0, 2, 3, 4, 1)
    conv_lanes = state_conv.transpose(0, 2, 1, 3)

    pack = lambda *vs: jnp.concatenate(vs, axis=1)
    lane_pad = lambda a: jnp.pad(a, ((0, 0), (0, LANES - a.shape[1])))
    w_in_all = _rearrange_w_in(w_in)
    wl = dict(w_gla_out=w_gla_out.astype(BF16), w_ssm_out=w_ssm_out.astype(BF16),
              w_ret_out=w_ret_out.astype(BF16), w_o=w_o.astype(BF16),
              w_ff1=w_ff1.astype(BF16), w_ff2=w_ff2.astype(BF16),
              vec=pack(ln1_w, ln1_b, b_ff2, ln2_w, ln2_b, b_ff1)[:, None, :])
    gla_w = [jnp.pad(w_gla_a2, ((0, 0), (0, LANES - GLA_RANK), (0, 0))),
             pack(b_gla_a, w_gla_norm)[:, None, :]]
    ssd_vec = pack(conv_b, lane_pad(dt_bias), lane_pad(a_log), jnp.repeat(d_skip, SSM_P, axis=1),
                   w_ssm_norm)
    ssd_w = [conv_w, ssd_vec[:, None, :]]
    ssd_cols = [conv_w.transpose(0, 2, 1), ssd_vec[:, :, None]]
    ln_w, ln_b = ln_in_w.reshape(1, -1), ln_in_b.reshape(1, -1)

    tm_in = _pick_tile(bp * tp, 2048)
    tm_d2 = _pick_tile(bp * tp, 512)
    tm_d2s = _pick_tile(n_small, 384)
    tn = 1536
    meta = dict(row0=n_sample, nseq=1, nchunk=1, rows=N_META, lb=N_META, tv=N_META, chained=True,
                per_seq_state=False)
    body = dict(row0=0, nseq=bp, nchunk=nchunk, rows=CHUNK, lb=CHUNK, tv=CHUNK, chained=True,
                per_seq_state=False)
    ret_rows = 4 * CHUNK if tp % (4 * CHUNK) == 0 else CHUNK
    ret_body = dict(body, nchunk=tp // ret_rows, rows=ret_rows, lb=ret_rows, tv=ret_rows)
    seqs_per_step = lambda want: max(n for n in (1, 2, 4, 8) if n <= want and bp % n == 0)
    samp = dict(row0=0, nseq=bs, nchunk=1, rows=rows_s, lb=SAMPLE_ROWS, tv=ts, chained=False,
                per_seq_state=True, rows_io=rows_s // SAMPLE_ROWS * ts)

    names = ("gla_p", "gla_s", "ssm_p", "ssm_s", "conv_p", "conv_s", "ret_p", "ret_s")
    st = {k: None for k in names}
    stk = lambda *keys: [] if st[keys[0]] is None else [st[k] for k in keys]
    xb, xs = x_body, x_small
    for l in range(DEPTH):
        proj_s = _in_proj(xs, ln_w, ln_b, w_in_all, l, apply_ln=(l == 0), tm=n_small, tn=tn,
                          name=f"inproj_small_{l}")
        yg_m, sg_m = _gla(proj_s, gla_w, zero_gla, layer=l, name=f"gla_meta_{l}", **meta)
        ys_m, cv_m, ss_m = _ssd(proj_s, ssd_w, zero_conv, zero_ssm, layer=l,
                                name=f"ssd_meta_{l}", **meta)
        yr_m, sr_m = _ret(proj_s, lg_ret, cos_m, sin_m, zero_ret, layer=l,
                          name=f"ret_meta_{l}", **meta)
        yg_s, st["gla_s"] = _gla(proj_s, gla_w, (state_gla, l), layer=l, stacked=stk("gla_s"),
                                 name=f"gla_sample_{l}", **samp)
        ys_s, st["conv_s"], st["ssm_s"] = _ssd_lanes(proj_s, ssd_cols, conv_lanes, ssm_lanes, layer=l,
                                                     nb=bs, ts=ts, stacked=stk("conv_s", "ssm_s"),
                                                     name=f"ssd_sample_{l}")
        yr_s, st["ret_s"] = _ret(proj_s, lg_ret, cos_s, sin_s, (state_ret, l), layer=l,
                                 stacked=stk("ret_s"), name=f"ret_sample_{l}", **samp)
        zpad = jnp.zeros((CHUNK - N_META, MIX), BF16)
        yg = jnp.concatenate([yg_s, yg_m, zpad], axis=0)
        ys = jnp.concatenate([ys_s, ys_m, zpad], axis=0)
        yr = jnp.concatenate([yr_s, yr_m, zpad], axis=0)
        xs = _dense2(xs, proj_s, yg, ys, yr, ln_w, ln_b, wl, l, input_ln=(l == 0), tm=tm_d2s,
                     name=f"dense2_small_{l}")

        proj_b = _in_proj(xb, ln_w, ln_b, w_in_all, l, apply_ln=(l == 0), tm=tm_in, tn=tn,
                          name=f"inproj_body_{l}")
        yg_b, st["gla_p"] = _gla(proj_b, gla_w, from_meta(sg_m), layer=l, stacked=stk("gla_p"),
                                 name=f"gla_body_{l}", nsq=seqs_per_step(8), **body)
        ys_b, st["conv_p"], st["ssm_p"] = _ssd(proj_b, ssd_w, from_meta(cv_m), from_meta(ss_m),
                                               layer=l, stacked=stk("conv_p", "ssm_p"),
                                               name=f"ssd_body_{l}", nsq=seqs_per_step(8), **body)
        yr_b, st["ret_p"] = _ret(proj_b, lg_ret, cos_b, sin_b, from_meta(sr_m), layer=l,
                                 stacked=stk("ret_p"), name=f"ret_body_{l}", nsq=seqs_per_step(4),
                                 **ret_body)
        xb = _dense2(xb, proj_b, yg_b, ys_b, yr_b, ln_w, ln_b, wl, l, input_ln=(l == 0), tm=tm_d2,
                     name=f"dense2_body_{l}")

    y_prompt = xb.reshape(bp, tp, d)
    y_sample = xs[:n_sample].reshape(bs, ts, d)
    tail3 = lambda c: c[:, :, SUBLANES - (SSM_CONV - 1):, :]
    return (y_prompt, y_sample, st["gla_p"], st["gla_s"], st["ssm_p"],
            st["ssm_s"].transpose(0, 4, 1, 2, 3), tail3(st["conv_p"]),
            st["conv_s"].transpose(0, 2, 1, 3), st["ret_p"], st["ret_s"])
```

```python
import functools

import jax
import jax.numpy as jnp
from jax import lax
from jax.experimental import pallas as pl
from jax.experimental.pallas import tpu as pltpu

F32 = jnp.float32
BF16 = jnp.bfloat16

D_MODEL = 1024
DEPTH = 2
N_META = 16
MIX = 512
GLA_H, GLA_DK, GLA_DV, GLA_RANK = 4, 64, 128, 16
GLA_GATE_NORM = 16.0
SSM_H, SSM_P, SSM_N, SSM_G, SSM_CONV = 8, 64, 64, 2, 4
SSM_CONV_DIM = MIX + 2 * SSM_G * SSM_N
SSM_PAIRS = SSM_H // 2
RET_H, RET_DK, RET_DV = 4, 64, 128
ROPE_BASE = 10000.0
D_FF = 4 * D_MODEL
ALPHA = (2 * DEPTH) ** 0.25
PAST_LEN = 16384
SPLIT_SIZES = (256, 256, 512, 512, 16, 512, 768, 8, 256, 256, 512, 512, 3072)

LANES = 128
SUBLANES = 8
VMEM_LIMIT = 56 * 1024 * 1024

COL = dict(gates=0, gla_v=3072, gla_r=3584, ssm_z=4096, ret_v=4608, ret_g=5120,
           gla_q=5632, gla_k=5888, ssm_xbc=6144, ret_q=6912, ret_k=7168, gla_a=7424, ssm_dt=7552)
N_PROJ = 7680
SAMPLE_ROWS = SUBLANES
GLA_BASE = SUBLANES
CHUNK = 128

def _spans(*widths):
    edges = [0]
    for w in widths:
        edges.append(edges[-1] + w)
    return tuple(zip(edges[:-1], edges[1:]))


GLA_VEC = _spans(GLA_H * GLA_DK, GLA_DV)
SSD_VEC = _spans(SSM_CONV_DIM, LANES, LANES, MIX, MIX)
D2_VEC = _spans(*(D_MODEL,) * 5, D_FF)

NN = (((1,), (0,)), ((), ()))
NT = (((1,), (1,)), ((), ()))
TN = (((0,), (0,)), ((), ()))


def _dot(a, b, dims=NN):
    return lax.dot_general(a.astype(BF16), b.astype(BF16), dims, preferred_element_type=F32)


def _dot_sel(sel, x, dims=NN, sel_first=True):
    hi = x.astype(BF16)
    lo = (x - hi.astype(F32)).astype(BF16)
    sb = sel.astype(BF16)
    out = None
    for part in (hi, lo):
        ops = (sb, part) if sel_first else (part, sb)
        term = lax.dot_general(*ops, dims, preferred_element_type=F32)
        out = term if out is None else out + term
    return out


def _layer_norm(x, w, b):
    mu = jnp.mean(x, axis=-1, keepdims=True)
    xc = x - mu
    var = jnp.mean(xc * xc, axis=-1, keepdims=True)
    return xc * lax.rsqrt(var + 1e-5) * w + b


def _rms(x):
    return x * lax.rsqrt(jnp.mean(x * x, axis=-1, keepdims=True) + 1e-6)


def _silu(x):
    return x * jax.nn.sigmoid(x)


def _iotas(rows):
    return (lax.broadcasted_iota(jnp.int32, (rows, rows), 0),
            lax.broadcasted_iota(jnp.int32, (rows, rows), 1))


def _tile_spread(rows, rows_io, lb, tv, transpose=False):
    shape = (rows_io, rows) if transpose else (rows, rows_io)
    r = lax.broadcasted_iota(jnp.int32, shape, 1 if transpose else 0)
    c = lax.broadcasted_iota(jnp.int32, shape, 0 if transpose else 1)
    return ((r // lb == c // tv) & (r % lb - (lb - tv) == c % tv)).astype(BF16)


def _row_in_block(rows, lb):
    return lax.broadcasted_iota(jnp.int32, (rows, 1), 0) % lb


def _inproj_kernel(x_ref, lnw_ref, lnb_ref, w_ref, proj_ref, xb_scr, *, apply_ln, parts):
    tm = x_ref.shape[0]

    def first_tile(rs):
        x = x_ref[rs, :]
        if apply_ln:
            x = _layer_norm(x, lnw_ref[...], lnb_ref[...])
        xb = x.astype(BF16)
        xb_scr[rs, :] = xb
        yield
        proj_ref[rs, :] = lax.dot_general(xb, w_ref[...], NT, preferred_element_type=F32).astype(BF16)

    @pl.when(pl.program_id(1) == 0)
    def _():
        _interleave([first_tile(pl.ds(p * (tm // parts), tm // parts)) for p in range(parts)],
                    stagger=True)

    @pl.when(pl.program_id(1) > 0)
    def _():
        proj_ref[...] = lax.dot_general(xb_scr[...], w_ref[...], NT,
                                        preferred_element_type=F32).astype(BF16)


def _in_proj(x, lnw, lnb, w, layer, *, apply_ln, tm, tn, name):
    m = x.shape[0]
    grid = (m // tm, N_PROJ // tn)
    return pl.pallas_call(
        functools.partial(_inproj_kernel, apply_ln=apply_ln,
                          parts=(4 if apply_ln and tm % 64 == 0 else 1)),
        grid=grid,
        in_specs=[pl.BlockSpec((tm, D_MODEL), lambda i, j: (i, 0)),
                  pl.BlockSpec((1, D_MODEL), lambda i, j: (0, 0)),
                  pl.BlockSpec((1, D_MODEL), lambda i, j: (0, 0)),
                  pl.BlockSpec((None, tn, D_MODEL), lambda i, j: (layer, j, 0))],
        out_specs=pl.BlockSpec((tm, tn), lambda i, j: (i, j)),
        out_shape=jax.ShapeDtypeStruct((m, N_PROJ), BF16),
        scratch_shapes=[pltpu.VMEM((tm, D_MODEL), BF16)],
        compiler_params=pltpu.CompilerParams(
            dimension_semantics=("parallel", "arbitrary"), vmem_limit_bytes=VMEM_LIMIT),
        name=name,
    )(x, lnw, lnb, w)


def _rec_call(kernel_fn, proj, segs, consts, states, out_widths, *, layer, row0, nseq, nchunk,
              rows, chained, per_seq_state, scratch, name, tables=(), stacked=None, nsq=1,
              rows_io=None):
    rows_io = rows_io or rows
    rb = row0 // rows_io
    nb = nsq if chained else rows // SAMPLE_ROWS
    grid = (nseq // nb, nchunk) if chained else (nseq // nb, 1)
    assert nseq % nb == 0 and (nsq == 1 or (chained and row0 == 0))
    out_mode = "plain" if stacked is None else ("first" if layer == 0 else "later")
    aliased = list(stacked) if out_mode == "later" else []

    def row_idx(b, c):
        return b * nchunk + c if chained else b

    def col(seg, w):
        cbi = COL[seg] // w
        if nsq > 1:
            return pl.BlockSpec((nsq, rows, w), lambda b, c: (b, c, cbi))
        return pl.BlockSpec((rows_io, w), lambda b, c: (rb + row_idx(b, c), cbi))

    def row_out(w):
        if nsq > 1:
            return pl.BlockSpec((nsq, rows, w), lambda b, c: (b, c, 0))
        return pl.BlockSpec((rows_io, w), lambda b, c: (row_idx(b, c), 0))

    def const_spec(a, lyr):
        zeros = (0,) * (a.ndim - 1)
        return pl.BlockSpec((None,) + a.shape[1:], lambda b, c: (lyr,) + zeros)

    def state_in_spec(a, lyr):
        zeros = (0,) * (a.ndim - 2)
        if per_seq_state:
            return pl.BlockSpec((None, nb) + a.shape[2:], lambda b, c: (lyr, b) + zeros)
        return pl.BlockSpec((None, 1) + a.shape[2:], lambda b, c: (lyr, 0) + zeros)

    def state_out_spec(a):
        zeros = (0,) * (a.ndim - 2)
        if out_mode == "plain":
            return pl.BlockSpec((nb,) + a.shape[2:], lambda b, c: (b,) + zeros)
        if out_mode == "first":
            return pl.BlockSpec((DEPTH, nb) + a.shape[2:], lambda b, c: (0, b) + zeros)
        return pl.BlockSpec((None, nb) + a.shape[2:], lambda b, c: (layer, b) + zeros)

    def state_out_shape(a):
        lead = (nseq,) if out_mode == "plain" else (DEPTH, nseq)
        return jax.ShapeDtypeStruct(lead + a.shape[2:], F32)

    consts = [c if isinstance(c, tuple) else (c, layer) for c in consts]
    in_specs = ([col(s, w) for s, w in segs] + [const_spec(a, lyr) for a, lyr in consts]
                + [pl.BlockSpec((rows, t.shape[1]), lambda b, c: (c, 0)) for t in tables]
                + [state_in_spec(a, lyr) for a, lyr in states]
                + [pl.BlockSpec(memory_space=pl.ANY) for _ in aliased])
    n_in = len(in_specs) - len(aliased)
    n_rows = nseq * nchunk * rows if chained else nseq // nb * rows_io
    out_specs = [row_out(w) for w in out_widths] + [state_out_spec(a) for a, _ in states]
    row_shape = (lambda w: (nseq, nchunk * rows, w)) if nsq > 1 else (lambda w: (n_rows, w))
    out_shape = ([jax.ShapeDtypeStruct(row_shape(w), BF16) for w in out_widths]
                 + [state_out_shape(a) for a, _ in states])
    if nsq > 1:
        proj = proj.reshape(nseq, nchunk * rows, proj.shape[-1])
    res = pl.pallas_call(
        functools.partial(kernel_fn, n_alias=len(aliased), out_mode=out_mode, nsq=nsq),
        grid=grid, in_specs=in_specs, out_specs=out_specs, out_shape=out_shape,
        scratch_shapes=scratch,
        input_output_aliases={n_in + i: len(out_widths) + i for i in range(len(aliased))},
        compiler_params=pltpu.CompilerParams(
            dimension_semantics=("parallel", "arbitrary"), vmem_limit_bytes=VMEM_LIMIT),
        name=name,
    )(*([proj] * len(segs)), *[a for a, _ in consts], *tables, *[a for a, _ in states], *aliased)
    if nsq > 1:
        res = ([r.reshape(n_rows, r.shape[-1]) for r in res[:len(out_widths)]]
               + list(res[len(out_widths):]))
    return res


def _state_slot(ref, out_mode):
    if out_mode != "first":
        return ref
    ref[1:] = jnp.zeros((DEPTH - 1,) + ref.shape[1:], F32)
    return ref.at[0]


def _interleave(stages, stagger=False):
    stages = list(stages)
    delay = {id(gen): (i if stagger else 0) for i, gen in enumerate(stages)}
    rnd = 0
    while stages:
        for gen in list(stages):
            if delay[id(gen)] > rnd:
                continue
            try:
                next(gen)
            except StopIteration:
                stages.remove(gen)
        rnd += 1


def _run_chunks(chunk, row_refs, y_ref, s0_ref, so_ref, scr, chained, nsq, out_mode):
    if not chained:
        _interleave([chunk(*row_refs, y_ref, None)])
        return
    s_all, = scr

    @pl.when(pl.program_id(1) == 0)
    def _():
        for j in range(nsq):
            s_all[j] = s0_ref[0]

    view = lambda r, j: r.at[j] if nsq > 1 else r
    _interleave([chunk(*[view(r, j) for r in row_refs], view(y_ref, j), s_all.at[j])
                 for j in range(nsq)])

    @pl.when(pl.program_id(1) == pl.num_programs(1) - 1)
    def _():
        dst = _state_slot(so_ref, out_mode)
        for j in range(nsq):
            dst[j] = s_all[j]


def _gla_kernel(*refs, rows, lb, tv, chained, nsq, n_alias, out_mode):
    row_refs = refs[:5]
    w2_ref, vec_ref, eb_ref, sh_ref, s0_ref = refs[5:10]
    ba_ref, wn_ref = (vec_ref.at[:, a:b] for a, b in GLA_VEC)
    y_out, so_ref = refs[10 + n_alias:12 + n_alias]
    scr = refs[12 + n_alias:]
    nblk = rows // lb
    width = GLA_H * GLA_DK

    def chunk(q_ref, k_ref, v_ref, r_ref, a_ref, y_ref, s_scr):
        states = [s_scr[h] for h in range(GLA_H)] if chained else None
        if q_ref.shape[0] != rows:
            to_tiles = _tile_spread(rows, q_ref.shape[0], lb, tv)
            load = lambda ref: jnp.dot(to_tiles, ref[...], preferred_element_type=F32)
            vb = load(v_ref).astype(BF16)
        else:
            load = lambda ref: ref[...].astype(F32)
            vb = v_ref[...]
        q = load(q_ref) * (GLA_DK ** -0.5)
        k = load(k_ref)
        a = _dot(load(a_ref), w2_ref[...]) + ba_ref[...]
        g = jax.nn.log_sigmoid(a) * (1.0 / GLA_GATE_NORM)
        t_in = _row_in_block(rows, lb)
        if tv < lb:
            valid = t_in >= lb - tv
            g = jnp.where(valid, g, 0.0)
            k = jnp.where(valid, k, 0.0)
        yield
        r_i, c_i = _iotas(rows)
        same = (r_i // lb) == (c_i // lb)
        sizes = []
        while 2 * GLA_BASE * 2 ** len(sizes) <= lb:
            sizes.append(2 * GLA_BASE * 2 ** len(sizes))
        sums = [same & (r_i >= c_i)] + ([same] if nblk > 1 else [])
        sums += [same & (c_i <= (r_i // sz) * sz + sz // 2 - 1) for sz in sizes]
        gsums = _dot_sel(jnp.concatenate(sums, axis=0), g)
        gcum = gsums[:rows]
        gtot = gcum[rows - 1:rows, :] if nblk == 1 else gsums[rows:2 * rows]
        g_mids = [gsums[(len(sums) - len(sizes) + i) * rows:(len(sums) - len(sizes) + i + 1) * rows]
                  for i in range(len(sizes))]
        qd = q * jnp.exp(gcum)
        kd = k * jnp.exp(gtot - gcum)
        sel = (lax.broadcasted_iota(jnp.int32, (rows, nblk * LANES), 0) // lb
               == lax.broadcasted_iota(jnp.int32, (rows, nblk * LANES), 1) // LANES)
        ds = jnp.exp(_dot_sel(sel, g, TN, sel_first=False))
        yield

        nbase = rows // GLA_BASE
        q3 = q.reshape(nbase, GLA_BASE, width)
        k3 = k.reshape(nbase, GLA_BASE, width)
        g3 = gcum.reshape(nbase, GLA_BASE, width)
        t3 = lax.broadcasted_iota(jnp.int32, (1, GLA_BASE, 1), 1)
        pieces = []
        for s in range(max(0, GLA_BASE - tv), GLA_BASE):
            dd = jnp.minimum(g3 - g3[:, s:s + 1, :], 0.0)
            w = q3 * k3[:, s:s + 1, :] * jnp.exp(dd)
            pieces.append(jnp.where(t3 >= s, w, 0.0).reshape(rows, width).astype(BF16))
            yield
        compact = jnp.dot(jnp.concatenate(pieces, axis=1), eb_ref[...], preferred_element_type=F32)
        spread = jnp.dot(compact.astype(BF16), sh_ref[...], preferred_element_type=F32)
        base_mask = (r_i // GLA_BASE) == (c_i // GLA_BASE)
        att = [jnp.where(base_mask, spread[:, h * LANES:h * LANES + rows], 0.0) for h in range(GLA_H)]
        yield

        for size, g_mid in zip(sizes, g_mids):
            second = (t_in % size) >= size // 2
            ql = jnp.where(second, q * jnp.exp(jnp.minimum(gcum - g_mid, 0.0)), 0.0)
            kl = jnp.where(second, 0.0, k * jnp.exp(jnp.minimum(g_mid - gcum, 0.0)))
            group = (r_i // size) == (c_i // size)
            for h in range(GLA_H):
                ks = slice(h * GLA_DK, (h + 1) * GLA_DK)
                att[h] = att[h] + jnp.where(group, _dot(ql[:, ks], kl[:, ks], NT), 0.0)
            yield

        so = _state_slot(so_ref, out_mode) if not chained else None
        o_heads = []
        for h in range(GLA_H):
            ks = slice(h * GLA_DK, (h + 1) * GLA_DK)
            vs = slice(h * GLA_DV, (h + 1) * GLA_DV)
            if chained:
                cur = states[h]
                if rows % LANES == 0:
                    oh = _dot(jnp.concatenate([att[h].astype(BF16), qd[:, ks].astype(BF16)], axis=1),
                              jnp.concatenate([vb[:, vs], cur.astype(BF16)], axis=0))
                else:
                    oh = _dot(att[h], vb[:, vs]) + _dot(qd[:, ks], cur)
                states[h] = ds[ks, :] * cur + _dot(kd[:, ks], vb[:, vs], TN)
            else:
                o_rows = []
                for b in range(nblk):
                    rs = slice(b * lb, (b + 1) * lb)
                    cur = s0_ref[b, h]
                    o_rows.append(_dot(qd[rs, ks], cur))
                    so[b, h] = ds[ks, b * LANES:(b + 1) * LANES] * cur + _dot(kd[rs, ks], vb[rs, vs], TN)
                oh = _dot(att[h], vb[:, vs]) + jnp.concatenate(o_rows, axis=0)
            o_heads.append(_rms(oh) * wn_ref[...])
            yield
        y = (_silu(load(r_ref)) * jnp.concatenate(o_heads, axis=1)).astype(BF16)
        if y_ref.shape[0] != rows:
            y = jnp.dot(_tile_spread(rows, y_ref.shape[0], lb, tv, transpose=True), y,
                        preferred_element_type=F32).astype(BF16)
        y_ref[...] = y
        if chained:
            for h in range(GLA_H):
                s_scr[h] = states[h]

    _run_chunks(chunk, row_refs, y_out, s0_ref, so_ref, scr, chained, nsq, out_mode)


def _gla_consts(tv):
    sources = jnp.arange(max(0, GLA_BASE - tv), GLA_BASE, dtype=jnp.int32)
    row = jnp.arange(sources.shape[0] * GLA_H * GLA_DK, dtype=jnp.int32)
    target = sources[row // (GLA_H * GLA_DK)] * GLA_H + (row % (GLA_H * GLA_DK)) // GLA_DK
    eb = (target[:, None] == jnp.arange(LANES, dtype=jnp.int32)[None, :]).astype(BF16)
    r = jnp.arange(LANES, dtype=jnp.int32)[:, None]
    c = jnp.arange(GLA_H * LANES, dtype=jnp.int32)[None, :]
    sh = ((r < GLA_BASE * GLA_H) & (r % GLA_H == c // LANES)
          & (r // GLA_H == (c % LANES) % GLA_BASE)).astype(BF16)
    return (eb[None], 0), (sh[None], 0)


def _gla(proj, weights, s0, *, lb, tv, rows, chained, name, nsq=1, **kw):
    kern = functools.partial(_gla_kernel, rows=rows, lb=lb, tv=tv, chained=chained)
    scratch = [pltpu.VMEM((nsq, GLA_H, GLA_DK, GLA_DV), F32)] if chained else []
    return _rec_call(kern, proj,
                     [("gla_q", 256), ("gla_k", 256), ("gla_v", 512), ("gla_r", 512), ("gla_a", 128)],
                     list(weights) + list(_gla_consts(tv)), [s0], [MIX], rows=rows,
                     chained=chained, scratch=scratch, name=name, nsq=nsq, **kw)


def _ssd_kernel(*refs, rows, nsq, n_alias, out_mode):
    row_refs = refs[:3]
    cw_ref, vec_ref, c0_ref, s0_ref = refs[3:7]
    cb_ref, dtb_ref, alog_ref, dsk_ref, wn_ref = (vec_ref.at[:, a:b] for a, b in SSD_VEC)
    y_out, co_ref, so_ref = refs[7 + n_alias:10 + n_alias]
    s_all, ext_all = refs[10 + n_alias:]
    low64 = lax.broadcasted_iota(jnp.int32, (SSM_N, LANES), 1) < SSM_P

    def chunk(z_ref, x_ref, dt_ref, y_ref, s_scr, ext_scr, j):
        states = [s_scr[p] for p in range(SSM_PAIRS)]
        xb = x_ref[...]
        xin = xb.astype(F32)
        tail = ext_scr[...]
        r_i, c_i = _iotas(rows)
        t8 = lax.broadcasted_iota(jnp.int32, (SUBLANES, 1), 0)
        conv = cb_ref[...] + cw_ref[SSM_CONV - 1:SSM_CONV, :] * xin
        head = jnp.zeros((SUBLANES, SSM_CONV_DIM), F32)
        for d in range(1, SSM_CONV):
            tap = cw_ref[SSM_CONV - 1 - d:SSM_CONV - d, :]
            shifted = jnp.dot((c_i == r_i - d).astype(BF16), xb, preferred_element_type=F32)
            conv = conv + tap * shifted
            head = head + tap * jnp.where(t8 < d, pltpu.roll(tail, d, axis=0), 0.0)
        conv = jnp.concatenate([conv[:SUBLANES] + head, conv[SUBLANES:]], axis=0)
        _state_slot(co_ref, out_mode)[j] = xin[rows - SUBLANES:, :]
        ext_scr[...] = xin[rows - SUBLANES:, :]
        act = _silu(conv)
        yield

        dt = jax.nn.softplus(dt_ref[...].astype(F32) + dtb_ref[...])
        gdt = dt * (-jnp.exp(alog_ref[...]))
        causal = r_i >= c_i
        gcum = _dot_sel(causal, gdt)
        if rows < LANES:
            gsq = jnp.concatenate([gcum, jnp.zeros((LANES - rows, LANES), F32)], axis=0)
            gcum_t = gsq.T[:, :rows]
        else:
            gcum_t = gcum.T
        gtot = gcum[rows - 1:rows, :]

        lane = lax.broadcasted_iota(jnp.int32, (rows, LANES), 1)
        low = lane < SSM_P
        bcol = act[:, MIX:MIX + LANES]
        ccol = act[:, MIX + LANES:MIX + 2 * LANES]
        bswap = pltpu.roll(bcol, SSM_N, axis=1)
        cswap = pltpu.roll(ccol, SSM_N, axis=1)
        b2 = (jnp.where(low, bcol, bswap), jnp.where(low, bswap, bcol))
        c2 = (jnp.where(low, ccol, cswap), jnp.where(low, cswap, ccol))
        cb = (_dot(jnp.where(low, ccol, 0.0), bcol, NT), _dot(jnp.where(low, 0.0, ccol), bcol, NT))

        def pair_lanes(x, p):
            return jnp.where(low[:x.shape[0]], x[:, 2 * p:2 * p + 1], x[:, 2 * p + 1:2 * p + 2])

        yield
        y_pairs = []
        for p in range(SSM_PAIRS):
            gi = p // (SSM_PAIRS // SSM_G)
            decs = []
            for h in (2 * p, 2 * p + 1):
                diff = jnp.minimum(gcum[:, h:h + 1] - gcum_t[h:h + 1, :], 0.0)
                decs.append(cb[gi] * jnp.where(causal, jnp.exp(diff), 0.0))
            xp = act[:, p * LANES:(p + 1) * LANES]
            vp = xp * pair_lanes(dt, p)
            vbd = jnp.concatenate([jnp.where(low, vp, 0.0), jnp.where(low, 0.0, vp)], axis=0)
            g2 = pair_lanes(gcum, p)
            ge2 = pair_lanes(gtot, p)
            cin = c2[gi] * jnp.exp(g2)
            bout = b2[gi] * jnp.exp(ge2 - g2)
            cur = states[p]
            bd = jnp.concatenate([jnp.where(low64, cur, 0.0), jnp.where(low64, 0.0, cur)], axis=0)
            if rows % LANES == 0:
                o = _dot(jnp.concatenate(decs + [cin], axis=1).astype(BF16),
                         jnp.concatenate([vbd.astype(BF16), bd.astype(BF16)], axis=0))
            else:
                o = _dot(jnp.concatenate(decs, axis=1), vbd) + _dot(cin, bd)
            u = _dot(bout, vp, TN)
            states[p] = jnp.exp(ge2) * cur + jnp.where(low64, u[:SSM_N, :], u[SSM_N:, :])
            y_pairs.append(o + dsk_ref[:, p * LANES:(p + 1) * LANES] * xp)
            yield
        y = jnp.concatenate(y_pairs, axis=1) * _silu(z_ref[...].astype(F32))
        half = MIX // SSM_G
        y = jnp.concatenate([_rms(y[:, gi * half:(gi + 1) * half]) for gi in range(SSM_G)], axis=1)
        y_ref[...] = (y * wn_ref[...]).astype(BF16)
        for p in range(SSM_PAIRS):
            s_scr[p] = states[p]

    @pl.when(pl.program_id(1) == 0)
    def _():
        for j in range(nsq):
            for p in range(SSM_PAIRS):
                s_all[j, p] = jnp.concatenate([s0_ref[0, 2 * p], s0_ref[0, 2 * p + 1]], axis=1)
            ext_all[j, 0:SUBLANES, :] = c0_ref[0]

    view = lambda r, j: r.at[j] if nsq > 1 else r
    _interleave([chunk(*[view(r, j) for r in row_refs], view(y_out, j), s_all.at[j],
                       ext_all.at[j], j) for j in range(nsq)])

    @pl.when(pl.program_id(1) == pl.num_programs(1) - 1)
    def _():
        dst = _state_slot(so_ref, out_mode)
        for j in range(nsq):
            for p in range(SSM_PAIRS):
                dst[j, 2 * p] = s_all[j, p][:, :SSM_P]
                dst[j, 2 * p + 1] = s_all[j, p][:, SSM_P:]


def _ssd(proj, weights, c0, s0, *, lb, tv, rows, chained, name, nsq=1, **kw):
    assert chained and lb == tv == rows
    kern = functools.partial(_ssd_kernel, rows=rows)
    scratch = [pltpu.VMEM((nsq, SSM_PAIRS, SSM_N, LANES), F32),
               pltpu.VMEM((nsq, SUBLANES, SSM_CONV_DIM), F32)]
    return _rec_call(kern, proj, [("ssm_z", 512), ("ssm_xbc", 768), ("ssm_dt", 128)],
                     weights, [c0, s0], [MIX], rows=rows, chained=chained, scratch=scratch,
                     name=name, nsq=nsq, **kw)


def _ssd_lanes_kernel(*refs, nb, ts, n_alias, out_mode):
    z_ref, x_ref, dt_ref, cw_ref, vec_ref, c0_ref, s0_ref = refs[:7]
    cb_ref, dtb_ref, alog_ref, dsk_ref, wn_ref = (vec_ref.at[a:b, :] for a, b in SSD_VEC)
    y_ref, co_ref, so_ref = refs[7 + n_alias:10 + n_alias]
    act_scr, z_scr, dt_scr, g_scr, cbs_scr, y_scr = refs[10 + n_alias:]
    h = pl.program_id(0)
    rows = nb * ts
    b_off, c_off = MIX, MIX + SSM_G * SSM_N

    @pl.when(h == 0)
    def _():
        seq = lax.broadcasted_iota(jnp.int32, (nb, rows), 0)
        row = lax.broadcasted_iota(jnp.int32, (nb, rows), 1)
        plain = [c0_ref[r] for r in range(SSM_CONV - 1)]
        for t in range(ts):
            pick = (row == seq * ts + t).astype(BF16)
            plain.append(jnp.dot(pick, x_ref[...], preferred_element_type=F32))
            z_scr[t] = jnp.dot(pick, z_ref[...], preferred_element_type=F32).T
            dt_scr[t] = jnp.dot(pick, dt_ref[...], preferred_element_type=F32).T
        co = _state_slot(co_ref, out_mode)
        for r in range(SSM_CONV - 1):
            co[r] = plain[ts + r]
        lanes = [p.T for p in plain]
        for t in range(ts):
            conv = cb_ref[...]
            for i in range(SSM_CONV):
                conv = conv + cw_ref[:, i:i + 1] * lanes[t + i]
            act_scr[t] = _silu(conv)
        a_neg = -jnp.exp(alog_ref[...])
        gsum = jnp.zeros((LANES, nb), F32)
        for t in range(ts):
            dt = jax.nn.softplus(dt_scr[t] + dtb_ref[...])
            dt_scr[t] = dt
            gsum = gsum + dt * a_neg
            g_scr[t] = gsum
        for gi in range(SSM_G):
            for t in range(ts):
                cm = act_scr[t, c_off + gi * SSM_N:c_off + (gi + 1) * SSM_N, :]
                for s in range(t + 1):
                    bm = act_scr[s, b_off + gi * SSM_N:b_off + (gi + 1) * SSM_N, :]
                    idx = (gi * ts + t) * ts + s
                    cbs_scr[idx:idx + 1, :] = jnp.sum(cm * bm, axis=0, keepdims=True)

    gi = h // (SSM_H // SSM_G)
    x_row = pl.multiple_of(h * SSM_P, SSM_P)
    xs = [act_scr[t, pl.ds(x_row, SSM_P), :] for t in range(ts)]
    dts = [dt_scr[t, pl.ds(h, 1), :] for t in range(ts)]
    gs = [g_scr[t, pl.ds(h, 1), :] for t in range(ts)]
    gtot = gs[-1]
    decay = jnp.exp(gtot)
    into_state = [jnp.exp(gtot - gs[t]) * dts[t] for t in range(ts)]
    so = _state_slot(so_ref, out_mode)

    def state_row(n, acc):
        s_n = s0_ref[n]
        new = decay * s_n
        out = []
        for t in range(ts):
            out.append(acc[t] + act_scr[t, pl.ds(c_off + gi * SSM_N + n, 1), :] * s_n)
            new = new + (act_scr[t, pl.ds(b_off + gi * SSM_N + n, 1), :] * into_state[t]) * xs[t]
        so[n] = new
        return tuple(out)

    acc = lax.fori_loop(0, SSM_N, state_row,
                        tuple(jnp.zeros((SSM_P, nb), F32) for _ in range(ts)), unroll=2)
    for t in range(ts):
        o = jnp.exp(gs[t]) * acc[t]
        for s in range(t + 1):
            cb = cbs_scr[pl.ds((gi * ts + t) * ts + s, 1), :]
            o = o + (cb * jnp.exp(gs[t] - gs[s]) * dts[s]) * xs[s]
        y_scr[t, pl.ds(x_row, SSM_P), :] = o + dsk_ref[pl.ds(x_row, SSM_P), :] * xs[t]

    @pl.when(h == SSM_H - 1)
    def _():
        row = lax.broadcasted_iota(jnp.int32, (rows, nb), 0)
        seq = lax.broadcasted_iota(jnp.int32, (rows, nb), 1)
        half = MIX // SSM_G
        out = jnp.zeros((rows, MIX), F32)
        for t in range(ts):
            y = y_scr[t] * _silu(z_scr[t])
            normed = []
            for g2 in range(SSM_G):
                blk = y[g2 * half:(g2 + 1) * half, :]
                ms = jnp.mean(blk * blk, axis=0, keepdims=True)
                normed.append(blk * lax.rsqrt(ms + 1e-6))
            y = (jnp.concatenate(normed, axis=0) * wn_ref[...]).T.astype(BF16)
            put = (row == seq * ts + t).astype(BF16)
            out = out + jnp.dot(put, y, preferred_element_type=F32)
        y_ref[...] = out.astype(BF16)


def _ssd_lanes(proj, weights, c0, s0, *, layer, nb, ts, stacked, name):
    rows = nb * ts
    out_mode = "first" if layer == 0 else "later"
    aliased = list(stacked) if out_mode == "later" else []
    seg = lambda name_, w: pl.BlockSpec((rows, w), lambda h: (0, COL[name_] // w))
    const = lambda a: pl.BlockSpec((None,) + a.shape[1:], lambda h: (layer,) + (0,) * (a.ndim - 1))
    if out_mode == "first":
        co_spec = pl.BlockSpec((DEPTH,) + c0.shape[1:], lambda h: (0, 0, 0, 0))
        so_spec = pl.BlockSpec((DEPTH, None) + s0.shape[2:], lambda h: (0, h, 0, 0, 0))
    else:
        co_spec = pl.BlockSpec((None,) + c0.shape[1:], lambda h: (layer, 0, 0, 0))
        so_spec = pl.BlockSpec((None, None) + s0.shape[2:], lambda h: (layer, h, 0, 0, 0))
    n_in = 3 + len(weights) + 2
    assert len(weights) == 2
    return pl.pallas_call(
        functools.partial(_ssd_lanes_kernel, nb=nb, ts=ts, n_alias=len(aliased), out_mode=out_mode),
        grid=(SSM_H,),
        in_specs=([seg("ssm_z", 512), seg("ssm_xbc", 768), seg("ssm_dt", 128)]
                  + [const(a) for a in weights] + [const(c0)]
                  + [pl.BlockSpec((None, None) + s0.shape[2:], lambda h: (layer, h, 0, 0, 0))]
                  + [pl.BlockSpec(memory_space=pl.ANY) for _ in aliased]),
        out_specs=[pl.BlockSpec((rows, MIX), lambda h: (0, 0)), co_spec, so_spec],
        out_shape=[jax.ShapeDtypeStruct((rows, MIX), BF16),
                   jax.ShapeDtypeStruct(c0.shape, F32), jax.ShapeDtypeStruct(s0.shape, F32)],
        scratch_shapes=[pltpu.VMEM((ts, SSM_CONV_DIM, nb), F32), pltpu.VMEM((ts, MIX, nb), F32),
                        pltpu.VMEM((ts, LANES, nb), F32), pltpu.VMEM((ts, LANES, nb), F32),
                        pltpu.VMEM((SSM_G * ts * ts, nb), F32), pltpu.VMEM((ts, MIX, nb), F32)],
        input_output_aliases={n_in + i: 1 + i for i in range(len(aliased))},
        compiler_params=pltpu.CompilerParams(
            dimension_semantics=("arbitrary",), vmem_limit_bytes=VMEM_LIMIT),
        name=name,
    )(proj, proj, proj, *weights, c0, s0, *aliased)


def _ret_kernel(*refs, rows, lb, tv, chained, nsq, n_alias, out_mode):
    row_refs = refs[:4]
    lg_ref, swap_ref, cos_ref, sin_ref, s0_ref = refs[4:9]
    y_out, so_ref = refs[9 + n_alias:11 + n_alias]
    scr = refs[11 + n_alias:]
    nblk = rows // lb
    t_col = _row_in_block(rows, lb)
    s_row = lax.broadcasted_iota(jnp.int32, (1, rows), 1) % lb
    n_col = jnp.maximum(t_col - (lb - tv) + 1, 0).astype(F32)
    n_row = jnp.maximum(s_row - (lb - tv) + 1, 0).astype(F32)

    def decay_matrix(h):
        r_i, c_i = _iotas(rows)
        causal = ((r_i // lb) == (c_i // lb)) & (r_i >= c_i)
        diff = (n_col - n_row) * lg_ref[:, h:h + 1]
        return jnp.where(causal, jnp.exp(jnp.minimum(diff, 0.0)), 0.0)

    width = RET_H * RET_DK

    def state_scales():
        head = lax.broadcasted_iota(jnp.int32, (1, width), 1) // RET_DK
        lg_lane = lg_ref[:, 0:1]
        for h in range(1, RET_H):
            lg_lane = jnp.where(head == h, lg_ref[:, h:h + 1], lg_lane)
        gc = n_col * lg_lane
        return jnp.exp(gc), jnp.exp(float(tv) * lg_lane - gc)

    if chained:
        dec_scr, scale_scr = scr[1:]

        @pl.when(pl.program_id(1) == 0)
        def _():
            for h in range(RET_H):
                dec_scr[h] = decay_matrix(h)
            scale_scr[0], scale_scr[1] = state_scales()

    lane = lax.broadcasted_iota(jnp.int32, (rows, width), 1)
    first_half = (lane % RET_DK) < (RET_DK // 2)
    cos = cos_ref[...]
    sin = jnp.where(first_half, -sin_ref[...], sin_ref[...])

    def rope(x):
        partner = jnp.dot(x, swap_ref[...], preferred_element_type=F32)
        return x.astype(F32) * cos + partner * sin

    def chunk(q_ref, k_ref, v_ref, g_ref, y_ref, s_scr):
        states = [s_scr[h] for h in range(RET_H)] if chained else None
        if q_ref.shape[0] != rows:
            to_tiles = _tile_spread(rows, q_ref.shape[0], lb, tv)
            load = lambda ref: jnp.dot(to_tiles, ref[...], preferred_element_type=F32).astype(BF16)
        else:
            load = lambda ref: ref[...]
        q = rope(load(q_ref))
        k = rope(load(k_ref)) * (RET_DK ** -0.5)
        v = load(v_ref)
        if tv < lb:
            k = jnp.where(t_col >= lb - tv, k, 0.0)
        q_scale, k_scale = (scale_scr[0], scale_scr[1]) if chained else state_scales()
        q_in = q * q_scale
        k_out = k * k_scale
        yield

        so = _state_slot(so_ref, out_mode) if not chained else None
        outs = []
        for h in range(RET_H):
            ks = slice(h * RET_DK, (h + 1) * RET_DK)
            vs = slice(h * RET_DV, (h + 1) * RET_DV)
            att = _dot(q[:, ks], k[:, ks], NT) * (dec_scr[h] if chained else decay_matrix(h))
            ge = float(tv) * lg_ref[:, h:h + 1]
            qin = q_in[:, ks]
            kout = k_out[:, ks]
            if chained:
                cur = states[h]
                if rows % LANES == 0:
                    oh = _dot(jnp.concatenate([att.astype(BF16), qin.astype(BF16)], axis=1),
                              jnp.concatenate([v[:, vs], cur.astype(BF16)], axis=0))
                else:
                    oh = _dot(att, v[:, vs]) + _dot(qin, cur)
                states[h] = jnp.exp(ge) * cur + _dot(kout, v[:, vs], TN)
            else:
                oh = _dot(att, v[:, vs])
                o_rows = []
                for b in range(nblk):
                    rs = slice(b * lb, (b + 1) * lb)
                    cur = s0_ref[b, h]
                    o_rows.append(_dot(qin[rs], cur))
                    so[b, h] = jnp.exp(ge) * cur + _dot(kout[rs], v[rs, vs], TN)
                oh = oh + jnp.concatenate(o_rows, axis=0)
            outs.append(_rms(oh))
            yield
        y = (_silu(load(g_ref).astype(F32)) * jnp.concatenate(outs, axis=1)).astype(BF16)
        if y_ref.shape[0] != rows:
            y = jnp.dot(_tile_spread(rows, y_ref.shape[0], lb, tv, transpose=True), y,
                        preferred_element_type=F32).astype(BF16)
        y_ref[...] = y
        if chained:
            for h in range(RET_H):
                s_scr[h] = states[h]

    _run_chunks(chunk, row_refs, y_out, s0_ref, so_ref, scr[:1], chained, nsq, out_mode)


def _rope_swap():
    width = RET_H * RET_DK
    r = jnp.arange(width, dtype=jnp.int32)[:, None]
    c = jnp.arange(width, dtype=jnp.int32)[None, :]
    same_head = (r // RET_DK) == (c // RET_DK)
    return (same_head & (r % RET_DK == (c % RET_DK + RET_DK // 2) % RET_DK)).astype(BF16)[None]


def _ret(proj, lg, cos, sin, s0, *, lb, tv, rows, chained, name, nsq=1, **kw):
    kern = functools.partial(_ret_kernel, rows=rows, lb=lb, tv=tv, chained=chained)
    scratch = ([pltpu.VMEM((nsq, RET_H, RET_DK, RET_DV), F32), pltpu.VMEM((RET_H, rows, rows), F32),
                pltpu.VMEM((2, rows, RET_H * RET_DK), F32)] if chained else [])
    return _rec_call(kern, proj, [("ret_q", 256), ("ret_k", 256), ("ret_v", 512), ("ret_g", 512)],
                     [(lg, 0), (_rope_swap(), 0)], [s0], [MIX], rows=rows, chained=chained,
                     scratch=scratch, name=name,
                     tables=(cos, sin), nsq=nsq, **kw)


def _dense2_kernel(x_ref, gate_ref, yg_ref, ys_ref, yr_ref, lnw_ref, lnb_ref, wg_ref, ws_ref, wr_ref,
                   wo_ref, w1_ref, w2_ref, vec_ref, o_ref, *, ff_chunk, parts, input_ln):
    tm = x_ref.shape[0]
    l1w_ref, l1b_ref, b2_ref, l2w_ref, l2b_ref, b1_ref = (vec_ref.at[:, a:b] for a, b in D2_VEC)

    def rows_stage(rs):
        branches = [_dot(y_ref[rs, :], w_ref[...])
                    for y_ref, w_ref in ((yg_ref, wg_ref), (ys_ref, ws_ref), (yr_ref, wr_ref))]
        yield
        gate = lambda i: jax.nn.sigmoid(gate_ref[rs, i * D_MODEL:(i + 1) * D_MODEL].astype(F32))
        merged = gate(0) * branches[0] + gate(1) * branches[1] + gate(2) * branches[2]
        yield
        mix = _dot(merged, wo_ref[...])
        yield
        x = x_ref[rs, :]
        if input_ln:
            x = _layer_norm(x, lnw_ref[...], lnb_ref[...])
        h = _layer_norm(ALPHA * x + mix, l1w_ref[...], l1b_ref[...])
        hb = h.astype(BF16)
        ff = jnp.zeros_like(h) + b2_ref[...]
        yield
        for c0 in range(0, D_FF, ff_chunk):
            hid = jnp.dot(hb, w1_ref[:, c0:c0 + ff_chunk], preferred_element_type=F32)
            hid = jnp.square(jnp.maximum(hid + b1_ref[:, c0:c0 + ff_chunk], 0.0))
            ff = ff + _dot(hid, w2_ref[c0:c0 + ff_chunk, :])
            yield
        o_ref[rs, :] = _layer_norm(ALPHA * h + ff, l2w_ref[...], l2b_ref[...])

    _interleave([rows_stage(pl.ds(i * (tm // parts), tm // parts)) for i in range(parts)],
                stagger=True)


def _dense2(x, proj, yg, ys, yr, lnw, lnb, wl, layer, *, input_ln, tm, name):
    m = x.shape[0]
    row = lambda w: pl.BlockSpec((tm, w), lambda i: (i, 0))
    vec = pl.BlockSpec((1, D_MODEL), lambda i: (0, 0))
    const = lambda r, w: pl.BlockSpec((None, r, w), lambda i: (layer, 0, 0),
                                      pipeline_mode=pl.Buffered(1))
    return pl.pallas_call(
        functools.partial(_dense2_kernel, ff_chunk=1024,
                          parts=next(p for p in (2, 3, 1) if tm % (16 * p) == 0),
                          input_ln=input_ln),
        grid=(m // tm,),
        in_specs=[row(D_MODEL), row(3 * D_MODEL), row(MIX), row(MIX), row(MIX), vec, vec,
                  const(MIX, D_MODEL), const(MIX, D_MODEL), const(MIX, D_MODEL),
                  const(D_MODEL, D_MODEL), const(D_MODEL, D_FF), const(D_FF, D_MODEL),
                  const(1, D2_VEC[-1][1])],
        out_specs=row(D_MODEL),
        out_shape=jax.ShapeDtypeStruct((m, D_MODEL), F32),
        compiler_params=pltpu.CompilerParams(
            dimension_semantics=("parallel",), vmem_limit_bytes=VMEM_LIMIT),
        name=name,
    )(x, proj, yg, ys, yr, lnw, lnb, wl["w_gla_out"], wl["w_ssm_out"], wl["w_ret_out"], wl["w_o"],
      wl["w_ff1"], wl["w_ff2"], wl["vec"])


def _pick_tile(n, pref):
    t = min(n, pref)
    while n % t or t % SUBLANES:
        t -= 1
    return t


def _rearrange_w_in(w):
    w = jnp.swapaxes(w, -1, -2)
    offs = [0]
    for s in SPLIT_SIZES:
        offs.append(offs[-1] + s)
    names = ("gla_q", "gla_k", "gla_v", "gla_r", "gla_a", "ssm_z", "ssm_xbc", "ssm_dt",
             "ret_q", "ret_k", "ret_v", "ret_g", "gates")
    seg = {n: w[..., offs[i]:offs[i + 1], :].astype(BF16) for i, n in enumerate(names)}
    pad = lambda a: jnp.pad(a, ((0, 0),) * (a.ndim - 2) + ((0, LANES - a.shape[-2]), (0, 0)))
    order = sorted(COL, key=COL.get)
    parts = [pad(seg[n]) if n in ("gla_a", "ssm_dt") else seg[n] for n in order]
    return jnp.concatenate(parts, axis=-2)


def _rope_tables(pos):
    half = RET_DK // 2
    inv_freq = ROPE_BASE ** (-jnp.arange(half, dtype=F32) / half)
    ang = pos.astype(F32)[:, None] * inv_freq[None, :]
    cos = jnp.tile(jnp.cos(ang), (1, 2 * RET_H))
    sin = jnp.tile(jnp.sin(ang), (1, 2 * RET_H))
    return cos, sin


def kernel(x_prompt, x_sample, state_gla, state_ssm, state_conv, state_ret, meta_tokens,
           ln_in_w, ln_in_b, w_in, w_gla_a2, b_gla_a, w_gla_norm, conv_w, conv_b, dt_bias,
           a_log, d_skip, w_ssm_norm, w_gla_out, w_ssm_out, w_ret_out, w_o, ln1_w, ln1_b,
           w_ff1, b_ff1, w_ff2, b_ff2, ln2_w, ln2_b):
    bp, tp, d = x_prompt.shape
    bs, ts, _ = x_sample.shape
    assert d == D_MODEL and tp % CHUNK == 0 and w_in.shape[0] == DEPTH
    assert SAMPLE_ROWS % ts == 0
    nchunk = tp // CHUNK
    pad_rows = SAMPLE_ROWS - ts

    x_body = x_prompt.reshape(bp * tp, d)
    n_sample = bs * ts
    rows_s = min(CHUNK, bs * SAMPLE_ROWS)
    assert (bs * SAMPLE_ROWS) % rows_s == 0 and n_sample % N_META == 0
    n_small = n_sample + N_META
    x_small = jnp.concatenate([x_sample.reshape(n_sample, d), meta_tokens.astype(F32)], axis=0)

    cos_b, sin_b = _rope_tables(N_META + jnp.arange(tp, dtype=jnp.int32))
    cos_m, sin_m = _rope_tables(jnp.arange(N_META, dtype=jnp.int32))
    pos_tile = PAST_LEN - pad_rows + jnp.arange(SAMPLE_ROWS, dtype=jnp.int32)
    cos_s, sin_s = _rope_tables(jnp.tile(pos_tile, rows_s // SAMPLE_ROWS))
    lg_ret = jnp.pad(jnp.log1p(-jnp.exp2(-5.0 - jnp.arange(RET_H, dtype=F32))),
                     (0, LANES - RET_H)).reshape(1, 1, LANES)

    zero_gla = (jnp.zeros((1, 1, GLA_H, GLA_DK, GLA_DV), F32), 0)
    zero_ssm = (jnp.zeros((1, 1, SSM_H, SSM_N, SSM_P), F32), 0)
    zero_conv = (jnp.zeros((1, 1, SUBLANES, SSM_CONV_DIM), F32), 0)
    zero_ret = (jnp.zeros((1, 1, RET_H, RET_DK, RET_DV), F32), 0)
    from_meta = lambda a: (a[None], 0)
    ssm_lanes = state_ssm.transpose(0, 2, 3, 4, 1)
    conv_lanes = state_conv.transpose(0, 2, 1, 3)

    pack = lambda *vs: jnp.concatenate(vs, axis=1)
    lane_pad = lambda a: jnp.pad(a, ((0, 0), (0, LANES - a.shape[1])))
    w_in_all = _rearrange_w_in(w_in)
    wl = dict(w_gla_out=w_gla_out.astype(BF16), w_ssm_out=w_ssm_out.astype(BF16),
              w_ret_out=w_ret_out.astype(BF16), w_o=w_o.astype(BF16),
              w_ff1=w_ff1.astype(BF16), w_ff2=w_ff2.astype(BF16),
              vec=pack(ln1_w, ln1_b, b_ff2, ln2_w, ln2_b, b_ff1)[:, None, :])
    gla_w = [jnp.pad(w_gla_a2, ((0, 0), (0, LANES - GLA_RANK), (0, 0))),
             pack(b_gla_a, w_gla_norm)[:, None, :]]
    ssd_vec = pack(conv_b, lane_pad(dt_bias), lane_pad(a_log), jnp.repeat(d_skip, SSM_P, axis=1),
                   w_ssm_norm)
    ssd_w = [conv_w, ssd_vec[:, None, :]]
    ssd_cols = [conv_w.transpose(0, 2, 1), ssd_vec[:, :, None]]
    ln_w, ln_b = ln_in_w.reshape(1, -1), ln_in_b.reshape(1, -1)

    tm_in = _pick_tile(bp * tp, 2048)
    tm_d2 = _pick_tile(bp * tp, 512)
    tm_d2s = n_small if n_small <= 1024 else _pick_tile(n_small, 512)
    tn = 1536
    meta = dict(row0=n_sample, nseq=1, nchunk=1, rows=N_META, lb=N_META, tv=N_META, chained=True,
                per_seq_state=False)
    body = dict(row0=0, nseq=bp, nchunk=nchunk, rows=CHUNK, lb=CHUNK, tv=CHUNK, chained=True,
                per_seq_state=False)
    ret_rows = 4 * CHUNK if tp % (4 * CHUNK) == 0 else CHUNK
    ret_body = dict(body, nchunk=tp // ret_rows, rows=ret_rows, lb=ret_rows, tv=ret_rows)
    seqs_per_step = lambda want: max(n for n in (1, 2, 4, 8) if n <= want and bp % n == 0)
    samp = dict(row0=0, nseq=bs, nchunk=1, rows=rows_s, lb=SAMPLE_ROWS, tv=ts, chained=False,
                per_seq_state=True, rows_io=rows_s // SAMPLE_ROWS * ts)

    names = ("gla_p", "gla_s", "ssm_p", "ssm_s", "conv_p", "conv_s", "ret_p", "ret_s")
    st = {k: None for k in names}
    stk = lambda *keys: [] if st[keys[0]] is None else [st[k] for k in keys]
    xb, xs = x_body, x_small
    for l in range(DEPTH):
        proj_s = _in_proj(xs, ln_w, ln_b, w_in_all, l, apply_ln=(l == 0), tm=n_small, tn=tn,
                          name=f"inproj_small_{l}")
        yg_m, sg_m = _gla(proj_s, gla_w, zero_gla, layer=l, name=f"gla_meta_{l}", **meta)
        ys_m, cv_m, ss_m = _ssd(proj_s, ssd_w, zero_conv, zero_ssm, layer=l,
                                name=f"ssd_meta_{l}", **meta)
        yr_m, sr_m = _ret(proj_s, lg_ret, cos_m, sin_m, zero_ret, layer=l,
                          name=f"ret_meta_{l}", **meta)
        yg_s, st["gla_s"] = _gla(proj_s, gla_w, (state_gla, l), layer=l, stacked=stk("gla_s"),
                                 name=f"gla_sample_{l}", **samp)
        ys_s, st["conv_s"], st["ssm_s"] = _ssd_lanes(proj_s, ssd_cols, conv_lanes, ssm_lanes, layer=l,
                                                     nb=bs, ts=ts, stacked=stk("conv_s", "ssm_s"),
                                                     name=f"ssd_sample_{l}")
        yr_s, st["ret_s"] = _ret(proj_s, lg_ret, cos_s, sin_s, (state_ret, l), layer=l,
                                 stacked=stk("ret_s"), name=f"ret_sample_{l}", **samp)
        yg = jnp.concatenate([yg_s, yg_m], axis=0)
        ys = jnp.concatenate([ys_s, ys_m], axis=0)
        yr = jnp.concatenate([yr_s, yr_m], axis=0)
        xs = _dense2(xs, proj_s, yg, ys, yr, ln_w, ln_b, wl, l, input_ln=(l == 0), tm=tm_d2s,
                     name=f"dense2_small_{l}")

        proj_b = _in_proj(xb, ln_w, ln_b, w_in_all, l, apply_ln=(l == 0), tm=tm_in, tn=tn,
                          name=f"inproj_body_{l}")
        yg_b, st["gla_p"] = _gla(proj_b, gla_w, from_meta(sg_m), layer=l, stacked=stk("gla_p"),
                                 name=f"gla_body_{l}", nsq=seqs_per_step(8), **body)
        ys_b, st["conv_p"], st["ssm_p"] = _ssd(proj_b, ssd_w, from_meta(cv_m), from_meta(ss_m),
                                               layer=l, stacked=stk("conv_p", "ssm_p"),
                                               name=f"ssd_body_{l}", nsq=seqs_per_step(8), **body)
        yr_b, st["ret_p"] = _ret(proj_b, lg_ret, cos_b, sin_b, from_meta(sr_m), layer=l,
                                 stacked=stk("ret_p"), name=f"ret_body_{l}", nsq=seqs_per_step(4),
                                 **ret_body)
        xb = _dense2(xb, proj_b, yg_b, ys_b, yr_b, ln_w, ln_b, wl, l, input_ln=(l == 0), tm=tm_d2,
                     name=f"dense2_body_{l}")

    y_prompt = xb.reshape(bp, tp, d)
    y_sample = xs[:n_sample].reshape(bs, ts, d)
    tail3 = lambda c: c[:, :, SUBLANES - (SSM_CONV - 1):, :]
    return (y_prompt, y_sample, st["gla_p"], st["gla_s"], st["ssm_p"],
            st["ssm_s"].transpose(0, 4, 1, 2, 3), tail3(st["conv_p"]),
            st["conv_s"].transpose(0, 2, 1, 3), st["ret_p"], st["ret_s"])
```

```python
import functools

import jax
import jax.numpy as jnp
from jax import lax
from jax.experimental import pallas as pl
from jax.experimental.pallas import tpu as pltpu

F32 = jnp.float32
BF16 = jnp.bfloat16

D_MODEL = 1024
DEPTH = 2
N_META = 16
MIX = 512
GLA_H, GLA_DK, GLA_DV, GLA_RANK = 4, 64, 128, 16
GLA_GATE_NORM = 16.0
SSM_H, SSM_P, SSM_N, SSM_G, SSM_CONV = 8, 64, 64, 2, 4
SSM_CONV_DIM = MIX + 2 * SSM_G * SSM_N
SSM_PAIRS = SSM_H // 2
RET_H, RET_DK, RET_DV = 4, 64, 128
ROPE_BASE = 10000.0
D_FF = 4 * D_MODEL
ALPHA = (2 * DEPTH) ** 0.25
PAST_LEN = 16384
SPLIT_SIZES = (256, 256, 512, 512, 16, 512, 768, 8, 256, 256, 512, 512, 3072)

LANES = 128
SUBLANES = 8
VMEM_LIMIT = 56 * 1024 * 1024

COL = dict(gates=0, gla_v=3072, gla_r=3584, ssm_z=4096, ret_v=4608, ret_g=5120,
           gla_q=5632, gla_k=5888, ssm_xbc=6144, ret_q=6912, ret_k=7168, gla_a=7424, ssm_dt=7552)
N_PROJ = 7680
SAMPLE_ROWS = SUBLANES
GLA_BASE = SUBLANES
CHUNK = 128

def _spans(*widths):
    edges = [0]
    for w in widths:
        edges.append(edges[-1] + w)
    return tuple(zip(edges[:-1], edges[1:]))


GLA_VEC = _spans(GLA_H * GLA_DK, GLA_DV)
SSD_VEC = _spans(SSM_CONV_DIM, LANES, LANES, MIX, MIX)
D2_VEC = _spans(*(D_MODEL,) * 5, D_FF)

NN = (((1,), (0,)), ((), ()))
NT = (((1,), (1,)), ((), ()))
TN = (((0,), (0,)), ((), ()))


def _dot(a, b, dims=NN):
    return lax.dot_general(a.astype(BF16), b.astype(BF16), dims, preferred_element_type=F32)


def _dot_sel(sel, x, dims=NN, sel_first=True):
    hi = x.astype(BF16)
    lo = (x - hi.astype(F32)).astype(BF16)
    sb = sel.astype(BF16)
    out = None
    for part in (hi, lo):
        ops = (sb, part) if sel_first else (part, sb)
        term = lax.dot_general(*ops, dims, preferred_element_type=F32)
        out = term if out is None else out + term
    return out


def _layer_norm(x, w, b):
    mu = jnp.mean(x, axis=-1, keepdims=True)
    xc = x - mu
    var = jnp.mean(xc * xc, axis=-1, keepdims=True)
    return xc * lax.rsqrt(var + 1e-5) * w + b


def _rms(x):
    return x * lax.rsqrt(jnp.mean(x * x, axis=-1, keepdims=True) + 1e-6)


def _silu(x):
    return x * jax.nn.sigmoid(x)


def _iotas(rows):
    return (lax.broadcasted_iota(jnp.int32, (rows, rows), 0),
            lax.broadcasted_iota(jnp.int32, (rows, rows), 1))


def _tile_spread(rows, rows_io, lb, tv, transpose=False):
    shape = (rows_io, rows) if transpose else (rows, rows_io)
    r = lax.broadcasted_iota(jnp.int32, shape, 1 if transpose else 0)
    c = lax.broadcasted_iota(jnp.int32, shape, 0 if transpose else 1)
    return ((r // lb == c // tv) & (r % lb - (lb - tv) == c % tv)).astype(BF16)


def _row_in_block(rows, lb):
    return lax.broadcasted_iota(jnp.int32, (rows, 1), 0) % lb


def _inproj_kernel(x_ref, lnw_ref, lnb_ref, w_ref, proj_ref, xb_scr, *, apply_ln, parts):
    tm = x_ref.shape[0]

    def first_tile(rs):
        x = x_ref[rs, :]
        if apply_ln:
            x = _layer_norm(x, lnw_ref[...], lnb_ref[...])
        xb = x.astype(BF16)
        xb_scr[rs, :] = xb
        yield
        proj_ref[rs, :] = lax.dot_general(xb, w_ref[...], NT, preferred_element_type=F32).astype(BF16)

    @pl.when(pl.program_id(1) == 0)
    def _():
        _interleave([first_tile(pl.ds(p * (tm // parts), tm // parts)) for p in range(parts)],
                    stagger=True)

    @pl.when(pl.program_id(1) > 0)
    def _():
        proj_ref[...] = lax.dot_general(xb_scr[...], w_ref[...], NT,
                                        preferred_element_type=F32).astype(BF16)


def _in_proj(x, lnw, lnb, w, layer, *, apply_ln, tm, tn, name):
    m = x.shape[0]
    grid = (m // tm, N_PROJ // tn)
    return pl.pallas_call(
        functools.partial(_inproj_kernel, apply_ln=apply_ln,
                          parts=(4 if apply_ln and tm % 64 == 0 else 1)),
        grid=grid,
        in_specs=[pl.BlockSpec((tm, D_MODEL), lambda i, j: (i, 0)),
                  pl.BlockSpec((1, D_MODEL), lambda i, j: (0, 0)),
                  pl.BlockSpec((1, D_MODEL), lambda i, j: (0, 0)),
                  pl.BlockSpec((None, tn, D_MODEL), lambda i, j: (layer, j, 0))],
        out_specs=pl.BlockSpec((tm, tn), lambda i, j: (i, j)),
        out_shape=jax.ShapeDtypeStruct((m, N_PROJ), BF16),
        scratch_shapes=[pltpu.VMEM((tm, D_MODEL), BF16)],
        compiler_params=pltpu.CompilerParams(
            dimension_semantics=("parallel", "arbitrary"), vmem_limit_bytes=VMEM_LIMIT),
        name=name,
    )(x, lnw, lnb, w)


def _rec_call(kernel_fn, proj, segs, consts, states, out_widths, *, layer, row0, nseq, nchunk,
              rows, chained, per_seq_state, scratch, name, tables=(), stacked=None, nsq=1,
              rows_io=None):
    rows_io = rows_io or rows
    rb = row0 // rows_io
    nb = nsq if chained else rows // SAMPLE_ROWS
    grid = (nseq // nb, nchunk) if chained else (nseq // nb, 1)
    assert nseq % nb == 0 and (nsq == 1 or (chained and row0 == 0))
    out_mode = "plain" if stacked is None else ("first" if layer == 0 else "later")
    aliased = list(stacked) if out_mode == "later" else []

    def row_idx(b, c):
        return b * nchunk + c if chained else b

    def col(seg, w):
        cbi = COL[seg] // w
        if nsq > 1:
            return pl.BlockSpec((nsq, rows, w), lambda b, c: (b, c, cbi))
        return pl.BlockSpec((rows_io, w), lambda b, c: (rb + row_idx(b, c), cbi))

    def row_out(w):
        if nsq > 1:
            return pl.BlockSpec((nsq, rows, w), lambda b, c: (b, c, 0))
        return pl.BlockSpec((rows_io, w), lambda b, c: (row_idx(b, c), 0))

    def const_spec(a, lyr):
        zeros = (0,) * (a.ndim - 1)
        return pl.BlockSpec((None,) + a.shape[1:], lambda b, c: (lyr,) + zeros)

    def state_in_spec(a, lyr):
        zeros = (0,) * (a.ndim - 2)
        if per_seq_state:
            return pl.BlockSpec((None, nb) + a.shape[2:], lambda b, c: (lyr, b) + zeros)
        return pl.BlockSpec((None, 1) + a.shape[2:], lambda b, c: (lyr, 0) + zeros)

    def state_out_spec(a):
        zeros = (0,) * (a.ndim - 2)
        if out_mode == "plain":
            return pl.BlockSpec((nb,) + a.shape[2:], lambda b, c: (b,) + zeros)
        if out_mode == "first":
            return pl.BlockSpec((DEPTH, nb) + a.shape[2:], lambda b, c: (0, b) + zeros)
        return pl.BlockSpec((None, nb) + a.shape[2:], lambda b, c: (layer, b) + zeros)

    def state_out_shape(a):
        lead = (nseq,) if out_mode == "plain" else (DEPTH, nseq)
        return jax.ShapeDtypeStruct(lead + a.shape[2:], F32)

    consts = [c if isinstance(c, tuple) else (c, layer) for c in consts]
    in_specs = ([col(s, w) for s, w in segs] + [const_spec(a, lyr) for a, lyr in consts]
                + [pl.BlockSpec((rows, t.shape[1]), lambda b, c: (c, 0)) for t in tables]
                + [state_in_spec(a, lyr) for a, lyr in states]
                + [pl.BlockSpec(memory_space=pl.ANY) for _ in aliased])
    n_in = len(in_specs) - len(aliased)
    n_rows = nseq * nchunk * rows if chained else nseq // nb * rows_io
    out_specs = [row_out(w) for w in out_widths] + [state_out_spec(a) for a, _ in states]
    row_shape = (lambda w: (nseq, nchunk * rows, w)) if nsq > 1 else (lambda w: (n_rows, w))
    out_shape = ([jax.ShapeDtypeStruct(row_shape(w), BF16) for w in out_widths]
                 + [state_out_shape(a) for a, _ in states])
    if nsq > 1:
        proj = proj.reshape(nseq, nchunk * rows, proj.shape[-1])
    res = pl.pallas_call(
        functools.partial(kernel_fn, n_alias=len(aliased), out_mode=out_mode, nsq=nsq),
        grid=grid, in_specs=in_specs, out_specs=out_specs, out_shape=out_shape,
        scratch_shapes=scratch,
        input_output_aliases={n_in + i: len(out_widths) + i for i in range(len(aliased))},
        compiler_params=pltpu.CompilerParams(
            dimension_semantics=("parallel", "arbitrary"), vmem_limit_bytes=VMEM_LIMIT),
        name=name,
    )(*([proj] * len(segs)), *[a for a, _ in consts], *tables, *[a for a, _ in states], *aliased)
    if nsq > 1:
        res = ([r.reshape(n_rows, r.shape[-1]) for r in res[:len(out_widths)]]
               + list(res[len(out_widths):]))
    return res


def _state_slot(ref, out_mode):
    if out_mode != "first":
        return ref
    ref[1:] = jnp.zeros((DEPTH - 1,) + ref.shape[1:], F32)
    return ref.at[0]


def _interleave(stages, stagger=False):
    stages = list(stages)
    delay = {id(gen): (i if stagger else 0) for i, gen in enumerate(stages)}
    rnd = 0
    while stages:
        for gen in list(stages):
            if delay[id(gen)] > rnd:
                continue
            try:
                next(gen)
            except StopIteration:
                stages.remove(gen)
        rnd += 1


def _run_chunks(chunk, row_refs, y_ref, s0_ref, so_ref, scr, chained, nsq, out_mode):
    if not chained:
        _interleave([chunk(*row_refs, y_ref, None)])
        return
    s_all, = scr

    @pl.when(pl.program_id(1) == 0)
    def _():
        for j in range(nsq):
            s_all[j] = s0_ref[0]

    view = lambda r, j: r.at[j] if nsq > 1 else r
    _interleave([chunk(*[view(r, j) for r in row_refs], view(y_ref, j), s_all.at[j])
                 for j in range(nsq)])

    @pl.when(pl.program_id(1) == pl.num_programs(1) - 1)
    def _():
        dst = _state_slot(so_ref, out_mode)
        for j in range(nsq):
            dst[j] = s_all[j]


def _gla_kernel(*refs, rows, lb, tv, chained, nsq, n_alias, out_mode):
    row_refs = refs[:5]
    w2_ref, vec_ref, eb_ref, sh_ref, s0_ref = refs[5:10]
    ba_ref, wn_ref = (vec_ref.at[:, a:b] for a, b in GLA_VEC)
    y_out, so_ref = refs[10 + n_alias:12 + n_alias]
    scr = refs[12 + n_alias:]
    nblk = rows // lb
    width = GLA_H * GLA_DK

    def chunk(q_ref, k_ref, v_ref, r_ref, a_ref, y_ref, s_scr):
        states = [s_scr[h] for h in range(GLA_H)] if chained else None
        if q_ref.shape[0] != rows:
            to_tiles = _tile_spread(rows, q_ref.shape[0], lb, tv)
            load = lambda ref: jnp.dot(to_tiles, ref[...], preferred_element_type=F32)
            vb = load(v_ref).astype(BF16)
        else:
            load = lambda ref: ref[...].astype(F32)
            vb = v_ref[...]
        q = load(q_ref) * (GLA_DK ** -0.5)
        k = load(k_ref)
        a = _dot(load(a_ref), w2_ref[...]) + ba_ref[...]
        g = jax.nn.log_sigmoid(a) * (1.0 / GLA_GATE_NORM)
        t_in = _row_in_block(rows, lb)
        if tv < lb:
            valid = t_in >= lb - tv
            g = jnp.where(valid, g, 0.0)
            k = jnp.where(valid, k, 0.0)
        yield
        r_i, c_i = _iotas(rows)
        same = (r_i // lb) == (c_i // lb)
        sizes = []
        while 2 * GLA_BASE * 2 ** len(sizes) <= lb:
            sizes.append(2 * GLA_BASE * 2 ** len(sizes))
        sums = [same & (r_i >= c_i)] + ([same] if nblk > 1 else [])
        sums += [same & (c_i <= (r_i // sz) * sz + sz // 2 - 1) for sz in sizes]
        gsums = _dot_sel(jnp.concatenate(sums, axis=0), g)
        gcum = gsums[:rows]
        gtot = gcum[rows - 1:rows, :] if nblk == 1 else gsums[rows:2 * rows]
        g_mids = [gsums[(len(sums) - len(sizes) + i) * rows:(len(sums) - len(sizes) + i + 1) * rows]
                  for i in range(len(sizes))]
        qd = q * jnp.exp(gcum)
        kd = k * jnp.exp(gtot - gcum)
        sel = (lax.broadcasted_iota(jnp.int32, (rows, nblk * LANES), 0) // lb
               == lax.broadcasted_iota(jnp.int32, (rows, nblk * LANES), 1) // LANES)
        ds = jnp.exp(_dot_sel(sel, g, TN, sel_first=False))
        yield

        nbase = rows // GLA_BASE
        q3 = q.reshape(nbase, GLA_BASE, width)
        k3 = k.reshape(nbase, GLA_BASE, width)
        g3 = gcum.reshape(nbase, GLA_BASE, width)
        t3 = lax.broadcasted_iota(jnp.int32, (1, GLA_BASE, 1), 1)
        pieces = []
        for s in range(max(0, GLA_BASE - tv), GLA_BASE):
            dd = jnp.minimum(g3 - g3[:, s:s + 1, :], 0.0)
            w = q3 * k3[:, s:s + 1, :] * jnp.exp(dd)
            pieces.append(jnp.where(t3 >= s, w, 0.0).reshape(rows, width).astype(BF16))
            yield
        compact = jnp.dot(jnp.concatenate(pieces, axis=1), eb_ref[...], preferred_element_type=F32)
        spread = jnp.dot(compact.astype(BF16), sh_ref[...], preferred_element_type=F32)
        base_mask = (r_i // GLA_BASE) == (c_i // GLA_BASE)
        att = [jnp.where(base_mask, spread[:, h * LANES:h * LANES + rows], 0.0) for h in range(GLA_H)]
        yield

        for size, g_mid in zip(sizes, g_mids):
            second = (t_in % size) >= size // 2
            ql = jnp.where(second, q * jnp.exp(jnp.minimum(gcum - g_mid, 0.0)), 0.0)
            kl = jnp.where(second, 0.0, k * jnp.exp(jnp.minimum(g_mid - gcum, 0.0)))
            group = (r_i // size) == (c_i // size)
            for h in range(GLA_H):
                ks = slice(h * GLA_DK, (h + 1) * GLA_DK)
                att[h] = att[h] + jnp.where(group, _dot(ql[:, ks], kl[:, ks], NT), 0.0)
            yield

        so = _state_slot(so_ref, out_mode) if not chained else None
        o_heads = []
        for h in range(GLA_H):
            ks = slice(h * GLA_DK, (h + 1) * GLA_DK)
            vs = slice(h * GLA_DV, (h + 1) * GLA_DV)
            if chained:
                cur = states[h]
                if rows % LANES == 0:
                    oh = _dot(jnp.concatenate([att[h].astype(BF16), qd[:, ks].astype(BF16)], axis=1),
                              jnp.concatenate([vb[:, vs], cur.astype(BF16)], axis=0))
                else:
                    oh = _dot(att[h], vb[:, vs]) + _dot(qd[:, ks], cur)
                states[h] = ds[ks, :] * cur + _dot(kd[:, ks], vb[:, vs], TN)
            else:
                o_rows = []
                for b in range(nblk):
                    rs = slice(b * lb, (b + 1) * lb)
                    cur = s0_ref[b, h]
                    o_rows.append(_dot(qd[rs, ks], cur))
                    so[b, h] = ds[ks, b * LANES:(b + 1) * LANES] * cur + _dot(kd[rs, ks], vb[rs, vs], TN)
                oh = _dot(att[h], vb[:, vs]) + jnp.concatenate(o_rows, axis=0)
            o_heads.append(_rms(oh) * wn_ref[...])
            yield
        y = (_silu(load(r_ref)) * jnp.concatenate(o_heads, axis=1)).astype(BF16)
        if y_ref.shape[0] != rows:
            y = jnp.dot(_tile_spread(rows, y_ref.shape[0], lb, tv, transpose=True), y,
                        preferred_element_type=F32).astype(BF16)
        y_ref[...] = y
        if chained:
            for h in range(GLA_H):
                s_scr[h] = states[h]

    _run_chunks(chunk, row_refs, y_out, s0_ref, so_ref, scr, chained, nsq, out_mode)


def _gla_consts(tv):
    sources = jnp.arange(max(0, GLA_BASE - tv), GLA_BASE, dtype=jnp.int32)
    row = jnp.arange(sources.shape[0] * GLA_H * GLA_DK, dtype=jnp.int32)
    target = sources[row // (GLA_H * GLA_DK)] * GLA_H + (row % (GLA_H * GLA_DK)) // GLA_DK
    eb = (target[:, None] == jnp.arange(LANES, dtype=jnp.int32)[None, :]).astype(BF16)
    r = jnp.arange(LANES, dtype=jnp.int32)[:, None]
    c = jnp.arange(GLA_H * LANES, dtype=jnp.int32)[None, :]
    sh = ((r < GLA_BASE * GLA_H) & (r % GLA_H == c // LANES)
          & (r // GLA_H == (c % LANES) % GLA_BASE)).astype(BF16)
    return (eb[None], 0), (sh[None], 0)


def _gla(proj, weights, s0, *, lb, tv, rows, chained, name, nsq=1, **kw):
    kern = functools.partial(_gla_kernel, rows=rows, lb=lb, tv=tv, chained=chained)
    scratch = [pltpu.VMEM((nsq, GLA_H, GLA_DK, GLA_DV), F32)] if chained else []
    return _rec_call(kern, proj,
                     [("gla_q", 256), ("gla_k", 256), ("gla_v", 512), ("gla_r", 512), ("gla_a", 128)],
                     list(weights) + list(_gla_consts(tv)), [s0], [MIX], rows=rows,
                     chained=chained, scratch=scratch, name=name, nsq=nsq, **kw)


def _ssd_kernel(*refs, rows, nsq, n_alias, out_mode):
    row_refs = refs[:3]
    cw_ref, vec_ref, c0_ref, s0_ref = refs[3:7]
    cb_ref, dtb_ref, alog_ref, dsk_ref, wn_ref = (vec_ref.at[:, a:b] for a, b in SSD_VEC)
    y_out, co_ref, so_ref = refs[7 + n_alias:10 + n_alias]
    s_all, ext_all = refs[10 + n_alias:]
    low64 = lax.broadcasted_iota(jnp.int32, (SSM_N, LANES), 1) < SSM_P

    def chunk(z_ref, x_ref, dt_ref, y_ref, s_scr, ext_scr, j):
        states = [s_scr[p] for p in range(SSM_PAIRS)]
        xb = x_ref[...]
        xin = xb.astype(F32)
        tail = ext_scr[...]
        r_i, c_i = _iotas(rows)
        t8 = lax.broadcasted_iota(jnp.int32, (SUBLANES, 1), 0)
        conv = cb_ref[...] + cw_ref[SSM_CONV - 1:SSM_CONV, :] * xin
        head = jnp.zeros((SUBLANES, SSM_CONV_DIM), F32)
        for d in range(1, SSM_CONV):
            tap = cw_ref[SSM_CONV - 1 - d:SSM_CONV - d, :]
            shifted = jnp.dot((c_i == r_i - d).astype(BF16), xb, preferred_element_type=F32)
            conv = conv + tap * shifted
            head = head + tap * jnp.where(t8 < d, pltpu.roll(tail, d, axis=0), 0.0)
        conv = jnp.concatenate([conv[:SUBLANES] + head, conv[SUBLANES:]], axis=0)
        _state_slot(co_ref, out_mode)[j] = xin[rows - SUBLANES:, :]
        ext_scr[...] = xin[rows - SUBLANES:, :]
        act = _silu(conv)
        yield

        dt = jax.nn.softplus(dt_ref[...].astype(F32) + dtb_ref[...])
        gdt = dt * (-jnp.exp(alog_ref[...]))
        causal = r_i >= c_i
        gcum = _dot_sel(causal, gdt)
        if rows < LANES:
            gsq = jnp.concatenate([gcum, jnp.zeros((LANES - rows, LANES), F32)], axis=0)
            gcum_t = gsq.T[:, :rows]
        else:
            gcum_t = gcum.T
        gtot = gcum[rows - 1:rows, :]

        lane = lax.broadcasted_iota(jnp.int32, (rows, LANES), 1)
        low = lane < SSM_P
        bcol = act[:, MIX:MIX + LANES]
        ccol = act[:, MIX + LANES:MIX + 2 * LANES]
        bswap = pltpu.roll(bcol, SSM_N, axis=1)
        cswap = pltpu.roll(ccol, SSM_N, axis=1)
        b2 = (jnp.where(low, bcol, bswap), jnp.where(low, bswap, bcol))
        c2 = (jnp.where(low, ccol, cswap), jnp.where(low, cswap, ccol))
        cb = (jnp.where(causal, _dot(jnp.where(low, ccol, 0.0), bcol, NT), 0.0),
              jnp.where(causal, _dot(jnp.where(low, 0.0, ccol), bcol, NT), 0.0))

        def pair_lanes(x, p):
            return jnp.where(low[:x.shape[0]], x[:, 2 * p:2 * p + 1], x[:, 2 * p + 1:2 * p + 2])

        yield
        y_pairs = []
        for p in range(SSM_PAIRS):
            gi = p // (SSM_PAIRS // SSM_G)
            decs = []
            for h in (2 * p, 2 * p + 1):
                diff = jnp.minimum(gcum[:, h:h + 1] - gcum_t[h:h + 1, :], 0.0)
                decs.append(cb[gi] * jnp.exp(diff))
            xp = act[:, p * LANES:(p + 1) * LANES]
            vp = xp * pair_lanes(dt, p)
            vbd = jnp.concatenate([jnp.where(low, vp, 0.0), jnp.where(low, 0.0, vp)], axis=0)
            g2 = pair_lanes(gcum, p)
            ge2 = pair_lanes(gtot, p)
            cin = c2[gi] * jnp.exp(g2)
            bout = b2[gi] * jnp.exp(ge2 - g2)
            cur = states[p]
            bd = jnp.concatenate([jnp.where(low64, cur, 0.0), jnp.where(low64, 0.0, cur)], axis=0)
            if rows % LANES == 0:
                o = _dot(jnp.concatenate(decs + [cin], axis=1).astype(BF16),
                         jnp.concatenate([vbd.astype(BF16), bd.astype(BF16)], axis=0))
            else:
                o = _dot(jnp.concatenate(decs, axis=1), vbd) + _dot(cin, bd)
            u = _dot(bout, vp, TN)
            states[p] = jnp.exp(ge2) * cur + jnp.where(low64, u[:SSM_N, :], u[SSM_N:, :])
            y_pairs.append(o + dsk_ref[:, p * LANES:(p + 1) * LANES] * xp)
            yield
        y = jnp.concatenate(y_pairs, axis=1) * _silu(z_ref[...].astype(F32))
        half = MIX // SSM_G
        y = jnp.concatenate([_rms(y[:, gi * half:(gi + 1) * half]) for gi in range(SSM_G)], axis=1)
        y_ref[...] = (y * wn_ref[...]).astype(BF16)
        for p in range(SSM_PAIRS):
            s_scr[p] = states[p]

    @pl.when(pl.program_id(1) == 0)
    def _():
        for j in range(nsq):
            for p in range(SSM_PAIRS):
                s_all[j, p] = jnp.concatenate([s0_ref[0, 2 * p], s0_ref[0, 2 * p + 1]], axis=1)
            ext_all[j, 0:SUBLANES, :] = c0_ref[0]

    view = lambda r, j: r.at[j] if nsq > 1 else r
    _interleave([chunk(*[view(r, j) for r in row_refs], view(y_out, j), s_all.at[j],
                       ext_all.at[j], j) for j in range(nsq)])

    @pl.when(pl.program_id(1) == pl.num_programs(1) - 1)
    def _():
        dst = _state_slot(so_ref, out_mode)
        for j in range(nsq):
            for p in range(SSM_PAIRS):
                dst[j, 2 * p] = s_all[j, p][:, :SSM_P]
                dst[j, 2 * p + 1] = s_all[j, p][:, SSM_P:]


def _ssd(proj, weights, c0, s0, *, lb, tv, rows, chained, name, nsq=1, **kw):
    assert chained and lb == tv == rows
    kern = functools.partial(_ssd_kernel, rows=rows)
    scratch = [pltpu.VMEM((nsq, SSM_PAIRS, SSM_N, LANES), F32),
               pltpu.VMEM((nsq, SUBLANES, SSM_CONV_DIM), F32)]
    return _rec_call(kern, proj, [("ssm_z", 512), ("ssm_xbc", 768), ("ssm_dt", 128)],
                     weights, [c0, s0], [MIX], rows=rows, chained=chained, scratch=scratch,
                     name=name, nsq=nsq, **kw)


def _ssd_lanes_kernel(*refs, nb, ts, n_alias, out_mode):
    z_ref, x_ref, dt_ref, cw_ref, vec_ref, c0_ref, s0_ref = refs[:7]
    cb_ref, dtb_ref, alog_ref, dsk_ref, wn_ref = (vec_ref.at[a:b, :] for a, b in SSD_VEC)
    y_ref, co_ref, so_ref = refs[7 + n_alias:10 + n_alias]
    act_scr, z_scr, dt_scr, g_scr, cbs_scr, y_scr = refs[10 + n_alias:]
    h = pl.program_id(0)
    rows = nb * ts
    b_off, c_off = MIX, MIX + SSM_G * SSM_N

    @pl.when(h == 0)
    def _():
        seq = lax.broadcasted_iota(jnp.int32, (nb, rows), 0)
        row = lax.broadcasted_iota(jnp.int32, (nb, rows), 1)
        plain = [c0_ref[r] for r in range(SSM_CONV - 1)]
        for t in range(ts):
            pick = (row == seq * ts + t).astype(BF16)
            plain.append(jnp.dot(pick, x_ref[...], preferred_element_type=F32))
            z_scr[t] = jnp.dot(pick, z_ref[...], preferred_element_type=F32).T
            dt_scr[t] = jnp.dot(pick, dt_ref[...], preferred_element_type=F32).T
        co = _state_slot(co_ref, out_mode)
        for r in range(SSM_CONV - 1):
            co[r] = plain[ts + r]
        lanes = [p.T for p in plain]
        for t in range(ts):
            conv = cb_ref[...]
            for i in range(SSM_CONV):
                conv = conv + cw_ref[:, i:i + 1] * lanes[t + i]
            act_scr[t] = _silu(conv)
        a_neg = -jnp.exp(alog_ref[...])
        gsum = jnp.zeros((LANES, nb), F32)
        for t in range(ts):
            dt = jax.nn.softplus(dt_scr[t] + dtb_ref[...])
            dt_scr[t] = dt
            gsum = gsum + dt * a_neg
            g_scr[t] = gsum
        for gi in range(SSM_G):
            for t in range(ts):
                cm = act_scr[t, c_off + gi * SSM_N:c_off + (gi + 1) * SSM_N, :]
                for s in range(t + 1):
                    bm = act_scr[s, b_off + gi * SSM_N:b_off + (gi + 1) * SSM_N, :]
                    idx = (gi * ts + t) * ts + s
                    cbs_scr[idx:idx + 1, :] = jnp.sum(cm * bm, axis=0, keepdims=True)

    gi = h // (SSM_H // SSM_G)
    x_row = pl.multiple_of(h * SSM_P, SSM_P)
    xs = [act_scr[t, pl.ds(x_row, SSM_P), :] for t in range(ts)]
    dts = [dt_scr[t, pl.ds(h, 1), :] for t in range(ts)]
    gs = [g_scr[t, pl.ds(h, 1), :] for t in range(ts)]
    gtot = gs[-1]
    decay = jnp.exp(gtot)
    into_state = [jnp.exp(gtot - gs[t]) * dts[t] for t in range(ts)]
    so = _state_slot(so_ref, out_mode)

    def state_row(n, acc):
        s_n = s0_ref[n]
        new = decay * s_n
        out = []
        for t in range(ts):
            out.append(acc[t] + act_scr[t, pl.ds(c_off + gi * SSM_N + n, 1), :] * s_n)
            new = new + (act_scr[t, pl.ds(b_off + gi * SSM_N + n, 1), :] * into_state[t]) * xs[t]
        so[n] = new
        return tuple(out)

    acc = lax.fori_loop(0, SSM_N, state_row,
                        tuple(jnp.zeros((SSM_P, nb), F32) for _ in range(ts)), unroll=4)
    for t in range(ts):
        o = jnp.exp(gs[t]) * acc[t]
        for s in range(t + 1):
            cb = cbs_scr[pl.ds((gi * ts + t) * ts + s, 1), :]
            o = o + (cb * jnp.exp(gs[t] - gs[s]) * dts[s]) * xs[s]
        y_scr[t, pl.ds(x_row, SSM_P), :] = o + dsk_ref[pl.ds(x_row, SSM_P), :] * xs[t]

    @pl.when(h == SSM_H - 1)
    def _():
        row = lax.broadcasted_iota(jnp.int32, (rows, nb), 0)
        seq = lax.broadcasted_iota(jnp.int32, (rows, nb), 1)
        half = MIX // SSM_G
        out = jnp.zeros((rows, MIX), F32)
        for t in range(ts):
            y = y_scr[t] * _silu(z_scr[t])
            normed = []
            for g2 in range(SSM_G):
                blk = y[g2 * half:(g2 + 1) * half, :]
                ms = jnp.mean(blk * blk, axis=0, keepdims=True)
                normed.append(blk * lax.rsqrt(ms + 1e-6))
            y = (jnp.concatenate(normed, axis=0) * wn_ref[...]).T.astype(BF16)
            put = (row == seq * ts + t).astype(BF16)
            out = out + jnp.dot(put, y, preferred_element_type=F32)
        y_ref[...] = out.astype(BF16)


def _ssd_lanes(proj, weights, c0, s0, *, layer, nb, ts, stacked, name):
    rows = nb * ts
    out_mode = "first" if layer == 0 else "later"
    aliased = list(stacked) if out_mode == "later" else []
    seg = lambda name_, w: pl.BlockSpec((rows, w), lambda h: (0, COL[name_] // w))
    const = lambda a: pl.BlockSpec((None,) + a.shape[1:], lambda h: (layer,) + (0,) * (a.ndim - 1))
    if out_mode == "first":
        co_spec = pl.BlockSpec((DEPTH,) + c0.shape[1:], lambda h: (0, 0, 0, 0))
        so_spec = pl.BlockSpec((DEPTH, None) + s0.shape[2:], lambda h: (0, h, 0, 0, 0))
    else:
        co_spec = pl.BlockSpec((None,) + c0.shape[1:], lambda h: (layer, 0, 0, 0))
        so_spec = pl.BlockSpec((None, None) + s0.shape[2:], lambda h: (layer, h, 0, 0, 0))
    n_in = 3 + len(weights) + 2
    assert len(weights) == 2
    return pl.pallas_call(
        functools.partial(_ssd_lanes_kernel, nb=nb, ts=ts, n_alias=len(aliased), out_mode=out_mode),
        grid=(SSM_H,),
        in_specs=([seg("ssm_z", 512), seg("ssm_xbc", 768), seg("ssm_dt", 128)]
                  + [const(a) for a in weights] + [const(c0)]
                  + [pl.BlockSpec((None, None) + s0.shape[2:], lambda h: (layer, h, 0, 0, 0))]
                  + [pl.BlockSpec(memory_space=pl.ANY) for _ in aliased]),
        out_specs=[pl.BlockSpec((rows, MIX), lambda h: (0, 0)), co_spec, so_spec],
        out_shape=[jax.ShapeDtypeStruct((rows, MIX), BF16),
                   jax.ShapeDtypeStruct(c0.shape, F32), jax.ShapeDtypeStruct(s0.shape, F32)],
        scratch_shapes=[pltpu.VMEM((ts, SSM_CONV_DIM, nb), F32), pltpu.VMEM((ts, MIX, nb), F32),
                        pltpu.VMEM((ts, LANES, nb), F32), pltpu.VMEM((ts, LANES, nb), F32),
                        pltpu.VMEM((SSM_G * ts * ts, nb), F32), pltpu.VMEM((ts, MIX, nb), F32)],
        input_output_aliases={n_in + i: 1 + i for i in range(len(aliased))},
        compiler_params=pltpu.CompilerParams(
            dimension_semantics=("arbitrary",), vmem_limit_bytes=VMEM_LIMIT),
        name=name,
    )(proj, proj, proj, *weights, c0, s0, *aliased)


def _ret_kernel(*refs, rows, lb, tv, chained, nsq, n_alias, out_mode):
    row_refs = refs[:4]
    lg_ref, swap_ref, cos_ref, sin_ref, s0_ref = refs[4:9]
    y_out, so_ref = refs[9 + n_alias:11 + n_alias]
    scr = refs[11 + n_alias:]
    nblk = rows // lb
    t_col = _row_in_block(rows, lb)
    s_row = lax.broadcasted_iota(jnp.int32, (1, rows), 1) % lb
    n_col = jnp.maximum(t_col - (lb - tv) + 1, 0).astype(F32)
    n_row = jnp.maximum(s_row - (lb - tv) + 1, 0).astype(F32)

    def decay_matrix(h):
        r_i, c_i = _iotas(rows)
        causal = ((r_i // lb) == (c_i // lb)) & (r_i >= c_i)
        diff = (n_col - n_row) * lg_ref[:, h:h + 1]
        return jnp.where(causal, jnp.exp(jnp.minimum(diff, 0.0)), 0.0)

    width = RET_H * RET_DK

    def state_scales():
        head = lax.broadcasted_iota(jnp.int32, (1, width), 1) // RET_DK
        lg_lane = lg_ref[:, 0:1]
        for h in range(1, RET_H):
            lg_lane = jnp.where(head == h, lg_ref[:, h:h + 1], lg_lane)
        gc = n_col * lg_lane
        return jnp.exp(gc), jnp.exp(float(tv) * lg_lane - gc)

    if chained:
        dec_scr, scale_scr = scr[1:]

        @pl.when(pl.program_id(1) == 0)
        def _():
            for h in range(RET_H):
                dec_scr[h] = decay_matrix(h)
            scale_scr[0], scale_scr[1] = state_scales()

    lane = lax.broadcasted_iota(jnp.int32, (rows, width), 1)
    first_half = (lane % RET_DK) < (RET_DK // 2)
    cos = cos_ref[...]
    sin = jnp.where(first_half, -sin_ref[...], sin_ref[...])

    def rope(x):
        partner = jnp.dot(x, swap_ref[...], preferred_element_type=F32)
        return x.astype(F32) * cos + partner * sin

    def chunk(q_ref, k_ref, v_ref, g_ref, y_ref, s_scr):
        states = [s_scr[h] for h in range(RET_H)] if chained else None
        if q_ref.shape[0] != rows:
            to_tiles = _tile_spread(rows, q_ref.shape[0], lb, tv)
            load = lambda ref: jnp.dot(to_tiles, ref[...], preferred_element_type=F32).astype(BF16)
        else:
            load = lambda ref: ref[...]
        q = rope(load(q_ref))
        k = rope(load(k_ref)) * (RET_DK ** -0.5)
        v = load(v_ref)
        if tv < lb:
            k = jnp.where(t_col >= lb - tv, k, 0.0)
        q_scale, k_scale = (scale_scr[0], scale_scr[1]) if chained else state_scales()
        q_in = q * q_scale
        k_out = k * k_scale
        yield

        so = _state_slot(so_ref, out_mode) if not chained else None
        outs = []
        for h in range(RET_H):
            ks = slice(h * RET_DK, (h + 1) * RET_DK)
            vs = slice(h * RET_DV, (h + 1) * RET_DV)
            att = _dot(q[:, ks], k[:, ks], NT) * (dec_scr[h] if chained else decay_matrix(h))
            ge = float(tv) * lg_ref[:, h:h + 1]
            qin = q_in[:, ks]
            kout = k_out[:, ks]
            if chained:
                cur = states[h]
                if rows % LANES == 0:
                    oh = _dot(jnp.concatenate([att.astype(BF16), qin.astype(BF16)], axis=1),
                              jnp.concatenate([v[:, vs], cur.astype(BF16)], axis=0))
                else:
                    oh = _dot(att, v[:, vs]) + _dot(qin, cur)
                states[h] = jnp.exp(ge) * cur + _dot(kout, v[:, vs], TN)
            else:
                oh = _dot(att, v[:, vs])
                o_rows = []
                for b in range(nblk):
                    rs = slice(b * lb, (b + 1) * lb)
                    cur = s0_ref[b, h]
                    o_rows.append(_dot(qin[rs], cur))
                    so[b, h] = jnp.exp(ge) * cur + _dot(kout[rs], v[rs, vs], TN)
                oh = oh + jnp.concatenate(o_rows, axis=0)
            outs.append(_rms(oh))
            yield
        y = (_silu(load(g_ref).astype(F32)) * jnp.concatenate(outs, axis=1)).astype(BF16)
        if y_ref.shape[0] != rows:
            y = jnp.dot(_tile_spread(rows, y_ref.shape[0], lb, tv, transpose=True), y,
                        preferred_element_type=F32).astype(BF16)
        y_ref[...] = y
        if chained:
            for h in range(RET_H):
                s_scr[h] = states[h]

    _run_chunks(chunk, row_refs, y_out, s0_ref, so_ref, scr[:1], chained, nsq, out_mode)


def _rope_swap():
    width = RET_H * RET_DK
    r = jnp.arange(width, dtype=jnp.int32)[:, None]
    c = jnp.arange(width, dtype=jnp.int32)[None, :]
    same_head = (r // RET_DK) == (c // RET_DK)
    return (same_head & (r % RET_DK == (c % RET_DK + RET_DK // 2) % RET_DK)).astype(BF16)[None]


def _ret(proj, lg, cos, sin, s0, *, lb, tv, rows, chained, name, nsq=1, **kw):
    kern = functools.partial(_ret_kernel, rows=rows, lb=lb, tv=tv, chained=chained)
    scratch = ([pltpu.VMEM((nsq, RET_H, RET_DK, RET_DV), F32), pltpu.VMEM((RET_H, rows, rows), F32),
                pltpu.VMEM((2, rows, RET_H * RET_DK), F32)] if chained else [])
    return _rec_call(kern, proj, [("ret_q", 256), ("ret_k", 256), ("ret_v", 512), ("ret_g", 512)],
                     [(lg, 0), (_rope_swap(), 0)], [s0], [MIX], rows=rows, chained=chained,
                     scratch=scratch, name=name,
                     tables=(cos, sin), nsq=nsq, **kw)


def _dense2_kernel(x_ref, gate_ref, yg_ref, ys_ref, yr_ref, lnw_ref, lnb_ref, wg_ref, ws_ref, wr_ref,
                   wo_ref, w1_ref, w2_ref, vec_ref, o_ref, *, ff_chunk, parts, input_ln):
    tm = x_ref.shape[0]
    l1w_ref, l1b_ref, b2_ref, l2w_ref, l2b_ref, b1_ref = (vec_ref.at[:, a:b] for a, b in D2_VEC)

    def rows_stage(rs):
        branches = [_dot(y_ref[rs, :], w_ref[...])
                    for y_ref, w_ref in ((yg_ref, wg_ref), (ys_ref, ws_ref), (yr_ref, wr_ref))]
        yield
        gate = lambda i: jax.nn.sigmoid(gate_ref[rs, i * D_MODEL:(i + 1) * D_MODEL].astype(F32))
        merged = gate(0) * branches[0] + gate(1) * branches[1] + gate(2) * branches[2]
        yield
        mix = _dot(merged, wo_ref[...])
        yield
        x = x_ref[rs, :]
        if input_ln:
            x = _layer_norm(x, lnw_ref[...], lnb_ref[...])
        h = _layer_norm(ALPHA * x + mix, l1w_ref[...], l1b_ref[...])
        hb = h.astype(BF16)
        ff = jnp.zeros_like(h) + b2_ref[...]
        yield
        for c0 in range(0, D_FF, ff_chunk):
            hid = jnp.dot(hb, w1_ref[:, c0:c0 + ff_chunk], preferred_element_type=F32)
            hid = jnp.square(jnp.maximum(hid + b1_ref[:, c0:c0 + ff_chunk], 0.0))
            ff = ff + _dot(hid, w2_ref[c0:c0 + ff_chunk, :])
            yield
        o_ref[rs, :] = _layer_norm(ALPHA * h + ff, l2w_ref[...], l2b_ref[...])

    _interleave([rows_stage(pl.ds(i * (tm // parts), tm // parts)) for i in range(parts)],
                stagger=True)


def _dense2(x, proj, yg, ys, yr, lnw, lnb, wl, layer, *, input_ln, tm, name):
    m = x.shape[0]
    row = lambda w: pl.BlockSpec((tm, w), lambda i: (i, 0))
    vec = pl.BlockSpec((1, D_MODEL), lambda i: (0, 0))
    const = lambda r, w: pl.BlockSpec((None, r, w), lambda i: (layer, 0, 0),
                                      pipeline_mode=pl.Buffered(1))
    return pl.pallas_call(
        functools.partial(_dense2_kernel, ff_chunk=1024,
                          parts=next(p for p in (2, 3, 1) if tm % (16 * p) == 0),
                          input_ln=input_ln),
        grid=(m // tm,),
        in_specs=[row(D_MODEL), row(3 * D_MODEL), row(MIX), row(MIX), row(MIX), vec, vec,
                  const(MIX, D_MODEL), const(MIX, D_MODEL), const(MIX, D_MODEL),
                  const(D_MODEL, D_MODEL), const(D_MODEL, D_FF), const(D_FF, D_MODEL),
                  const(1, D2_VEC[-1][1])],
        out_specs=row(D_MODEL),
        out_shape=jax.ShapeDtypeStruct((m, D_MODEL), F32),
        compiler_params=pltpu.CompilerParams(
            dimension_semantics=("parallel",), vmem_limit_bytes=VMEM_LIMIT),
        name=name,
    )(x, proj, yg, ys, yr, lnw, lnb, wl["w_gla_out"], wl["w_ssm_out"], wl["w_ret_out"], wl["w_o"],
      wl["w_ff1"], wl["w_ff2"], wl["vec"])


def _pick_tile(n, pref):
    t = min(n, pref)
    while n % t or t % SUBLANES:
        t -= 1
    return t


def _rearrange_w_in(w):
    w = jnp.swapaxes(w, -1, -2)
    offs = [0]
    for s in SPLIT_SIZES:
        offs.append(offs[-1] + s)
    names = ("gla_q", "gla_k", "gla_v", "gla_r", "gla_a", "ssm_z", "ssm_xbc", "ssm_dt",
             "ret_q", "ret_k", "ret_v", "ret_g", "gates")
    seg = {n: w[..., offs[i]:offs[i + 1], :].astype(BF16) for i, n in enumerate(names)}
    pad = lambda a: jnp.pad(a, ((0, 0),) * (a.ndim - 2) + ((0, LANES - a.shape[-2]), (0, 0)))
    order = sorted(COL, key=COL.get)
    parts = [pad(seg[n]) if n in ("gla_a", "ssm_dt") else seg[n] for n in order]
    return jnp.concatenate(parts, axis=-2)


def _rope_tables(pos):
    half = RET_DK // 2
    inv_freq = ROPE_BASE ** (-jnp.arange(half, dtype=F32) / half)
    ang = pos.astype(F32)[:, None] * inv_freq[None, :]
    cos = jnp.tile(jnp.cos(ang), (1, 2 * RET_H))
    sin = jnp.tile(jnp.sin(ang), (1, 2 * RET_H))
    return cos, sin


def kernel(x_prompt, x_sample, state_gla, state_ssm, state_conv, state_ret, meta_tokens,
           ln_in_w, ln_in_b, w_in, w_gla_a2, b_gla_a, w_gla_norm, conv_w, conv_b, dt_bias,
           a_log, d_skip, w_ssm_norm, w_gla_out, w_ssm_out, w_ret_out, w_o, ln1_w, ln1_b,
           w_ff1, b_ff1, w_ff2, b_ff2, ln2_w, ln2_b):
    bp, tp, d = x_prompt.shape
    bs, ts, _ = x_sample.shape
    assert d == D_MODEL and tp % CHUNK == 0 and w_in.shape[0] == DEPTH
    assert SAMPLE_ROWS % ts == 0
    nchunk = tp // CHUNK
    pad_rows = SAMPLE_ROWS - ts

    x_body = x_prompt.reshape(bp * tp, d)
    n_sample = bs * ts
    rows_s = min(CHUNK, bs * SAMPLE_ROWS)
    assert (bs * SAMPLE_ROWS) % rows_s == 0 and n_sample % N_META == 0
    n_small = n_sample + N_META
    x_small = jnp.concatenate([x_sample.reshape(n_sample, d), meta_tokens.astype(F32)], axis=0)

    cos_b, sin_b = _rope_tables(N_META + jnp.arange(tp, dtype=jnp.int32))
    cos_m, sin_m = _rope_tables(jnp.arange(N_META, dtype=jnp.int32))
    pos_tile = PAST_LEN - pad_rows + jnp.arange(SAMPLE_ROWS, dtype=jnp.int32)
    cos_s, sin_s = _rope_tables(jnp.tile(pos_tile, rows_s // SAMPLE_ROWS))
    lg_ret = jnp.pad(jnp.log1p(-jnp.exp2(-5.0 - jnp.arange(RET_H, dtype=F32))),
                     (0, LANES - RET_H)).reshape(1, 1, LANES)

    zero_gla = (jnp.zeros((1, 1, GLA_H, GLA_DK, GLA_DV), F32), 0)
    zero_ssm = (jnp.zeros((1, 1, SSM_H, SSM_N, SSM_P), F32), 0)
    zero_conv = (jnp.zeros((1, 1, SUBLANES, SSM_CONV_DIM), F32), 0)
    zero_ret = (jnp.zeros((1, 1, RET_H, RET_DK, RET_DV), F32), 0)
    from_meta = lambda a: (a[None], 0)
    ssm_lanes = state_ssm.transpose(0, 2, 3, 4, 1)
    conv_lanes = state_conv.transpose(0, 2, 1, 3)

    pack = lambda *vs: jnp.concatenate(vs, axis=1)
    lane_pad = lambda a: jnp.pad(a, ((0, 0), (0, LANES - a.shape[1])))
    w_in_all = _rearrange_w_in(w_in)
    wl = dict(w_gla_out=w_gla_out.astype(BF16), w_ssm_out=w_ssm_out.astype(BF16),
              w_ret_out=w_ret_out.astype(BF16), w_o=w_o.astype(BF16),
              w_ff1=w_ff1.astype(BF16), w_ff2=w_ff2.astype(BF16),
              vec=pack(ln1_w, ln1_b, b_ff2, ln2_w, ln2_b, b_ff1)[:, None, :])
    gla_w = [jnp.pad(w_gla_a2, ((0, 0), (0, LANES - GLA_RANK), (0, 0))),
             pack(b_gla_a, w_gla_norm)[:, None, :]]
    ssd_vec = pack(conv_b, lane_pad(dt_bias), lane_pad(a_log), jnp.repeat(d_skip, SSM_P, axis=1),
                   w_ssm_norm)
    ssd_w = [conv_w, ssd_vec[:, None, :]]
    ssd_cols = [conv_w.transpose(0, 2, 1), ssd_vec[:, :, None]]
    ln_w, ln_b = ln_in_w.reshape(1, -1), ln_in_b.reshape(1, -1)

    tm_in = _pick_tile(bp * tp, 2048)
    tm_d2 = _pick_tile(bp * tp, 512)
    tm_d2s = n_small if n_small <= 1024 else _pick_tile(n_small, 512)
    tn = 1536
    meta = dict(row0=n_sample, nseq=1, nchunk=1, rows=N_META, lb=N_META, tv=N_META, chained=True,
                per_seq_state=False)
    body = dict(row0=0, nseq=bp, nchunk=nchunk, rows=CHUNK, lb=CHUNK, tv=CHUNK, chained=True,
                per_seq_state=False)
    ret_rows = 4 * CHUNK if tp % (4 * CHUNK) == 0 else CHUNK
    ret_body = dict(body, nchunk=tp // ret_rows, rows=ret_rows, lb=ret_rows, tv=ret_rows)
    seqs_per_step = lambda want: max(n for n in (1, 2, 4, 8) if n <= want and bp % n == 0)
    samp = dict(row0=0, nseq=bs, nchunk=1, rows=rows_s, lb=SAMPLE_ROWS, tv=ts, chained=False,
                per_seq_state=True, rows_io=rows_s // SAMPLE_ROWS * ts)

    names = ("gla_p", "gla_s", "ssm_p", "ssm_s", "conv_p", "conv_s", "ret_p", "ret_s")
    st = {k: None for k in names}
    stk = lambda *keys: [] if st[keys[0]] is None else [st[k] for k in keys]
    xb, xs = x_body, x_small
    for l in range(DEPTH):
        proj_s = _in_proj(xs, ln_w, ln_b, w_in_all, l, apply_ln=(l == 0), tm=n_small, tn=tn,
                          name=f"inproj_small_{l}")
        yg_m, sg_m = _gla(proj_s, gla_w, zero_gla, layer=l, name=f"gla_meta_{l}", **meta)
        ys_m, cv_m, ss_m = _ssd(proj_s, ssd_w, zero_conv, zero_ssm, layer=l,
                                name=f"ssd_meta_{l}", **meta)
        yr_m, sr_m = _ret(proj_s, lg_ret, cos_m, sin_m, zero_ret, layer=l,
                          name=f"ret_meta_{l}", **meta)
        yg_s, st["gla_s"] = _gla(proj_s, gla_w, (state_gla, l), layer=l, stacked=stk("gla_s"),
                                 name=f"gla_sample_{l}", **samp)
        ys_s, st["conv_s"], st["ssm_s"] = _ssd_lanes(proj_s, ssd_cols, conv_lanes, ssm_lanes, layer=l,
                                                     nb=bs, ts=ts, stacked=stk("conv_s", "ssm_s"),
                                                     name=f"ssd_sample_{l}")
        yr_s, st["ret_s"] = _ret(proj_s, lg_ret, cos_s, sin_s, (state_ret, l), layer=l,
                                 stacked=stk("ret_s"), name=f"ret_sample_{l}", **samp)
        yg = jnp.concatenate([yg_s, yg_m], axis=0)
        ys = jnp.concatenate([ys_s, ys_m], axis=0)
        yr = jnp.concatenate([yr_s, yr_m], axis=0)
        xs = _dense2(xs, proj_s, yg, ys, yr, ln_w, ln_b, wl, l, input_ln=(l == 0), tm=tm_d2s,
                     name=f"dense2_small_{l}")

        proj_b = _in_proj(xb, ln_w, ln_b, w_in_all, l, apply_ln=(l == 0), tm=tm_in, tn=tn,
                          name=f"inproj_body_{l}")
        yg_b, st["gla_p"] = _gla(proj_b, gla_w, from_meta(sg_m), layer=l, stacked=stk("gla_p"),
                                 name=f"gla_body_{l}", nsq=seqs_per_step(8), **body)
        ys_b, st["conv_p"], st["ssm_p"] = _ssd(proj_b, ssd_w, from_meta(cv_m), from_meta(ss_m),
                                               layer=l, stacked=stk("conv_p", "ssm_p"),
                                               name=f"ssd_body_{l}", nsq=seqs_per_step(8), **body)
        yr_b, st["ret_p"] = _ret(proj_b, lg_ret, cos_b, sin_b, from_meta(sr_m), layer=l,
                                 stacked=stk("ret_p"), name=f"ret_body_{l}", nsq=seqs_per_step(4),
                                 **ret_body)
        xb = _dense2(xb, proj_b, yg_b, ys_b, yr_b, ln_w, ln_b, wl, l, input_ln=(l == 0), tm=tm_d2,
                     name=f"dense2_body_{l}")

    y_prompt = xb.reshape(bp, tp, d)
    y_sample = xs[:n_sample].reshape(bs, ts, d)
    tail3 = lambda c: c[:, :, SUBLANES - (SSM_CONV - 1):, :]
    return (y_prompt, y_sample, st["gla_p"], st["gla_s"], st["ssm_p"],
            st["ssm_s"].transpose(0, 4, 1, 2, 3), tail3(st["conv_p"]),
            st["conv_s"].transpose(0, 2, 1, 3), st["ret_p"], st["ret_s"])
```
